```python
import math
import jax, jax.numpy as jnp
from jax import lax
import numpy as np

D_MODEL = 1024
BATCH = 8
SEQ = 4096
DEPTH = 2

CTX_LEN = 256
GRID_W = 64
N_BRANCH = 4
N_HEADS = 4
HEAD_DIM = 64
CONV_CH = 256
CONV_WIDTH = 31
Q_LORA = 192
KV_LORA = 128
QK_NOPE = 64
QK_ROPE = 32
V_HEAD = 64
DIFF_DIM = 32
DIFF_V = 2 * DIFF_DIM
NA_KH = 8
NA_KW = 16
NA_QB = 16
ROPE_DIM = 32
ROPE_BASE = 10000.0
QBLOCK = 128
N_EXPERTS = 32
TOP_K = 4
D_FF = 1024
SWIGLU_LIMIT = 7.0
SWIGLU_ALPHA = 1.702
EPS = 1e-6
NEG_INF = -1e30

A_IN = 2 * CONV_CH
B_IN = Q_LORA + KV_LORA + QK_ROPE
C_IN = N_HEADS * (4 * DIFF_DIM + DIFF_V)
D_IN = N_HEADS * 3 * HEAD_DIM
OFF_B = A_IN
OFF_C = OFF_B + B_IN
OFF_D = OFF_C + C_IN
IN_WIDTH = OFF_D + D_IN

kernel_name = 'hybrid_gated_mixer_moe_dit'

F32 = jnp.float32


def rmsnorm(x, g):
    xf = x.astype(F32)
    y = xf * lax.rsqrt(jnp.mean(xf * xf, axis=-1, keepdims=True) + EPS)
    return (y * g.astype(F32)).astype(x.dtype)


def layernorm(x, g, b):
    xf = x.astype(F32)
    mu = jnp.mean(xf, axis=-1, keepdims=True)
    var = jnp.mean(jnp.square(xf - mu), axis=-1, keepdims=True)
    y = (xf - mu) * lax.rsqrt(var + EPS)
    return (y * g.astype(F32) + b.astype(F32)).astype(x.dtype)


def rope(x, cos, sin):
    half = x.shape[-1] // 2
    x1, x2 = x[..., :half], x[..., half:]
    return jnp.concatenate([x1 * cos - x2 * sin, x2 * cos + x1 * sin], axis=-1)


def axial_rope_tables(n_tokens, dtype):
    t = jnp.arange(n_tokens, dtype=jnp.int32)
    rows = (t // GRID_W).astype(F32)
    cols = (t % GRID_W).astype(F32)
    axis_dim = ROPE_DIM // 2
    inv = ROPE_BASE ** (-jnp.arange(0, axis_dim, 2, dtype=F32) / axis_dim)
    theta = jnp.concatenate([rows[:, None] * inv, cols[:, None] * inv], axis=-1)
    return jnp.cos(theta).astype(dtype), jnp.sin(theta).astype(dtype)


def merge_heads(o):
    b, h, n, d = o.shape
    return o.transpose(0, 2, 1, 3).reshape(b, n, h * d)


def sweep_query_blocks(fn, q):
    b, h, n = q.shape[:3]
    nb = n // QBLOCK
    qb = jnp.moveaxis(q.reshape(b, h, nb, QBLOCK, *q.shape[3:]), 2, 0)
    out = lax.map(fn, qb)
    return jnp.moveaxis(out, 0, 2).reshape(b, h, n, out.shape[-1])


def softmax_attend(q, k, v):
    s = jnp.einsum('bhqd,bhkd->bhqk', q, k).astype(F32) * (q.shape[-1] ** -0.5)
    p = jax.nn.softmax(s, axis=-1).astype(v.dtype)
    return jnp.einsum('bhqk,bhkd->bhqd', p, v)


def dense_attention(q, k, v):
    return sweep_query_blocks(lambda qb: softmax_attend(qb, k, v), q)


def diff_attend(q, k, v, lam):
    s = jnp.einsum('bhqcd,bhkcd->bhcqk', q, k).astype(F32) * (q.shape[-1] ** -0.5)
    p = jax.nn.softmax(s, axis=-1)
    w = (p[:, :, 0] - lam * p[:, :, 1]).astype(v.dtype)
    return jnp.einsum('bhqk,bhkd->bhqd', w, v)


def conformer_branch(pa, p):
    u = pa[..., :CONV_CH] * jax.nn.sigmoid(pa[..., CONV_CH:])
    u = lax.conv_general_dilated(
        u, p['conv_w'][:, None, :], window_strides=(1,),
        padding=[(CONV_WIDTH // 2, CONV_WIDTH // 2)],
        dimension_numbers=('NWC', 'WIO', 'NWC'), feature_group_count=CONV_CH) + p['conv_b']
    u = jax.nn.silu(layernorm(u, p['conv_ln_g'], p['conv_ln_b']))
    return u @ p['conv_out']


def mla_qkv(pb, p, cos, sin):
    b, n = pb.shape[:2]
    c_q = rmsnorm(pb[..., :Q_LORA], p['mla_cq_g'])
    c_kv = rmsnorm(pb[..., Q_LORA:Q_LORA + KV_LORA], p['mla_ckv_g'])
    k_pe = pb[..., Q_LORA + KV_LORA:]
    q = (c_q @ p['mla_w_uq']).reshape(b, n, N_HEADS, QK_NOPE + QK_ROPE)
    kv = (c_kv @ p['mla_w_ukv']).reshape(b, n, N_HEADS, QK_NOPE + V_HEAD)
    k = jnp.concatenate(
        [kv[..., :QK_NOPE], jnp.broadcast_to(k_pe[:, :, None, :], (b, n, N_HEADS, QK_ROPE))], axis=-1)
    q = rmsnorm(q, p['mla_qn_g']).transpose(0, 2, 1, 3)
    k = rmsnorm(k, p['mla_kn_g']).transpose(0, 2, 1, 3)
    v = kv[..., QK_NOPE:].transpose(0, 2, 1, 3)
    if cos is not None:
        q = jnp.concatenate([q[..., :QK_NOPE], rope(q[..., QK_NOPE:], cos, sin)], axis=-1)
        k = jnp.concatenate([k[..., :QK_NOPE], rope(k[..., QK_NOPE:], cos, sin)], axis=-1)
    return q, k, v


def diff_qkv(pc, p, cos, sin):
    b, n = pc.shape[:2]
    t = pc.reshape(b, n, N_HEADS, 4 * DIFF_DIM + DIFF_V)
    q = t[..., :2 * DIFF_DIM].reshape(b, n, N_HEADS, 2, DIFF_DIM)
    k = t[..., 2 * DIFF_DIM:4 * DIFF_DIM].reshape(b, n, N_HEADS, 2, DIFF_DIM)
    v = t[..., 4 * DIFF_DIM:].transpose(0, 2, 1, 3)
    q = rmsnorm(q, p['diff_qn_g']).transpose(0, 2, 1, 3, 4)
    k = rmsnorm(k, p['diff_kn_g']).transpose(0, 2, 1, 3, 4)
    if cos is not None:
        q = rope(q, cos[:, None], sin[:, None])
        k = rope(k, cos[:, None], sin[:, None])
    return q, k, v


def diff_project(o, p, lam_init):
    o = rmsnorm(o, p['diff_subln_g']) * (1.0 - lam_init)
    return merge_heads(o) @ p['diff_out']


def na_qkv(pd, p):
    b, n = pd.shape[:2]
    t = pd.reshape(b, n, N_HEADS, 3 * HEAD_DIM)
    q = rmsnorm(t[..., :HEAD_DIM], p['na_qn_g']).transpose(0, 2, 1, 3)
    k = rmsnorm(t[..., HEAD_DIM:2 * HEAD_DIM], p['na_kn_g']).transpose(0, 2, 1, 3)
    v = t[..., 2 * HEAD_DIM:].transpose(0, 2, 1, 3)
    return q, k, v


def na_index_tables(rows):
    kh = min(NA_KH, rows)
    ncb = GRID_W // NA_QB
    band_w = NA_QB + NA_KW
    r = np.arange(rows)
    row_start = np.clip(r - kh // 2, 0, rows - kh)
    key_rows = row_start[:, None] + np.arange(kh)
    row_off = key_rows - r[:, None] + NA_KH - 1
    cb = np.arange(ncb)
    col_start = np.clip(cb * NA_QB - NA_KW // 2, 0, GRID_W - band_w)
    key_cols = col_start[:, None] + np.arange(band_w)
    q_cols = cb[:, None] * NA_QB + np.arange(NA_QB)
    win_start = np.clip(q_cols - NA_KW // 2, 0, GRID_W - NA_KW)
    d = key_cols[:, None, :] - win_start[:, :, None]
    col_valid = (d >= 0) & (d < NA_KW)
    col_off = np.clip(key_cols[:, None, :] - q_cols[:, :, None] + NA_KW - 1, 0, 2 * NA_KW - 2)
    return key_rows.astype(np.int32), row_off, key_cols, col_off, col_valid


def neighbourhood_attention(q, k, v, kc, vc, rpb):
    b, h, n, d = q.shape
    rows = n // GRID_W
    key_rows, row_off, key_cols, col_off, col_valid = na_index_tables(rows)
    kh = key_rows.shape[1]
    ncb, band_w = key_cols.shape
    bias = rpb.astype(F32)[:, row_off[:, None, None, :, None], col_off[None, :, :, None, :]]
    bias = jnp.where(col_valid[None, None, :, :, None, :], bias, NEG_INF)
    bias = jnp.moveaxis(bias.reshape(h, rows, ncb, NA_QB, kh * band_w), 1, 0)
    scale = d ** -0.5
    kg = k.reshape(b, h, rows, GRID_W, d)
    vg = v.reshape(b, h, rows, GRID_W, v.shape[-1])
    qr = jnp.moveaxis(q.reshape(b, h, rows, ncb, NA_QB, d), 2, 0)
    n_ctx = kc.shape[2]

    def gather_band(t, rows_r):
        t = jnp.take(t, rows_r, axis=2)[:, :, :, key_cols]
        return t.transpose(0, 1, 3, 2, 4, 5).reshape(b, h, ncb, kh * band_w, t.shape[-1])

    def row_block(args):
        q_r, rows_r, bias_r = args
        k_band = gather_band(kg, rows_r)
        v_band = gather_band(vg, rows_r)
        s_loc = jnp.einsum('bhnqd,bhnkd->bhnqk', q_r, k_band).astype(F32) * scale + bias_r
        s_ctx = jnp.einsum('bhnqd,bhcd->bhnqc', q_r, kc).astype(F32) * scale
        pr = jax.nn.softmax(jnp.concatenate([s_ctx, s_loc], axis=-1), axis=-1).astype(v.dtype)
        return (jnp.einsum('bhnqc,bhcd->bhnqd', pr[..., :n_ctx], vc)
                + jnp.einsum('bhnqk,bhnkd->bhnqd', pr[..., n_ctx:], v_band))

    out = lax.map(row_block, (qr, jnp.asarray(key_rows), bias))
    return jnp.moveaxis(out, 0, 2).reshape(b, h, n, out.shape[-1])


def gated_merge(h, y_conv, y_mla, y_diff, y_na, p):
    g = jax.nn.sigmoid((h @ p['gate_w'] + p['gate_b']).astype(F32)).astype(h.dtype)
    g = g.reshape(*h.shape[:-1], N_BRANCH, D_MODEL)
    y = (g[..., 0, :] * y_conv + g[..., 1, :] * y_mla
         + g[..., 2, :] * y_diff + g[..., 3, :] * y_na)
    return y @ p['w_o']


def token_mixer(h, hc, p, lam_init, cos, sin, with_ctx_out):
    proj = h @ p['w_in']
    projc = hc @ p['w_in']
    lam_vec = p['diff_lam'].astype(F32)
    lam = jnp.exp(jnp.sum(lam_vec[0] * lam_vec[1])) - jnp.exp(jnp.sum(lam_vec[2] * lam_vec[3])) + lam_init

    y_conv = conformer_branch(proj[..., :OFF_B], p)

    q_b, k_b, v_b = mla_qkv(proj[..., OFF_B:OFF_C], p, cos, sin)
    qc_b, kc_b, vc_b = mla_qkv(projc[..., OFF_B:OFF_C], p, None, None)
    y_mla = merge_heads(dense_attention(
        q_b, jnp.concatenate([kc_b, k_b], axis=2), jnp.concatenate([vc_b, v_b], axis=2))) @ p['mla_out']

    q_c, k_c, v_c = diff_qkv(proj[..., OFF_C:OFF_D], p, cos, sin)
    qc_c, kc_c, vc_c = diff_qkv(projc[..., OFF_C:OFF_D], p, None, None)
    k_all = jnp.concatenate([kc_c, k_c], axis=2)
    v_all = jnp.concatenate([vc_c, v_c], axis=2)
    y_diff = diff_project(sweep_query_blocks(lambda qb: diff_attend(qb, k_all, v_all, lam), q_c), p, lam_init)

    q_d, k_d, v_d = na_qkv(proj[..., OFF_D:], p)
    qc_d, kc_d, vc_d = na_qkv(projc[..., OFF_D:], p)
    y_na = merge_heads(neighbourhood_attention(q_d, k_d, v_d, kc_d, vc_d, p['na_rpb'])) @ p['na_out']

    y = gated_merge(h, y_conv, y_mla, y_diff, y_na, p)
    if not with_ctx_out:
        return y, None
    yc_conv = conformer_branch(projc[..., :OFF_B], p)
    yc_mla = merge_heads(dense_attention(qc_b, kc_b, vc_b)) @ p['mla_out']
    yc_diff = diff_project(sweep_query_blocks(lambda qb: diff_attend(qb, kc_c, vc_c, lam), qc_c), p, lam_init)
    yc_na = merge_heads(dense_attention(qc_d, kc_d, vc_d)) @ p['na_out']
    return y, gated_merge(hc, yc_conv, yc_mla, yc_diff, yc_na, p)


def moe_ffn(h, p):
    shp = h.shape
    t = h.reshape(-1, shp[-1])
    logits = (t @ p['router_w'] + p['router_b']).astype(F32)
    top_val, top_idx = lax.top_k(logits, TOP_K)
    w = jax.nn.softmax(top_val, axis=-1)
    gates = jnp.einsum('nk,nke->ne', w, jax.nn.one_hot(top_idx, N_EXPERTS, dtype=F32)).astype(h.dtype)
    out = jnp.zeros_like(t)
    for e in range(N_EXPERTS):
        gu = t @ p['exp_w_gu'][e] + p['exp_b_gu'][e]
        g = jnp.minimum(gu[:, :D_FF], SWIGLU_LIMIT)
        u = jnp.clip(gu[:, D_FF:], -SWIGLU_LIMIT, SWIGLU_LIMIT)
        y = ((u + 1.0) * (g * jax.nn.sigmoid(SWIGLU_ALPHA * g))) @ p['exp_w_down'][e] + p['exp_b_down'][e]
        out = out + gates[:, e:e + 1] * y
    return out.reshape(shp)


def modulate(x, g, shift, scale):
    return rmsnorm(x, g) * (1.0 + scale) + shift


def setup_inputs(seed: int = 0) -> dict:
    key = jax.random.key(seed)
    ks = iter(jax.random.split(key, 64))
    L, D = DEPTH, D_MODEL

    def nrm(shape, scale):
        return jax.random.normal(next(ks), shape, F32) * scale

    def gain(shape):
        return 1.0 + nrm(shape, 0.02)

    return {
        'x': nrm((BATCH, SEQ, D), 1.0),
        'c': nrm((BATCH, D), 1.0),
        'ctx': nrm((BATCH, CTX_LEN, D), 1.0),
        'c_ctx': nrm((D,), 1.0),
        'ada_w': nrm((L, D, 6 * D), 0.5 * D ** -0.5),
        'ada_b': nrm((L, 6 * D), 0.02),
        'norm1_g': gain((L, D)),
        'norm2_g': gain((L, D)),
        'w_in': nrm((L, D, IN_WIDTH), D ** -0.5),
        'conv_w': nrm((L, CONV_WIDTH, CONV_CH), CONV_WIDTH ** -0.5),
        'conv_b': nrm((L, CONV_CH), 0.02),
        'conv_ln_g': gain((L, CONV_CH)),
        'conv_ln_b': nrm((L, CONV_CH), 0.02),
        'conv_out': nrm((L, CONV_CH, D), CONV_CH ** -0.5),
        'mla_cq_g': gain((L, Q_LORA)),
        'mla_ckv_g': gain((L, KV_LORA)),
        'mla_w_uq': nrm((L, Q_LORA, N_HEADS * (QK_NOPE + QK_ROPE)), Q_LORA ** -0.5),
        'mla_w_ukv': nrm((L, KV_LORA, N_HEADS * (QK_NOPE + V_HEAD)), KV_LORA ** -0.5),
        'mla_qn_g': gain((L, QK_NOPE + QK_ROPE)),
        'mla_kn_g': gain((L, QK_NOPE + QK_ROPE)),
        'mla_out': nrm((L, N_HEADS * V_HEAD, D), (N_HEADS * V_HEAD) ** -0.5),
        'diff_qn_g': gain((L, DIFF_DIM)),
        'diff_kn_g': gain((L, DIFF_DIM)),
        'diff_lam': nrm((L, 4, DIFF_DIM), 0.1),
        'diff_subln_g': gain((L, DIFF_V)),
        'diff_out': nrm((L, N_HEADS * DIFF_V, D), (N_HEADS * DIFF_V) ** -0.5),
        'na_qn_g': gain((L, HEAD_DIM)),
        'na_kn_g': gain((L, HEAD_DIM)),
        'na_rpb': nrm((L, N_HEADS, 2 * NA_KH - 1, 2 * NA_KW - 1), 0.1),
        'na_out': nrm((L, N_HEADS * HEAD_DIM, D), (N_HEADS * HEAD_DIM) ** -0.5),
        'gate_w': nrm((L, D, N_BRANCH * D), D ** -0.5),
        'gate_b': nrm((L, N_BRANCH * D), 0.02),
        'w_o': nrm((L, D, D), D ** -0.5),
        'router_w': nrm((L, D, N_EXPERTS), D ** -0.5),
        'router_b': nrm((L, N_EXPERTS), 0.01),
        'exp_w_gu': nrm((L, N_EXPERTS, D, 2 * D_FF), D ** -0.5),
        'exp_b_gu': nrm((L, N_EXPERTS, 2 * D_FF), 0.02),
        'exp_w_down': nrm((L, N_EXPERTS, D_FF, D), D_FF ** -0.5),
        'exp_b_down': nrm((L, N_EXPERTS, D), 0.02),
    }


def reference(x, c, ctx, c_ctx, ada_w, ada_b, norm1_g, norm2_g, w_in, conv_w, conv_b, conv_ln_g,
              conv_ln_b, conv_out, mla_cq_g, mla_ckv_g, mla_w_uq, mla_w_ukv, mla_qn_g, mla_kn_g,
              mla_out, diff_qn_g, diff_kn_g, diff_lam, diff_subln_g, diff_out, na_qn_g, na_kn_g,
              na_rpb, na_out, gate_w, gate_b, w_o, router_w, router_b, exp_w_gu, exp_b_gu,
              exp_w_down, exp_b_down):
    cos, sin = axial_rope_tables(x.shape[1], x.dtype)
    xc = ctx
    for l in range(DEPTH):
        p = dict(
            w_in=w_in[l], conv_w=conv_w[l], conv_b=conv_b[l], conv_ln_g=conv_ln_g[l],
            conv_ln_b=conv_ln_b[l], conv_out=conv_out[l], mla_cq_g=mla_cq_g[l],
            mla_ckv_g=mla_ckv_g[l], mla_w_uq=mla_w_uq[l], mla_w_ukv=mla_w_ukv[l],
            mla_qn_g=mla_qn_g[l], mla_kn_g=mla_kn_g[l], mla_out=mla_out[l],
            diff_qn_g=diff_qn_g[l], diff_kn_g=diff_kn_g[l], diff_lam=diff_lam[l],
            diff_subln_g=diff_subln_g[l], diff_out=diff_out[l], na_qn_g=na_qn_g[l],
            na_kn_g=na_kn_g[l], na_rpb=na_rpb[l], na_out=na_out[l], gate_w=gate_w[l],
            gate_b=gate_b[l], w_o=w_o[l], router_w=router_w[l], router_b=router_b[l],
            exp_w_gu=exp_w_gu[l], exp_b_gu=exp_b_gu[l], exp_w_down=exp_w_down[l],
            exp_b_down=exp_b_down[l])
        last = l == DEPTH - 1
        lam_init = 0.8 - 0.6 * math.exp(-0.3 * l)
        sh1, sc1, g1, sh2, sc2, g2 = jnp.split(jax.nn.silu(c) @ ada_w[l] + ada_b[l], 6, axis=-1)
        shc1, scc1, gc1, shc2, scc2, gc2 = jnp.split(jax.nn.silu(c_ctx) @ ada_w[l] + ada_b[l], 6, axis=-1)
        h = modulate(x, norm1_g[l], sh1[:, None], sc1[:, None])
        hc = modulate(xc, norm1_g[l], shc1, scc1)
        y, yc = token_mixer(h, hc, p, lam_init, cos, sin, not last)
        x = x + g1[:, None] * y
        x = x + g2[:, None] * moe_ffn(modulate(x, norm2_g[l], sh2[:, None], sc2[:, None]), p)
        if not last:
            xc = xc + gc1 * yc
            xc = xc + gc2 * moe_ffn(modulate(xc, norm2_g[l], shc2, scc2), p)
    return x
```

```python
import functools
import math

import numpy as np
import jax
import jax.numpy as jnp
from jax import lax
from jax.experimental import pallas as pl
from jax.experimental.pallas import tpu as pltpu

F32 = jnp.float32
BF16 = jnp.bfloat16

D_MODEL = 1024
GRID_W = 64
N_BRANCH = 4
N_HEADS = 4
HEAD_DIM = 64
CONV_CH = 256
CONV_WIDTH = 31
Q_LORA = 192
KV_LORA = 128
QK_NOPE = 64
QK_ROPE = 32
V_HEAD = 64
DIFF_DIM = 32
DIFF_V = 2 * DIFF_DIM
NA_KH = 8
NA_KW = 16
ROPE_DIM = 32
ROPE_BASE = 10000.0
N_EXPERTS = 32
TOP_K = 4
D_FF = 1024
SWIGLU_LIMIT = 7.0
SWIGLU_ALPHA = 1.702
EPS = 1e-6
NEG_INF = -1e30

A_IN = 2 * CONV_CH
B_IN = Q_LORA + KV_LORA + QK_ROPE
C_IN = N_HEADS * (4 * DIFF_DIM + DIFF_V)
D_IN = N_HEADS * 3 * HEAD_DIM
OFF_B = A_IN
OFF_C = OFF_B + B_IN
OFF_D = OFF_C + C_IN

LANE = 128
MLA_QK = QK_NOPE + QK_ROPE
BR_W = N_HEADS * HEAD_DIM
PROJ_W = 2560
NA_BAND_ROWS = 10
VMEM_LIMIT = 52 * 1024 * 1024

P_A, P_G, P_CQ, P_CKV, P_KPE = 0, 256, 512, 768, 896
P_DQ, P_DK, P_DV = 1024, 1280, 1536
P_NQ, P_NK, P_NV = 1792, 2048, 2304


def _sigmoid(x):
    return 1.0 / (1.0 + jnp.exp(-x))


def _modulate(x, g, shift, scale):
    ms = jnp.mean(x * x, axis=-1, keepdims=True)
    return (x * lax.rsqrt(ms + EPS) * g) * (1.0 + scale) + shift


def _cparams(sem):
    return pltpu.CompilerParams(dimension_semantics=sem, vmem_limit_bytes=VMEM_LIMIT)


def _const_spec(shape):
    n = len(shape)
    return pl.BlockSpec(shape, lambda *_: (0,) * n)


def _proj_perm():
    perm = -np.ones((PROJ_W,), np.int64)
    perm[P_A:P_A + A_IN] = np.arange(A_IN)
    perm[P_CQ:P_CQ + Q_LORA] = OFF_B + np.arange(Q_LORA)
    perm[P_CKV:P_CKV + KV_LORA] = OFF_B + Q_LORA + np.arange(KV_LORA)
    perm[P_KPE:P_KPE + QK_ROPE] = OFF_B + Q_LORA + KV_LORA + np.arange(QK_ROPE)
    per_head_c = 4 * DIFF_DIM + DIFF_V
    for h in range(N_HEADS):
        base = OFF_C + h * per_head_c
        perm[P_DQ + 64 * h:P_DQ + 64 * h + 64] = base + np.arange(64)
        perm[P_DK + 64 * h:P_DK + 64 * h + 64] = base + 64 + np.arange(64)
        perm[P_DV + 64 * h:P_DV + 64 * h + 64] = base + 128 + np.arange(64)
        base = OFF_D + h * 3 * HEAD_DIM
        perm[P_NQ + 64 * h:P_NQ + 64 * h + 64] = base + np.arange(64)
        perm[P_NK + 64 * h:P_NK + 64 * h + 64] = base + 64 + np.arange(64)
        perm[P_NV + 64 * h:P_NV + 64 * h + 64] = base + 128 + np.arange(64)
    return perm


def _group_ones(width, slot, real):
    i = np.arange(width)
    valid = (i % slot) < real
    same = (i[:, None] // slot) == (i[None, :] // slot)
    return (same & valid[:, None] & valid[None, :]).astype(np.float32)


def _rot_matrix(width, slot, start, half):
    r = np.zeros((width, width), np.float32)
    for s0 in range(0, width, slot):
        for i in range(half):
            a, b = s0 + start + i, s0 + start + half + i
            r[b, a] = -1.0
            r[a, b] = 1.0
    return r


def _rope_lane_tables(n_tokens):
    t = jnp.arange(n_tokens, dtype=jnp.int32)
    rows = (t // GRID_W).astype(F32)
    cols = (t % GRID_W).astype(F32)
    axis_dim = ROPE_DIM // 2
    inv = ROPE_BASE ** (-jnp.arange(0, axis_dim, 2, dtype=F32) / axis_dim)
    theta = jnp.concatenate([rows[:, None] * inv, cols[:, None] * inv], axis=-1)
    cos, sin = jnp.cos(theta), jnp.sin(theta)
    half = ROPE_DIM // 2
    ones = jnp.ones((n_tokens, QK_NOPE), F32)
    zeros = jnp.zeros((n_tokens, QK_NOPE), F32)
    pad1 = jnp.ones((n_tokens, LANE - MLA_QK), F32)
    pad0 = jnp.zeros((n_tokens, LANE - MLA_QK), F32)
    cm = jnp.tile(jnp.concatenate([ones, cos, cos, pad1], -1), (1, N_HEADS))
    sm = jnp.tile(jnp.concatenate([zeros, sin, sin, pad0], -1), (1, N_HEADS))
    cd = jnp.tile(jnp.concatenate([cos, cos], -1), (1, 2 * N_HEADS))
    sd = jnp.tile(jnp.concatenate([sin, sin], -1), (1, 2 * N_HEADS))
    assert half * 2 == DIFF_DIM
    return cm, sm, cd, sd


def _na_bias_tables(rpb, rows):
    kh = min(NA_KH, rows)
    nj = rows // 2
    reps = [0, 1, 2, nj - 2, nj - 1]
    qi = np.arange(2 * GRID_W)
    ki = np.arange(NA_BAND_ROWS * GRID_W)
    ro_all, co_all, valid_all = [], [], []
    for j in reps:
        start = int(np.clip(2 * j - 4, 0, rows - NA_BAND_ROWS))
        r = 2 * j + qi // GRID_W
        qc = qi % GRID_W
        kr = start + ki // GRID_W
        kc = ki % GRID_W
        row_start = np.clip(r - kh // 2, 0, rows - kh)
        win_start = np.clip(qc - NA_KW // 2, 0, GRID_W - NA_KW)
        vr = (kr[None, :] >= row_start[:, None]) & (kr[None, :] < row_start[:, None] + kh)
        vc = (kc[None, :] >= win_start[:, None]) & (kc[None, :] < win_start[:, None] + NA_KW)
        ro = np.clip(kr[None, :] - r[:, None] + NA_KH - 1, 0, 2 * NA_KH - 2)
        co = np.clip(kc[None, :] - qc[:, None] + NA_KW - 1, 0, 2 * NA_KW - 2)
        ro_all.append(ro)
        co_all.append(co)
        valid_all.append(vr & vc)
    ro = np.stack(ro_all)
    co = np.stack(co_all)
    valid = np.stack(valid_all)
    b = rpb.astype(F32)[:, ro, co]
    b = jnp.where(valid[None], b, NEG_INF)
    return jnp.transpose(b, (1, 0, 2, 3))


def _ada_kernel(c_ref, w_ref, b_ref, o_ref):
    c = c_ref[...]
    s = c * _sigmoid(c)
    o_ref[0] = jnp.dot(s, w_ref[0], preferred_element_type=F32,
                       precision=lax.Precision.HIGHEST) + b_ref[0]


def _ada_call(cs, ada_w, ada_b):
    depth, d, n = ada_w.shape
    rows = cs.shape[0]
    tn = 1536
    return pl.pallas_call(
        _ada_kernel,
        grid=(depth, n // tn),
        in_specs=[
            pl.BlockSpec((rows, d), lambda l, j: (0, 0)),
            pl.BlockSpec((1, d, tn), lambda l, j: (l, 0, j)),
            pl.BlockSpec((1, 1, tn), lambda l, j: (l, 0, j)),
        ],
        out_specs=pl.BlockSpec((1, rows, tn), lambda l, j: (l, 0, j)),
        out_shape=jax.ShapeDtypeStruct((depth, rows, n), F32),
        compiler_params=_cparams(("arbitrary", "arbitrary")),
        name="ada",
    )(cs, ada_w, ada_b.reshape(depth, 1, n))


def _group_norm(x, ones_ref, inv_n):
    sq = (x * x).astype(BF16)
    ms = jnp.dot(sq, ones_ref[...], preferred_element_type=F32) * inv_n
    return x * lax.rsqrt(ms + EPS)


def _rope(x, rot_ref, cos_ref, sin_ref):
    rot = jnp.dot(x.astype(BF16), rot_ref[...], preferred_element_type=F32)
    return x * cos_ref[...] + rot * sin_ref[...]


def _proj_kernel(x_ref, mod_ref, n1g_ref, win_ref, gains_ref, wuq_ref, wk_ref, ppe_ref, wv_ref,
                 g96_ref, g32_ref, g64_ref, rm_ref, rd_ref, cm_ref, sm_ref, cd_ref, sd_ref,
                 u_ref, mq_ref, mk_ref, mv_ref, dq_ref, dk_ref, dv_ref, nq_ref, nk_ref, nv_ref,
                 *, use_rope):
    x = x_ref[0]
    mod = mod_ref[0]
    gains = gains_ref[...]
    h = _modulate(x, n1g_ref[...], mod[0:1], mod[1:2]).astype(BF16)
    proj = jnp.dot(h, win_ref[...], preferred_element_type=F32)

    u_ref[0] = proj[:, P_A:P_A + CONV_CH] * _sigmoid(proj[:, P_G:P_G + CONV_CH])

    cq = proj[:, P_CQ:P_CQ + 256]
    ms = jnp.sum(cq * cq, axis=-1, keepdims=True) * (1.0 / Q_LORA)
    cqn = (cq * lax.rsqrt(ms + EPS) * gains[0:1, :256]).astype(BF16)
    q = jnp.dot(cqn, wuq_ref[...], preferred_element_type=F32)
    q = _group_norm(q, g96_ref, 1.0 / MLA_QK) * gains[1:2, :]
    if use_rope:
        q = _rope(q, rm_ref, cm_ref, sm_ref)
    mq_ref[0] = q.astype(BF16)

    ckv = proj[:, P_CKV:P_CKV + KV_LORA]
    ms = jnp.mean(ckv * ckv, axis=-1, keepdims=True)
    ckvn = (ckv * lax.rsqrt(ms + EPS) * gains[2:3, :KV_LORA]).astype(BF16)
    kpe = proj[:, P_KPE:P_KPE + LANE].astype(BF16)
    k = (jnp.dot(ckvn, wk_ref[...], preferred_element_type=F32)
         + jnp.dot(kpe, ppe_ref[...], preferred_element_type=F32))
    k = _group_norm(k, g96_ref, 1.0 / MLA_QK) * gains[3:4, :]
    if use_rope:
        k = _rope(k, rm_ref, cm_ref, sm_ref)
    mk_ref[0] = k.astype(BF16)
    mv_ref[0] = jnp.dot(ckvn, wv_ref[...], preferred_element_type=F32).astype(BF16)

    qd = _group_norm(proj[:, P_DQ:P_DQ + BR_W], g32_ref, 1.0 / DIFF_DIM) * gains[4:5, :BR_W]
    kd = _group_norm(proj[:, P_DK:P_DK + BR_W], g32_ref, 1.0 / DIFF_DIM) * gains[5:6, :BR_W]
    if use_rope:
        qd = _rope(qd, rd_ref, cd_ref, sd_ref)
        kd = _rope(kd, rd_ref, cd_ref, sd_ref)
    dq_ref[0] = qd.astype(BF16)
    dk_ref[0] = kd.astype(BF16)
    dv_ref[0] = proj[:, P_DV:P_DV + BR_W].astype(BF16)

    qn = _group_norm(proj[:, P_NQ:P_NQ + BR_W], g64_ref, 1.0 / HEAD_DIM) * gains[6:7, :BR_W]
    kn = _group_norm(proj[:, P_NK:P_NK + BR_W], g64_ref, 1.0 / HEAD_DIM) * gains[7:8, :BR_W]
    nq_ref[0] = qn.astype(BF16)
    nk_ref[0] = kn.astype(BF16)
    nv_ref[0] = proj[:, P_NV:P_NV + BR_W].astype(BF16)


def _proj_call(x, mods, mod_row, lw, tabs, use_rope):
    b, s, d = x.shape
    t = min(s, 512)
    grid = (b, s // t)
    if mod_row is None:
        mod_map = lambda i, j: (i, 0, 0)
    else:
        mod_map = lambda i, j: (mod_row, 0, 0)
    tok = lambda w: pl.BlockSpec((1, t, w), lambda i, j: (i, j, 0))
    tab = lambda w: pl.BlockSpec((t, w), lambda i, j: (j, 0))
    in_specs = [
        tok(d),
        pl.BlockSpec((1, 6, d), mod_map),
        _const_spec((1, d)),
        _const_spec((d, PROJ_W)),
        _const_spec((8, 512)),
        _const_spec((256, 512)),
        _const_spec((KV_LORA, 512)),
        _const_spec((LANE, 512)),
        _const_spec((KV_LORA, BR_W)),
        _const_spec((512, 512)),
        _const_spec((BR_W, BR_W)),
        _const_spec((BR_W, BR_W)),
        _const_spec((512, 512)),
        _const_spec((BR_W, BR_W)),
        tab(512), tab(512), tab(BR_W), tab(BR_W),
    ]
    widths = [CONV_CH, 512, 512, BR_W, BR_W, BR_W, BR_W, BR_W, BR_W, BR_W]
    dtypes = [F32] + [BF16] * 9
    out_specs = [tok(w) for w in widths]
    out_shape = [jax.ShapeDtypeStruct((b, s, w), dt) for w, dt in zip(widths, dtypes)]
    return pl.pallas_call(
        functools.partial(_proj_kernel, use_rope=use_rope),
        grid=grid, in_specs=in_specs, out_specs=out_specs, out_shape=out_shape,
        compiler_params=_cparams(("parallel", "parallel")),
        name="proj",
    )(x, mods, lw["n1g"], lw["w_in"], lw["gains"], lw["wuq"], lw["wk"], lw["ppe"], lw["wv"],
      lw["g96"], lw["g32"], lw["g64"], lw["rm"], lw["rd"], tabs[0], tabs[1], tabs[2], tabs[3])


CONV_TILE = 128
CONV_PAD = 16


def _conv_kernel(u_ref, w_ref, cb_ref, lg_ref, lb_ref, o_ref, pad_ref, *, seq):
    zeros = jnp.zeros((CONV_PAD, CONV_CH), F32)
    pad_ref[0:CONV_PAD, :] = zeros
    pad_ref[CONV_PAD + seq:2 * CONV_PAD + seq, :] = zeros

    def fill(i, carry):
        base = pl.multiple_of(i * CONV_TILE, CONV_TILE)
        pad_ref[pl.ds(base + CONV_PAD, CONV_TILE), :] = u_ref[0, pl.ds(base, CONV_TILE), :]
        return carry

    lax.fori_loop(0, seq // CONV_TILE, fill, 0)
    w = w_ref[...]
    cb, lg, lb = cb_ref[...], lg_ref[...], lb_ref[...]

    def tile(i, carry):
        base = pl.multiple_of(i * CONV_TILE, CONV_TILE)
        win = pad_ref[pl.ds(base, CONV_TILE + 2 * CONV_PAD), :]
        acc = jnp.zeros((CONV_TILE, CONV_CH), F32)
        for j in range(CONV_WIDTH):
            acc = acc + win[j + 1:j + 1 + CONV_TILE, :] * w[j:j + 1, :]
        c = acc + cb
        mu = jnp.mean(c, axis=-1, keepdims=True)
        cc = c - mu
        var = jnp.mean(cc * cc, axis=-1, keepdims=True)
        y = cc * lax.rsqrt(var + EPS) * lg + lb
        o_ref[0, pl.ds(base, CONV_TILE), :] = (y * _sigmoid(y)).astype(BF16)
        return carry

    lax.fori_loop(0, seq // CONV_TILE, tile, 0)


def _conv_call(u, lw):
    b, s, ch = u.shape
    return pl.pallas_call(
        functools.partial(_conv_kernel, seq=s),
        grid=(b,),
        in_specs=[
            pl.BlockSpec((1, s, ch), lambda i: (i, 0, 0)),
            _const_spec((32, ch)), _const_spec((1, ch)), _const_spec((1, ch)), _const_spec((1, ch)),
        ],
        out_specs=pl.BlockSpec((1, s, ch), lambda i: (i, 0, 0)),
        out_shape=jax.ShapeDtypeStruct((b, s, ch), BF16),
        scratch_shapes=[pltpu.VMEM((s + 2 * CONV_PAD, ch), F32)],
        compiler_params=_cparams(("parallel",)),
        name="conv",
    )(u, lw["conv_w"], lw["conv_b"], lw["conv_ln_g"], lw["conv_ln_b"])


def _lane_mask(width, lo, hi):
    lane = lax.broadcasted_iota(jnp.int32, (1, width), 1)
    return (lane >= lo) & (lane < hi)


def _softmax_pv(qw, kt, v):
    s = jnp.dot(qw, kt, preferred_element_type=F32)
    m = jnp.max(s, axis=-1, keepdims=True)
    p = jnp.exp(s - m)
    l = jnp.sum(p, axis=-1, keepdims=True)
    return jnp.dot(p.astype(BF16), v, preferred_element_type=F32) * (1.0 / l)


def _attn_kernel(q_ref, kt_ref, v_ref, lam_ref, g64_ref, sg_ref, o_ref, *, maps, diff, lam_init):
    v = v_ref[0]
    tq = q_ref.shape[1]
    acc = jnp.zeros((tq, BR_W), F32)
    if diff:
        lv = lam_ref[...]
        lam = (jnp.exp(jnp.sum(lv[0:1] * lv[1:2], axis=-1, keepdims=True))
               - jnp.exp(jnp.sum(lv[2:3] * lv[3:4], axis=-1, keepdims=True)) + lam_init)
    for h in range(N_HEADS):
        outs = []
        for (w0, lo, hi) in maps[h]:
            qw = q_ref[0, :, w0:w0 + LANE]
            if (lo, hi) != (0, LANE):
                qw = jnp.where(_lane_mask(LANE, lo, hi), qw, jnp.zeros_like(qw))
            outs.append(_softmax_pv(qw, kt_ref[0, w0:w0 + LANE, :], v))
        o_h = outs[0] - lam * outs[1] if diff else outs[0]
        acc = jnp.where(_lane_mask(BR_W, V_HEAD * h, V_HEAD * (h + 1)), o_h, acc)
    if diff:
        acc = _group_norm(acc, g64_ref, 1.0 / DIFF_V) * sg_ref[...]
    o_ref[0] = acc.astype(BF16)


MAPS_MLA = tuple(((LANE * h, 0, LANE),) for h in range(N_HEADS))
MAPS_DIFF = tuple(tuple((LANE * (h // 2), 64 * (h % 2) + 32 * c, 64 * (h % 2) + 32 * c + 32) for c in range(2))
                  for h in range(N_HEADS))
MAPS_NA = tuple(((LANE * (h // 2), 64 * (h % 2), 64 * (h % 2) + 64),) for h in range(N_HEADS))


def _attn_call(q, kt, v, lw, maps, diff=False, lam_init=0.0):
    b, s, wq = q.shape
    sk = kt.shape[2]
    tq = min(s, 256)
    return pl.pallas_call(
        functools.partial(_attn_kernel, maps=maps, diff=diff, lam_init=lam_init),
        grid=(b, s // tq),
        in_specs=[
            pl.BlockSpec((1, tq, wq), lambda i, j: (i, j, 0)),
            pl.BlockSpec((1, wq, sk), lambda i, j: (i, 0, 0)),
            pl.BlockSpec((1, sk, BR_W), lambda i, j: (i, 0, 0)),
            _const_spec((4, DIFF_DIM)),
            _const_spec((BR_W, BR_W)),
            _const_spec((1, BR_W)),
        ],
        out_specs=pl.BlockSpec((1, tq, BR_W), lambda i, j: (i, j, 0)),
        out_shape=jax.ShapeDtypeStruct((b, s, BR_W), BF16),
        compiler_params=_cparams(("parallel", "parallel")),
        name="attn_diff" if diff else "attn",
    )(q, kt, v, lw["diff_lam"], lw["g64"], lw["subln"])


_NT = (((1,), (1,)), ((), ()))


def _na_kernel(q_ref, k_ref, v_ref, kc_ref, vc_ref, bias_ref, o_ref, *, rows):
    j = pl.program_id(1)
    nj = rows // 2
    start = jnp.clip(2 * j - 4, 0, rows - NA_BAND_ROWS)
    base = pl.multiple_of(start * GRID_W, 2 * GRID_W)
    cls = jnp.where(j < 2, j, jnp.where(j >= nj - 2, j - (nj - 2) + 3, 2))
    band = NA_BAND_ROWS * GRID_W
    kw = k_ref[0, pl.ds(base, band), :]
    vw = v_ref[0, pl.ds(base, band), :]
    kc = kc_ref[0]
    vc = vc_ref[0]
    q = q_ref[0]
    acc = jnp.zeros((2 * GRID_W, BR_W), F32)
    for h in range(N_HEADS):
        qm = jnp.where(_lane_mask(BR_W, HEAD_DIM * h, HEAD_DIM * (h + 1)), q, jnp.zeros_like(q))
        s_loc = lax.dot_general(qm, kw, _NT, preferred_element_type=F32) + bias_ref[cls, h]
        s_ctx = lax.dot_general(qm, kc, _NT, preferred_element_type=F32)
        m = jnp.maximum(jnp.max(s_loc, axis=-1, keepdims=True), jnp.max(s_ctx, axis=-1, keepdims=True))
        p_loc = jnp.exp(s_loc - m)
        p_ctx = jnp.exp(s_ctx - m)
        l = jnp.sum(p_loc, axis=-1, keepdims=True) + jnp.sum(p_ctx, axis=-1, keepdims=True)
        o = (jnp.dot(p_ctx.astype(BF16), vc, preferred_element_type=F32)
             + jnp.dot(p_loc.astype(BF16), vw, preferred_element_type=F32)) * (1.0 / l)
        acc = jnp.where(_lane_mask(BR_W, HEAD_DIM * h, HEAD_DIM * (h + 1)), o, acc)
    o_ref[0] = acc.astype(BF16)


def _na_call(q, k, v, kc, vc, bias):
    b, s, w = q.shape
    n_ctx = kc.shape[1]
    rows = s // GRID_W
    tq = 2 * GRID_W
    return pl.pallas_call(
        functools.partial(_na_kernel, rows=rows),
        grid=(b, rows // 2),
        in_specs=[
            pl.BlockSpec((1, tq, w), lambda i, j: (i, j, 0)),
            pl.BlockSpec((1, s, w), lambda i, j: (i, 0, 0)),
            pl.BlockSpec((1, s, w), lambda i, j: (i, 0, 0)),
            pl.BlockSpec((1, n_ctx, w), lambda i, j: (i, 0, 0)),
            pl.BlockSpec((1, n_ctx, w), lambda i, j: (i, 0, 0)),
            _const_spec(bias.shape),
        ],
        out_specs=pl.BlockSpec((1, tq, w), lambda i, j: (i, j, 0)),
        out_shape=jax.ShapeDtypeStruct((b, s, w), BF16),
        compiler_params=_cparams(("parallel", "arbitrary")),
        name="na",
    )(q, k, v, kc, vc, bias)


def _merge_kernel(x_ref, mod_ref, n1g_ref, n2g_ref, uc_ref, om_ref, od_ref, on_ref,
                  gw_ref, gb_ref, wc_ref, wm_ref, wd_ref, wn_ref, wo_ref, rwt_ref, rb_ref,
                  x1_ref, h2_ref, ids_ref, wts_ref):
    x = x_ref[0]
    mod = mod_ref[0]
    h = _modulate(x, n1g_ref[...], mod[0:1], mod[1:2]).astype(BF16)
    y = jnp.zeros(x.shape, F32)
    branches = ((uc_ref, wc_ref), (om_ref, wm_ref), (od_ref, wd_ref), (on_ref, wn_ref))
    for i, (o_ref, w_ref) in enumerate(branches):
        lo = D_MODEL * i
        g = _sigmoid(jnp.dot(h, gw_ref[:, lo:lo + D_MODEL], preferred_element_type=F32)
                     + gb_ref[:, lo:lo + D_MODEL])
        y = y + g * jnp.dot(o_ref[0], w_ref[...], preferred_element_type=F32)
    out = jnp.dot(y.astype(BF16), wo_ref[...], preferred_element_type=F32)
    x1 = x + mod[2:3] * out
    x1_ref[0] = x1
    h2 = _modulate(x1, n2g_ref[...], mod[3:4], mod[4:5])
    h2_ref[0] = h2

    logits = lax.dot_general(rwt_ref[...], h2, _NT, preferred_element_type=F32,
                             precision=lax.Precision.HIGHEST) + rb_ref[...]
    eidx = lax.broadcasted_iota(jnp.int32, logits.shape, 0).astype(F32)
    vals, idxs = [], []
    cur = logits
    for _ in range(TOP_K):
        m = jnp.max(cur, axis=0, keepdims=True)
        idx = jnp.min(jnp.where(cur == m, eidx, float(N_EXPERTS)), axis=0, keepdims=True)
        vals.append(m)
        idxs.append(idx)
        cur = jnp.where(eidx == idx, -jnp.inf, cur)
    es = [jnp.exp(vk - vals[0]) for vk in vals]
    den = es[0] + es[1] + es[2] + es[3]
    ids_ref[0] = jnp.concatenate(idxs, axis=0).astype(jnp.int32)
    wts_ref[0] = jnp.concatenate([e / den for e in es], axis=0)


def _merge_call(x, mods, mod_row, lw, uc, om, od, on):
    b, s, d = x.shape
    t = min(s, 512)
    if mod_row is None:
        mod_map = lambda i, j: (i, 0, 0)
    else:
        mod_map = lambda i, j: (mod_row, 0, 0)
    tok = lambda w: pl.BlockSpec((1, t, w), lambda i, j: (i, j, 0))
    rt = pl.BlockSpec((1, TOP_K, t), lambda i, j: (i, 0, j))
    return pl.pallas_call(
        _merge_kernel,
        grid=(b, s // t),
        in_specs=[
            tok(d), pl.BlockSpec((1, 6, d), mod_map), _const_spec((1, d)), _const_spec((1, d)),
            tok(BR_W), tok(BR_W), tok(BR_W), tok(BR_W),
            _const_spec((d, N_BRANCH * d)), _const_spec((1, N_BRANCH * d)),
            _const_spec((BR_W, d)), _const_spec((BR_W, d)), _const_spec((BR_W, d)), _const_spec((BR_W, d)),
            _const_spec((d, d)), _const_spec((N_EXPERTS, d)), _const_spec((N_EXPERTS, 1)),
        ],
        out_specs=[tok(d), tok(d), rt, rt],
        out_shape=[jax.ShapeDtypeStruct((b, s, d), F32), jax.ShapeDtypeStruct((b, s, d), F32),
                   jax.ShapeDtypeStruct((b, TOP_K, s), jnp.int32), jax.ShapeDtypeStruct((b, TOP_K, s), F32)],
        compiler_params=_cparams(("parallel", "parallel")),
        name="merge",
    )(x, mods, lw["n1g"], lw["n2g"], uc, om, od, on, lw["gate_w"], lw["gate_b"],
      lw["conv_out"], lw["mla_out"], lw["diff_out"], lw["na_out"], lw["w_o"], lw["router_wt"], lw["router_b"])


def _route(ids, tile):
    n = ids.shape[1]
    p = TOP_K * n
    e = ids.reshape(p)
    onehot = (e[:, None] == jnp.arange(N_EXPERTS, dtype=jnp.int32)[None, :]).astype(jnp.int32)
    csum = jnp.cumsum(onehot, axis=0)
    rank = jnp.take_along_axis(csum, e[:, None], axis=1)[:, 0] - 1
    counts = csum[-1]
    padded = ((counts + tile - 1) // tile) * tile
    gend = jnp.cumsum(padded)
    gstart = gend - padded
    slot = (gstart[e] + rank).astype(jnp.int32)
    n_tiles = p // tile + N_EXPERTS
    tok = jnp.tile(jnp.arange(n, dtype=jnp.int32), TOP_K)
    tok_of_slot = jnp.zeros((n_tiles * tile,), jnp.int32).at[slot].set(tok)
    tile_start = jnp.arange(n_tiles, dtype=jnp.int32) * tile
    texp = jnp.minimum(jnp.searchsorted(gend, tile_start, side="right"), N_EXPERTS - 1).astype(jnp.int32)
    tval = (tile_start < gend[-1]).astype(jnp.int32)
    return slot.reshape(TOP_K, n), tok_of_slot.reshape(n_tiles, 1, tile), texp, tval


def _row_copy(src_hbm, dst, sem, src_row, dst_row):
    return pltpu.make_async_copy(src_hbm.at[pl.ds(src_row, 1)], dst.at[pl.ds(dst_row, 1)], sem)


def _ffn_kernel(texp_ref, tval_ref, tok_ref, h_hbm, wgu_ref, bgu_ref, wd_ref, bd_ref, y_ref, xbuf, sem,
                *, tile):
    i = pl.program_id(0)

    @pl.when(tval_ref[i] > 0)
    def _():
        def issue(r, carry):
            _row_copy(h_hbm, xbuf, sem, tok_ref[0, 0, r], r).start()
            return carry

        lax.fori_loop(0, tile, issue, 0, unroll=8)
        pltpu.make_async_copy(h_hbm.at[pl.ds(0, tile)], xbuf, sem).wait()
        x = xbuf[...].astype(BF16)
        gu = jnp.dot(x, wgu_ref[0], preferred_element_type=F32) + bgu_ref[0]
        g = jnp.minimum(gu[:, :D_FF], SWIGLU_LIMIT)
        u = jnp.clip(gu[:, D_FF:], -SWIGLU_LIMIT, SWIGLU_LIMIT)
        act = ((u + 1.0) * (g * _sigmoid(SWIGLU_ALPHA * g))).astype(BF16)
        y_ref[...] = jnp.dot(act, wd_ref[0], preferred_element_type=F32) + bd_ref[0]

    @pl.when(tval_ref[i] == 0)
    def _():
        y_ref[...] = jnp.zeros(y_ref.shape, F32)


def _ffn_call(h2, tok_of_slot, texp, tval, lw, tile):
    n_tiles = tok_of_slot.shape[0]
    d = h2.shape[1]
    grid_spec = pltpu.PrefetchScalarGridSpec(
        num_scalar_prefetch=2,
        grid=(n_tiles,),
        in_specs=[
            pl.BlockSpec((1, 1, tile), lambda i, te, tv: (i, 0, 0), memory_space=pltpu.SMEM),
            pl.BlockSpec(memory_space=pl.ANY),
            pl.BlockSpec((1, d, 2 * D_FF), lambda i, te, tv: (te[i], 0, 0)),
            pl.BlockSpec((1, 1, 2 * D_FF), lambda i, te, tv: (te[i], 0, 0)),
            pl.BlockSpec((1, D_FF, d), lambda i, te, tv: (te[i], 0, 0)),
            pl.BlockSpec((1, 1, d), lambda i, te, tv: (te[i], 0, 0)),
        ],
        out_specs=pl.BlockSpec((tile, d), lambda i, te, tv: (i, 0)),
        scratch_shapes=[pltpu.VMEM((tile, d), F32), pltpu.SemaphoreType.DMA(())],
    )
    return pl.pallas_call(
        functools.partial(_ffn_kernel, tile=tile),
        grid_spec=grid_spec,
        out_shape=jax.ShapeDtypeStruct((n_tiles * tile, d), F32),
        compiler_params=_cparams(("arbitrary",)),
        name="moe_ffn",
    )(texp, tval, tok_of_slot, h2, lw["exp_w_gu"], lw["exp_b_gu"], lw["exp_w_down"], lw["exp_b_down"])


def _combine_kernel(slot_ref, x1_ref, mod_ref, w_ref, y_hbm, o_ref, ybuf, sem, *, tile):
    for k in range(TOP_K):
        def issue(r, carry, k=k):
            pltpu.make_async_copy(y_hbm.at[pl.ds(slot_ref[0, k, r], 1)], ybuf.at[k, pl.ds(r, 1)], sem).start()
            return carry

        lax.fori_loop(0, tile, issue, 0, unroll=8)
    for k in range(TOP_K):
        pltpu.make_async_copy(y_hbm.at[pl.ds(0, tile)], ybuf.at[k], sem).wait()
    w = w_ref[0]
    acc = w[:, 0:1] * ybuf[0]
    for k in range(1, TOP_K):
        acc = acc + w[:, k:k + 1] * ybuf[k]
    o_ref[0] = x1_ref[0] + mod_ref[0][5:6] * acc


def _combine_call(x1, mods, mod_row, slots, wts, y):
    b, s, d = x1.shape
    t = min(s, 256)
    nt = s // t
    if mod_row is None:
        mod_map = lambda i, j: (i, 0, 0)
    else:
        mod_map = lambda i, j: (mod_row, 0, 0)
    slots = slots.reshape(TOP_K, b * nt, t).transpose(1, 0, 2)
    wts = wts.transpose(0, 2, 1)
    return pl.pallas_call(
        functools.partial(_combine_kernel, tile=t),
        grid=(b, nt),
        in_specs=[
            pl.BlockSpec((1, TOP_K, t), lambda i, j: (i * nt + j, 0, 0), memory_space=pltpu.SMEM),
            pl.BlockSpec((1, t, d), lambda i, j: (i, j, 0)),
            pl.BlockSpec((1, 6, d), mod_map),
            pl.BlockSpec((1, t, TOP_K), lambda i, j: (i, j, 0)),
            pl.BlockSpec(memory_space=pl.ANY),
        ],
        out_specs=pl.BlockSpec((1, t, d), lambda i, j: (i, j, 0)),
        out_shape=jax.ShapeDtypeStruct((b, s, d), F32),
        scratch_shapes=[pltpu.VMEM((TOP_K, t, d), F32), pltpu.SemaphoreType.DMA(())],
        compiler_params=_cparams(("arbitrary", "arbitrary")),
        name="moe_combine",
    )(slots, x1, mods, wts, y)


def _moe(x1, h2, ids, wts, mods, mod_row, lw):
    b, s, d = x1.shape
    n = b * s
    tile = 512 if TOP_K * n >= 512 * N_EXPERTS * 4 else 256
    ids_flat = ids.transpose(1, 0, 2).reshape(TOP_K, n)
    slots, tok_of_slot, texp, tval = _route(ids_flat, tile)
    y = _ffn_call(h2.reshape(n, d), tok_of_slot, texp, tval, lw, tile)
    return _combine_call(x1, mods, mod_row, slots, wts, y)


def _layer_weights(l, p, lam_init):
    perm = _proj_perm()
    w_in = jnp.where(perm[None, :] >= 0, jnp.take(p["w_in"][l], np.maximum(perm, 0), axis=1), 0.0)

    def head_slots(w, src_w, src_off, take, slot, dst_off=0):
        out = jnp.zeros((w.shape[0], N_HEADS * slot), w.dtype)
        for h in range(N_HEADS):
            out = out.at[:, h * slot + dst_off:h * slot + dst_off + take].set(
                w[:, h * src_w + src_off:h * src_w + src_off + take])
        return out

    wuq = head_slots(p["mla_w_uq"][l], MLA_QK, 0, MLA_QK, LANE)
    wuq = jnp.concatenate([wuq, jnp.zeros((256 - Q_LORA, 512), F32)], axis=0)
    wk = head_slots(p["mla_w_ukv"][l], QK_NOPE + V_HEAD, 0, QK_NOPE, LANE)
    wv = head_slots(p["mla_w_ukv"][l], QK_NOPE + V_HEAD, QK_NOPE, V_HEAD, V_HEAD)
    ppe = np.zeros((LANE, 512), np.float32)
    for h in range(N_HEADS):
        for i in range(QK_ROPE):
            ppe[i, h * LANE + QK_NOPE + i] = 1.0

    def slot_gain(g, scale):
        g = jnp.concatenate([g * scale, jnp.zeros((LANE - MLA_QK,), F32)])
        return jnp.tile(g, N_HEADS)

    def row512(v):
        return jnp.concatenate([v, jnp.zeros((512 - v.shape[0],), F32)])

    gains = jnp.stack([
        row512(p["mla_cq_g"][l]),
        slot_gain(p["mla_qn_g"][l], MLA_QK ** -0.5),
        row512(p["mla_ckv_g"][l]),
        slot_gain(p["mla_kn_g"][l], 1.0),
        row512(jnp.tile(p["diff_qn_g"][l], 2 * N_HEADS) * DIFF_DIM ** -0.5),
        row512(jnp.tile(p["diff_kn_g"][l], 2 * N_HEADS)),
        row512(jnp.tile(p["na_qn_g"][l], N_HEADS) * HEAD_DIM ** -0.5),
        row512(jnp.tile(p["na_kn_g"][l], N_HEADS)),
    ])
    conv_w = jnp.concatenate([p["conv_w"][l], jnp.zeros((1, CONV_CH), F32)], axis=0)
    return dict(
        n1g=p["norm1_g"][l][None, :], n2g=p["norm2_g"][l][None, :],
        w_in=w_in.astype(BF16), gains=gains,
        wuq=wuq.astype(BF16), wk=wk.astype(BF16), wv=wv.astype(BF16), ppe=jnp.asarray(ppe, BF16),
        g96=jnp.asarray(_group_ones(512, LANE, MLA_QK), BF16),
        g32=jnp.asarray(_group_ones(BR_W, DIFF_DIM, DIFF_DIM), BF16),
        g64=jnp.asarray(_group_ones(BR_W, HEAD_DIM, HEAD_DIM), BF16),
        rm=jnp.asarray(_rot_matrix(512, LANE, QK_NOPE, QK_ROPE // 2), BF16),
        rd=jnp.asarray(_rot_matrix(BR_W, DIFF_DIM, 0, DIFF_DIM // 2), BF16),
        conv_w=conv_w, conv_b=p["conv_b"][l][None, :],
        conv_ln_g=p["conv_ln_g"][l][None, :], conv_ln_b=p["conv_ln_b"][l][None, :],
        diff_lam=p["diff_lam"][l],
        subln=(jnp.tile(p["diff_subln_g"][l], N_HEADS) * (1.0 - lam_init))[None, :],
        gate_w=p["gate_w"][l].astype(BF16), gate_b=p["gate_b"][l][None, :],
        conv_out=p["conv_out"][l].astype(BF16), mla_out=p["mla_out"][l].astype(BF16),
        diff_out=p["diff_out"][l].astype(BF16), na_out=p["na_out"][l].astype(BF16),
        w_o=p["w_o"][l].astype(BF16),
        router_wt=p["router_w"][l].T, router_b=p["router_b"][l][:, None],
        exp_w_gu=p["exp_w_gu"][l].astype(BF16), exp_b_gu=p["exp_b_gu"][l][:, None, :],
        exp_w_down=p["exp_w_down"][l].astype(BF16), exp_b_down=p["exp_b_down"][l][:, None, :],
    )


def _kt(kc, k):
    return jnp.concatenate([kc, k], axis=1).transpose(0, 2, 1)


def kernel(x, c, ctx, c_ctx, ada_w, ada_b, norm1_g, norm2_g, w_in, conv_w, conv_b, conv_ln_g, conv_ln_b, conv_out, mla_cq_g, mla_ckv_g, mla_w_uq, mla_w_ukv, mla_qn_g, mla_kn_g, mla_out, diff_qn_g, diff_kn_g, diff_lam, diff_subln_g, diff_out, na_qn_g, na_kn_g, na_rpb, na_out, gate_w, gate_b, w_o, router_w, router_b, exp_w_gu, exp_b_gu, exp_w_down, exp_b_down):
    p = dict(norm1_g=norm1_g, norm2_g=norm2_g, w_in=w_in, conv_w=conv_w, conv_b=conv_b,
             conv_ln_g=conv_ln_g, conv_ln_b=conv_ln_b, conv_out=conv_out, mla_cq_g=mla_cq_g,
             mla_ckv_g=mla_ckv_g, mla_w_uq=mla_w_uq, mla_w_ukv=mla_w_ukv, mla_qn_g=mla_qn_g,
             mla_kn_g=mla_kn_g, mla_out=mla_out, diff_qn_g=diff_qn_g, diff_kn_g=diff_kn_g,
             diff_lam=diff_lam, diff_subln_g=diff_subln_g, diff_out=diff_out, na_qn_g=na_qn_g,
             na_kn_g=na_kn_g, na_out=na_out, gate_w=gate_w, gate_b=gate_b, w_o=w_o,
             router_w=router_w, router_b=router_b, exp_w_gu=exp_w_gu, exp_b_gu=exp_b_gu,
             exp_w_down=exp_w_down, exp_b_down=exp_b_down)
    b, s, d = x.shape
    n_ctx = ctx.shape[1]
    depth = ada_w.shape[0]
    rows = s // GRID_W
    assert d == D_MODEL and s % (2 * GRID_W) == 0 and rows >= NA_BAND_ROWS and n_ctx % LANE == 0

    mod_rows = -(-(b + 1) // 8) * 8
    cs = jnp.concatenate([c, c_ctx[None, :], jnp.zeros((mod_rows - b - 1, d), F32)], axis=0)
    mods_all = _ada_call(cs, ada_w, ada_b).reshape(depth, mod_rows, 6, d)

    tabs_x = _rope_lane_tables(s)
    tabs_c = (jnp.ones((n_ctx, 512), F32), jnp.zeros((n_ctx, 512), F32),
              jnp.ones((n_ctx, BR_W), F32), jnp.zeros((n_ctx, BR_W), F32))

    xc = ctx
    for l in range(depth):
        last = l == depth - 1
        lam_init = 0.8 - 0.6 * math.exp(-0.3 * l)
        lw = _layer_weights(l, p, lam_init)
        mods = mods_all[l]
        bias = _na_bias_tables(na_rpb[l], rows)

        u, mq, mk, mv, dq, dk, dv, nq, nk, nv = _proj_call(x, mods, None, lw, tabs_x, True)
        uc, mqc, mkc, mvc, dqc, dkc, dvc, nqc, nkc, nvc = _proj_call(xc, mods, b, lw, tabs_c, False)

        y_conv = _conv_call(u, lw)
        y_mla = _attn_call(mq, _kt(mkc, mk), jnp.concatenate([mvc, mv], axis=1), lw, MAPS_MLA)
        y_diff = _attn_call(dq, _kt(dkc, dk), jnp.concatenate([dvc, dv], axis=1), lw, MAPS_DIFF,
                            diff=True, lam_init=lam_init)
        y_na = _na_call(nq, nk, nv, nkc, nvc, bias)
        x1, h2, ids, wts = _merge_call(x, mods, None, lw, y_conv, y_mla, y_diff, y_na)
        x = _moe(x1, h2, ids, wts, mods, None, lw)

        if not last:
            yc_conv = _conv_call(uc, lw)
            yc_mla = _attn_call(mqc, mkc.transpose(0, 2, 1), mvc, lw, MAPS_MLA)
            yc_diff = _attn_call(dqc, dkc.transpose(0, 2, 1), dvc, lw, MAPS_DIFF, diff=True, lam_init=lam_init)
            yc_na = _attn_call(nqc, nkc.transpose(0, 2, 1), nvc, lw, MAPS_NA)
            xc1, h2c, idsc, wtsc = _merge_call(xc, mods, b, lw, yc_conv, yc_mla, yc_diff, yc_na)
            xc = _moe(xc1, h2c, idsc, wtsc, mods, b, lw)
    return x
```

```python
import functools
import math

import numpy as np
import jax
import jax.numpy as jnp
from jax import lax
from jax.experimental import pallas as pl
from jax.experimental.pallas import tpu as pltpu

F32 = jnp.float32
BF16 = jnp.bfloat16

D_MODEL = 1024
GRID_W = 64
N_BRANCH = 4
N_HEADS = 4
HEAD_DIM = 64
CONV_CH = 256
CONV_WIDTH = 31
Q_LORA = 192
KV_LORA = 128
QK_NOPE = 64
QK_ROPE = 32
V_HEAD = 64
DIFF_DIM = 32
DIFF_V = 2 * DIFF_DIM
NA_KH = 8
NA_KW = 16
ROPE_DIM = 32
ROPE_BASE = 10000.0
N_EXPERTS = 32
TOP_K = 4
D_FF = 1024
SWIGLU_LIMIT = 7.0
SWIGLU_ALPHA = 1.702
EPS = 1e-6
NEG_INF = -1e30

A_IN = 2 * CONV_CH
B_IN = Q_LORA + KV_LORA + QK_ROPE
C_IN = N_HEADS * (4 * DIFF_DIM + DIFF_V)
D_IN = N_HEADS * 3 * HEAD_DIM
OFF_B = A_IN
OFF_C = OFF_B + B_IN
OFF_D = OFF_C + C_IN

LANE = 128
MLA_QK = QK_NOPE + QK_ROPE
BR_W = N_HEADS * HEAD_DIM
PROJ_W = 2560
NA_BAND_ROWS = 10
VMEM_LIMIT = 52 * 1024 * 1024

P_A, P_G, P_CQ, P_CKV, P_KPE = 0, 256, 512, 768, 896
P_DQ, P_DK, P_DV = 1024, 1280, 1536
P_NQ, P_NK, P_NV = 1792, 2048, 2304


def _sigmoid(x):
    return 1.0 / (1.0 + jnp.exp(-x))


def _modulate(x, g, shift, scale):
    ms = jnp.mean(x * x, axis=-1, keepdims=True)
    return (x * lax.rsqrt(ms + EPS) * g) * (1.0 + scale) + shift


def _cparams(sem):
    return pltpu.CompilerParams(dimension_semantics=sem, vmem_limit_bytes=VMEM_LIMIT)


def _const_spec(shape):
    n = len(shape)
    return pl.BlockSpec(shape, lambda *_: (0,) * n)


def _proj_perm():
    perm = -np.ones((PROJ_W,), np.int64)
    perm[P_A:P_A + A_IN] = np.arange(A_IN)
    perm[P_CQ:P_CQ + Q_LORA] = OFF_B + np.arange(Q_LORA)
    perm[P_CKV:P_CKV + KV_LORA] = OFF_B + Q_LORA + np.arange(KV_LORA)
    perm[P_KPE:P_KPE + QK_ROPE] = OFF_B + Q_LORA + KV_LORA + np.arange(QK_ROPE)
    per_head_c = 4 * DIFF_DIM + DIFF_V
    for h in range(N_HEADS):
        base = OFF_C + h * per_head_c
        perm[P_DQ + 64 * h:P_DQ + 64 * h + 64] = base + np.arange(64)
        perm[P_DK + 64 * h:P_DK + 64 * h + 64] = base + 64 + np.arange(64)
        perm[P_DV + 64 * h:P_DV + 64 * h + 64] = base + 128 + np.arange(64)
        base = OFF_D + h * 3 * HEAD_DIM
        perm[P_NQ + 64 * h:P_NQ + 64 * h + 64] = base + np.arange(64)
        perm[P_NK + 64 * h:P_NK + 64 * h + 64] = base + 64 + np.arange(64)
        perm[P_NV + 64 * h:P_NV + 64 * h + 64] = base + 128 + np.arange(64)
    return perm


def _group_ones(width, slot, real):
    i = np.arange(width)
    valid = (i % slot) < real
    same = (i[:, None] // slot) == (i[None, :] // slot)
    return (same & valid[:, None] & valid[None, :]).astype(np.float32)


def _rot_matrix(width, slot, start, half):
    r = np.zeros((width, width), np.float32)
    for s0 in range(0, width, slot):
        for i in range(half):
            a, b = s0 + start + i, s0 + start + half + i
            r[b, a] = -1.0
            r[a, b] = 1.0
    return r


def _rope_lane_tables(n_tokens):
    t = jnp.arange(n_tokens, dtype=jnp.int32)
    rows = (t // GRID_W).astype(F32)
    cols = (t % GRID_W).astype(F32)
    axis_dim = ROPE_DIM // 2
    inv = ROPE_BASE ** (-jnp.arange(0, axis_dim, 2, dtype=F32) / axis_dim)
    theta = jnp.concatenate([rows[:, None] * inv, cols[:, None] * inv], axis=-1)
    cos, sin = jnp.cos(theta), jnp.sin(theta)
    half = ROPE_DIM // 2
    ones = jnp.ones((n_tokens, QK_NOPE), F32)
    zeros = jnp.zeros((n_tokens, QK_NOPE), F32)
    pad1 = jnp.ones((n_tokens, LANE - MLA_QK), F32)
    pad0 = jnp.zeros((n_tokens, LANE - MLA_QK), F32)
    cm = jnp.tile(jnp.concatenate([ones, cos, cos, pad1], -1), (1, N_HEADS))
    sm = jnp.tile(jnp.concatenate([zeros, sin, sin, pad0], -1), (1, N_HEADS))
    cd = jnp.tile(jnp.concatenate([cos, cos], -1), (1, 2 * N_HEADS))
    sd = jnp.tile(jnp.concatenate([sin, sin], -1), (1, 2 * N_HEADS))
    assert half * 2 == DIFF_DIM
    return cm, sm, cd, sd


def _na_bias_tables(rpb, rows):
    kh = min(NA_KH, rows)
    nj = rows // 2
    reps = [0, 1, 2, nj - 2, nj - 1]
    qi = np.arange(2 * GRID_W)
    ki = np.arange(NA_BAND_ROWS * GRID_W)
    ro_all, co_all, valid_all = [], [], []
    for j in reps:
        start = int(np.clip(2 * j - 4, 0, rows - NA_BAND_ROWS))
        r = 2 * j + qi // GRID_W
        qc = qi % GRID_W
        kr = start + ki // GRID_W
        kc = ki % GRID_W
        row_start = np.clip(r - kh // 2, 0, rows - kh)
        win_start = np.clip(qc - NA_KW // 2, 0, GRID_W - NA_KW)
        vr = (kr[None, :] >= row_start[:, None]) & (kr[None, :] < row_start[:, None] + kh)
        vc = (kc[None, :] >= win_start[:, None]) & (kc[None, :] < win_start[:, None] + NA_KW)
        ro = np.clip(kr[None, :] - r[:, None] + NA_KH - 1, 0, 2 * NA_KH - 2)
        co = np.clip(kc[None, :] - qc[:, None] + NA_KW - 1, 0, 2 * NA_KW - 2)
        ro_all.append(ro)
        co_all.append(co)
        valid_all.append(vr & vc)
    ro = np.stack(ro_all)
    co = np.stack(co_all)
    valid = np.stack(valid_all)
    b = rpb.astype(F32)[:, ro, co]
    b = jnp.where(valid[None], b, NEG_INF)
    return jnp.transpose(b, (1, 0, 2, 3))


def _ada_kernel(c_ref, w_ref, b_ref, o_ref):
    c = c_ref[...]
    s = c * _sigmoid(c)
    o_ref[0] = jnp.dot(s, w_ref[0], preferred_element_type=F32,
                       precision=lax.Precision.HIGHEST) + b_ref[0]


def _ada_call(cs, ada_w, ada_b):
    depth, d, n = ada_w.shape
    rows = cs.shape[0]
    tn = 1536
    return pl.pallas_call(
        _ada_kernel,
        grid=(depth, n // tn),
        in_specs=[
            pl.BlockSpec((rows, d), lambda l, j: (0, 0)),
            pl.BlockSpec((1, d, tn), lambda l, j: (l, 0, j)),
            pl.BlockSpec((1, 1, tn), lambda l, j: (l, 0, j)),
        ],
        out_specs=pl.BlockSpec((1, rows, tn), lambda l, j: (l, 0, j)),
        out_shape=jax.ShapeDtypeStruct((depth, rows, n), F32),
        compiler_params=_cparams(("arbitrary", "arbitrary")),
        name="ada",
    )(cs, ada_w, ada_b.reshape(depth, 1, n))


def _group_norm(x, ones_ref, inv_n):
    sq = (x * x).astype(BF16)
    ms = jnp.dot(sq, ones_ref[...], preferred_element_type=F32) * inv_n
    return x * lax.rsqrt(ms + EPS)


def _rope(x, rot_ref, cos_ref, sin_ref):
    rot = jnp.dot(x.astype(BF16), rot_ref[...], preferred_element_type=F32)
    return x * cos_ref[...] + rot * sin_ref[...]


def _proj_kernel(x_ref, mod_ref, n1g_ref, win_ref, gains_ref, wuq_ref, wk_ref, ppe_ref, wv_ref,
                 g96_ref, g32_ref, g64_ref, rm_ref, rd_ref, cm_ref, sm_ref, cd_ref, sd_ref,
                 u_ref, mq_ref, mk_ref, mv_ref, dq_ref, dk_ref, dv_ref, nq_ref, nk_ref, nv_ref,
                 *, use_rope):
    x = x_ref[0]
    mod = mod_ref[0]
    gains = gains_ref[...]
    h = _modulate(x, n1g_ref[...], mod[0:1], mod[1:2]).astype(BF16)
    proj = jnp.dot(h, win_ref[...], preferred_element_type=F32)

    u_ref[0] = proj[:, P_A:P_A + CONV_CH] * _sigmoid(proj[:, P_G:P_G + CONV_CH])

    cq = proj[:, P_CQ:P_CQ + 256]
    ms = jnp.sum(cq * cq, axis=-1, keepdims=True) * (1.0 / Q_LORA)
    cqn = (cq * lax.rsqrt(ms + EPS) * gains[0:1, :256]).astype(BF16)
    q = jnp.dot(cqn, wuq_ref[...], preferred_element_type=F32)
    q = _group_norm(q, g96_ref, 1.0 / MLA_QK) * gains[1:2, :]
    if use_rope:
        q = _rope(q, rm_ref, cm_ref, sm_ref)
    mq_ref[0] = q.astype(BF16)

    ckv = proj[:, P_CKV:P_CKV + KV_LORA]
    ms = jnp.mean(ckv * ckv, axis=-1, keepdims=True)
    ckvn = (ckv * lax.rsqrt(ms + EPS) * gains[2:3, :KV_LORA]).astype(BF16)
    kpe = proj[:, P_KPE:P_KPE + LANE].astype(BF16)
    k = (jnp.dot(ckvn, wk_ref[...], preferred_element_type=F32)
         + jnp.dot(kpe, ppe_ref[...], preferred_element_type=F32))
    k = _group_norm(k, g96_ref, 1.0 / MLA_QK) * gains[3:4, :]
    if use_rope:
        k = _rope(k, rm_ref, cm_ref, sm_ref)
    mk_ref[0] = k.astype(BF16)
    mv_ref[0] = jnp.dot(ckvn, wv_ref[...], preferred_element_type=F32).astype(BF16)

    qd = _group_norm(proj[:, P_DQ:P_DQ + BR_W], g32_ref, 1.0 / DIFF_DIM) * gains[4:5, :BR_W]
    kd = _group_norm(proj[:, P_DK:P_DK + BR_W], g32_ref, 1.0 / DIFF_DIM) * gains[5:6, :BR_W]
    if use_rope:
        qd = _rope(qd, rd_ref, cd_ref, sd_ref)
        kd = _rope(kd, rd_ref, cd_ref, sd_ref)
    dq_ref[0] = qd.astype(BF16)
    dk_ref[0] = kd.astype(BF16)
    dv_ref[0] = proj[:, P_DV:P_DV + BR_W].astype(BF16)

    qn = _group_norm(proj[:, P_NQ:P_NQ + BR_W], g64_ref, 1.0 / HEAD_DIM) * gains[6:7, :BR_W]
    kn = _group_norm(proj[:, P_NK:P_NK + BR_W], g64_ref, 1.0 / HEAD_DIM) * gains[7:8, :BR_W]
    nq_ref[0] = qn.astype(BF16)
    nk_ref[0] = kn.astype(BF16)
    nv_ref[0] = proj[:, P_NV:P_NV + BR_W].astype(BF16)


def _proj_call(x, mods, mod_row, lw, tabs, use_rope):
    b, s, d = x.shape
    t = min(s, 512)
    grid = (b, s // t)
    if mod_row is None:
        mod_map = lambda i, j: (i, 0, 0)
    else:
        mod_map = lambda i, j: (mod_row, 0, 0)
    tok = lambda w: pl.BlockSpec((1, t, w), lambda i, j: (i, j, 0))
    tab = lambda w: pl.BlockSpec((t, w), lambda i, j: (j, 0))
    in_specs = [
        tok(d),
        pl.BlockSpec((1, 6, d), mod_map),
        _const_spec((1, d)),
        _const_spec((d, PROJ_W)),
        _const_spec((8, 512)),
        _const_spec((256, 512)),
        _const_spec((KV_LORA, 512)),
        _const_spec((LANE, 512)),
        _const_spec((KV_LORA, BR_W)),
        _const_spec((512, 512)),
        _const_spec((BR_W, BR_W)),
        _const_spec((BR_W, BR_W)),
        _const_spec((512, 512)),
        _const_spec((BR_W, BR_W)),
        tab(512), tab(512), tab(BR_W), tab(BR_W),
    ]
    widths = [CONV_CH, 512, 512, BR_W, BR_W, BR_W, BR_W, BR_W, BR_W, BR_W]
    dtypes = [F32] + [BF16] * 9
    out_specs = [tok(w) for w in widths]
    out_shape = [jax.ShapeDtypeStruct((b, s, w), dt) for w, dt in zip(widths, dtypes)]
    return pl.pallas_call(
        functools.partial(_proj_kernel, use_rope=use_rope),
        grid=grid, in_specs=in_specs, out_specs=out_specs, out_shape=out_shape,
        compiler_params=_cparams(("parallel", "parallel")),
        name="proj",
    )(x, mods, lw["n1g"], lw["w_in"], lw["gains"], lw["wuq"], lw["wk"], lw["ppe"], lw["wv"],
      lw["g96"], lw["g32"], lw["g64"], lw["rm"], lw["rd"], tabs[0], tabs[1], tabs[2], tabs[3])


CONV_TILE = 128
CONV_PAD = 16


def _conv_kernel(u_ref, w_ref, cb_ref, lg_ref, lb_ref, o_ref, pad_ref, *, seq):
    zeros = jnp.zeros((CONV_PAD, CONV_CH), F32)
    pad_ref[0:CONV_PAD, :] = zeros
    pad_ref[CONV_PAD + seq:2 * CONV_PAD + seq, :] = zeros

    def fill(i, carry):
        base = pl.multiple_of(i * CONV_TILE, CONV_TILE)
        pad_ref[pl.ds(base + CONV_PAD, CONV_TILE), :] = u_ref[0, pl.ds(base, CONV_TILE), :]
        return carry

    lax.fori_loop(0, seq // CONV_TILE, fill, 0)
    w = w_ref[...]
    cb, lg, lb = cb_ref[...], lg_ref[...], lb_ref[...]

    def tile(i, carry):
        base = pl.multiple_of(i * CONV_TILE, CONV_TILE)
        win = pad_ref[pl.ds(base, CONV_TILE + 2 * CONV_PAD), :]
        acc = jnp.zeros((CONV_TILE, CONV_CH), F32)
        for j in range(CONV_WIDTH):
            acc = acc + win[j + 1:j + 1 + CONV_TILE, :] * w[j:j + 1, :]
        c = acc + cb
        mu = jnp.mean(c, axis=-1, keepdims=True)
        cc = c - mu
        var = jnp.mean(cc * cc, axis=-1, keepdims=True)
        y = cc * lax.rsqrt(var + EPS) * lg + lb
        o_ref[0, pl.ds(base, CONV_TILE), :] = (y * _sigmoid(y)).astype(BF16)
        return carry

    lax.fori_loop(0, seq // CONV_TILE, tile, 0)


def _conv_call(u, lw):
    b, s, ch = u.shape
    return pl.pallas_call(
        functools.partial(_conv_kernel, seq=s),
        grid=(b,),
        in_specs=[
            pl.BlockSpec((1, s, ch), lambda i: (i, 0, 0)),
            _const_spec((32, ch)), _const_spec((1, ch)), _const_spec((1, ch)), _const_spec((1, ch)),
        ],
        out_specs=pl.BlockSpec((1, s, ch), lambda i: (i, 0, 0)),
        out_shape=jax.ShapeDtypeStruct((b, s, ch), BF16),
        scratch_shapes=[pltpu.VMEM((s + 2 * CONV_PAD, ch), F32)],
        compiler_params=_cparams(("parallel",)),
        name="conv",
    )(u, lw["conv_w"], lw["conv_b"], lw["conv_ln_g"], lw["conv_ln_b"])


def _lane_mask(width, lo, hi):
    lane = lax.broadcasted_iota(jnp.int32, (1, width), 1)
    return (lane >= lo) & (lane < hi)


def _softmax_pv(qw, kt, v):
    s = jnp.dot(qw, kt, preferred_element_type=F32)
    m = jnp.max(s, axis=-1, keepdims=True)
    p = jnp.exp(s - m)
    l = jnp.sum(p, axis=-1, keepdims=True)
    return jnp.dot(p.astype(BF16), v, preferred_element_type=F32) * (1.0 / l)


def _attn_kernel(q_ref, kt_ref, v_ref, lam_ref, g64_ref, sg_ref, o_ref, *, maps, diff, lam_init):
    v = v_ref[0]
    tq = q_ref.shape[1]
    acc = jnp.zeros((tq, BR_W), F32)
    if diff:
        lv = lam_ref[...]
        lam = (jnp.exp(jnp.sum(lv[0:1] * lv[1:2], axis=-1, keepdims=True))
               - jnp.exp(jnp.sum(lv[2:3] * lv[3:4], axis=-1, keepdims=True)) + lam_init)
    for h in range(N_HEADS):
        outs = []
        for (w0, lo, hi) in maps[h]:
            qw = q_ref[0, :, w0:w0 + LANE]
            if (lo, hi) != (0, LANE):
                qw = jnp.where(_lane_mask(LANE, lo, hi), qw, jnp.zeros_like(qw))
            outs.append(_softmax_pv(qw, kt_ref[0, w0:w0 + LANE, :], v))
        o_h = outs[0] - lam * outs[1] if diff else outs[0]
        acc = jnp.where(_lane_mask(BR_W, V_HEAD * h, V_HEAD * (h + 1)), o_h, acc)
    if diff:
        acc = _group_norm(acc, g64_ref, 1.0 / DIFF_V) * sg_ref[...]
    o_ref[0] = acc.astype(BF16)


MAPS_MLA = tuple(((LANE * h, 0, LANE),) for h in range(N_HEADS))
MAPS_DIFF = tuple(tuple((LANE * (h // 2), 64 * (h % 2) + 32 * c, 64 * (h % 2) + 32 * c + 32) for c in range(2))
                  for h in range(N_HEADS))
MAPS_NA = tuple(((LANE * (h // 2), 64 * (h % 2), 64 * (h % 2) + 64),) for h in range(N_HEADS))


def _attn_call(q, kt, v, lw, maps, diff=False, lam_init=0.0):
    b, s, wq = q.shape
    sk = kt.shape[2]
    tq = min(s, 256)
    return pl.pallas_call(
        functools.partial(_attn_kernel, maps=maps, diff=diff, lam_init=lam_init),
        grid=(b, s // tq),
        in_specs=[
            pl.BlockSpec((1, tq, wq), lambda i, j: (i, j, 0)),
            pl.BlockSpec((1, wq, sk), lambda i, j: (i, 0, 0)),
            pl.BlockSpec((1, sk, BR_W), lambda i, j: (i, 0, 0)),
            _const_spec((4, DIFF_DIM)),
            _const_spec((BR_W, BR_W)),
            _const_spec((1, BR_W)),
        ],
        out_specs=pl.BlockSpec((1, tq, BR_W), lambda i, j: (i, j, 0)),
        out_shape=jax.ShapeDtypeStruct((b, s, BR_W), BF16),
        compiler_params=_cparams(("parallel", "parallel")),
        name="attn_diff" if diff else "attn",
    )(q, kt, v, lw["diff_lam"], lw["g64"], lw["subln"])


_NT = (((1,), (1,)), ((), ()))


def _na_kernel(q_ref, k_ref, v_ref, kc_ref, vc_ref, bias_ref, o_ref, *, rows):
    j = pl.program_id(1)
    nj = rows // 2
    start = jnp.clip(2 * j - 4, 0, rows - NA_BAND_ROWS)
    base = pl.multiple_of(start * GRID_W, 2 * GRID_W)
    cls = jnp.where(j < 2, j, jnp.where(j >= nj - 2, j - (nj - 2) + 3, 2))
    band = NA_BAND_ROWS * GRID_W
    kw = k_ref[0, pl.ds(base, band), :]
    vw = v_ref[0, pl.ds(base, band), :]
    kc = kc_ref[0]
    vc = vc_ref[0]
    q = q_ref[0]
    acc = jnp.zeros((2 * GRID_W, BR_W), F32)
    for h in range(N_HEADS):
        qm = jnp.where(_lane_mask(BR_W, HEAD_DIM * h, HEAD_DIM * (h + 1)), q, jnp.zeros_like(q))
        s_loc = lax.dot_general(qm, kw, _NT, preferred_element_type=F32) + bias_ref[cls, h]
        s_ctx = lax.dot_general(qm, kc, _NT, preferred_element_type=F32)
        m = jnp.maximum(jnp.max(s_loc, axis=-1, keepdims=True), jnp.max(s_ctx, axis=-1, keepdims=True))
        p_loc = jnp.exp(s_loc - m)
        p_ctx = jnp.exp(s_ctx - m)
        l = jnp.sum(p_loc, axis=-1, keepdims=True) + jnp.sum(p_ctx, axis=-1, keepdims=True)
        o = (jnp.dot(p_ctx.astype(BF16), vc, preferred_element_type=F32)
             + jnp.dot(p_loc.astype(BF16), vw, preferred_element_type=F32)) * (1.0 / l)
        acc = jnp.where(_lane_mask(BR_W, HEAD_DIM * h, HEAD_DIM * (h + 1)), o, acc)
    o_ref[0] = acc.astype(BF16)


def _na_call(q, k, v, kc, vc, bias):
    b, s, w = q.shape
    n_ctx = kc.shape[1]
    rows = s // GRID_W
    tq = 2 * GRID_W
    return pl.pallas_call(
        functools.partial(_na_kernel, rows=rows),
        grid=(b, rows // 2),
        in_specs=[
            pl.BlockSpec((1, tq, w), lambda i, j: (i, j, 0)),
            pl.BlockSpec((1, s, w), lambda i, j: (i, 0, 0)),
            pl.BlockSpec((1, s, w), lambda i, j: (i, 0, 0)),
            pl.BlockSpec((1, n_ctx, w), lambda i, j: (i, 0, 0)),
            pl.BlockSpec((1, n_ctx, w), lambda i, j: (i, 0, 0)),
            _const_spec(bias.shape),
        ],
        out_specs=pl.BlockSpec((1, tq, w), lambda i, j: (i, j, 0)),
        out_shape=jax.ShapeDtypeStruct((b, s, w), BF16),
        compiler_params=_cparams(("parallel", "arbitrary")),
        name="na",
    )(q, k, v, kc, vc, bias)


def _merge_kernel(x_ref, mod_ref, n1g_ref, n2g_ref, uc_ref, om_ref, od_ref, on_ref,
                  gw_ref, gb_ref, wc_ref, wm_ref, wd_ref, wn_ref, wo_ref, rwt_ref, rb_ref,
                  x1_ref, h2_ref, ids_ref, wts_ref):
    x = x_ref[0]
    mod = mod_ref[0]
    h = _modulate(x, n1g_ref[...], mod[0:1], mod[1:2]).astype(BF16)
    y = jnp.zeros(x.shape, F32)
    branches = ((uc_ref, wc_ref), (om_ref, wm_ref), (od_ref, wd_ref), (on_ref, wn_ref))
    for i, (o_ref, w_ref) in enumerate(branches):
        lo = D_MODEL * i
        g = _sigmoid(jnp.dot(h, gw_ref[:, lo:lo + D_MODEL], preferred_element_type=F32)
                     + gb_ref[:, lo:lo + D_MODEL])
        y = y + g * jnp.dot(o_ref[0], w_ref[...], preferred_element_type=F32)
    out = jnp.dot(y.astype(BF16), wo_ref[...], preferred_element_type=F32)
    x1 = x + mod[2:3] * out
    x1_ref[0] = x1
    h2 = _modulate(x1, n2g_ref[...], mod[3:4], mod[4:5])
    h2_ref[0] = h2

    logits = lax.dot_general(rwt_ref[...], h2, _NT, preferred_element_type=F32,
                             precision=lax.Precision.HIGHEST) + rb_ref[...]
    eidx = lax.broadcasted_iota(jnp.int32, logits.shape, 0).astype(F32)
    vals, idxs = [], []
    cur = logits
    for _ in range(TOP_K):
        m = jnp.max(cur, axis=0, keepdims=True)
        idx = jnp.min(jnp.where(cur == m, eidx, float(N_EXPERTS)), axis=0, keepdims=True)
        vals.append(m)
        idxs.append(idx)
        cur = jnp.where(eidx == idx, -jnp.inf, cur)
    es = [jnp.exp(vk - vals[0]) for vk in vals]
    den = es[0] + es[1] + es[2] + es[3]
    ids_ref[0] = jnp.concatenate(idxs, axis=0).astype(jnp.int32)
    wts_ref[0] = jnp.concatenate([e / den for e in es], axis=0)


def _merge_call(x, mods, mod_row, lw, uc, om, od, on):
    b, s, d = x.shape
    t = min(s, 512)
    if mod_row is None:
        mod_map = lambda i, j: (i, 0, 0)
    else:
        mod_map = lambda i, j: (mod_row, 0, 0)
    tok = lambda w: pl.BlockSpec((1, t, w), lambda i, j: (i, j, 0))
    rt = pl.BlockSpec((1, TOP_K, t), lambda i, j: (i, 0, j))
    return pl.pallas_call(
        _merge_kernel,
        grid=(b, s // t),
        in_specs=[
            tok(d), pl.BlockSpec((1, 6, d), mod_map), _const_spec((1, d)), _const_spec((1, d)),
            tok(BR_W), tok(BR_W), tok(BR_W), tok(BR_W),
            _const_spec((d, N_BRANCH * d)), _const_spec((1, N_BRANCH * d)),
            _const_spec((BR_W, d)), _const_spec((BR_W, d)), _const_spec((BR_W, d)), _const_spec((BR_W, d)),
            _const_spec((d, d)), _const_spec((N_EXPERTS, d)), _const_spec((N_EXPERTS, 1)),
        ],
        out_specs=[tok(d), tok(d), rt, rt],
        out_shape=[jax.ShapeDtypeStruct((b, s, d), F32), jax.ShapeDtypeStruct((b, s, d), F32),
                   jax.ShapeDtypeStruct((b, TOP_K, s), jnp.int32), jax.ShapeDtypeStruct((b, TOP_K, s), F32)],
        compiler_params=_cparams(("parallel", "parallel")),
        name="merge",
    )(x, mods, lw["n1g"], lw["n2g"], uc, om, od, on, lw["gate_w"], lw["gate_b"],
      lw["conv_out"], lw["mla_out"], lw["diff_out"], lw["na_out"], lw["w_o"], lw["router_wt"], lw["router_b"])


def _route(ids, tile):
    n = ids.shape[1]
    p = TOP_K * n
    e = ids.reshape(p)
    onehot = (e[:, None] == jnp.arange(N_EXPERTS, dtype=jnp.int32)[None, :]).astype(jnp.int32)
    csum = jnp.cumsum(onehot, axis=0)
    counts = csum[-1]
    padded = ((counts + tile - 1) // tile) * tile
    gend = jnp.cumsum(padded)
    gstart = gend - padded
    slot = jnp.sum(onehot * (csum - 1 + gstart[None, :]), axis=1).astype(jnp.int32)
    n_tiles = p // tile + N_EXPERTS
    n_slots = n_tiles * tile
    pair_of_slot = jnp.full((n_slots,), -1, jnp.int32).at[slot].set(
        jnp.arange(p, dtype=jnp.int32), unique_indices=True)
    real = pair_of_slot >= 0
    src_tok = jnp.where(real, pair_of_slot % n, 0)
    dst_row = jnp.where(real, pair_of_slot, p + jnp.arange(n_slots, dtype=jnp.int32))
    tile_start = jnp.arange(n_tiles, dtype=jnp.int32) * tile
    texp = jnp.minimum(jnp.searchsorted(gend, tile_start, side="right"), N_EXPERTS - 1).astype(jnp.int32)
    tval = (tile_start < gend[-1]).astype(jnp.int32)
    return src_tok.reshape(n_tiles, 1, tile), dst_row.reshape(n_tiles, 1, tile), texp, tval


def _ffn_kernel(texp_ref, tval_ref, src_ref, nsrc_ref, dst_ref, h_hbm, wgu_ref, bgu_ref, wd_ref, bd_ref,
                y_hbm, xbuf, ybuf, sem_in, sem_out, *, tile, n_tiles):
    i = pl.program_id(0)
    slot = i % 2

    def gather(idx_ref, buf_slot):
        def issue(r, carry):
            pltpu.make_async_copy(h_hbm.at[pl.ds(idx_ref[0, 0, r], 1)], xbuf.at[buf_slot, pl.ds(r, 1)],
                                  sem_in.at[buf_slot]).start()
            return carry

        lax.fori_loop(0, tile, issue, 0, unroll=8)

    def wait_rows(buf, sem):
        pltpu.make_async_copy(h_hbm.at[pl.ds(0, tile)], buf, sem).wait()

    @pl.when((i == 0) & (tval_ref[0] > 0))
    def _():
        gather(src_ref, 0)

    @pl.when((i + 1 < n_tiles) & (tval_ref[jnp.minimum(i + 1, n_tiles - 1)] > 0))
    def _():
        gather(nsrc_ref, 1 - slot)

    @pl.when((i >= 2) & (tval_ref[jnp.maximum(i - 2, 0)] > 0))
    def _():
        wait_rows(ybuf.at[slot], sem_out.at[slot])

    @pl.when(tval_ref[i] > 0)
    def _():
        wait_rows(xbuf.at[slot], sem_in.at[slot])
        x = xbuf[slot].astype(BF16)
        gu = jnp.dot(x, wgu_ref[0], preferred_element_type=F32) + bgu_ref[0]
        g = jnp.minimum(gu[:, :D_FF], SWIGLU_LIMIT)
        u = jnp.clip(gu[:, D_FF:], -SWIGLU_LIMIT, SWIGLU_LIMIT)
        act = ((u + 1.0) * (g * _sigmoid(SWIGLU_ALPHA * g))).astype(BF16)
        ybuf[slot] = jnp.dot(act, wd_ref[0], preferred_element_type=F32) + bd_ref[0]

        def issue(r, carry):
            pltpu.make_async_copy(ybuf.at[slot, pl.ds(r, 1)], y_hbm.at[pl.ds(dst_ref[0, 0, r], 1)],
                                  sem_out.at[slot]).start()
            return carry

        lax.fori_loop(0, tile, issue, 0, unroll=8)

    @pl.when(i == n_tiles - 1)
    def _():
        @pl.when((i >= 1) & (tval_ref[jnp.maximum(i - 1, 0)] > 0))
        def _():
            wait_rows(ybuf.at[1 - slot], sem_out.at[1 - slot])

        @pl.when(tval_ref[i] > 0)
        def _():
            wait_rows(ybuf.at[slot], sem_out.at[slot])


def _ffn_call(h2, src_tok, dst_row, texp, tval, lw, tile, n_out_rows):
    n_tiles = src_tok.shape[0]
    d = h2.shape[1]
    idx_spec = lambda f: pl.BlockSpec((1, 1, tile), f, memory_space=pltpu.SMEM)
    grid_spec = pltpu.PrefetchScalarGridSpec(
        num_scalar_prefetch=2,
        grid=(n_tiles,),
        in_specs=[
            idx_spec(lambda i, te, tv: (i, 0, 0)),
            idx_spec(lambda i, te, tv: (jnp.minimum(i + 1, n_tiles - 1), 0, 0)),
            idx_spec(lambda i, te, tv: (i, 0, 0)),
            pl.BlockSpec(memory_space=pl.ANY),
            pl.BlockSpec((1, d, 2 * D_FF), lambda i, te, tv: (te[i], 0, 0)),
            pl.BlockSpec((1, 1, 2 * D_FF), lambda i, te, tv: (te[i], 0, 0)),
            pl.BlockSpec((1, D_FF, d), lambda i, te, tv: (te[i], 0, 0)),
            pl.BlockSpec((1, 1, d), lambda i, te, tv: (te[i], 0, 0)),
        ],
        out_specs=pl.BlockSpec(memory_space=pl.ANY),
        scratch_shapes=[pltpu.VMEM((2, tile, d), F32), pltpu.VMEM((2, tile, d), F32),
                        pltpu.SemaphoreType.DMA((2,)), pltpu.SemaphoreType.DMA((2,))],
    )
    return pl.pallas_call(
        functools.partial(_ffn_kernel, tile=tile, n_tiles=n_tiles),
        grid_spec=grid_spec,
        out_shape=jax.ShapeDtypeStruct((n_out_rows, d), F32),
        compiler_params=_cparams(("arbitrary",)),
        name="moe_ffn",
    )(texp, tval, src_tok, src_tok, dst_row, h2, lw["exp_w_gu"], lw["exp_b_gu"], lw["exp_w_down"], lw["exp_b_down"])


def _combine_kernel(x1_ref, mod_ref, w_ref, y0_ref, y1_ref, y2_ref, y3_ref, o_ref):
    w = w_ref[0]
    acc = w[:, 0:1] * y0_ref[...]
    for k, y_ref in ((1, y1_ref), (2, y2_ref), (3, y3_ref)):
        acc = acc + w[:, k:k + 1] * y_ref[...]
    o_ref[0] = x1_ref[0] + mod_ref[0][5:6] * acc


def _combine_call(x1, mods, mod_row, wts, y):
    b, s, d = x1.shape
    t = min(s, 512)
    nt = s // t
    if mod_row is None:
        mod_map = lambda i, j: (i, 0, 0)
    else:
        mod_map = lambda i, j: (mod_row, 0, 0)
    wts = wts.transpose(0, 2, 1)
    y_spec = lambda k: pl.BlockSpec((t, d), lambda i, j: (k * b * nt + i * nt + j, 0))
    return pl.pallas_call(
        _combine_kernel,
        grid=(b, nt),
        in_specs=[
            pl.BlockSpec((1, t, d), lambda i, j: (i, j, 0)),
            pl.BlockSpec((1, 6, d), mod_map),
            pl.BlockSpec((1, t, TOP_K), lambda i, j: (i, j, 0)),
            y_spec(0), y_spec(1), y_spec(2), y_spec(3),
        ],
        out_specs=pl.BlockSpec((1, t, d), lambda i, j: (i, j, 0)),
        out_shape=jax.ShapeDtypeStruct((b, s, d), F32),
        compiler_params=_cparams(("parallel", "parallel")),
        name="moe_combine",
    )(x1, mods, wts, y, y, y, y)


def _moe(x1, h2, ids, wts, mods, mod_row, lw):
    b, s, d = x1.shape
    n = b * s
    tile = 512 if TOP_K * n >= 512 * N_EXPERTS * 4 else 256
    ids_flat = ids.transpose(1, 0, 2).reshape(TOP_K, n)
    src_tok, dst_row, texp, tval = _route(ids_flat, tile)
    n_out_rows = TOP_K * n + src_tok.shape[0] * tile
    y = _ffn_call(h2.reshape(n, d), src_tok, dst_row, texp, tval, lw, tile, n_out_rows)
    return _combine_call(x1, mods, mod_row, wts, y)


def _layer_weights(l, p, lam_init):
    perm = _proj_perm()
    w_in = jnp.where(perm[None, :] >= 0, jnp.take(p["w_in"][l], np.maximum(perm, 0), axis=1), 0.0)

    def head_slots(w, src_w, src_off, take, slot, dst_off=0):
        out = jnp.zeros((w.shape[0], N_HEADS * slot), w.dtype)
        for h in range(N_HEADS):
            out = out.at[:, h * slot + dst_off:h * slot + dst_off + take].set(
                w[:, h * src_w + src_off:h * src_w + src_off + take])
        return out

    wuq = head_slots(p["mla_w_uq"][l], MLA_QK, 0, MLA_QK, LANE)
    wuq = jnp.concatenate([wuq, jnp.zeros((256 - Q_LORA, 512), F32)], axis=0)
    wk = head_slots(p["mla_w_ukv"][l], QK_NOPE + V_HEAD, 0, QK_NOPE, LANE)
    wv = head_slots(p["mla_w_ukv"][l], QK_NOPE + V_HEAD, QK_NOPE, V_HEAD, V_HEAD)
    ppe = np.zeros((LANE, 512), np.float32)
    for h in range(N_HEADS):
        for i in range(QK_ROPE):
            ppe[i, h * LANE + QK_NOPE + i] = 1.0

    def slot_gain(g, scale):
        g = jnp.concatenate([g * scale, jnp.zeros((LANE - MLA_QK,), F32)])
        return jnp.tile(g, N_HEADS)

    def row512(v):
        return jnp.concatenate([v, jnp.zeros((512 - v.shape[0],), F32)])

    gains = jnp.stack([
        row512(p["mla_cq_g"][l]),
        slot_gain(p["mla_qn_g"][l], MLA_QK ** -0.5),
        row512(p["mla_ckv_g"][l]),
        slot_gain(p["mla_kn_g"][l], 1.0),
        row512(jnp.tile(p["diff_qn_g"][l], 2 * N_HEADS) * DIFF_DIM ** -0.5),
        row512(jnp.tile(p["diff_kn_g"][l], 2 * N_HEADS)),
        row512(jnp.tile(p["na_qn_g"][l], N_HEADS) * HEAD_DIM ** -0.5),
        row512(jnp.tile(p["na_kn_g"][l], N_HEADS)),
    ])
    conv_w = jnp.concatenate([p["conv_w"][l], jnp.zeros((1, CONV_CH), F32)], axis=0)
    return dict(
        n1g=p["norm1_g"][l][None, :], n2g=p["norm2_g"][l][None, :],
        w_in=w_in.astype(BF16), gains=gains,
        wuq=wuq.astype(BF16), wk=wk.astype(BF16), wv=wv.astype(BF16), ppe=jnp.asarray(ppe, BF16),
        g96=jnp.asarray(_group_ones(512, LANE, MLA_QK), BF16),
        g32=jnp.asarray(_group_ones(BR_W, DIFF_DIM, DIFF_DIM), BF16),
        g64=jnp.asarray(_group_ones(BR_W, HEAD_DIM, HEAD_DIM), BF16),
        rm=jnp.asarray(_rot_matrix(512, LANE, QK_NOPE, QK_ROPE // 2), BF16),
        rd=jnp.asarray(_rot_matrix(BR_W, DIFF_DIM, 0, DIFF_DIM // 2), BF16),
        conv_w=conv_w, conv_b=p["conv_b"][l][None, :],
        conv_ln_g=p["conv_ln_g"][l][None, :], conv_ln_b=p["conv_ln_b"][l][None, :],
        diff_lam=p["diff_lam"][l],
        subln=(jnp.tile(p["diff_subln_g"][l], N_HEADS) * (1.0 - lam_init))[None, :],
        gate_w=p["gate_w"][l].astype(BF16), gate_b=p["gate_b"][l][None, :],
        conv_out=p["conv_out"][l].astype(BF16), mla_out=p["mla_out"][l].astype(BF16),
        diff_out=p["diff_out"][l].astype(BF16), na_out=p["na_out"][l].astype(BF16),
        w_o=p["w_o"][l].astype(BF16),
        router_wt=p["router_w"][l].T, router_b=p["router_b"][l][:, None],
        exp_w_gu=p["exp_w_gu"][l].astype(BF16), exp_b_gu=p["exp_b_gu"][l][:, None, :],
        exp_w_down=p["exp_w_down"][l].astype(BF16), exp_b_down=p["exp_b_down"][l][:, None, :],
    )


def _kt(kc, k):
    return jnp.concatenate([kc, k], axis=1).transpose(0, 2, 1)


def kernel(x, c, ctx, c_ctx, ada_w, ada_b, norm1_g, norm2_g, w_in, conv_w, conv_b, conv_ln_g, conv_ln_b, conv_out, mla_cq_g, mla_ckv_g, mla_w_uq, mla_w_ukv, mla_qn_g, mla_kn_g, mla_out, diff_qn_g, diff_kn_g, diff_lam, diff_subln_g, diff_out, na_qn_g, na_kn_g, na_rpb, na_out, gate_w, gate_b, w_o, router_w, router_b, exp_w_gu, exp_b_gu, exp_w_down, exp_b_down):
    p = dict(norm1_g=norm1_g, norm2_g=norm2_g, w_in=w_in, conv_w=conv_w, conv_b=conv_b,
             conv_ln_g=conv_ln_g, conv_ln_b=conv_ln_b, conv_out=conv_out, mla_cq_g=mla_cq_g,
             mla_ckv_g=mla_ckv_g, mla_w_uq=mla_w_uq, mla_w_ukv=mla_w_ukv, mla_qn_g=mla_qn_g,
             mla_kn_g=mla_kn_g, mla_out=mla_out, diff_qn_g=diff_qn_g, diff_kn_g=diff_kn_g,
             diff_lam=diff_lam, diff_subln_g=diff_subln_g, diff_out=diff_out, na_qn_g=na_qn_g,
             na_kn_g=na_kn_g, na_out=na_out, gate_w=gate_w, gate_b=gate_b, w_o=w_o,
             router_w=router_w, router_b=router_b, exp_w_gu=exp_w_gu, exp_b_gu=exp_b_gu,
             exp_w_down=exp_w_down, exp_b_down=exp_b_down)
    b, s, d = x.shape
    n_ctx = ctx.shape[1]
    depth = ada_w.shape[0]
    rows = s // GRID_W
    assert d == D_MODEL and s % (2 * GRID_W) == 0 and rows >= NA_BAND_ROWS and n_ctx % LANE == 0

    mod_rows = -(-(b + 1) // 8) * 8
    cs = jnp.concatenate([c, c_ctx[None, :], jnp.zeros((mod_rows - b - 1, d), F32)], axis=0)
    mods_all = _ada_call(cs, ada_w, ada_b).reshape(depth, mod_rows, 6, d)

    tabs_x = _rope_lane_tables(s)
    tabs_c = (jnp.ones((n_ctx, 512), F32), jnp.zeros((n_ctx, 512), F32),
              jnp.ones((n_ctx, BR_W), F32), jnp.zeros((n_ctx, BR_W), F32))

    xc = ctx
    for l in range(depth):
        last = l == depth - 1
        lam_init = 0.8 - 0.6 * math.exp(-0.3 * l)
        lw = _layer_weights(l, p, lam_init)
        mods = mods_all[l]
        bias = _na_bias_tables(na_rpb[l], rows)

        u, mq, mk, mv, dq, dk, dv, nq, nk, nv = _proj_call(x, mods, None, lw, tabs_x, True)
        uc, mqc, mkc, mvc, dqc, dkc, dvc, nqc, nkc, nvc = _proj_call(xc, mods, b, lw, tabs_c, False)

        y_conv = _conv_call(u, lw)
        y_mla = _attn_call(mq, _kt(mkc, mk), jnp.concatenate([mvc, mv], axis=1), lw, MAPS_MLA)
        y_diff = _attn_call(dq, _kt(dkc, dk), jnp.concatenate([dvc, dv], axis=1), lw, MAPS_DIFF,
                            diff=True, lam_init=lam_init)
        y_na = _na_call(nq, nk, nv, nkc, nvc, bias)
        x1, h2, ids, wts = _merge_call(x, mods, None, lw, y_conv, y_mla, y_diff, y_na)
        x = _moe(x1, h2, ids, wts, mods, None, lw)

        if not last:
            yc_conv = _conv_call(uc, lw)
            yc_mla = _attn_call(mqc, mkc.transpose(0, 2, 1), mvc, lw, MAPS_MLA)
            yc_diff = _attn_call(dqc, dkc.transpose(0, 2, 1), dvc, lw, MAPS_DIFF, diff=True, lam_init=lam_init)
            yc_na = _attn_call(nqc, nkc.transpose(0, 2, 1), nvc, lw, MAPS_NA)
            xc1, h2c, idsc, wtsc = _merge_call(xc, mods, b, lw, yc_conv, yc_mla, yc_diff, yc_na)
            xc = _moe(xc1, h2c, idsc, wtsc, mods, b, lw)
    return x
```

```python
import functools
import math

import numpy as np
import jax
import jax.numpy as jnp
from jax import lax
from jax.experimental import pallas as pl
from jax.experimental.pallas import tpu as pltpu

F32 = jnp.float32
BF16 = jnp.bfloat16

D_MODEL = 1024
GRID_W = 64
N_BRANCH = 4
N_HEADS = 4
HEAD_DIM = 64
CONV_CH = 256
CONV_WIDTH = 31
Q_LORA = 192
KV_LORA = 128
QK_NOPE = 64
QK_ROPE = 32
V_HEAD = 64
DIFF_DIM = 32
DIFF_V = 2 * DIFF_DIM
NA_KH = 8
NA_KW = 16
ROPE_DIM = 32
ROPE_BASE = 10000.0
N_EXPERTS = 32
TOP_K = 4
D_FF = 1024
SWIGLU_LIMIT = 7.0
SWIGLU_ALPHA = 1.702
EPS = 1e-6
NEG_INF = -1e30

A_IN = 2 * CONV_CH
B_IN = Q_LORA + KV_LORA + QK_ROPE
C_IN = N_HEADS * (4 * DIFF_DIM + DIFF_V)
D_IN = N_HEADS * 3 * HEAD_DIM
OFF_B = A_IN
OFF_C = OFF_B + B_IN
OFF_D = OFF_C + C_IN

LANE = 128
MLA_QK = QK_NOPE + QK_ROPE
BR_W = N_HEADS * HEAD_DIM
PROJ_W = 2560
NA_BAND_ROWS = 10
FFN_CHUNK = 256
VMEM_LIMIT = 52 * 1024 * 1024

P_A, P_G, P_CQ, P_CKV, P_KPE = 0, 256, 512, 768, 896
P_DQ, P_DK, P_DV = 1024, 1280, 1536
P_NQ, P_NK, P_NV = 1792, 2048, 2304


def _sigmoid(x):
    return 1.0 / (1.0 + jnp.exp(-x))


def _modulate(x, g, shift, scale):
    ms = jnp.mean(x * x, axis=-1, keepdims=True)
    return (x * lax.rsqrt(ms + EPS) * g) * (1.0 + scale) + shift


def _cparams(sem):
    return pltpu.CompilerParams(dimension_semantics=sem, vmem_limit_bytes=VMEM_LIMIT)


def _const_spec(shape):
    n = len(shape)
    return pl.BlockSpec(shape, lambda *_: (0,) * n)


def _group_ones(width, slot, real):
    i = np.arange(width)
    valid = (i % slot) < real
    same = (i[:, None] // slot) == (i[None, :] // slot)
    return (same & valid[:, None] & valid[None, :]).astype(np.float32)


def _rot_matrix(width, slot, start, half):
    r = np.zeros((width, width), np.float32)
    for s0 in range(0, width, slot):
        for i in range(half):
            a, b = s0 + start + i, s0 + start + half + i
            r[b, a] = -1.0
            r[a, b] = 1.0
    return r


def _rope_lane_tables(n_tokens):
    t = jnp.arange(n_tokens, dtype=jnp.int32)
    rows = (t // GRID_W).astype(F32)
    cols = (t % GRID_W).astype(F32)
    axis_dim = ROPE_DIM // 2
    inv = ROPE_BASE ** (-jnp.arange(0, axis_dim, 2, dtype=F32) / axis_dim)
    theta = jnp.concatenate([rows[:, None] * inv, cols[:, None] * inv], axis=-1)
    cos, sin = jnp.cos(theta), jnp.sin(theta)
    half = ROPE_DIM // 2
    ones = jnp.ones((n_tokens, QK_NOPE), F32)
    zeros = jnp.zeros((n_tokens, QK_NOPE), F32)
    pad1 = jnp.ones((n_tokens, LANE - MLA_QK), F32)
    pad0 = jnp.zeros((n_tokens, LANE - MLA_QK), F32)
    cm = jnp.tile(jnp.concatenate([ones, cos, cos, pad1], -1), (1, N_HEADS))
    sm = jnp.tile(jnp.concatenate([zeros, sin, sin, pad0], -1), (1, N_HEADS))
    cd = jnp.tile(jnp.concatenate([cos, cos], -1), (1, 2 * N_HEADS))
    sd = jnp.tile(jnp.concatenate([sin, sin], -1), (1, 2 * N_HEADS))
    assert half * 2 == DIFF_DIM
    return cm, sm, cd, sd


def _na_bias_tables(rpb, rows):
    kh = min(NA_KH, rows)
    nj = rows // 2
    reps = np.array([0, 1, 2, nj - 2, nj - 1])
    n_ro, n_co = 2 * NA_KH - 1, 2 * NA_KW - 1
    start = np.clip(2 * reps - 4, 0, rows - NA_BAND_ROWS)
    r = 2 * reps[:, None] + np.arange(2)[None, :]
    kr = start[:, None] + np.arange(NA_BAND_ROWS)[None, :]
    row_start = np.clip(r - kh // 2, 0, rows - kh)
    vr = (kr[:, None, :] >= row_start[:, :, None]) & (kr[:, None, :] < row_start[:, :, None] + kh)
    ro = np.clip(kr[:, None, :] - r[:, :, None] + NA_KH - 1, 0, n_ro - 1)
    qc = np.arange(GRID_W)
    win_start = np.clip(qc - NA_KW // 2, 0, GRID_W - NA_KW)
    vc = (qc[None, :] >= win_start[:, None]) & (qc[None, :] < win_start[:, None] + NA_KW)
    co = np.clip(qc[None, :] - qc[:, None] + NA_KW - 1, 0, n_co - 1)
    rsel = (ro[..., None] == np.arange(n_ro)).astype(np.float32)
    csel = (co[None] == np.arange(n_co)[:, None, None]).astype(np.float32)
    hi = lax.Precision.HIGHEST
    t1 = jnp.einsum("cqav,hvw->hcqaw", rsel, rpb.astype(F32), precision=hi)
    b = jnp.einsum("hcqaw,wxy->chqxay", t1, csel, precision=hi)
    valid = vr[:, None, :, None, :, None] & vc[None, None, None, :, None, :]
    b = jnp.where(valid, b, NEG_INF)
    return b.reshape(len(reps), N_HEADS, 2 * GRID_W, NA_BAND_ROWS * GRID_W)


def _ada_kernel(c_ref, w_ref, b_ref, o_ref):
    c = c_ref[...]
    s = c * _sigmoid(c)
    o_ref[0] = jnp.dot(s, w_ref[0], preferred_element_type=F32,
                       precision=lax.Precision.HIGHEST) + b_ref[0]


def _ada_call(cs, ada_w, ada_b):
    depth, d, n = ada_w.shape
    rows = cs.shape[0]
    tn = 1536
    return pl.pallas_call(
        _ada_kernel,
        grid=(depth, n // tn),
        in_specs=[
            pl.BlockSpec((rows, d), lambda l, j: (0, 0)),
            pl.BlockSpec((1, d, tn), lambda l, j: (l, 0, j)),
            pl.BlockSpec((1, 1, tn), lambda l, j: (l, 0, j)),
        ],
        out_specs=pl.BlockSpec((1, rows, tn), lambda l, j: (l, 0, j)),
        out_shape=jax.ShapeDtypeStruct((depth, rows, n), F32),
        compiler_params=_cparams(("arbitrary", "arbitrary")),
        name="ada",
    )(cs, ada_w, ada_b.reshape(depth, 1, n))


def _group_norm(x, ones_ref, inv_n):
    sq = (x * x).astype(BF16)
    ms = jnp.dot(sq, ones_ref[...], preferred_element_type=F32) * inv_n
    return x * lax.rsqrt(ms + EPS)


def _rope(x, rot_ref, cos_ref, sin_ref):
    rot = jnp.dot(x.astype(BF16), rot_ref[...], preferred_element_type=F32)
    return x * cos_ref[...] + rot * sin_ref[...]


def _proj_kernel(x_ref, mod_ref, n1g_ref, win_ref, gains_ref, wuq_ref, wk_ref, ppe_ref, wv_ref,
                 g96_ref, g32_ref, g64_ref, rm_ref, rd_ref, cm_ref, sm_ref, cd_ref, sd_ref,
                 u_ref, mq_ref, mk_ref, mv_ref, dq_ref, dk_ref, dv_ref, nq_ref, nk_ref, nv_ref,
                 *, use_rope):
    x = x_ref[0]
    mod = mod_ref[0]
    gains = gains_ref[...]
    h = _modulate(x, n1g_ref[...], mod[0:1], mod[1:2]).astype(BF16)
    proj = jnp.dot(h, win_ref[...], preferred_element_type=F32)

    u_ref[0] = proj[:, P_A:P_A + CONV_CH] * _sigmoid(proj[:, P_G:P_G + CONV_CH])

    cq = proj[:, P_CQ:P_CQ + 256]
    ms = jnp.sum(cq * cq, axis=-1, keepdims=True) * (1.0 / Q_LORA)
    cqn = (cq * lax.rsqrt(ms + EPS) * gains[0:1, :256]).astype(BF16)
    q = jnp.dot(cqn, wuq_ref[...], preferred_element_type=F32)
    q = _group_norm(q, g96_ref, 1.0 / MLA_QK) * gains[1:2, :]
    if use_rope:
        q = _rope(q, rm_ref, cm_ref, sm_ref)
    mq_ref[0] = q.astype(BF16)

    ckv = proj[:, P_CKV:P_CKV + KV_LORA]
    ms = jnp.mean(ckv * ckv, axis=-1, keepdims=True)
    ckvn = (ckv * lax.rsqrt(ms + EPS) * gains[2:3, :KV_LORA]).astype(BF16)
    kpe = proj[:, P_KPE:P_KPE + LANE].astype(BF16)
    k = (jnp.dot(ckvn, wk_ref[...], preferred_element_type=F32)
         + jnp.dot(kpe, ppe_ref[...], preferred_element_type=F32))
    k = _group_norm(k, g96_ref, 1.0 / MLA_QK) * gains[3:4, :]
    if use_rope:
        k = _rope(k, rm_ref, cm_ref, sm_ref)
    mk_ref[0] = k.astype(BF16)
    mv_ref[0] = jnp.dot(ckvn, wv_ref[...], preferred_element_type=F32).astype(BF16)

    qd = _group_norm(proj[:, P_DQ:P_DQ + BR_W], g32_ref, 1.0 / DIFF_DIM) * gains[4:5, :BR_W]
    kd = _group_norm(proj[:, P_DK:P_DK + BR_W], g32_ref, 1.0 / DIFF_DIM) * gains[5:6, :BR_W]
    if use_rope:
        qd = _rope(qd, rd_ref, cd_ref, sd_ref)
        kd = _rope(kd, rd_ref, cd_ref, sd_ref)
    dq_ref[0] = qd.astype(BF16)
    dk_ref[0] = kd.astype(BF16)
    dv_ref[0] = proj[:, P_DV:P_DV + BR_W].astype(BF16)

    qn = _group_norm(proj[:, P_NQ:P_NQ + BR_W], g64_ref, 1.0 / HEAD_DIM) * gains[6:7, :BR_W]
    kn = _group_norm(proj[:, P_NK:P_NK + BR_W], g64_ref, 1.0 / HEAD_DIM) * gains[7:8, :BR_W]
    nq_ref[0] = qn.astype(BF16)
    nk_ref[0] = kn.astype(BF16)
    nv_ref[0] = proj[:, P_NV:P_NV + BR_W].astype(BF16)


def _proj_call(x, mods, mod_row, lw, tabs, use_rope):
    b, s, d = x.shape
    t = min(s, 512)
    grid = (b, s // t)
    if mod_row is None:
        mod_map = lambda i, j: (i, 0, 0)
    else:
        mod_map = lambda i, j: (mod_row, 0, 0)
    tok = lambda w: pl.BlockSpec((1, t, w), lambda i, j: (i, j, 0))
    tab = lambda w: pl.BlockSpec((t, w), lambda i, j: (j, 0))
    in_specs = [
        tok(d),
        pl.BlockSpec((1, 6, d), mod_map),
        _const_spec((1, d)),
        _const_spec((d, PROJ_W)),
        _const_spec((8, 512)),
        _const_spec((256, 512)),
        _const_spec((KV_LORA, 512)),
        _const_spec((LANE, 512)),
        _const_spec((KV_LORA, BR_W)),
        _const_spec((512, 512)),
        _const_spec((BR_W, BR_W)),
        _const_spec((BR_W, BR_W)),
        _const_spec((512, 512)),
        _const_spec((BR_W, BR_W)),
        tab(512), tab(512), tab(BR_W), tab(BR_W),
    ]
    widths = [CONV_CH, 512, 512, BR_W, BR_W, BR_W, BR_W, BR_W, BR_W, BR_W]
    dtypes = [F32] + [BF16] * 9
    out_specs = [tok(w) for w in widths]
    out_shape = [jax.ShapeDtypeStruct((b, s, w), dt) for w, dt in zip(widths, dtypes)]
    return pl.pallas_call(
        functools.partial(_proj_kernel, use_rope=use_rope),
        grid=grid, in_specs=in_specs, out_specs=out_specs, out_shape=out_shape,
        compiler_params=_cparams(("parallel", "parallel")),
        name="proj",
    )(x, mods, lw["n1g"], lw["w_in"], lw["gains"], lw["wuq"], lw["wk"], lw["ppe"], lw["wv"],
      lw["g96"], lw["g32"], lw["g64"], lw["rm"], lw["rd"], tabs[0], tabs[1], tabs[2], tabs[3])


CONV_TILE = 128
CONV_PAD = 16


def _conv_kernel(u_ref, w_ref, cb_ref, lg_ref, lb_ref, o_ref, pad_ref, *, seq):
    zeros = jnp.zeros((CONV_PAD, CONV_CH), F32)
    pad_ref[0:CONV_PAD, :] = zeros
    pad_ref[CONV_PAD + seq:2 * CONV_PAD + seq, :] = zeros

    def fill(i, carry):
        base = pl.multiple_of(i * CONV_TILE, CONV_TILE)
        pad_ref[pl.ds(base + CONV_PAD, CONV_TILE), :] = u_ref[0, pl.ds(base, CONV_TILE), :]
        return carry

    lax.fori_loop(0, seq // CONV_TILE, fill, 0)
    w = w_ref[...]
    cb, lg, lb = cb_ref[...], lg_ref[...], lb_ref[...]

    def tile(i, carry):
        base = pl.multiple_of(i * CONV_TILE, CONV_TILE)
        win = pad_ref[pl.ds(base, CONV_TILE + 2 * CONV_PAD), :]
        acc = jnp.zeros((CONV_TILE, CONV_CH), F32)
        for j in range(CONV_WIDTH):
            acc = acc + win[j + 1:j + 1 + CONV_TILE, :] * w[j:j + 1, :]
        c = acc + cb
        mu = jnp.mean(c, axis=-1, keepdims=True)
        cc = c - mu
        var = jnp.mean(cc * cc, axis=-1, keepdims=True)
        y = cc * lax.rsqrt(var + EPS) * lg + lb
        o_ref[0, pl.ds(base, CONV_TILE), :] = (y * _sigmoid(y)).astype(BF16)
        return carry

    lax.fori_loop(0, seq // CONV_TILE, tile, 0)


def _conv_call(u, lw):
    b, s, ch = u.shape
    return pl.pallas_call(
        functools.partial(_conv_kernel, seq=s),
        grid=(b,),
        in_specs=[
            pl.BlockSpec((1, s, ch), lambda i: (i, 0, 0)),
            _const_spec((32, ch)), _const_spec((1, ch)), _const_spec((1, ch)), _const_spec((1, ch)),
        ],
        out_specs=pl.BlockSpec((1, s, ch), lambda i: (i, 0, 0)),
        out_shape=jax.ShapeDtypeStruct((b, s, ch), BF16),
        scratch_shapes=[pltpu.VMEM((s + 2 * CONV_PAD, ch), F32)],
        compiler_params=_cparams(("parallel",)),
        name="conv",
    )(u, lw["conv_w"], lw["conv_b"], lw["conv_ln_g"], lw["conv_ln_b"])


def _lane_mask(width, lo, hi):
    lane = lax.broadcasted_iota(jnp.int32, (1, width), 1)
    return (lane >= lo) & (lane < hi)


def _softmax_pv(qw, kt, v):
    s = jnp.dot(qw, kt, preferred_element_type=F32)
    m = jnp.max(s, axis=-1, keepdims=True)
    p = jnp.exp(s - m)
    l = jnp.sum(p, axis=-1, keepdims=True)
    return jnp.dot(p.astype(BF16), v, preferred_element_type=F32) * (1.0 / l)


def _attn_kernel(q_ref, kt_ref, v_ref, lam_ref, g64_ref, sg_ref, o_ref, *, maps, diff, lam_init):
    v = v_ref[0]
    tq = q_ref.shape[1]
    acc = jnp.zeros((tq, BR_W), F32)
    if diff:
        lv = lam_ref[...]
        lam = (jnp.exp(jnp.sum(lv[0:1] * lv[1:2], axis=-1, keepdims=True))
               - jnp.exp(jnp.sum(lv[2:3] * lv[3:4], axis=-1, keepdims=True)) + lam_init)
    for h in range(N_HEADS):
        outs = []
        for (w0, lo, hi) in maps[h]:
            qw = q_ref[0, :, w0:w0 + LANE]
            if (lo, hi) != (0, LANE):
                qw = jnp.where(_lane_mask(LANE, lo, hi), qw, jnp.zeros_like(qw))
            outs.append(_softmax_pv(qw, kt_ref[0, w0:w0 + LANE, :], v))
        o_h = outs[0] - lam * outs[1] if diff else outs[0]
        acc = jnp.where(_lane_mask(BR_W, V_HEAD * h, V_HEAD * (h + 1)), o_h, acc)
    if diff:
        acc = _group_norm(acc, g64_ref, 1.0 / DIFF_V) * sg_ref[...]
    o_ref[0] = acc.astype(BF16)


MAPS_MLA = tuple(((LANE * h, 0, LANE),) for h in range(N_HEADS))
MAPS_DIFF = tuple(tuple((LANE * (h // 2), 64 * (h % 2) + 32 * c, 64 * (h % 2) + 32 * c + 32) for c in range(2))
                  for h in range(N_HEADS))
MAPS_NA = tuple(((LANE * (h // 2), 64 * (h % 2), 64 * (h % 2) + 64),) for h in range(N_HEADS))


def _attn_call(q, kt, v, lw, maps, diff=False, lam_init=0.0):
    b, s, wq = q.shape
    sk = kt.shape[2]
    tq = min(s, 256)
    return pl.pallas_call(
        functools.partial(_attn_kernel, maps=maps, diff=diff, lam_init=lam_init),
        grid=(b, s // tq),
        in_specs=[
            pl.BlockSpec((1, tq, wq), lambda i, j: (i, j, 0)),
            pl.BlockSpec((1, wq, sk), lambda i, j: (i, 0, 0)),
            pl.BlockSpec((1, sk, BR_W), lambda i, j: (i, 0, 0)),
            _const_spec((4, DIFF_DIM)),
            _const_spec((BR_W, BR_W)),
            _const_spec((1, BR_W)),
        ],
        out_specs=pl.BlockSpec((1, tq, BR_W), lambda i, j: (i, j, 0)),
        out_shape=jax.ShapeDtypeStruct((b, s, BR_W), BF16),
        compiler_params=_cparams(("parallel", "parallel")),
        name="attn_diff" if diff else "attn",
    )(q, kt, v, lw["diff_lam"], lw["g64"], lw["subln"])


_NT = (((1,), (1,)), ((), ()))


def _na_kernel(q_ref, k_ref, v_ref, kc_ref, vc_ref, bias_ref, o_ref, *, rows):
    j = pl.program_id(1)
    nj = rows // 2
    start = jnp.clip(2 * j - 4, 0, rows - NA_BAND_ROWS)
    base = pl.multiple_of(start * GRID_W, 2 * GRID_W)
    cls = jnp.where(j < 2, j, jnp.where(j >= nj - 2, j - (nj - 2) + 3, 2))
    band = NA_BAND_ROWS * GRID_W
    kw = k_ref[0, pl.ds(base, band), :]
    vw = v_ref[0, pl.ds(base, band), :]
    kc = kc_ref[0]
    vc = vc_ref[0]
    q = q_ref[0]
    acc = jnp.zeros((2 * GRID_W, BR_W), F32)
    for h in range(N_HEADS):
        qm = jnp.where(_lane_mask(BR_W, HEAD_DIM * h, HEAD_DIM * (h + 1)), q, jnp.zeros_like(q))
        s_loc = lax.dot_general(qm, kw, _NT, preferred_element_type=F32) + bias_ref[cls, h]
        s_ctx = lax.dot_general(qm, kc, _NT, preferred_element_type=F32)
        m = jnp.maximum(jnp.max(s_loc, axis=-1, keepdims=True), jnp.max(s_ctx, axis=-1, keepdims=True))
        p_loc = jnp.exp(s_loc - m)
        p_ctx = jnp.exp(s_ctx - m)
        l = jnp.sum(p_loc, axis=-1, keepdims=True) + jnp.sum(p_ctx, axis=-1, keepdims=True)
        o = (jnp.dot(p_ctx.astype(BF16), vc, preferred_element_type=F32)
             + jnp.dot(p_loc.astype(BF16), vw, preferred_element_type=F32)) * (1.0 / l)
        acc = jnp.where(_lane_mask(BR_W, HEAD_DIM * h, HEAD_DIM * (h + 1)), o, acc)
    o_ref[0] = acc.astype(BF16)


def _na_call(q, k, v, kc, vc, bias):
    b, s, w = q.shape
    n_ctx = kc.shape[1]
    rows = s // GRID_W
    tq = 2 * GRID_W
    return pl.pallas_call(
        functools.partial(_na_kernel, rows=rows),
        grid=(b, rows // 2),
        in_specs=[
            pl.BlockSpec((1, tq, w), lambda i, j: (i, j, 0)),
            pl.BlockSpec((1, s, w), lambda i, j: (i, 0, 0)),
            pl.BlockSpec((1, s, w), lambda i, j: (i, 0, 0)),
            pl.BlockSpec((1, n_ctx, w), lambda i, j: (i, 0, 0)),
            pl.BlockSpec((1, n_ctx, w), lambda i, j: (i, 0, 0)),
            _const_spec(bias.shape),
        ],
        out_specs=pl.BlockSpec((1, tq, w), lambda i, j: (i, j, 0)),
        out_shape=jax.ShapeDtypeStruct((b, s, w), BF16),
        compiler_params=_cparams(("parallel", "arbitrary")),
        name="na",
    )(q, k, v, kc, vc, bias)


def _merge_kernel(x_ref, mod_ref, n1g_ref, n2g_ref, uc_ref, om_ref, od_ref, on_ref,
                  gw_ref, gb_ref, wc_ref, wm_ref, wd_ref, wn_ref, wo_ref, rwt_ref, rb_ref,
                  x1_ref, h2_ref, ids_ref, wts_ref):
    x = x_ref[0]
    mod = mod_ref[0]
    h = _modulate(x, n1g_ref[...], mod[0:1], mod[1:2]).astype(BF16)
    y = jnp.zeros(x.shape, F32)
    branches = ((uc_ref, wc_ref), (om_ref, wm_ref), (od_ref, wd_ref), (on_ref, wn_ref))
    for i, (o_ref, w_ref) in enumerate(branches):
        lo = D_MODEL * i
        g = _sigmoid(jnp.dot(h, gw_ref[:, lo:lo + D_MODEL], preferred_element_type=F32)
                     + gb_ref[:, lo:lo + D_MODEL])
        y = y + g * jnp.dot(o_ref[0], w_ref[...], preferred_element_type=F32)
    out = jnp.dot(y.astype(BF16), wo_ref[...], preferred_element_type=F32)
    x1 = x + mod[2:3] * out
    x1_ref[0] = x1
    h2 = _modulate(x1, n2g_ref[...], mod[3:4], mod[4:5])
    h2_ref[0] = h2

    logits = lax.dot_general(rwt_ref[...], h2, _NT, preferred_element_type=F32,
                             precision=lax.Precision.HIGHEST) + rb_ref[...]
    eidx = lax.broadcasted_iota(jnp.int32, logits.shape, 0).astype(F32)
    vals, idxs = [], []
    cur = logits
    for _ in range(TOP_K):
        m = jnp.max(cur, axis=0, keepdims=True)
        idx = jnp.min(jnp.where(cur == m, eidx, float(N_EXPERTS)), axis=0, keepdims=True)
        vals.append(m)
        idxs.append(idx)
        cur = jnp.where(eidx == idx, -jnp.inf, cur)
    es = [jnp.exp(vk - vals[0]) for vk in vals]
    den = es[0] + es[1] + es[2] + es[3]
    ids_ref[0] = jnp.concatenate(idxs, axis=0).astype(jnp.int32)
    wts_ref[0] = jnp.concatenate([e / den for e in es], axis=0)


def _merge_call(x, mods, mod_row, lw, uc, om, od, on):
    b, s, d = x.shape
    t = min(s, 512)
    if mod_row is None:
        mod_map = lambda i, j: (i, 0, 0)
    else:
        mod_map = lambda i, j: (mod_row, 0, 0)
    tok = lambda w: pl.BlockSpec((1, t, w), lambda i, j: (i, j, 0))
    rt = pl.BlockSpec((1, TOP_K, t), lambda i, j: (i, 0, j))
    return pl.pallas_call(
        _merge_kernel,
        grid=(b, s // t),
        in_specs=[
            tok(d), pl.BlockSpec((1, 6, d), mod_map), _const_spec((1, d)), _const_spec((1, d)),
            tok(BR_W), tok(BR_W), tok(BR_W), tok(BR_W),
            _const_spec((d, N_BRANCH * d)), _const_spec((1, N_BRANCH * d)),
            _const_spec((BR_W, d)), _const_spec((BR_W, d)), _const_spec((BR_W, d)), _const_spec((BR_W, d)),
            _const_spec((d, d)), _const_spec((N_EXPERTS, d)), _const_spec((N_EXPERTS, 1)),
        ],
        out_specs=[tok(d), tok(d), rt, rt],
        out_shape=[jax.ShapeDtypeStruct((b, s, d), F32), jax.ShapeDtypeStruct((b, s, d), F32),
                   jax.ShapeDtypeStruct((b, TOP_K, s), jnp.int32), jax.ShapeDtypeStruct((b, TOP_K, s), F32)],
        compiler_params=_cparams(("parallel", "parallel")),
        name="merge",
    )(x, mods, lw["n1g"], lw["n2g"], uc, om, od, on, lw["gate_w"], lw["gate_b"],
      lw["conv_out"], lw["mla_out"], lw["diff_out"], lw["na_out"], lw["w_o"], lw["router_wt"], lw["router_b"])


def _route(ids, tile):
    n = ids.shape[1]
    p = TOP_K * n
    e = ids.reshape(p)
    onehot = (e[:, None] == jnp.arange(N_EXPERTS, dtype=jnp.int32)[None, :]).astype(jnp.int32)
    csum = jnp.cumsum(onehot, axis=0)
    counts = csum[-1]
    padded = ((counts + tile - 1) // tile) * tile
    gend = jnp.cumsum(padded)
    gstart = gend - padded
    slot = jnp.sum(onehot * (csum - 1 + gstart[None, :]), axis=1).astype(jnp.int32)
    n_tiles = p // tile + N_EXPERTS
    n_slots = n_tiles * tile
    pair_of_slot = jnp.full((n_slots,), -1, jnp.int32).at[slot].set(
        jnp.arange(p, dtype=jnp.int32), unique_indices=True)
    real = pair_of_slot >= 0
    src_tok = jnp.where(real, pair_of_slot % n, 0)
    dst_row = jnp.where(real, pair_of_slot, p + jnp.arange(n_slots, dtype=jnp.int32))
    tile_start = jnp.arange(n_tiles, dtype=jnp.int32) * tile
    texp = jnp.minimum(jnp.searchsorted(gend, tile_start, side="right"), N_EXPERTS - 1).astype(jnp.int32)
    tval = (tile_start < gend[-1]).astype(jnp.int32)
    return src_tok.reshape(n_tiles, 1, tile), dst_row.reshape(n_tiles, 1, tile), texp, tval


def _ffn_kernel(texp_ref, tval_ref, src_ref, nsrc_ref, dst_ref, pdst_ref, h_hbm, wgu_ref, bgu_ref, wd_ref, bd_ref,
                y_hbm, xbuf, ybuf, sem_in, sem_out, *, tile, n_tiles):
    i = pl.program_id(0)
    slot = i % 2
    valid = tval_ref[i] > 0
    prev_valid = (i >= 1) & (tval_ref[jnp.maximum(i - 1, 0)] > 0)

    def gather_row(idx_ref, buf_slot, r):
        pltpu.make_async_copy(h_hbm.at[pl.ds(idx_ref[0, 0, r], 1)], xbuf.at[buf_slot, pl.ds(r, 1)],
                              sem_in.at[buf_slot]).start()

    def scatter_row(idx_ref, buf_slot, r):
        pltpu.make_async_copy(ybuf.at[buf_slot, pl.ds(r, 1)], y_hbm.at[pl.ds(idx_ref[0, 0, r], 1)],
                              sem_out.at[buf_slot]).start()

    def rolled(fn):
        def body(r, carry):
            fn(r)
            return carry

        lax.fori_loop(0, tile, body, 0, unroll=8)

    def wait_rows(buf, sem):
        pltpu.make_async_copy(h_hbm.at[pl.ds(0, tile)], buf, sem).wait()

    def ffn(x):
        gu = jnp.dot(x, wgu_ref[0], preferred_element_type=F32) + bgu_ref[0]
        g = jnp.minimum(gu[:, :D_FF], SWIGLU_LIMIT)
        u = jnp.clip(gu[:, D_FF:], -SWIGLU_LIMIT, SWIGLU_LIMIT)
        act = ((u + 1.0) * (g * _sigmoid(SWIGLU_ALPHA * g))).astype(BF16)
        return jnp.dot(act, wd_ref[0], preferred_element_type=F32) + bd_ref[0]

    @pl.when((i == 0) & valid)
    def _():
        rolled(lambda r: gather_row(src_ref, 0, r))
        wait_rows(xbuf.at[0], sem_in.at[0])
        x = xbuf[0].astype(BF16)
        rolled(lambda r: gather_row(nsrc_ref, 1, r))
        ybuf[0] = ffn(x)

    @pl.when((i >= 2) & valid & prev_valid)
    def _():
        wait_rows(ybuf.at[slot], sem_out.at[slot])

    @pl.when((i >= 1) & valid)
    def _():
        wait_rows(xbuf.at[slot], sem_in.at[slot])
        n_chunk = (2 * D_FF) // FFN_CHUNK
        per = tile // n_chunk
        gu = []
        for c in range(n_chunk):
            x = xbuf[slot].astype(BF16)
            for r in range(c * per, (c + 1) * per):
                gather_row(nsrc_ref, 1 - slot, r)
                scatter_row(pdst_ref, 1 - slot, r)
            lo = c * FFN_CHUNK
            gu.append(jnp.dot(x, wgu_ref[0, :, lo:lo + FFN_CHUNK], preferred_element_type=F32)
                      + bgu_ref[0, :, lo:lo + FFN_CHUNK])
        half = n_chunk // 2
        act = []
        for c in range(half):
            g = jnp.minimum(gu[c], SWIGLU_LIMIT)
            u = jnp.clip(gu[half + c], -SWIGLU_LIMIT, SWIGLU_LIMIT)
            act.append(((u + 1.0) * (g * _sigmoid(SWIGLU_ALPHA * g))).astype(BF16))
        act = jnp.concatenate(act, axis=1)
        for c in range(D_MODEL // FFN_CHUNK):
            lo = c * FFN_CHUNK
            ybuf[slot, :, lo:lo + FFN_CHUNK] = (
                jnp.dot(act, wd_ref[0, :, lo:lo + FFN_CHUNK], preferred_element_type=F32)
                + bd_ref[0, :, lo:lo + FFN_CHUNK])

    def drain(last_slot):
        wait_rows(xbuf.at[1 - last_slot], sem_in.at[1 - last_slot])
        wait_rows(ybuf.at[last_slot], sem_out.at[last_slot])

    @pl.when(jnp.logical_not(valid) & prev_valid)
    def _():
        @pl.when(i >= 2)
        def _():
            wait_rows(ybuf.at[slot], sem_out.at[slot])

        rolled(lambda r: scatter_row(pdst_ref, 1 - slot, r))
        drain(1 - slot)

    @pl.when((i == n_tiles - 1) & valid)
    def _():
        wait_rows(ybuf.at[1 - slot], sem_out.at[1 - slot])
        rolled(lambda r: scatter_row(dst_ref, slot, r))
        drain(slot)


def _ffn_call(h2, src_tok, dst_row, texp, tval, lw, tile, n_out_rows):
    n_tiles = src_tok.shape[0]
    d = h2.shape[1]
    idx_spec = lambda f: pl.BlockSpec((1, 1, tile), f, memory_space=pltpu.SMEM)
    grid_spec = pltpu.PrefetchScalarGridSpec(
        num_scalar_prefetch=2,
        grid=(n_tiles,),
        in_specs=[
            idx_spec(lambda i, te, tv: (i, 0, 0)),
            idx_spec(lambda i, te, tv: (jnp.minimum(i + 1, n_tiles - 1), 0, 0)),
            idx_spec(lambda i, te, tv: (i, 0, 0)),
            idx_spec(lambda i, te, tv: (jnp.maximum(i - 1, 0), 0, 0)),
            pl.BlockSpec(memory_space=pl.ANY),
            pl.BlockSpec((1, d, 2 * D_FF), lambda i, te, tv: (te[i], 0, 0)),
            pl.BlockSpec((1, 1, 2 * D_FF), lambda i, te, tv: (te[i], 0, 0)),
            pl.BlockSpec((1, D_FF, d), lambda i, te, tv: (te[i], 0, 0)),
            pl.BlockSpec((1, 1, d), lambda i, te, tv: (te[i], 0, 0)),
        ],
        out_specs=pl.BlockSpec(memory_space=pl.ANY),
        scratch_shapes=[pltpu.VMEM((2, tile, d), F32), pltpu.VMEM((2, tile, d), F32),
                        pltpu.SemaphoreType.DMA((2,)), pltpu.SemaphoreType.DMA((2,))],
    )
    return pl.pallas_call(
        functools.partial(_ffn_kernel, tile=tile, n_tiles=n_tiles),
        grid_spec=grid_spec,
        out_shape=jax.ShapeDtypeStruct((n_out_rows, d), F32),
        compiler_params=_cparams(("arbitrary",)),
        name="moe_ffn",
    )(texp, tval, src_tok, src_tok, dst_row, dst_row, h2, lw["exp_w_gu"], lw["exp_b_gu"], lw["exp_w_down"], lw["exp_b_down"])


def _combine_kernel(x1_ref, mod_ref, w_ref, y0_ref, y1_ref, y2_ref, y3_ref, o_ref):
    w = w_ref[0]
    acc = w[:, 0:1] * y0_ref[...]
    for k, y_ref in ((1, y1_ref), (2, y2_ref), (3, y3_ref)):
        acc = acc + w[:, k:k + 1] * y_ref[...]
    o_ref[0] = x1_ref[0] + mod_ref[0][5:6] * acc


def _combine_call(x1, mods, mod_row, wts, y):
    b, s, d = x1.shape
    t = min(s, 512)
    nt = s // t
    if mod_row is None:
        mod_map = lambda i, j: (i, 0, 0)
    else:
        mod_map = lambda i, j: (mod_row, 0, 0)
    wts = wts.transpose(0, 2, 1)
    y_spec = lambda k: pl.BlockSpec((t, d), lambda i, j: (k * b * nt + i * nt + j, 0))
    return pl.pallas_call(
        _combine_kernel,
        grid=(b, nt),
        in_specs=[
            pl.BlockSpec((1, t, d), lambda i, j: (i, j, 0)),
            pl.BlockSpec((1, 6, d), mod_map),
            pl.BlockSpec((1, t, TOP_K), lambda i, j: (i, j, 0)),
            y_spec(0), y_spec(1), y_spec(2), y_spec(3),
        ],
        out_specs=pl.BlockSpec((1, t, d), lambda i, j: (i, j, 0)),
        out_shape=jax.ShapeDtypeStruct((b, s, d), F32),
        compiler_params=_cparams(("parallel", "parallel")),
        name="moe_combine",
    )(x1, mods, wts, y, y, y, y)


def _moe(x1, h2, ids, wts, mods, mod_row, lw):
    b, s, d = x1.shape
    n = b * s
    tile = 512 if TOP_K * n >= 512 * N_EXPERTS * 4 else 256
    ids_flat = ids.transpose(1, 0, 2).reshape(TOP_K, n)
    src_tok, dst_row, texp, tval = _route(ids_flat, tile)
    n_out_rows = TOP_K * n + src_tok.shape[0] * tile
    y = _ffn_call(h2.reshape(n, d), src_tok, dst_row, texp, tval, lw, tile, n_out_rows)
    return _combine_call(x1, mods, mod_row, wts, y)


def _layer_weights(l, p, lam_init):
    w = p["w_in"][l]
    d = w.shape[0]
    zcols = lambda n: jnp.zeros((d, n), w.dtype)
    regroup = lambda blk: blk.reshape(d, N_HEADS, 3, HEAD_DIM).transpose(0, 2, 1, 3).reshape(d, 3 * BR_W)
    w_in = jnp.concatenate([
        w[:, :A_IN],
        w[:, OFF_B:OFF_B + Q_LORA], zcols(P_CKV - P_CQ - Q_LORA),
        w[:, OFF_B + Q_LORA:OFF_B + Q_LORA + KV_LORA],
        w[:, OFF_B + Q_LORA + KV_LORA:OFF_C], zcols(P_DQ - P_KPE - QK_ROPE),
        regroup(w[:, OFF_C:OFF_D]), regroup(w[:, OFF_D:]),
    ], axis=1)
    assert w_in.shape[1] == PROJ_W

    def head_slots(w, src_w, src_off, take, slot, dst_off=0):
        out = jnp.zeros((w.shape[0], N_HEADS * slot), w.dtype)
        for h in range(N_HEADS):
            out = out.at[:, h * slot + dst_off:h * slot + dst_off + take].set(
                w[:, h * src_w + src_off:h * src_w + src_off + take])
        return out

    wuq = head_slots(p["mla_w_uq"][l], MLA_QK, 0, MLA_QK, LANE)
    wuq = jnp.concatenate([wuq, jnp.zeros((256 - Q_LORA, 512), F32)], axis=0)
    wk = head_slots(p["mla_w_ukv"][l], QK_NOPE + V_HEAD, 0, QK_NOPE, LANE)
    wv = head_slots(p["mla_w_ukv"][l], QK_NOPE + V_HEAD, QK_NOPE, V_HEAD, V_HEAD)
    ppe = np.zeros((LANE, 512), np.float32)
    for h in range(N_HEADS):
        for i in range(QK_ROPE):
            ppe[i, h * LANE + QK_NOPE + i] = 1.0

    def slot_gain(g, scale):
        g = jnp.concatenate([g * scale, jnp.zeros((LANE - MLA_QK,), F32)])
        return jnp.tile(g, N_HEADS)

    def row512(v):
        return jnp.concatenate([v, jnp.zeros((512 - v.shape[0],), F32)])

    gains = jnp.stack([
        row512(p["mla_cq_g"][l]),
        slot_gain(p["mla_qn_g"][l], MLA_QK ** -0.5),
        row512(p["mla_ckv_g"][l]),
        slot_gain(p["mla_kn_g"][l], 1.0),
        row512(jnp.tile(p["diff_qn_g"][l], 2 * N_HEADS) * DIFF_DIM ** -0.5),
        row512(jnp.tile(p["diff_kn_g"][l], 2 * N_HEADS)),
        row512(jnp.tile(p["na_qn_g"][l], N_HEADS) * HEAD_DIM ** -0.5),
        row512(jnp.tile(p["na_kn_g"][l], N_HEADS)),
    ])
    conv_w = jnp.concatenate([p["conv_w"][l], jnp.zeros((1, CONV_CH), F32)], axis=0)
    return dict(
        n1g=p["norm1_g"][l][None, :], n2g=p["norm2_g"][l][None, :],
        w_in=w_in.astype(BF16), gains=gains,
        wuq=wuq.astype(BF16), wk=wk.astype(BF16), wv=wv.astype(BF16), ppe=jnp.asarray(ppe, BF16),
        g96=jnp.asarray(_group_ones(512, LANE, MLA_QK), BF16),
        g32=jnp.asarray(_group_ones(BR_W, DIFF_DIM, DIFF_DIM), BF16),
        g64=jnp.asarray(_group_ones(BR_W, HEAD_DIM, HEAD_DIM), BF16),
        rm=jnp.asarray(_rot_matrix(512, LANE, QK_NOPE, QK_ROPE // 2), BF16),
        rd=jnp.asarray(_rot_matrix(BR_W, DIFF_DIM, 0, DIFF_DIM // 2), BF16),
        conv_w=conv_w, conv_b=p["conv_b"][l][None, :],
        conv_ln_g=p["conv_ln_g"][l][None, :], conv_ln_b=p["conv_ln_b"][l][None, :],
        diff_lam=p["diff_lam"][l],
        subln=(jnp.tile(p["diff_subln_g"][l], N_HEADS) * (1.0 - lam_init))[None, :],
        gate_w=p["gate_w"][l].astype(BF16), gate_b=p["gate_b"][l][None, :],
        conv_out=p["conv_out"][l].astype(BF16), mla_out=p["mla_out"][l].astype(BF16),
        diff_out=p["diff_out"][l].astype(BF16), na_out=p["na_out"][l].astype(BF16),
        w_o=p["w_o"][l].astype(BF16),
        router_wt=p["router_w"][l].T, router_b=p["router_b"][l][:, None],
        exp_w_gu=p["exp_w_gu"][l].astype(BF16), exp_b_gu=p["exp_b_gu"][l][:, None, :],
        exp_w_down=p["exp_w_down"][l].astype(BF16), exp_b_down=p["exp_b_down"][l][:, None, :],
    )


def _kt(kc, k):
    return jnp.concatenate([kc, k], axis=1).transpose(0, 2, 1)


def kernel(x, c, ctx, c_ctx, ada_w, ada_b, norm1_g, norm2_g, w_in, conv_w, conv_b, conv_ln_g, conv_ln_b, conv_out, mla_cq_g, mla_ckv_g, mla_w_uq, mla_w_ukv, mla_qn_g, mla_kn_g, mla_out, diff_qn_g, diff_kn_g, diff_lam, diff_subln_g, diff_out, na_qn_g, na_kn_g, na_rpb, na_out, gate_w, gate_b, w_o, router_w, router_b, exp_w_gu, exp_b_gu, exp_w_down, exp_b_down):
    p = dict(norm1_g=norm1_g, norm2_g=norm2_g, w_in=w_in, conv_w=conv_w, conv_b=conv_b,
             conv_ln_g=conv_ln_g, conv_ln_b=conv_ln_b, conv_out=conv_out, mla_cq_g=mla_cq_g,
             mla_ckv_g=mla_ckv_g, mla_w_uq=mla_w_uq, mla_w_ukv=mla_w_ukv, mla_qn_g=mla_qn_g,
             mla_kn_g=mla_kn_g, mla_out=mla_out, diff_qn_g=diff_qn_g, diff_kn_g=diff_kn_g,
             diff_lam=diff_lam, diff_subln_g=diff_subln_g, diff_out=diff_out, na_qn_g=na_qn_g,
             na_kn_g=na_kn_g, na_out=na_out, gate_w=gate_w, gate_b=gate_b, w_o=w_o,
             router_w=router_w, router_b=router_b, exp_w_gu=exp_w_gu, exp_b_gu=exp_b_gu,
             exp_w_down=exp_w_down, exp_b_down=exp_b_down)
    b, s, d = x.shape
    n_ctx = ctx.shape[1]
    depth = ada_w.shape[0]
    rows = s // GRID_W
    assert d == D_MODEL and s % (2 * GRID_W) == 0 and rows >= NA_BAND_ROWS and n_ctx % LANE == 0

    mod_rows = -(-(b + 1) // 8) * 8
    cs = jnp.concatenate([c, c_ctx[None, :], jnp.zeros((mod_rows - b - 1, d), F32)], axis=0)
    mods_all = _ada_call(cs, ada_w, ada_b).reshape(depth, mod_rows, 6, d)

    tabs_x = _rope_lane_tables(s)
    tabs_c = (jnp.ones((n_ctx, 512), F32), jnp.zeros((n_ctx, 512), F32),
              jnp.ones((n_ctx, BR_W), F32), jnp.zeros((n_ctx, BR_W), F32))

    xc = ctx
    for l in range(depth):
        last = l == depth - 1
        lam_init = 0.8 - 0.6 * math.exp(-0.3 * l)
        lw = _layer_weights(l, p, lam_init)
        mods = mods_all[l]
        bias = _na_bias_tables(na_rpb[l], rows)

        u, mq, mk, mv, dq, dk, dv, nq, nk, nv = _proj_call(x, mods, None, lw, tabs_x, True)
        uc, mqc, mkc, mvc, dqc, dkc, dvc, nqc, nkc, nvc = _proj_call(xc, mods, b, lw, tabs_c, False)

        y_conv = _conv_call(u, lw)
        y_mla = _attn_call(mq, _kt(mkc, mk), jnp.concatenate([mvc, mv], axis=1), lw, MAPS_MLA)
        y_diff = _attn_call(dq, _kt(dkc, dk), jnp.concatenate([dvc, dv], axis=1), lw, MAPS_DIFF,
                            diff=True, lam_init=lam_init)
        y_na = _na_call(nq, nk, nv, nkc, nvc, bias)
        x1, h2, ids, wts = _merge_call(x, mods, None, lw, y_conv, y_mla, y_diff, y_na)
        x = _moe(x1, h2, ids, wts, mods, None, lw)

        if not last:
            yc_conv = _conv_call(uc, lw)
            yc_mla = _attn_call(mqc, mkc.transpose(0, 2, 1), mvc, lw, MAPS_MLA)
            yc_diff = _attn_call(dqc, dkc.transpose(0, 2, 1), dvc, lw, MAPS_DIFF, diff=True, lam_init=lam_init)
            yc_na = _attn_call(nqc, nkc.transpose(0, 2, 1), nvc, lw, MAPS_NA)
            xc1, h2c, idsc, wtsc = _merge_call(xc, mods, b, lw, yc_conv, yc_mla, yc_diff, yc_na)
            xc = _moe(xc1, h2c, idsc, wtsc, mods, b, lw)
    return x
```

```python
import functools
import math

import numpy as np
import jax
import jax.numpy as jnp
from jax import lax
from jax.experimental import pallas as pl
from jax.experimental.pallas import tpu as pltpu

F32 = jnp.float32
BF16 = jnp.bfloat16

D_MODEL = 1024
GRID_W = 64
N_BRANCH = 4
N_HEADS = 4
HEAD_DIM = 64
CONV_CH = 256
CONV_WIDTH = 31
Q_LORA = 192
KV_LORA = 128
QK_NOPE = 64
QK_ROPE = 32
V_HEAD = 64
DIFF_DIM = 32
DIFF_V = 2 * DIFF_DIM
NA_KH = 8
NA_KW = 16
ROPE_DIM = 32
ROPE_BASE = 10000.0
N_EXPERTS = 32
TOP_K = 4
D_FF = 1024
SWIGLU_LIMIT = 7.0
SWIGLU_ALPHA = 1.702
EPS = 1e-6
NEG_INF = -1e30

A_IN = 2 * CONV_CH
B_IN = Q_LORA + KV_LORA + QK_ROPE
C_IN = N_HEADS * (4 * DIFF_DIM + DIFF_V)
D_IN = N_HEADS * 3 * HEAD_DIM
OFF_B = A_IN
OFF_C = OFF_B + B_IN
OFF_D = OFF_C + C_IN

LANE = 128
MLA_QK = QK_NOPE + QK_ROPE
BR_W = N_HEADS * HEAD_DIM
PROJ_W = 2560
NA_BAND_ROWS = 10
FFN_CHUNK = 256
ATTN_TQ = 512
LOG2E = math.log2(math.e)
VMEM_LIMIT = 52 * 1024 * 1024

P_A, P_G, P_CQ, P_CKV, P_KPE = 0, 256, 512, 768, 896
P_DQ, P_DK, P_DV = 1024, 1280, 1536
P_NQ, P_NK, P_NV = 1792, 2048, 2304


def _sigmoid(x):
    return 1.0 / (1.0 + jnp.exp(-x))


def _modulate(x, g, shift, scale):
    ms = jnp.mean(x * x, axis=-1, keepdims=True)
    return (x * lax.rsqrt(ms + EPS) * g) * (1.0 + scale) + shift


def _cparams(sem):
    return pltpu.CompilerParams(dimension_semantics=sem, vmem_limit_bytes=VMEM_LIMIT)


def _const_spec(shape):
    n = len(shape)
    return pl.BlockSpec(shape, lambda *_: (0,) * n)


def _group_ones(width, slot, real):
    i = np.arange(width)
    valid = (i % slot) < real
    same = (i[:, None] // slot) == (i[None, :] // slot)
    return (same & valid[:, None] & valid[None, :]).astype(np.float32)


def _rot_matrix(width, slot, start, half):
    r = np.zeros((width, width), np.float32)
    for s0 in range(0, width, slot):
        for i in range(half):
            a, b = s0 + start + i, s0 + start + half + i
            r[b, a] = -1.0
            r[a, b] = 1.0
    return r


def _rope_lane_tables(n_tokens):
    t = jnp.arange(n_tokens, dtype=jnp.int32)
    rows = (t // GRID_W).astype(F32)
    cols = (t % GRID_W).astype(F32)
    axis_dim = ROPE_DIM // 2
    inv = ROPE_BASE ** (-jnp.arange(0, axis_dim, 2, dtype=F32) / axis_dim)
    theta = jnp.concatenate([rows[:, None] * inv, cols[:, None] * inv], axis=-1)
    cos, sin = jnp.cos(theta), jnp.sin(theta)
    half = ROPE_DIM // 2
    ones = jnp.ones((n_tokens, QK_NOPE), F32)
    zeros = jnp.zeros((n_tokens, QK_NOPE), F32)
    pad1 = jnp.ones((n_tokens, LANE - MLA_QK), F32)
    pad0 = jnp.zeros((n_tokens, LANE - MLA_QK), F32)
    cm = jnp.tile(jnp.concatenate([ones, cos, cos, pad1], -1), (1, N_HEADS))
    sm = jnp.tile(jnp.concatenate([zeros, sin, sin, pad0], -1), (1, N_HEADS))
    cd = jnp.tile(jnp.concatenate([cos, cos], -1), (1, 2 * N_HEADS))
    sd = jnp.tile(jnp.concatenate([sin, sin], -1), (1, 2 * N_HEADS))
    assert half * 2 == DIFF_DIM
    return cm, sm, cd, sd


def _na_bias_tables(rpb, rows):
    kh = min(NA_KH, rows)
    nj = rows // 2
    reps = np.array([0, 1, 2, nj - 2, nj - 1])
    n_ro, n_co = 2 * NA_KH - 1, 2 * NA_KW - 1
    start = np.clip(2 * reps - 4, 0, rows - NA_BAND_ROWS)
    r = 2 * reps[:, None] + np.arange(2)[None, :]
    kr = start[:, None] + np.arange(NA_BAND_ROWS)[None, :]
    row_start = np.clip(r - kh // 2, 0, rows - kh)
    vr = (kr[:, None, :] >= row_start[:, :, None]) & (kr[:, None, :] < row_start[:, :, None] + kh)
    ro = np.clip(kr[:, None, :] - r[:, :, None] + NA_KH - 1, 0, n_ro - 1)
    qc = np.arange(GRID_W)
    win_start = np.clip(qc - NA_KW // 2, 0, GRID_W - NA_KW)
    vc = (qc[None, :] >= win_start[:, None]) & (qc[None, :] < win_start[:, None] + NA_KW)
    co = np.clip(qc[None, :] - qc[:, None] + NA_KW - 1, 0, n_co - 1)
    rsel = (ro[..., None] == np.arange(n_ro)).astype(np.float32)
    csel = (co[None] == np.arange(n_co)[:, None, None]).astype(np.float32)
    hi = lax.Precision.HIGHEST
    t1 = jnp.einsum("cqav,hvw->hcqaw", rsel, rpb.astype(F32), precision=hi)
    b = jnp.einsum("hcqaw,wxy->chqxay", t1, csel, precision=hi)
    valid = vr[:, None, :, None, :, None] & vc[None, None, None, :, None, :]
    b = jnp.where(valid, b * LOG2E, NEG_INF)
    return b.reshape(len(reps), N_HEADS, 2 * GRID_W, NA_BAND_ROWS * GRID_W)


def _ada_kernel(c_ref, w_ref, b_ref, o_ref):
    c = c_ref[...]
    s = c * _sigmoid(c)
    o_ref[0] = jnp.dot(s, w_ref[0], preferred_element_type=F32,
                       precision=lax.Precision.HIGHEST) + b_ref[0]


def _ada_call(cs, ada_w, ada_b):
    depth, d, n = ada_w.shape
    rows = cs.shape[0]
    tn = 1536
    return pl.pallas_call(
        _ada_kernel,
        grid=(depth, n // tn),
        in_specs=[
            pl.BlockSpec((rows, d), lambda l, j: (0, 0)),
            pl.BlockSpec((1, d, tn), lambda l, j: (l, 0, j)),
            pl.BlockSpec((1, 1, tn), lambda l, j: (l, 0, j)),
        ],
        out_specs=pl.BlockSpec((1, rows, tn), lambda l, j: (l, 0, j)),
        out_shape=jax.ShapeDtypeStruct((depth, rows, n), F32),
        compiler_params=_cparams(("arbitrary", "arbitrary")),
        name="ada",
    )(cs, ada_w, ada_b.reshape(depth, 1, n))


def _group_norm(x, ones_ref, inv_n):
    sq = (x * x).astype(BF16)
    ms = jnp.dot(sq, ones_ref[...], preferred_element_type=F32) * inv_n
    return x * lax.rsqrt(ms + EPS)


def _rope(x, rot_ref, cos_ref, sin_ref):
    rot = jnp.dot(x.astype(BF16), rot_ref[...], preferred_element_type=F32)
    return x * cos_ref[...] + rot * sin_ref[...]


def _proj_kernel(x_ref, mod_ref, n1g_ref, win_ref, gains_ref, wuq_ref, wk_ref, ppe_ref, wv_ref,
                 g96_ref, g32_ref, g64_ref, rm_ref, rd_ref, cm_ref, sm_ref, cd_ref, sd_ref,
                 u_ref, mq_ref, mk_ref, mv_ref, dq_ref, dk_ref, dv_ref, nq_ref, nk_ref, nv_ref,
                 *, use_rope):
    x = x_ref[0]
    mod = mod_ref[0]
    gains = gains_ref[...]
    h = _modulate(x, n1g_ref[...], mod[0:1], mod[1:2]).astype(BF16)
    proj = jnp.dot(h, win_ref[...], preferred_element_type=F32)

    u_ref[0] = proj[:, P_A:P_A + CONV_CH] * _sigmoid(proj[:, P_G:P_G + CONV_CH])

    cq = proj[:, P_CQ:P_CQ + 256]
    ms = jnp.sum(cq * cq, axis=-1, keepdims=True) * (1.0 / Q_LORA)
    cqn = (cq * lax.rsqrt(ms + EPS) * gains[0:1, :256]).astype(BF16)
    q = jnp.dot(cqn, wuq_ref[...], preferred_element_type=F32)
    q = _group_norm(q, g96_ref, 1.0 / MLA_QK) * gains[1:2, :]
    if use_rope:
        q = _rope(q, rm_ref, cm_ref, sm_ref)
    mq_ref[0] = q.astype(BF16)

    ckv = proj[:, P_CKV:P_CKV + KV_LORA]
    ms = jnp.mean(ckv * ckv, axis=-1, keepdims=True)
    ckvn = (ckv * lax.rsqrt(ms + EPS) * gains[2:3, :KV_LORA]).astype(BF16)
    kpe = proj[:, P_KPE:P_KPE + LANE].astype(BF16)
    k = (jnp.dot(ckvn, wk_ref[...], preferred_element_type=F32)
         + jnp.dot(kpe, ppe_ref[...], preferred_element_type=F32))
    k = _group_norm(k, g96_ref, 1.0 / MLA_QK) * gains[3:4, :]
    if use_rope:
        k = _rope(k, rm_ref, cm_ref, sm_ref)
    mk_ref[0] = k.astype(BF16)
    mv_ref[0] = jnp.dot(ckvn, wv_ref[...], preferred_element_type=F32).astype(BF16)

    qd = _group_norm(proj[:, P_DQ:P_DQ + BR_W], g32_ref, 1.0 / DIFF_DIM) * gains[4:5, :BR_W]
    kd = _group_norm(proj[:, P_DK:P_DK + BR_W], g32_ref, 1.0 / DIFF_DIM) * gains[5:6, :BR_W]
    if use_rope:
        qd = _rope(qd, rd_ref, cd_ref, sd_ref)
        kd = _rope(kd, rd_ref, cd_ref, sd_ref)
    dq_ref[0] = qd.astype(BF16)
    dk_ref[0] = kd.astype(BF16)
    dv_ref[0] = proj[:, P_DV:P_DV + BR_W].astype(BF16)

    qn = _group_norm(proj[:, P_NQ:P_NQ + BR_W], g64_ref, 1.0 / HEAD_DIM) * gains[6:7, :BR_W]
    kn = _group_norm(proj[:, P_NK:P_NK + BR_W], g64_ref, 1.0 / HEAD_DIM) * gains[7:8, :BR_W]
    nq_ref[0] = qn.astype(BF16)
    nk_ref[0] = kn.astype(BF16)
    nv_ref[0] = proj[:, P_NV:P_NV + BR_W].astype(BF16)


def _proj_call(x, mods, mod_row, lw, tabs, use_rope):
    b, s, d = x.shape
    t = min(s, 512)
    grid = (b, s // t)
    if mod_row is None:
        mod_map = lambda i, j: (i, 0, 0)
    else:
        mod_map = lambda i, j: (mod_row, 0, 0)
    tok = lambda w: pl.BlockSpec((1, t, w), lambda i, j: (i, j, 0))
    tab = lambda w: pl.BlockSpec((t, w), lambda i, j: (j, 0))
    in_specs = [
        tok(d),
        pl.BlockSpec((1, 6, d), mod_map),
        _const_spec((1, d)),
        _const_spec((d, PROJ_W)),
        _const_spec((8, 512)),
        _const_spec((256, 512)),
        _const_spec((KV_LORA, 512)),
        _const_spec((LANE, 512)),
        _const_spec((KV_LORA, BR_W)),
        _const_spec((512, 512)),
        _const_spec((BR_W, BR_W)),
        _const_spec((BR_W, BR_W)),
        _const_spec((512, 512)),
        _const_spec((BR_W, BR_W)),
        tab(512), tab(512), tab(BR_W), tab(BR_W),
    ]
    widths = [CONV_CH, 512, 512, BR_W, BR_W, BR_W, BR_W, BR_W, BR_W, BR_W]
    dtypes = [F32] + [BF16] * 9
    out_specs = [tok(w) for w in widths]
    out_shape = [jax.ShapeDtypeStruct((b, s, w), dt) for w, dt in zip(widths, dtypes)]
    return pl.pallas_call(
        functools.partial(_proj_kernel, use_rope=use_rope),
        grid=grid, in_specs=in_specs, out_specs=out_specs, out_shape=out_shape,
        compiler_params=_cparams(("parallel", "parallel")),
        name="proj",
    )(x, mods, lw["n1g"], lw["w_in"], lw["gains"], lw["wuq"], lw["wk"], lw["ppe"], lw["wv"],
      lw["g96"], lw["g32"], lw["g64"], lw["rm"], lw["rd"], tabs[0], tabs[1], tabs[2], tabs[3])


CONV_TILE = 128
CONV_PAD = 16


def _conv_kernel(u_ref, w_ref, cb_ref, lg_ref, lb_ref, o_ref, pad_ref, *, seq):
    zeros = jnp.zeros((CONV_PAD, CONV_CH), F32)
    pad_ref[0:CONV_PAD, :] = zeros
    pad_ref[CONV_PAD + seq:2 * CONV_PAD + seq, :] = zeros

    def fill(i, carry):
        base = pl.multiple_of(i * CONV_TILE, CONV_TILE)
        pad_ref[pl.ds(base + CONV_PAD, CONV_TILE), :] = u_ref[0, pl.ds(base, CONV_TILE), :]
        return carry

    lax.fori_loop(0, seq // CONV_TILE, fill, 0)
    w = w_ref[...]
    cb, lg, lb = cb_ref[...], lg_ref[...], lb_ref[...]

    def tile(i, carry):
        base = pl.multiple_of(i * CONV_TILE, CONV_TILE)
        win = pad_ref[pl.ds(base, CONV_TILE + 2 * CONV_PAD), :]
        acc = jnp.zeros((CONV_TILE, CONV_CH), F32)
        for j in range(CONV_WIDTH):
            acc = acc + win[j + 1:j + 1 + CONV_TILE, :] * w[j:j + 1, :]
        c = acc + cb
        mu = jnp.mean(c, axis=-1, keepdims=True)
        cc = c - mu
        var = jnp.mean(cc * cc, axis=-1, keepdims=True)
        y = cc * lax.rsqrt(var + EPS) * lg + lb
        o_ref[0, pl.ds(base, CONV_TILE), :] = (y * _sigmoid(y)).astype(BF16)
        return carry

    lax.fori_loop(0, seq // CONV_TILE, tile, 0)


def _conv_call(u, lw):
    b, s, ch = u.shape
    return pl.pallas_call(
        functools.partial(_conv_kernel, seq=s),
        grid=(b,),
        in_specs=[
            pl.BlockSpec((1, s, ch), lambda i: (i, 0, 0)),
            _const_spec((32, ch)), _const_spec((1, ch)), _const_spec((1, ch)), _const_spec((1, ch)),
        ],
        out_specs=pl.BlockSpec((1, s, ch), lambda i: (i, 0, 0)),
        out_shape=jax.ShapeDtypeStruct((b, s, ch), BF16),
        scratch_shapes=[pltpu.VMEM((s + 2 * CONV_PAD, ch), F32)],
        compiler_params=_cparams(("parallel",)),
        name="conv",
    )(u, lw["conv_w"], lw["conv_b"], lw["conv_ln_g"], lw["conv_ln_b"])


def _lane_mask(width, lo, hi):
    lane = lax.broadcasted_iota(jnp.int32, (1, width), 1)
    return (lane >= lo) & (lane < hi)


def _softmax_pv(qw, kt, v1):
    s = jnp.dot(qw, kt, preferred_element_type=F32)
    m = jnp.max(s, axis=-1, keepdims=True)
    p = jnp.exp2(s - m).astype(BF16)
    o = jnp.dot(p, v1, preferred_element_type=F32)
    return o * (1.0 / o[:, V_HEAD:V_HEAD + 1])


def _attn_kernel(q_ref, kt_ref, v_ref, lam_ref, g64_ref, sg_ref, o_ref, *, maps, diff, lam_init):
    if diff:
        lv = lam_ref[...]
        lam = (jnp.exp(jnp.sum(lv[0:1] * lv[1:2], axis=-1, keepdims=True))
               - jnp.exp(jnp.sum(lv[2:3] * lv[3:4], axis=-1, keepdims=True)) + lam_init)
    heads = []
    for h in range(N_HEADS):
        outs = []
        for (w0, lo, hi) in maps[h]:
            qw = q_ref[0, :, w0:w0 + LANE]
            if (lo, hi) != (0, LANE):
                qw = jnp.where(_lane_mask(LANE, lo, hi), qw, jnp.zeros_like(qw))
            outs.append(_softmax_pv(qw, kt_ref[0, w0:w0 + LANE, :], v_ref[0, h]))
        heads.append(outs[0] - lam * outs[1] if diff else outs[0])
    low = _lane_mask(LANE, 0, V_HEAD)
    acc = jnp.concatenate(
        [jnp.where(low, heads[h], pltpu.roll(heads[h + 1], V_HEAD, axis=1)) for h in range(0, N_HEADS, 2)],
        axis=1)
    if diff:
        acc = _group_norm(acc, g64_ref, 1.0 / DIFF_V) * sg_ref[...]
    o_ref[0] = acc.astype(BF16)


MAPS_MLA = tuple(((LANE * h, 0, LANE),) for h in range(N_HEADS))
MAPS_DIFF = tuple(tuple((LANE * (h // 2), 64 * (h % 2) + 32 * c, 64 * (h % 2) + 32 * c + 32) for c in range(2))
                  for h in range(N_HEADS))
MAPS_NA = tuple(((LANE * (h // 2), 64 * (h % 2), 64 * (h % 2) + 64),) for h in range(N_HEADS))


def _attn_call(q, kt, v, lw, maps, diff=False, lam_init=0.0):
    b, s, wq = q.shape
    sk = kt.shape[2]
    tq = min(s, ATTN_TQ)
    vh = v.reshape(b, sk, N_HEADS, V_HEAD).transpose(0, 2, 1, 3)
    v = jnp.concatenate([vh, jnp.ones_like(vh)], axis=-1)
    return pl.pallas_call(
        functools.partial(_attn_kernel, maps=maps, diff=diff, lam_init=lam_init),
        grid=(b, s // tq),
        in_specs=[
            pl.BlockSpec((1, tq, wq), lambda i, j: (i, j, 0)),
            pl.BlockSpec((1, wq, sk), lambda i, j: (i, 0, 0)),
            pl.BlockSpec((1, N_HEADS, sk, LANE), lambda i, j: (i, 0, 0, 0)),
            _const_spec((4, DIFF_DIM)),
            _const_spec((BR_W, BR_W)),
            _const_spec((1, BR_W)),
        ],
        out_specs=pl.BlockSpec((1, tq, BR_W), lambda i, j: (i, j, 0)),
        out_shape=jax.ShapeDtypeStruct((b, s, BR_W), BF16),
        compiler_params=_cparams(("parallel", "parallel")),
        name="attn_diff" if diff else "attn",
    )(q, kt, v, lw["diff_lam"], lw["g64"], lw["subln"])


_NT = (((1,), (1,)), ((), ()))


def _na_kernel(q_ref, k_ref, v_ref, kc_ref, vc_ref, bias_ref, o_ref, *, rows):
    j = pl.program_id(1)
    nj = rows // 2
    start = jnp.clip(2 * j - 4, 0, rows - NA_BAND_ROWS)
    base = pl.multiple_of(start * GRID_W, 2 * GRID_W)
    cls = jnp.where(j < 2, j, jnp.where(j >= nj - 2, j - (nj - 2) + 3, 2))
    band = NA_BAND_ROWS * GRID_W
    kw = k_ref[0, pl.ds(base, band), :]
    vw = v_ref[0, pl.ds(base, band), :]
    kc = kc_ref[0]
    vc = vc_ref[0]
    q = q_ref[0]
    acc = jnp.zeros((2 * GRID_W, BR_W), F32)
    for h in range(N_HEADS):
        qm = jnp.where(_lane_mask(BR_W, HEAD_DIM * h, HEAD_DIM * (h + 1)), q, jnp.zeros_like(q))
        s_loc = lax.dot_general(qm, kw, _NT, preferred_element_type=F32) + bias_ref[cls, h]
        s_ctx = lax.dot_general(qm, kc, _NT, preferred_element_type=F32)
        m = jnp.maximum(jnp.max(s_loc, axis=-1, keepdims=True), jnp.max(s_ctx, axis=-1, keepdims=True))
        p_loc = jnp.exp2(s_loc - m)
        p_ctx = jnp.exp2(s_ctx - m)
        l = jnp.sum(p_loc, axis=-1, keepdims=True) + jnp.sum(p_ctx, axis=-1, keepdims=True)
        o = (jnp.dot(p_ctx.astype(BF16), vc, preferred_element_type=F32)
             + jnp.dot(p_loc.astype(BF16), vw, preferred_element_type=F32)) * (1.0 / l)
        acc = jnp.where(_lane_mask(BR_W, HEAD_DIM * h, HEAD_DIM * (h + 1)), o, acc)
    o_ref[0] = acc.astype(BF16)


def _na_call(q, k, v, kc, vc, bias):
    b, s, w = q.shape
    n_ctx = kc.shape[1]
    rows = s // GRID_W
    tq = 2 * GRID_W
    return pl.pallas_call(
        functools.partial(_na_kernel, rows=rows),
        grid=(b, rows // 2),
        in_specs=[
            pl.BlockSpec((1, tq, w), lambda i, j: (i, j, 0)),
            pl.BlockSpec((1, s, w), lambda i, j: (i, 0, 0)),
            pl.BlockSpec((1, s, w), lambda i, j: (i, 0, 0)),
            pl.BlockSpec((1, n_ctx, w), lambda i, j: (i, 0, 0)),
            pl.BlockSpec((1, n_ctx, w), lambda i, j: (i, 0, 0)),
            _const_spec(bias.shape),
        ],
        out_specs=pl.BlockSpec((1, tq, w), lambda i, j: (i, j, 0)),
        out_shape=jax.ShapeDtypeStruct((b, s, w), BF16),
        compiler_params=_cparams(("parallel", "arbitrary")),
        name="na",
    )(q, k, v, kc, vc, bias)


def _merge_kernel(x_ref, mod_ref, n1g_ref, n2g_ref, uc_ref, om_ref, od_ref, on_ref,
                  gw_ref, gb_ref, wc_ref, wm_ref, wd_ref, wn_ref, wo_ref, rwt_ref, rb_ref,
                  x1_ref, h2_ref, ids_ref, wts_ref):
    x = x_ref[0]
    mod = mod_ref[0]
    h = _modulate(x, n1g_ref[...], mod[0:1], mod[1:2]).astype(BF16)
    y = jnp.zeros(x.shape, F32)
    branches = ((uc_ref, wc_ref), (om_ref, wm_ref), (od_ref, wd_ref), (on_ref, wn_ref))
    for i, (o_ref, w_ref) in enumerate(branches):
        lo = D_MODEL * i
        g = _sigmoid(jnp.dot(h, gw_ref[:, lo:lo + D_MODEL], preferred_element_type=F32)
                     + gb_ref[:, lo:lo + D_MODEL])
        y = y + g * jnp.dot(o_ref[0], w_ref[...], preferred_element_type=F32)
    out = jnp.dot(y.astype(BF16), wo_ref[...], preferred_element_type=F32)
    x1 = x + mod[2:3] * out
    x1_ref[0] = x1
    h2 = _modulate(x1, n2g_ref[...], mod[3:4], mod[4:5])
    h2_ref[0] = h2

    logits = lax.dot_general(rwt_ref[...], h2, _NT, preferred_element_type=F32,
                             precision=lax.Precision.HIGHEST) + rb_ref[...]
    eidx = lax.broadcasted_iota(jnp.int32, logits.shape, 0).astype(F32)
    vals, idxs = [], []
    cur = logits
    for _ in range(TOP_K):
        m = jnp.max(cur, axis=0, keepdims=True)
        idx = jnp.min(jnp.where(cur == m, eidx, float(N_EXPERTS)), axis=0, keepdims=True)
        vals.append(m)
        idxs.append(idx)
        cur = jnp.where(eidx == idx, -jnp.inf, cur)
    es = [jnp.exp(vk - vals[0]) for vk in vals]
    den = es[0] + es[1] + es[2] + es[3]
    ids_ref[0] = jnp.concatenate(idxs, axis=0).astype(jnp.int32)
    wts_ref[0] = jnp.concatenate([e / den for e in es], axis=0)


def _merge_call(x, mods, mod_row, lw, uc, om, od, on):
    b, s, d = x.shape
    t = min(s, 512)
    if mod_row is None:
        mod_map = lambda i, j: (i, 0, 0)
    else:
        mod_map = lambda i, j: (mod_row, 0, 0)
    tok = lambda w: pl.BlockSpec((1, t, w), lambda i, j: (i, j, 0))
    rt = pl.BlockSpec((1, TOP_K, t), lambda i, j: (i, 0, j))
    return pl.pallas_call(
        _merge_kernel,
        grid=(b, s // t),
        in_specs=[
            tok(d), pl.BlockSpec((1, 6, d), mod_map), _const_spec((1, d)), _const_spec((1, d)),
            tok(BR_W), tok(BR_W), tok(BR_W), tok(BR_W),
            _const_spec((d, N_BRANCH * d)), _const_spec((1, N_BRANCH * d)),
            _const_spec((BR_W, d)), _const_spec((BR_W, d)), _const_spec((BR_W, d)), _const_spec((BR_W, d)),
            _const_spec((d, d)), _const_spec((N_EXPERTS, d)), _const_spec((N_EXPERTS, 1)),
        ],
        out_specs=[tok(d), tok(d), rt, rt],
        out_shape=[jax.ShapeDtypeStruct((b, s, d), F32), jax.ShapeDtypeStruct((b, s, d), F32),
                   jax.ShapeDtypeStruct((b, TOP_K, s), jnp.int32), jax.ShapeDtypeStruct((b, TOP_K, s), F32)],
        compiler_params=_cparams(("parallel", "parallel")),
        name="merge",
    )(x, mods, lw["n1g"], lw["n2g"], uc, om, od, on, lw["gate_w"], lw["gate_b"],
      lw["conv_out"], lw["mla_out"], lw["diff_out"], lw["na_out"], lw["w_o"], lw["router_wt"], lw["router_b"])


def _route(ids, tile):
    n = ids.shape[1]
    p = TOP_K * n
    e = ids.reshape(p)
    onehot = (e[:, None] == jnp.arange(N_EXPERTS, dtype=jnp.int32)[None, :]).astype(jnp.int32)
    csum = jnp.cumsum(onehot, axis=0)
    counts = csum[-1]
    padded = ((counts + tile - 1) // tile) * tile
    gend = jnp.cumsum(padded)
    gstart = gend - padded
    slot = jnp.sum(onehot * (csum - 1 + gstart[None, :]), axis=1).astype(jnp.int32)
    n_tiles = p // tile + N_EXPERTS
    n_slots = n_tiles * tile
    pair_of_slot = jnp.full((n_slots,), -1, jnp.int32).at[slot].set(
        jnp.arange(p, dtype=jnp.int32), unique_indices=True)
    real = pair_of_slot >= 0
    src_tok = jnp.where(real, pair_of_slot, jnp.arange(n_slots, dtype=jnp.int32)) % n
    dst_row = jnp.where(real, pair_of_slot, p + jnp.arange(n_slots, dtype=jnp.int32))
    tile_start = jnp.arange(n_tiles, dtype=jnp.int32) * tile
    texp = jnp.minimum(jnp.searchsorted(gend, tile_start, side="right"), N_EXPERTS - 1).astype(jnp.int32)
    tval = (tile_start < gend[-1]).astype(jnp.int32)
    return src_tok.reshape(n_tiles, 1, tile), dst_row.reshape(n_tiles, 1, tile), texp, tval


def _ffn_kernel(texp_ref, tval_ref, src_ref, nsrc_ref, dst_ref, pdst_ref, h_hbm, wgu_ref, bgu_ref, wd_ref, bd_ref,
                y_hbm, xbuf, ybuf, sem_in, sem_out, *, tile, n_tiles):
    i = pl.program_id(0)
    slot = i % 2
    valid = tval_ref[i] > 0
    prev_valid = (i >= 1) & (tval_ref[jnp.maximum(i - 1, 0)] > 0)

    def gather_row(idx_ref, buf_slot, r):
        pltpu.make_async_copy(h_hbm.at[pl.ds(idx_ref[0, 0, r], 1)], xbuf.at[buf_slot, pl.ds(r, 1)],
                              sem_in.at[buf_slot]).start()

    def scatter_row(idx_ref, buf_slot, r):
        pltpu.make_async_copy(ybuf.at[buf_slot, pl.ds(r, 1)], y_hbm.at[pl.ds(idx_ref[0, 0, r], 1)],
                              sem_out.at[buf_slot]).start()

    def rolled(fn):
        def body(r, carry):
            fn(r)
            return carry

        lax.fori_loop(0, tile, body, 0, unroll=8)

    def wait_rows(buf, sem):
        pltpu.make_async_copy(h_hbm.at[pl.ds(0, tile)], buf, sem).wait()

    def ffn(x):
        gu = jnp.dot(x, wgu_ref[0], preferred_element_type=F32) + bgu_ref[0]
        g = jnp.minimum(gu[:, :D_FF], SWIGLU_LIMIT)
        u = jnp.clip(gu[:, D_FF:], -SWIGLU_LIMIT, SWIGLU_LIMIT)
        act = ((u + 1.0) * (g * _sigmoid(SWIGLU_ALPHA * g))).astype(BF16)
        return jnp.dot(act, wd_ref[0], preferred_element_type=F32) + bd_ref[0]

    @pl.when((i == 0) & valid)
    def _():
        rolled(lambda r: gather_row(src_ref, 0, r))
        wait_rows(xbuf.at[0], sem_in.at[0])
        x = xbuf[0].astype(BF16)
        rolled(lambda r: gather_row(nsrc_ref, 1, r))
        ybuf[0] = ffn(x)

    @pl.when((i >= 2) & valid & prev_valid)
    def _():
        wait_rows(ybuf.at[slot], sem_out.at[slot])

    @pl.when((i >= 1) & valid)
    def _():
        wait_rows(xbuf.at[slot], sem_in.at[slot])
        n_chunk = (2 * D_FF) // FFN_CHUNK
        per = tile // n_chunk
        gu = []
        for c in range(n_chunk):
            x = xbuf[slot].astype(BF16)
            for r in range(c * per, (c + 1) * per):
                gather_row(nsrc_ref, 1 - slot, r)
                scatter_row(pdst_ref, 1 - slot, r)
            lo = c * FFN_CHUNK
            gu.append(jnp.dot(x, wgu_ref[0, :, lo:lo + FFN_CHUNK], preferred_element_type=F32)
                      + bgu_ref[0, :, lo:lo + FFN_CHUNK])
        half = n_chunk // 2
        act = []
        for c in range(half):
            g = jnp.minimum(gu[c], SWIGLU_LIMIT)
            u = jnp.clip(gu[half + c], -SWIGLU_LIMIT, SWIGLU_LIMIT)
            act.append(((u + 1.0) * (g * _sigmoid(SWIGLU_ALPHA * g))).astype(BF16))
        act = jnp.concatenate(act, axis=1)
        for c in range(D_MODEL // FFN_CHUNK):
            lo = c * FFN_CHUNK
            ybuf[slot, :, lo:lo + FFN_CHUNK] = (
                jnp.dot(act, wd_ref[0, :, lo:lo + FFN_CHUNK], preferred_element_type=F32)
                + bd_ref[0, :, lo:lo + FFN_CHUNK])

    def drain(last_slot):
        wait_rows(xbuf.at[1 - last_slot], sem_in.at[1 - last_slot])
        wait_rows(ybuf.at[last_slot], sem_out.at[last_slot])

    @pl.when(jnp.logical_not(valid) & prev_valid)
    def _():
        @pl.when(i >= 2)
        def _():
            wait_rows(ybuf.at[slot], sem_out.at[slot])

        rolled(lambda r: scatter_row(pdst_ref, 1 - slot, r))
        drain(1 - slot)

    @pl.when((i == n_tiles - 1) & valid)
    def _():
        wait_rows(ybuf.at[1 - slot], sem_out.at[1 - slot])
        rolled(lambda r: scatter_row(dst_ref, slot, r))
        drain(slot)


def _ffn_call(h2, src_tok, dst_row, texp, tval, lw, tile, n_out_rows):
    n_tiles = src_tok.shape[0]
    d = h2.shape[1]
    idx_spec = lambda f: pl.BlockSpec((1, 1, tile), f, memory_space=pltpu.SMEM)
    grid_spec = pltpu.PrefetchScalarGridSpec(
        num_scalar_prefetch=2,
        grid=(n_tiles,),
        in_specs=[
            idx_spec(lambda i, te, tv: (i, 0, 0)),
            idx_spec(lambda i, te, tv: (jnp.minimum(i + 1, n_tiles - 1), 0, 0)),
            idx_spec(lambda i, te, tv: (i, 0, 0)),
            idx_spec(lambda i, te, tv: (jnp.maximum(i - 1, 0), 0, 0)),
            pl.BlockSpec(memory_space=pl.ANY),
            pl.BlockSpec((1, d, 2 * D_FF), lambda i, te, tv: (te[i], 0, 0)),
            pl.BlockSpec((1, 1, 2 * D_FF), lambda i, te, tv: (te[i], 0, 0)),
            pl.BlockSpec((1, D_FF, d), lambda i, te, tv: (te[i], 0, 0)),
            pl.BlockSpec((1, 1, d), lambda i, te, tv: (te[i], 0, 0)),
        ],
        out_specs=pl.BlockSpec(memory_space=pl.ANY),
        scratch_shapes=[pltpu.VMEM((2, tile, d), F32), pltpu.VMEM((2, tile, d), F32),
                        pltpu.SemaphoreType.DMA((2,)), pltpu.SemaphoreType.DMA((2,))],
    )
    return pl.pallas_call(
        functools.partial(_ffn_kernel, tile=tile, n_tiles=n_tiles),
        grid_spec=grid_spec,
        out_shape=jax.ShapeDtypeStruct((n_out_rows, d), F32),
        compiler_params=_cparams(("arbitrary",)),
        name="moe_ffn",
    )(texp, tval, src_tok, src_tok, dst_row, dst_row, h2, lw["exp_w_gu"], lw["exp_b_gu"], lw["exp_w_down"], lw["exp_b_down"])


def _combine_kernel(x1_ref, mod_ref, w_ref, y0_ref, y1_ref, y2_ref, y3_ref, o_ref):
    w = w_ref[0]
    acc = w[:, 0:1] * y0_ref[...]
    for k, y_ref in ((1, y1_ref), (2, y2_ref), (3, y3_ref)):
        acc = acc + w[:, k:k + 1] * y_ref[...]
    o_ref[0] = x1_ref[0] + mod_ref[0][5:6] * acc


def _combine_call(x1, mods, mod_row, wts, y):
    b, s, d = x1.shape
    t = min(s, 512)
    nt = s // t
    if mod_row is None:
        mod_map = lambda i, j: (i, 0, 0)
    else:
        mod_map = lambda i, j: (mod_row, 0, 0)
    wts = wts.transpose(0, 2, 1)
    y_spec = lambda k: pl.BlockSpec((t, d), lambda i, j: (k * b * nt + i * nt + j, 0))
    return pl.pallas_call(
        _combine_kernel,
        grid=(b, nt),
        in_specs=[
            pl.BlockSpec((1, t, d), lambda i, j: (i, j, 0)),
            pl.BlockSpec((1, 6, d), mod_map),
            pl.BlockSpec((1, t, TOP_K), lambda i, j: (i, j, 0)),
            y_spec(0), y_spec(1), y_spec(2), y_spec(3),
        ],
        out_specs=pl.BlockSpec((1, t, d), lambda i, j: (i, j, 0)),
        out_shape=jax.ShapeDtypeStruct((b, s, d), F32),
        compiler_params=_cparams(("parallel", "parallel")),
        name="moe_combine",
    )(x1, mods, wts, y, y, y, y)


def _moe(x1, h2, ids, wts, mods, mod_row, lw):
    b, s, d = x1.shape
    n = b * s
    tile = 512 if TOP_K * n >= 512 * N_EXPERTS * 4 else 256
    ids_flat = ids.transpose(1, 0, 2).reshape(TOP_K, n)
    src_tok, dst_row, texp, tval = _route(ids_flat, tile)
    n_out_rows = TOP_K * n + src_tok.shape[0] * tile
    y = _ffn_call(h2.reshape(n, d), src_tok, dst_row, texp, tval, lw, tile, n_out_rows)
    return _combine_call(x1, mods, mod_row, wts, y)


def _layer_weights(l, p, lam_init):
    w = p["w_in"][l]
    d = w.shape[0]
    zcols = lambda n: jnp.zeros((d, n), w.dtype)
    regroup = lambda blk: blk.reshape(d, N_HEADS, 3, HEAD_DIM).transpose(0, 2, 1, 3).reshape(d, 3 * BR_W)
    w_in = jnp.concatenate([
        w[:, :A_IN],
        w[:, OFF_B:OFF_B + Q_LORA], zcols(P_CKV - P_CQ - Q_LORA),
        w[:, OFF_B + Q_LORA:OFF_B + Q_LORA + KV_LORA],
        w[:, OFF_B + Q_LORA + KV_LORA:OFF_C], zcols(P_DQ - P_KPE - QK_ROPE),
        regroup(w[:, OFF_C:OFF_D]), regroup(w[:, OFF_D:]),
    ], axis=1)
    assert w_in.shape[1] == PROJ_W

    def head_slots(w3, slot):
        w3 = jnp.pad(w3, ((0, 0), (0, 0), (0, slot - w3.shape[2])))
        return w3.reshape(w3.shape[0], N_HEADS * slot)

    wuq = head_slots(p["mla_w_uq"][l].reshape(Q_LORA, N_HEADS, MLA_QK), LANE)
    wuq = jnp.pad(wuq, ((0, 256 - Q_LORA), (0, 0)))
    wukv = p["mla_w_ukv"][l].reshape(KV_LORA, N_HEADS, QK_NOPE + V_HEAD)
    wk = head_slots(wukv[:, :, :QK_NOPE], LANE)
    wv = head_slots(wukv[:, :, QK_NOPE:], V_HEAD)
    ppe = np.zeros((LANE, 512), np.float32)
    for h in range(N_HEADS):
        for i in range(QK_ROPE):
            ppe[i, h * LANE + QK_NOPE + i] = 1.0

    def slot_gain(g, scale):
        g = jnp.concatenate([g * scale, jnp.zeros((LANE - MLA_QK,), F32)])
        return jnp.tile(g, N_HEADS)

    def row512(v):
        return jnp.concatenate([v, jnp.zeros((512 - v.shape[0],), F32)])

    gains = jnp.stack([
        row512(p["mla_cq_g"][l]),
        slot_gain(p["mla_qn_g"][l], MLA_QK ** -0.5 * LOG2E),
        row512(p["mla_ckv_g"][l]),
        slot_gain(p["mla_kn_g"][l], 1.0),
        row512(jnp.tile(p["diff_qn_g"][l], 2 * N_HEADS) * DIFF_DIM ** -0.5 * LOG2E),
        row512(jnp.tile(p["diff_kn_g"][l], 2 * N_HEADS)),
        row512(jnp.tile(p["na_qn_g"][l], N_HEADS) * HEAD_DIM ** -0.5 * LOG2E),
        row512(jnp.tile(p["na_kn_g"][l], N_HEADS)),
    ])
    conv_w = jnp.concatenate([p["conv_w"][l], jnp.zeros((1, CONV_CH), F32)], axis=0)
    return dict(
        n1g=p["norm1_g"][l][None, :], n2g=p["norm2_g"][l][None, :],
        w_in=w_in.astype(BF16), gains=gains,
        wuq=wuq.astype(BF16), wk=wk.astype(BF16), wv=wv.astype(BF16), ppe=jnp.asarray(ppe, BF16),
        g96=jnp.asarray(_group_ones(512, LANE, MLA_QK), BF16),
        g32=jnp.asarray(_group_ones(BR_W, DIFF_DIM, DIFF_DIM), BF16),
        g64=jnp.asarray(_group_ones(BR_W, HEAD_DIM, HEAD_DIM), BF16),
        rm=jnp.asarray(_rot_matrix(512, LANE, QK_NOPE, QK_ROPE // 2), BF16),
        rd=jnp.asarray(_rot_matrix(BR_W, DIFF_DIM, 0, DIFF_DIM // 2), BF16),
        conv_w=conv_w, conv_b=p["conv_b"][l][None, :],
        conv_ln_g=p["conv_ln_g"][l][None, :], conv_ln_b=p["conv_ln_b"][l][None, :],
        diff_lam=p["diff_lam"][l],
        subln=(jnp.tile(p["diff_subln_g"][l], N_HEADS) * (1.0 - lam_init))[None, :],
        gate_w=p["gate_w"][l].astype(BF16), gate_b=p["gate_b"][l][None, :],
        conv_out=p["conv_out"][l].astype(BF16), mla_out=p["mla_out"][l].astype(BF16),
        diff_out=p["diff_out"][l].astype(BF16), na_out=p["na_out"][l].astype(BF16),
        w_o=p["w_o"][l].astype(BF16),
        router_wt=p["router_w"][l].T, router_b=p["router_b"][l][:, None],
        exp_w_gu=p["exp_w_gu"][l].astype(BF16), exp_b_gu=p["exp_b_gu"][l][:, None, :],
        exp_w_down=p["exp_w_down"][l].astype(BF16), exp_b_down=p["exp_b_down"][l][:, None, :],
    )


def _kt(kc, k):
    return jnp.concatenate([kc, k], axis=1).transpose(0, 2, 1)


def kernel(x, c, ctx, c_ctx, ada_w, ada_b, norm1_g, norm2_g, w_in, conv_w, conv_b, conv_ln_g, conv_ln_b, conv_out, mla_cq_g, mla_ckv_g, mla_w_uq, mla_w_ukv, mla_qn_g, mla_kn_g, mla_out, diff_qn_g, diff_kn_g, diff_lam, diff_subln_g, diff_out, na_qn_g, na_kn_g, na_rpb, na_out, gate_w, gate_b, w_o, router_w, router_b, exp_w_gu, exp_b_gu, exp_w_down, exp_b_down):
    p = dict(norm1_g=norm1_g, norm2_g=norm2_g, w_in=w_in, conv_w=conv_w, conv_b=conv_b,
             conv_ln_g=conv_ln_g, conv_ln_b=conv_ln_b, conv_out=conv_out, mla_cq_g=mla_cq_g,
             mla_ckv_g=mla_ckv_g, mla_w_uq=mla_w_uq, mla_w_ukv=mla_w_ukv, mla_qn_g=mla_qn_g,
             mla_kn_g=mla_kn_g, mla_out=mla_out, diff_qn_g=diff_qn_g, diff_kn_g=diff_kn_g,
             diff_lam=diff_lam, diff_subln_g=diff_subln_g, diff_out=diff_out, na_qn_g=na_qn_g,
             na_kn_g=na_kn_g, na_out=na_out, gate_w=gate_w, gate_b=gate_b, w_o=w_o,
             router_w=router_w, router_b=router_b, exp_w_gu=exp_w_gu, exp_b_gu=exp_b_gu,
             exp_w_down=exp_w_down, exp_b_down=exp_b_down)
    b, s, d = x.shape
    n_ctx = ctx.shape[1]
    depth = ada_w.shape[0]
    rows = s // GRID_W
    assert d == D_MODEL and s % (2 * GRID_W) == 0 and rows >= NA_BAND_ROWS and n_ctx % LANE == 0

    mod_rows = -(-(b + 1) // 8) * 8
    cs = jnp.concatenate([c, c_ctx[None, :], jnp.zeros((mod_rows - b - 1, d), F32)], axis=0)
    mods_all = _ada_call(cs, ada_w, ada_b).reshape(depth, mod_rows, 6, d)

    tabs_x = _rope_lane_tables(s)
    tabs_c = (jnp.ones((n_ctx, 512), F32), jnp.zeros((n_ctx, 512), F32),
              jnp.ones((n_ctx, BR_W), F32), jnp.zeros((n_ctx, BR_W), F32))

    xc = ctx
    for l in range(depth):
        last = l == depth - 1
        lam_init = 0.8 - 0.6 * math.exp(-0.3 * l)
        lw = _layer_weights(l, p, lam_init)
        mods = mods_all[l]
        bias = _na_bias_tables(na_rpb[l], rows)

        u, mq, mk, mv, dq, dk, dv, nq, nk, nv = _proj_call(x, mods, None, lw, tabs_x, True)
        uc, mqc, mkc, mvc, dqc, dkc, dvc, nqc, nkc, nvc = _proj_call(xc, mods, b, lw, tabs_c, False)

        y_conv = _conv_call(u, lw)
        y_mla = _attn_call(mq, _kt(mkc, mk), jnp.concatenate([mvc, mv], axis=1), lw, MAPS_MLA)
        y_diff = _attn_call(dq, _kt(dkc, dk), jnp.concatenate([dvc, dv], axis=1), lw, MAPS_DIFF,
                            diff=True, lam_init=lam_init)
        y_na = _na_call(nq, nk, nv, nkc, nvc, bias)
        x1, h2, ids, wts = _merge_call(x, mods, None, lw, y_conv, y_mla, y_diff, y_na)
        x = _moe(x1, h2, ids, wts, mods, None, lw)

        if not last:
            yc_conv = _conv_call(uc, lw)
            yc_mla = _attn_call(mqc, mkc.transpose(0, 2, 1), mvc, lw, MAPS_MLA)
            yc_diff = _attn_call(dqc, dkc.transpose(0, 2, 1), dvc, lw, MAPS_DIFF, diff=True, lam_init=lam_init)
            yc_na = _attn_call(nqc, nkc.transpose(0, 2, 1), nvc, lw, MAPS_NA)
            xc1, h2c, idsc, wtsc = _merge_call(xc, mods, b, lw, yc_conv, yc_mla, yc_diff, yc_na)
            xc = _moe(xc1, h2c, idsc, wtsc, mods, b, lw)
    return x
```

```python
import functools
import math

import numpy as np
import jax
import jax.numpy as jnp
from jax import lax
from jax.experimental import pallas as pl
from jax.experimental.pallas import tpu as pltpu
from jax.experimental.pallas import tpu_sc as plsc

F32 = jnp.float32
BF16 = jnp.bfloat16

D_MODEL = 1024
GRID_W = 64
N_BRANCH = 4
N_HEADS = 4
HEAD_DIM = 64
CONV_CH = 256
CONV_WIDTH = 31
Q_LORA = 192
KV_LORA = 128
QK_NOPE = 64
QK_ROPE = 32
V_HEAD = 64
DIFF_DIM = 32
DIFF_V = 2 * DIFF_DIM
NA_KH = 8
NA_KW = 16
ROPE_DIM = 32
ROPE_BASE = 10000.0
N_EXPERTS = 32
TOP_K = 4
D_FF = 1024
SWIGLU_LIMIT = 7.0
SWIGLU_ALPHA = 1.702
EPS = 1e-6
NEG_INF = -1e30

A_IN = 2 * CONV_CH
B_IN = Q_LORA + KV_LORA + QK_ROPE
C_IN = N_HEADS * (4 * DIFF_DIM + DIFF_V)
D_IN = N_HEADS * 3 * HEAD_DIM
OFF_B = A_IN
OFF_C = OFF_B + B_IN
OFF_D = OFF_C + C_IN

LANE = 128
MLA_QK = QK_NOPE + QK_ROPE
BR_W = N_HEADS * HEAD_DIM
PROJ_W = 2560
NA_BAND_ROWS = 10
FFN_CHUNK = 256
ATTN_TQ = 512
LOG2E = math.log2(math.e)
VMEM_LIMIT = 52 * 1024 * 1024

P_A, P_G, P_CQ, P_CKV, P_KPE = 0, 256, 512, 768, 896
P_DQ, P_DK, P_DV = 1024, 1280, 1536
P_NQ, P_NK, P_NV = 1792, 2048, 2304


def _sigmoid(x):
    return 1.0 / (1.0 + jnp.exp(-x))


def _modulate(x, g, shift, scale):
    ms = jnp.mean(x * x, axis=-1, keepdims=True)
    return (x * lax.rsqrt(ms + EPS) * g) * (1.0 + scale) + shift


def _cparams(sem):
    return pltpu.CompilerParams(dimension_semantics=sem, vmem_limit_bytes=VMEM_LIMIT)


def _const_spec(shape):
    n = len(shape)
    return pl.BlockSpec(shape, lambda *_: (0,) * n)


def _group_ones(width, slot, real):
    i = np.arange(width)
    valid = (i % slot) < real
    same = (i[:, None] // slot) == (i[None, :] // slot)
    return (same & valid[:, None] & valid[None, :]).astype(np.float32)


def _rot_matrix(width, slot, start, half):
    r = np.zeros((width, width), np.float32)
    for s0 in range(0, width, slot):
        for i in range(half):
            a, b = s0 + start + i, s0 + start + half + i
            r[b, a] = -1.0
            r[a, b] = 1.0
    return r


def _rope_lane_tables(n_tokens):
    t = jnp.arange(n_tokens, dtype=jnp.int32)
    rows = (t // GRID_W).astype(F32)
    cols = (t % GRID_W).astype(F32)
    axis_dim = ROPE_DIM // 2
    inv = ROPE_BASE ** (-jnp.arange(0, axis_dim, 2, dtype=F32) / axis_dim)
    theta = jnp.concatenate([rows[:, None] * inv, cols[:, None] * inv], axis=-1)
    cos, sin = jnp.cos(theta), jnp.sin(theta)
    half = ROPE_DIM // 2
    ones = jnp.ones((n_tokens, QK_NOPE), F32)
    zeros = jnp.zeros((n_tokens, QK_NOPE), F32)
    pad1 = jnp.ones((n_tokens, LANE - MLA_QK), F32)
    pad0 = jnp.zeros((n_tokens, LANE - MLA_QK), F32)
    cm = jnp.tile(jnp.concatenate([ones, cos, cos, pad1], -1), (1, N_HEADS))
    sm = jnp.tile(jnp.concatenate([zeros, sin, sin, pad0], -1), (1, N_HEADS))
    cd = jnp.tile(jnp.concatenate([cos, cos], -1), (1, 2 * N_HEADS))
    sd = jnp.tile(jnp.concatenate([sin, sin], -1), (1, 2 * N_HEADS))
    assert half * 2 == DIFF_DIM
    return cm, sm, cd, sd


def _na_bias_tables(rpb, rows):
    kh = min(NA_KH, rows)
    nj = rows // 2
    reps = np.array([0, 1, 2, nj - 2, nj - 1])
    n_ro, n_co = 2 * NA_KH - 1, 2 * NA_KW - 1
    start = np.clip(2 * reps - 4, 0, rows - NA_BAND_ROWS)
    r = 2 * reps[:, None] + np.arange(2)[None, :]
    kr = start[:, None] + np.arange(NA_BAND_ROWS)[None, :]
    row_start = np.clip(r - kh // 2, 0, rows - kh)
    vr = (kr[:, None, :] >= row_start[:, :, None]) & (kr[:, None, :] < row_start[:, :, None] + kh)
    ro = np.clip(kr[:, None, :] - r[:, :, None] + NA_KH - 1, 0, n_ro - 1)
    qc = np.arange(GRID_W)
    win_start = np.clip(qc - NA_KW // 2, 0, GRID_W - NA_KW)
    vc = (qc[None, :] >= win_start[:, None]) & (qc[None, :] < win_start[:, None] + NA_KW)
    co = np.clip(qc[None, :] - qc[:, None] + NA_KW - 1, 0, n_co - 1)
    rsel = (ro[..., None] == np.arange(n_ro)).astype(np.float32)
    csel = (co[None] == np.arange(n_co)[:, None, None]).astype(np.float32)
    hi = lax.Precision.HIGHEST
    t1 = jnp.einsum("cqav,hvw->hcqaw", rsel, rpb.astype(F32), precision=hi)
    b = jnp.einsum("hcqaw,wxy->chqxay", t1, csel, precision=hi)
    valid = vr[:, None, :, None, :, None] & vc[None, None, None, :, None, :]
    b = jnp.where(valid, b * LOG2E, NEG_INF)
    return b.reshape(len(reps), N_HEADS, 2 * GRID_W, NA_BAND_ROWS * GRID_W)


def _ada_kernel(c_ref, w_ref, b_ref, o_ref):
    c = c_ref[...]
    s = c * _sigmoid(c)
    o_ref[0] = jnp.dot(s, w_ref[0], preferred_element_type=F32,
                       precision=lax.Precision.HIGHEST) + b_ref[0]


def _ada_call(cs, ada_w, ada_b):
    depth, d, n = ada_w.shape
    rows = cs.shape[0]
    tn = 1536
    return pl.pallas_call(
        _ada_kernel,
        grid=(depth, n // tn),
        in_specs=[
            pl.BlockSpec((rows, d), lambda l, j: (0, 0)),
            pl.BlockSpec((1, d, tn), lambda l, j: (l, 0, j)),
            pl.BlockSpec((1, 1, tn), lambda l, j: (l, 0, j)),
        ],
        out_specs=pl.BlockSpec((1, rows, tn), lambda l, j: (l, 0, j)),
        out_shape=jax.ShapeDtypeStruct((depth, rows, n), F32),
        compiler_params=_cparams(("arbitrary", "arbitrary")),
        name="ada",
    )(cs, ada_w, ada_b.reshape(depth, 1, n))


def _group_norm(x, ones_ref, inv_n):
    sq = (x * x).astype(BF16)
    ms = jnp.dot(sq, ones_ref[...], preferred_element_type=F32) * inv_n
    return x * lax.rsqrt(ms + EPS)


def _rope(x, rot_ref, cos_ref, sin_ref):
    rot = jnp.dot(x.astype(BF16), rot_ref[...], preferred_element_type=F32)
    return x * cos_ref[...] + rot * sin_ref[...]


def _proj_kernel(x_ref, mod_ref, n1g_ref, win_ref, gains_ref, wuq_ref, wk_ref, ppe_ref, wv_ref,
                 g96_ref, g32_ref, g64_ref, rm_ref, rd_ref, cm_ref, sm_ref, cd_ref, sd_ref,
                 u_ref, mq_ref, mk_ref, mv_ref, dq_ref, dk_ref, dv_ref, nq_ref, nk_ref, nv_ref,
                 *, use_rope):
    x = x_ref[0]
    mod = mod_ref[0]
    gains = gains_ref[...]
    h = _modulate(x, n1g_ref[...], mod[0:1], mod[1:2]).astype(BF16)
    proj = jnp.dot(h, win_ref[...], preferred_element_type=F32)

    u_ref[0] = proj[:, P_A:P_A + CONV_CH] * _sigmoid(proj[:, P_G:P_G + CONV_CH])

    cq = proj[:, P_CQ:P_CQ + 256]
    ms = jnp.sum(cq * cq, axis=-1, keepdims=True) * (1.0 / Q_LORA)
    cqn = (cq * lax.rsqrt(ms + EPS) * gains[0:1, :256]).astype(BF16)
    q = jnp.dot(cqn, wuq_ref[...], preferred_element_type=F32)
    q = _group_norm(q, g96_ref, 1.0 / MLA_QK) * gains[1:2, :]
    if use_rope:
        q = _rope(q, rm_ref, cm_ref, sm_ref)
    mq_ref[0] = q.astype(BF16)

    ckv = proj[:, P_CKV:P_CKV + KV_LORA]
    ms = jnp.mean(ckv * ckv, axis=-1, keepdims=True)
    ckvn = (ckv * lax.rsqrt(ms + EPS) * gains[2:3, :KV_LORA]).astype(BF16)
    kpe = proj[:, P_KPE:P_KPE + LANE].astype(BF16)
    k = (jnp.dot(ckvn, wk_ref[...], preferred_element_type=F32)
         + jnp.dot(kpe, ppe_ref[...], preferred_element_type=F32))
    k = _group_norm(k, g96_ref, 1.0 / MLA_QK) * gains[3:4, :]
    if use_rope:
        k = _rope(k, rm_ref, cm_ref, sm_ref)
    mk_ref[0] = k.astype(BF16)
    mv_ref[0] = jnp.dot(ckvn, wv_ref[...], preferred_element_type=F32).astype(BF16)

    qd = _group_norm(proj[:, P_DQ:P_DQ + BR_W], g32_ref, 1.0 / DIFF_DIM) * gains[4:5, :BR_W]
    kd = _group_norm(proj[:, P_DK:P_DK + BR_W], g32_ref, 1.0 / DIFF_DIM) * gains[5:6, :BR_W]
    if use_rope:
        qd = _rope(qd, rd_ref, cd_ref, sd_ref)
        kd = _rope(kd, rd_ref, cd_ref, sd_ref)
    dq_ref[0] = qd.astype(BF16)
    dk_ref[0] = kd.astype(BF16)
    dv_ref[0] = proj[:, P_DV:P_DV + BR_W].astype(BF16)

    qn = _group_norm(proj[:, P_NQ:P_NQ + BR_W], g64_ref, 1.0 / HEAD_DIM) * gains[6:7, :BR_W]
    kn = _group_norm(proj[:, P_NK:P_NK + BR_W], g64_ref, 1.0 / HEAD_DIM) * gains[7:8, :BR_W]
    nq_ref[0] = qn.astype(BF16)
    nk_ref[0] = kn.astype(BF16)
    nv_ref[0] = proj[:, P_NV:P_NV + BR_W].astype(BF16)


def _proj_call(x, mods, mod_row, lw, tabs, use_rope):
    b, s, d = x.shape
    t = min(s, 512)
    grid = (b, s // t)
    if mod_row is None:
        mod_map = lambda i, j: (i, 0, 0)
    else:
        mod_map = lambda i, j: (mod_row, 0, 0)
    tok = lambda w: pl.BlockSpec((1, t, w), lambda i, j: (i, j, 0))
    tab = lambda w: pl.BlockSpec((t, w), lambda i, j: (j, 0))
    in_specs = [
        tok(d),
        pl.BlockSpec((1, 6, d), mod_map),
        _const_spec((1, d)),
        _const_spec((d, PROJ_W)),
        _const_spec((8, 512)),
        _const_spec((256, 512)),
        _const_spec((KV_LORA, 512)),
        _const_spec((LANE, 512)),
        _const_spec((KV_LORA, BR_W)),
        _const_spec((512, 512)),
        _const_spec((BR_W, BR_W)),
        _const_spec((BR_W, BR_W)),
        _const_spec((512, 512)),
        _const_spec((BR_W, BR_W)),
        tab(512), tab(512), tab(BR_W), tab(BR_W),
    ]
    widths = [CONV_CH, 512, 512, BR_W, BR_W, BR_W, BR_W, BR_W, BR_W, BR_W]
    dtypes = [F32] + [BF16] * 9
    out_specs = [tok(w) for w in widths]
    out_shape = [jax.ShapeDtypeStruct((b, s, w), dt) for w, dt in zip(widths, dtypes)]
    return pl.pallas_call(
        functools.partial(_proj_kernel, use_rope=use_rope),
        grid=grid, in_specs=in_specs, out_specs=out_specs, out_shape=out_shape,
        compiler_params=_cparams(("parallel", "parallel")),
        name="proj",
    )(x, mods, lw["n1g"], lw["w_in"], lw["gains"], lw["wuq"], lw["wk"], lw["ppe"], lw["wv"],
      lw["g96"], lw["g32"], lw["g64"], lw["rm"], lw["rd"], tabs[0], tabs[1], tabs[2], tabs[3])


CONV_TILE = 128
CONV_PAD = 16


def _conv_kernel(u_ref, w_ref, cb_ref, lg_ref, lb_ref, o_ref, pad_ref, *, seq):
    zeros = jnp.zeros((CONV_PAD, CONV_CH), F32)
    pad_ref[0:CONV_PAD, :] = zeros
    pad_ref[CONV_PAD + seq:2 * CONV_PAD + seq, :] = zeros

    def fill(i, carry):
        base = pl.multiple_of(i * CONV_TILE, CONV_TILE)
        pad_ref[pl.ds(base + CONV_PAD, CONV_TILE), :] = u_ref[0, pl.ds(base, CONV_TILE), :]
        return carry

    lax.fori_loop(0, seq // CONV_TILE, fill, 0)
    w = w_ref[...]
    cb, lg, lb = cb_ref[...], lg_ref[...], lb_ref[...]

    def tile(i, carry):
        base = pl.multiple_of(i * CONV_TILE, CONV_TILE)
        win = pad_ref[pl.ds(base, CONV_TILE + 2 * CONV_PAD), :]
        acc = jnp.zeros((CONV_TILE, CONV_CH), F32)
        for j in range(CONV_WIDTH):
            acc = acc + win[j + 1:j + 1 + CONV_TILE, :] * w[j:j + 1, :]
        c = acc + cb
        mu = jnp.mean(c, axis=-1, keepdims=True)
        cc = c - mu
        var = jnp.mean(cc * cc, axis=-1, keepdims=True)
        y = cc * lax.rsqrt(var + EPS) * lg + lb
        o_ref[0, pl.ds(base, CONV_TILE), :] = (y * _sigmoid(y)).astype(BF16)
        return carry

    lax.fori_loop(0, seq // CONV_TILE, tile, 0)


def _conv_call(u, lw):
    b, s, ch = u.shape
    return pl.pallas_call(
        functools.partial(_conv_kernel, seq=s),
        grid=(b,),
        in_specs=[
            pl.BlockSpec((1, s, ch), lambda i: (i, 0, 0)),
            _const_spec((32, ch)), _const_spec((1, ch)), _const_spec((1, ch)), _const_spec((1, ch)),
        ],
        out_specs=pl.BlockSpec((1, s, ch), lambda i: (i, 0, 0)),
        out_shape=jax.ShapeDtypeStruct((b, s, ch), BF16),
        scratch_shapes=[pltpu.VMEM((s + 2 * CONV_PAD, ch), F32)],
        compiler_params=_cparams(("parallel",)),
        name="conv",
    )(u, lw["conv_w"], lw["conv_b"], lw["conv_ln_g"], lw["conv_ln_b"])


def _lane_mask(width, lo, hi):
    lane = lax.broadcasted_iota(jnp.int32, (1, width), 1)
    return (lane >= lo) & (lane < hi)


def _softmax_pv(qw, kt, v1):
    s = jnp.dot(qw, kt, preferred_element_type=F32)
    m = jnp.max(s, axis=-1, keepdims=True)
    p = jnp.exp2(s - m).astype(BF16)
    o = jnp.dot(p, v1, preferred_element_type=F32)
    return o * (1.0 / o[:, V_HEAD:V_HEAD + 1])


def _attn_kernel(q_ref, kt_ref, v_ref, lam_ref, g64_ref, sg_ref, o_ref, *, maps, diff, lam_init):
    if diff:
        lv = lam_ref[...]
        lam = (jnp.exp(jnp.sum(lv[0:1] * lv[1:2], axis=-1, keepdims=True))
               - jnp.exp(jnp.sum(lv[2:3] * lv[3:4], axis=-1, keepdims=True)) + lam_init)
    heads = []
    for h in range(N_HEADS):
        outs = []
        for (w0, lo, hi) in maps[h]:
            qw = q_ref[0, :, w0:w0 + LANE]
            if (lo, hi) != (0, LANE):
                qw = jnp.where(_lane_mask(LANE, lo, hi), qw, jnp.zeros_like(qw))
            outs.append(_softmax_pv(qw, kt_ref[0, w0:w0 + LANE, :], v_ref[0, h]))
        heads.append(outs[0] - lam * outs[1] if diff else outs[0])
    low = _lane_mask(LANE, 0, V_HEAD)
    acc = jnp.concatenate(
        [jnp.where(low, heads[h], pltpu.roll(heads[h + 1], V_HEAD, axis=1)) for h in range(0, N_HEADS, 2)],
        axis=1)
    if diff:
        acc = _group_norm(acc, g64_ref, 1.0 / DIFF_V) * sg_ref[...]
    o_ref[0] = acc.astype(BF16)


MAPS_MLA = tuple(((LANE * h, 0, LANE),) for h in range(N_HEADS))
MAPS_DIFF = tuple(tuple((LANE * (h // 2), 64 * (h % 2) + 32 * c, 64 * (h % 2) + 32 * c + 32) for c in range(2))
                  for h in range(N_HEADS))
MAPS_NA = tuple(((LANE * (h // 2), 64 * (h % 2), 64 * (h % 2) + 64),) for h in range(N_HEADS))


def _attn_call(q, kt, v, lw, maps, diff=False, lam_init=0.0):
    b, s, wq = q.shape
    sk = kt.shape[2]
    tq = min(s, ATTN_TQ)
    vh = v.reshape(b, sk, N_HEADS, V_HEAD).transpose(0, 2, 1, 3)
    v = jnp.concatenate([vh, jnp.ones_like(vh)], axis=-1)
    return pl.pallas_call(
        functools.partial(_attn_kernel, maps=maps, diff=diff, lam_init=lam_init),
        grid=(b, s // tq),
        in_specs=[
            pl.BlockSpec((1, tq, wq), lambda i, j: (i, j, 0)),
            pl.BlockSpec((1, wq, sk), lambda i, j: (i, 0, 0)),
            pl.BlockSpec((1, N_HEADS, sk, LANE), lambda i, j: (i, 0, 0, 0)),
            _const_spec((4, DIFF_DIM)),
            _const_spec((BR_W, BR_W)),
            _const_spec((1, BR_W)),
        ],
        out_specs=pl.BlockSpec((1, tq, BR_W), lambda i, j: (i, j, 0)),
        out_shape=jax.ShapeDtypeStruct((b, s, BR_W), BF16),
        compiler_params=_cparams(("parallel", "parallel")),
        name="attn_diff" if diff else "attn",
    )(q, kt, v, lw["diff_lam"], lw["g64"], lw["subln"])


_NT = (((1,), (1,)), ((), ()))


def _na_kernel(q_ref, k_ref, v_ref, kc_ref, vc_ref, bias_ref, o_ref, *, rows):
    j = pl.program_id(1)
    nj = rows // 2
    start = jnp.clip(2 * j - 4, 0, rows - NA_BAND_ROWS)
    base = pl.multiple_of(start * GRID_W, 2 * GRID_W)
    cls = jnp.where(j < 2, j, jnp.where(j >= nj - 2, j - (nj - 2) + 3, 2))
    band = NA_BAND_ROWS * GRID_W
    kw = k_ref[0, pl.ds(base, band), :]
    vw = v_ref[0, pl.ds(base, band), :]
    kc = kc_ref[0]
    vc = vc_ref[0]
    q = q_ref[0]
    acc = jnp.zeros((2 * GRID_W, BR_W), F32)
    for h in range(N_HEADS):
        qm = jnp.where(_lane_mask(BR_W, HEAD_DIM * h, HEAD_DIM * (h + 1)), q, jnp.zeros_like(q))
        s_loc = lax.dot_general(qm, kw, _NT, preferred_element_type=F32) + bias_ref[cls, h]
        s_ctx = lax.dot_general(qm, kc, _NT, preferred_element_type=F32)
        m = jnp.maximum(jnp.max(s_loc, axis=-1, keepdims=True), jnp.max(s_ctx, axis=-1, keepdims=True))
        p_loc = jnp.exp2(s_loc - m)
        p_ctx = jnp.exp2(s_ctx - m)
        l = jnp.sum(p_loc, axis=-1, keepdims=True) + jnp.sum(p_ctx, axis=-1, keepdims=True)
        o = (jnp.dot(p_ctx.astype(BF16), vc, preferred_element_type=F32)
             + jnp.dot(p_loc.astype(BF16), vw, preferred_element_type=F32)) * (1.0 / l)
        acc = jnp.where(_lane_mask(BR_W, HEAD_DIM * h, HEAD_DIM * (h + 1)), o, acc)
    o_ref[0] = acc.astype(BF16)


def _na_call(q, k, v, kc, vc, bias):
    b, s, w = q.shape
    n_ctx = kc.shape[1]
    rows = s // GRID_W
    tq = 2 * GRID_W
    return pl.pallas_call(
        functools.partial(_na_kernel, rows=rows),
        grid=(b, rows // 2),
        in_specs=[
            pl.BlockSpec((1, tq, w), lambda i, j: (i, j, 0)),
            pl.BlockSpec((1, s, w), lambda i, j: (i, 0, 0)),
            pl.BlockSpec((1, s, w), lambda i, j: (i, 0, 0)),
            pl.BlockSpec((1, n_ctx, w), lambda i, j: (i, 0, 0)),
            pl.BlockSpec((1, n_ctx, w), lambda i, j: (i, 0, 0)),
            _const_spec(bias.shape),
        ],
        out_specs=pl.BlockSpec((1, tq, w), lambda i, j: (i, j, 0)),
        out_shape=jax.ShapeDtypeStruct((b, s, w), BF16),
        compiler_params=_cparams(("parallel", "arbitrary")),
        name="na",
    )(q, k, v, kc, vc, bias)


def _merge_kernel(x_ref, mod_ref, n1g_ref, n2g_ref, uc_ref, om_ref, od_ref, on_ref,
                  gw_ref, gb_ref, wc_ref, wm_ref, wd_ref, wn_ref, wo_ref, rwt_ref, rb_ref,
                  x1_ref, h2_ref, ids_ref, wts_ref):
    x = x_ref[0]
    mod = mod_ref[0]
    h = _modulate(x, n1g_ref[...], mod[0:1], mod[1:2]).astype(BF16)
    y = jnp.zeros(x.shape, F32)
    branches = ((uc_ref, wc_ref), (om_ref, wm_ref), (od_ref, wd_ref), (on_ref, wn_ref))
    for i, (o_ref, w_ref) in enumerate(branches):
        lo = D_MODEL * i
        g = _sigmoid(jnp.dot(h, gw_ref[:, lo:lo + D_MODEL], preferred_element_type=F32)
                     + gb_ref[:, lo:lo + D_MODEL])
        y = y + g * jnp.dot(o_ref[0], w_ref[...], preferred_element_type=F32)
    out = jnp.dot(y.astype(BF16), wo_ref[...], preferred_element_type=F32)
    x1 = x + mod[2:3] * out
    x1_ref[0] = x1
    h2 = _modulate(x1, n2g_ref[...], mod[3:4], mod[4:5])
    for q in range(4):
        h2_ref[q, 0] = h2[:, ROW_Q * q:ROW_Q * (q + 1)]

    logits = lax.dot_general(rwt_ref[...], h2, _NT, preferred_element_type=F32,
                             precision=lax.Precision.HIGHEST) + rb_ref[...]
    eidx = lax.broadcasted_iota(jnp.int32, logits.shape, 0).astype(F32)
    vals, idxs = [], []
    cur = logits
    for _ in range(TOP_K):
        m = jnp.max(cur, axis=0, keepdims=True)
        idx = jnp.min(jnp.where(cur == m, eidx, float(N_EXPERTS)), axis=0, keepdims=True)
        vals.append(m)
        idxs.append(idx)
        cur = jnp.where(eidx == idx, -jnp.inf, cur)
    es = [jnp.exp(vk - vals[0]) for vk in vals]
    den = es[0] + es[1] + es[2] + es[3]
    ids_ref[0] = jnp.concatenate(idxs, axis=0).astype(jnp.int32)
    wts_ref[0] = jnp.concatenate([e / den for e in es], axis=0)


def _merge_call(x, mods, mod_row, lw, uc, om, od, on):
    b, s, d = x.shape
    t = min(s, 512)
    if mod_row is None:
        mod_map = lambda i, j: (i, 0, 0)
    else:
        mod_map = lambda i, j: (mod_row, 0, 0)
    tok = lambda w: pl.BlockSpec((1, t, w), lambda i, j: (i, j, 0))
    rt = pl.BlockSpec((1, TOP_K, t), lambda i, j: (i, 0, j))
    return pl.pallas_call(
        _merge_kernel,
        grid=(b, s // t),
        in_specs=[
            tok(d), pl.BlockSpec((1, 6, d), mod_map), _const_spec((1, d)), _const_spec((1, d)),
            tok(BR_W), tok(BR_W), tok(BR_W), tok(BR_W),
            _const_spec((d, N_BRANCH * d)), _const_spec((1, N_BRANCH * d)),
            _const_spec((BR_W, d)), _const_spec((BR_W, d)), _const_spec((BR_W, d)), _const_spec((BR_W, d)),
            _const_spec((d, d)), _const_spec((N_EXPERTS, d)), _const_spec((N_EXPERTS, 1)),
        ],
        out_specs=[tok(d), pl.BlockSpec((4, 1, t, ROW_Q), lambda i, j: (0, i, j, 0)), rt, rt],
        out_shape=[jax.ShapeDtypeStruct((b, s, d), F32), jax.ShapeDtypeStruct((4, b, s, ROW_Q), F32),
                   jax.ShapeDtypeStruct((b, TOP_K, s), jnp.int32), jax.ShapeDtypeStruct((b, TOP_K, s), F32)],
        compiler_params=_cparams(("parallel", "parallel")),
        name="merge",
    )(x, mods, lw["n1g"], lw["n2g"], uc, om, od, on, lw["gate_w"], lw["gate_b"],
      lw["conv_out"], lw["mla_out"], lw["diff_out"], lw["na_out"], lw["w_o"], lw["router_wt"], lw["router_b"])


def _route(ids, tile):
    n = ids.shape[1]
    p = TOP_K * n
    e = ids.reshape(p)
    onehot = (e[:, None] == jnp.arange(N_EXPERTS, dtype=jnp.int32)[None, :]).astype(jnp.int32)
    csum = jnp.cumsum(onehot, axis=0)
    counts = csum[-1]
    padded = ((counts + tile - 1) // tile) * tile
    gend = jnp.cumsum(padded)
    gstart = gend - padded
    slot = jnp.sum(onehot * (csum - 1 + gstart[None, :]), axis=1).astype(jnp.int32)
    n_tiles = p // tile + N_EXPERTS
    n_slots = n_tiles * tile
    pair_of_slot = jnp.full((n_slots,), -1, jnp.int32).at[slot].set(
        jnp.arange(p, dtype=jnp.int32), unique_indices=True)
    real = pair_of_slot >= 0
    src_tok = jnp.where(real, pair_of_slot, jnp.arange(n_slots, dtype=jnp.int32)) % n
    dst_row = jnp.where(real, pair_of_slot, p + jnp.arange(n_slots, dtype=jnp.int32))
    tile_start = jnp.arange(n_tiles, dtype=jnp.int32) * tile
    texp = jnp.minimum(jnp.searchsorted(gend, tile_start, side="right"), N_EXPERTS - 1).astype(jnp.int32)
    tval = (tile_start < gend[-1]).astype(jnp.int32)
    return src_tok.reshape(n_tiles, 1, tile), dst_row.reshape(n_tiles, 1, tile), texp, tval


def _ffn_kernel(texp_ref, tval_ref, src_ref, nsrc_ref, dst_ref, pdst_ref, h_hbm, wgu_ref, bgu_ref, wd_ref, bd_ref,
                y_hbm, xbuf, ybuf, sem_in, sem_out, *, tile, n_tiles):
    i = pl.program_id(0)
    slot = i % 2
    valid = tval_ref[i] > 0
    prev_valid = (i >= 1) & (tval_ref[jnp.maximum(i - 1, 0)] > 0)

    def gather_row(idx_ref, buf_slot, r):
        pltpu.make_async_copy(h_hbm.at[pl.ds(idx_ref[0, 0, r], 1)], xbuf.at[buf_slot, pl.ds(r, 1)],
                              sem_in.at[buf_slot]).start()

    def scatter_row(idx_ref, buf_slot, r):
        pltpu.make_async_copy(ybuf.at[buf_slot, pl.ds(r, 1)], y_hbm.at[pl.ds(idx_ref[0, 0, r], 1)],
                              sem_out.at[buf_slot]).start()

    def rolled(fn):
        def body(r, carry):
            fn(r)
            return carry

        lax.fori_loop(0, tile, body, 0, unroll=8)

    def wait_rows(buf, sem):
        pltpu.make_async_copy(h_hbm.at[pl.ds(0, tile)], buf, sem).wait()

    def ffn(x):
        gu = jnp.dot(x, wgu_ref[0], preferred_element_type=F32) + bgu_ref[0]
        g = jnp.minimum(gu[:, :D_FF], SWIGLU_LIMIT)
        u = jnp.clip(gu[:, D_FF:], -SWIGLU_LIMIT, SWIGLU_LIMIT)
        act = ((u + 1.0) * (g * _sigmoid(SWIGLU_ALPHA * g))).astype(BF16)
        return jnp.dot(act, wd_ref[0], preferred_element_type=F32) + bd_ref[0]

    @pl.when((i == 0) & valid)
    def _():
        rolled(lambda r: gather_row(src_ref, 0, r))
        wait_rows(xbuf.at[0], sem_in.at[0])
        x = xbuf[0].astype(BF16)
        rolled(lambda r: gather_row(nsrc_ref, 1, r))
        ybuf[0] = ffn(x)

    @pl.when((i >= 2) & valid & prev_valid)
    def _():
        wait_rows(ybuf.at[slot], sem_out.at[slot])

    @pl.when((i >= 1) & valid)
    def _():
        wait_rows(xbuf.at[slot], sem_in.at[slot])
        n_chunk = (2 * D_FF) // FFN_CHUNK
        per = tile // n_chunk
        gu = []
        for c in range(n_chunk):
            x = xbuf[slot].astype(BF16)
            for r in range(c * per, (c + 1) * per):
                gather_row(nsrc_ref, 1 - slot, r)
                scatter_row(pdst_ref, 1 - slot, r)
            lo = c * FFN_CHUNK
            gu.append(jnp.dot(x, wgu_ref[0, :, lo:lo + FFN_CHUNK], preferred_element_type=F32)
                      + bgu_ref[0, :, lo:lo + FFN_CHUNK])
        half = n_chunk // 2
        act = []
        for c in range(half):
            g = jnp.minimum(gu[c], SWIGLU_LIMIT)
            u = jnp.clip(gu[half + c], -SWIGLU_LIMIT, SWIGLU_LIMIT)
            act.append(((u + 1.0) * (g * _sigmoid(SWIGLU_ALPHA * g))).astype(BF16))
        act = jnp.concatenate(act, axis=1)
        for c in range(D_MODEL // FFN_CHUNK):
            lo = c * FFN_CHUNK
            ybuf[slot, :, lo:lo + FFN_CHUNK] = (
                jnp.dot(act, wd_ref[0, :, lo:lo + FFN_CHUNK], preferred_element_type=F32)
                + bd_ref[0, :, lo:lo + FFN_CHUNK])

    def drain(last_slot):
        wait_rows(xbuf.at[1 - last_slot], sem_in.at[1 - last_slot])
        wait_rows(ybuf.at[last_slot], sem_out.at[last_slot])

    @pl.when(jnp.logical_not(valid) & prev_valid)
    def _():
        @pl.when(i >= 2)
        def _():
            wait_rows(ybuf.at[slot], sem_out.at[slot])

        rolled(lambda r: scatter_row(pdst_ref, 1 - slot, r))
        drain(1 - slot)

    @pl.when((i == n_tiles - 1) & valid)
    def _():
        wait_rows(ybuf.at[1 - slot], sem_out.at[1 - slot])
        rolled(lambda r: scatter_row(dst_ref, slot, r))
        drain(slot)


def _ffn_call(h2, src_tok, dst_row, texp, tval, lw, tile, n_out_rows):
    n_tiles = src_tok.shape[0]
    d = h2.shape[1]
    idx_spec = lambda f: pl.BlockSpec((1, 1, tile), f, memory_space=pltpu.SMEM)
    grid_spec = pltpu.PrefetchScalarGridSpec(
        num_scalar_prefetch=2,
        grid=(n_tiles,),
        in_specs=[
            idx_spec(lambda i, te, tv: (i, 0, 0)),
            idx_spec(lambda i, te, tv: (jnp.minimum(i + 1, n_tiles - 1), 0, 0)),
            idx_spec(lambda i, te, tv: (i, 0, 0)),
            idx_spec(lambda i, te, tv: (jnp.maximum(i - 1, 0), 0, 0)),
            pl.BlockSpec(memory_space=pl.ANY),
            pl.BlockSpec((1, d, 2 * D_FF), lambda i, te, tv: (te[i], 0, 0)),
            pl.BlockSpec((1, 1, 2 * D_FF), lambda i, te, tv: (te[i], 0, 0)),
            pl.BlockSpec((1, D_FF, d), lambda i, te, tv: (te[i], 0, 0)),
            pl.BlockSpec((1, 1, d), lambda i, te, tv: (te[i], 0, 0)),
        ],
        out_specs=pl.BlockSpec(memory_space=pl.ANY),
        scratch_shapes=[pltpu.VMEM((2, tile, d), F32), pltpu.VMEM((2, tile, d), F32),
                        pltpu.SemaphoreType.DMA((2,)), pltpu.SemaphoreType.DMA((2,))],
    )
    return pl.pallas_call(
        functools.partial(_ffn_kernel, tile=tile, n_tiles=n_tiles),
        grid_spec=grid_spec,
        out_shape=jax.ShapeDtypeStruct((n_out_rows, d), F32),
        compiler_params=_cparams(("arbitrary",)),
        name="moe_ffn",
    )(texp, tval, src_tok, src_tok, dst_row, dst_row, h2, lw["exp_w_gu"], lw["exp_b_gu"], lw["exp_w_down"], lw["exp_b_down"])


def _combine_kernel(x1_ref, mod_ref, w_ref, y0_ref, y1_ref, y2_ref, y3_ref, o_ref):
    w = w_ref[0]
    acc = w[:, 0:1] * y0_ref[...]
    for k, y_ref in ((1, y1_ref), (2, y2_ref), (3, y3_ref)):
        acc = acc + w[:, k:k + 1] * y_ref[...]
    o_ref[0] = x1_ref[0] + mod_ref[0][5:6] * acc


def _combine_call(x1, mods, mod_row, wts, y):
    b, s, d = x1.shape
    t = min(s, 512)
    nt = s // t
    if mod_row is None:
        mod_map = lambda i, j: (i, 0, 0)
    else:
        mod_map = lambda i, j: (mod_row, 0, 0)
    wts = wts.transpose(0, 2, 1)
    y_spec = lambda k: pl.BlockSpec((t, d), lambda i, j: (k * b * nt + i * nt + j, 0))
    return pl.pallas_call(
        _combine_kernel,
        grid=(b, nt),
        in_specs=[
            pl.BlockSpec((1, t, d), lambda i, j: (i, j, 0)),
            pl.BlockSpec((1, 6, d), mod_map),
            pl.BlockSpec((1, t, TOP_K), lambda i, j: (i, j, 0)),
            y_spec(0), y_spec(1), y_spec(2), y_spec(3),
        ],
        out_specs=pl.BlockSpec((1, t, d), lambda i, j: (i, j, 0)),
        out_shape=jax.ShapeDtypeStruct((b, s, d), F32),
        compiler_params=_cparams(("parallel", "parallel")),
        name="moe_combine",
    )(x1, mods, wts, y, y, y, y)


def _moe(x1, h2, ids, wts, mods, mod_row, lw):
    b, s, d = x1.shape
    n = b * s
    tile = 512 if TOP_K * n >= 512 * N_EXPERTS * 4 else 256
    ids_flat = ids.transpose(1, 0, 2).reshape(TOP_K, n)
    src_tok, dst_row, texp, tval = _route(ids_flat, tile)
    n_out_rows = TOP_K * n + src_tok.shape[0] * tile
    y = _ffn_call(h2.reshape(n, d), src_tok, dst_row, texp, tval, lw, tile, n_out_rows)
    return _combine_call(x1, mods, mod_row, wts, y)


SC_WINDOW = 128
ROW_Q = D_MODEL // 4


def _route_slots(ids, tile):
    n = ids.shape[1]
    p = TOP_K * n
    e = ids.reshape(p)
    onehot = (e[:, None] == jnp.arange(N_EXPERTS, dtype=jnp.int32)[None, :]).astype(jnp.int32)
    csum = jnp.cumsum(onehot, axis=0)
    counts = csum[-1]
    padded = ((counts + tile - 1) // tile) * tile
    gend = jnp.cumsum(padded)
    gstart = gend - padded
    slot = jnp.sum(onehot * (csum - 1 + gstart[None, :]), axis=1).astype(jnp.int32)
    n_tiles = p // tile + N_EXPERTS
    tile_start = jnp.arange(n_tiles, dtype=jnp.int32) * tile
    texp = jnp.minimum(jnp.searchsorted(gend, tile_start, side="right"), N_EXPERTS - 1).astype(jnp.int32)
    nreal = jnp.clip(gstart[texp] + counts[texp] - tile_start, 0, tile)
    nreal = jnp.where(tile_start < gend[-1], nreal, 0).astype(jnp.int32)
    return slot, texp, nreal, n_tiles


def _sc_mesh():
    return plsc.VectorSubcoreMesh(core_axis_name="c", subcore_axis_name="s")


def _sc_dispatch(hq, idx, n_slots):
    _, n, w = hq.shape
    src = hq.reshape(4 * n, w)
    m = idx.shape[0]
    blocks_per_q = n // SC_WINDOW
    per_q = TOP_K * blocks_per_q

    @pl.kernel(out_type=jax.ShapeDtypeStruct((4 * n_slots, w), hq.dtype), mesh=_sc_mesh(), scratch_types=[])
    def kern(x_hbm, i_hbm, o_hbm):
        def body(x_vmem, i_vmem):
            pltpu.sync_copy(x_vmem, o_hbm.at[i_vmem.at[0]])

        pltpu.emit_pipeline(
            body,
            grid=(m // SC_WINDOW,),
            in_specs=[
                pl.BlockSpec((SC_WINDOW, w), index_map=lambda i: ((i // per_q) * blocks_per_q + i % blocks_per_q, 0)),
                pl.BlockSpec((1, SC_WINDOW), index_map=lambda i: (0, i)),
            ],
            out_specs=[],
            core_axis_name=("c", "s"),
            dimension_semantics=(pltpu.PARALLEL,),
        )(x_hbm, i_hbm)

    return kern(src, idx.reshape(1, m)).reshape(4, n_slots, w)


def _sc_collect(ys, idx):
    _, n_slots, w = ys.shape
    src = ys.reshape(4 * n_slots, w)
    m = idx.shape[0]

    @pl.kernel(out_type=jax.ShapeDtypeStruct((m, w), ys.dtype), mesh=_sc_mesh(), scratch_types=[])
    def kern(x_hbm, i_hbm, o_hbm):
        def body(i_vmem, o_vmem):
            pltpu.sync_copy(x_hbm.at[i_vmem.at[0]], o_vmem)

        pltpu.emit_pipeline(
            body,
            grid=(m // SC_WINDOW,),
            in_specs=[pl.BlockSpec((1, SC_WINDOW), index_map=lambda i: (0, i))],
            out_specs=[pl.BlockSpec((SC_WINDOW, w), index_map=lambda i: (i, 0))],
            core_axis_name=("c", "s"),
            dimension_semantics=(pltpu.PARALLEL,),
        )(i_hbm, o_hbm)

    return kern(src, idx.reshape(1, m))


def _ffn_sorted_kernel(texp_ref, nreal_ref, x_ref, wgu_ref, bgu_ref, wd_ref, bd_ref, y_ref, *, tile):
    i = pl.program_id(0)
    nreal = nreal_ref[i]

    @pl.when(nreal > 0)
    def _():
        x = jnp.concatenate([x_ref[q] for q in range(4)], axis=1)
        row = lax.broadcasted_iota(jnp.int32, (tile, 1), 0)
        x = jnp.where(row < nreal, x, 0.0).astype(BF16)
        gu = jnp.dot(x, wgu_ref[0], preferred_element_type=F32) + bgu_ref[0]
        g = jnp.minimum(gu[:, :D_FF], SWIGLU_LIMIT)
        u = jnp.clip(gu[:, D_FF:], -SWIGLU_LIMIT, SWIGLU_LIMIT)
        act = ((u + 1.0) * (g * _sigmoid(SWIGLU_ALPHA * g))).astype(BF16)
        y = jnp.dot(act, wd_ref[0], preferred_element_type=F32) + bd_ref[0]
        for q in range(4):
            y_ref[q] = y[:, ROW_Q * q:ROW_Q * (q + 1)]

    @pl.when(nreal == 0)
    def _():
        y_ref[...] = jnp.zeros(y_ref.shape, F32)


def _ffn_sorted_call(xs, texp, nreal, lw, tile):
    _, n_slots, w = xs.shape
    n_tiles = n_slots // tile
    d = D_MODEL
    off = lw["exp_off"]
    grid_spec = pltpu.PrefetchScalarGridSpec(
        num_scalar_prefetch=2,
        grid=(n_tiles,),
        in_specs=[
            pl.BlockSpec((4, tile, w), lambda i, te, nr: (0, i, 0)),
            pl.BlockSpec((1, d, 2 * D_FF), lambda i, te, nr: (te[i] + off, 0, 0)),
            pl.BlockSpec((1, 1, 2 * D_FF), lambda i, te, nr: (te[i] + off, 0, 0)),
            pl.BlockSpec((1, D_FF, d), lambda i, te, nr: (te[i] + off, 0, 0)),
            pl.BlockSpec((1, 1, d), lambda i, te, nr: (te[i] + off, 0, 0)),
        ],
        out_specs=pl.BlockSpec((4, tile, w), lambda i, te, nr: (0, i, 0)),
    )
    return pl.pallas_call(
        functools.partial(_ffn_sorted_kernel, tile=tile),
        grid_spec=grid_spec,
        out_shape=jax.ShapeDtypeStruct((4, n_slots, w), F32),
        compiler_params=_cparams(("arbitrary",)),
        name="moe_ffn",
    )(texp, nreal, xs, lw["exp_w_gu"], lw["exp_b_gu"], lw["exp_w_down"], lw["exp_b_down"])


def _combine_q_kernel(x1_ref, mod_ref, w_ref, y_ref, o_ref):
    w = w_ref[0]
    g2 = mod_ref[0][5:6]
    for q in range(4):
        lo = ROW_Q * q
        acc = w[:, 0:1] * y_ref[q, 0, 0]
        for k in range(1, TOP_K):
            acc = acc + w[:, k:k + 1] * y_ref[q, k, 0]
        o_ref[0, :, lo:lo + ROW_Q] = x1_ref[0, :, lo:lo + ROW_Q] + g2[:, lo:lo + ROW_Q] * acc


def _combine_q_call(x1, mods, mod_row, wts, y):
    b, s, d = x1.shape
    t = min(s, 512)
    if mod_row is None:
        mod_map = lambda i, j: (i, 0, 0)
    else:
        mod_map = lambda i, j: (mod_row, 0, 0)
    wts = wts.transpose(0, 2, 1)
    y = y.reshape(4, TOP_K, b, s, ROW_Q)
    return pl.pallas_call(
        _combine_q_kernel,
        grid=(b, s // t),
        in_specs=[
            pl.BlockSpec((1, t, d), lambda i, j: (i, j, 0)),
            pl.BlockSpec((1, 6, d), mod_map),
            pl.BlockSpec((1, t, TOP_K), lambda i, j: (i, j, 0)),
            pl.BlockSpec((4, TOP_K, 1, t, ROW_Q), lambda i, j: (0, 0, i, j, 0)),
        ],
        out_specs=pl.BlockSpec((1, t, d), lambda i, j: (i, j, 0)),
        out_shape=jax.ShapeDtypeStruct((b, s, d), F32),
        compiler_params=_cparams(("parallel", "parallel")),
        name="moe_combine",
    )(x1, mods, wts, y)


def _moe_sc(x1, hq, ids, wts, mods, mod_row, lw):
    b, s, d = x1.shape
    n = b * s
    tile = 512 if TOP_K * n >= 512 * N_EXPERTS * 4 else 256
    ids_flat = ids.transpose(1, 0, 2).reshape(TOP_K, n)
    slot, texp, nreal, n_tiles = _route_slots(ids_flat, tile)
    n_slots = n_tiles * tile
    idx = (slot[None, :] + (jnp.arange(4, dtype=jnp.int32) * n_slots)[:, None]).reshape(-1)
    xs = _sc_dispatch(hq.reshape(4, n, ROW_Q), idx, n_slots)
    ys = _ffn_sorted_call(xs, texp, nreal, lw, tile)
    y = _sc_collect(ys, idx)
    return _combine_q_call(x1, mods, mod_row, wts, y)


def _layer_weights(l, p, lam_init):
    w = p["w_in"][l]
    d = w.shape[0]
    zcols = lambda n: jnp.zeros((d, n), w.dtype)
    regroup = lambda blk: blk.reshape(d, N_HEADS, 3, HEAD_DIM).transpose(0, 2, 1, 3).reshape(d, 3 * BR_W)
    w_in = jnp.concatenate([
        w[:, :A_IN],
        w[:, OFF_B:OFF_B + Q_LORA], zcols(P_CKV - P_CQ - Q_LORA),
        w[:, OFF_B + Q_LORA:OFF_B + Q_LORA + KV_LORA],
        w[:, OFF_B + Q_LORA + KV_LORA:OFF_C], zcols(P_DQ - P_KPE - QK_ROPE),
        regroup(w[:, OFF_C:OFF_D]), regroup(w[:, OFF_D:]),
    ], axis=1)
    assert w_in.shape[1] == PROJ_W

    def head_slots(w3, slot):
        w3 = jnp.pad(w3, ((0, 0), (0, 0), (0, slot - w3.shape[2])))
        return w3.reshape(w3.shape[0], N_HEADS * slot)

    wuq = head_slots(p["mla_w_uq"][l].reshape(Q_LORA, N_HEADS, MLA_QK), LANE)
    wuq = jnp.pad(wuq, ((0, 256 - Q_LORA), (0, 0)))
    wukv = p["mla_w_ukv"][l].reshape(KV_LORA, N_HEADS, QK_NOPE + V_HEAD)
    wk = head_slots(wukv[:, :, :QK_NOPE], LANE)
    wv = head_slots(wukv[:, :, QK_NOPE:], V_HEAD)
    ppe = np.zeros((LANE, 512), np.float32)
    for h in range(N_HEADS):
        for i in range(QK_ROPE):
            ppe[i, h * LANE + QK_NOPE + i] = 1.0

    def slot_gain(g, scale):
        g = jnp.concatenate([g * scale, jnp.zeros((LANE - MLA_QK,), F32)])
        return jnp.tile(g, N_HEADS)

    def row512(v):
        return jnp.concatenate([v, jnp.zeros((512 - v.shape[0],), F32)])

    gains = jnp.stack([
        row512(p["mla_cq_g"][l]),
        slot_gain(p["mla_qn_g"][l], MLA_QK ** -0.5 * LOG2E),
        row512(p["mla_ckv_g"][l]),
        slot_gain(p["mla_kn_g"][l], 1.0),
        row512(jnp.tile(p["diff_qn_g"][l], 2 * N_HEADS) * DIFF_DIM ** -0.5 * LOG2E),
        row512(jnp.tile(p["diff_kn_g"][l], 2 * N_HEADS)),
        row512(jnp.tile(p["na_qn_g"][l], N_HEADS) * HEAD_DIM ** -0.5 * LOG2E),
        row512(jnp.tile(p["na_kn_g"][l], N_HEADS)),
    ])
    conv_w = jnp.concatenate([p["conv_w"][l], jnp.zeros((1, CONV_CH), F32)], axis=0)
    return dict(
        n1g=p["norm1_g"][l][None, :], n2g=p["norm2_g"][l][None, :],
        w_in=w_in.astype(BF16), gains=gains,
        wuq=wuq.astype(BF16), wk=wk.astype(BF16), wv=wv.astype(BF16), ppe=jnp.asarray(ppe, BF16),
        g96=jnp.asarray(_group_ones(512, LANE, MLA_QK), BF16),
        g32=jnp.asarray(_group_ones(BR_W, DIFF_DIM, DIFF_DIM), BF16),
        g64=jnp.asarray(_group_ones(BR_W, HEAD_DIM, HEAD_DIM), BF16),
        rm=jnp.asarray(_rot_matrix(512, LANE, QK_NOPE, QK_ROPE // 2), BF16),
        rd=jnp.asarray(_rot_matrix(BR_W, DIFF_DIM, 0, DIFF_DIM // 2), BF16),
        conv_w=conv_w, conv_b=p["conv_b"][l][None, :],
        conv_ln_g=p["conv_ln_g"][l][None, :], conv_ln_b=p["conv_ln_b"][l][None, :],
        diff_lam=p["diff_lam"][l],
        subln=(jnp.tile(p["diff_subln_g"][l], N_HEADS) * (1.0 - lam_init))[None, :],
        gate_w=p["gate_w"][l].astype(BF16), gate_b=p["gate_b"][l][None, :],
        conv_out=p["conv_out"][l].astype(BF16), mla_out=p["mla_out"][l].astype(BF16),
        diff_out=p["diff_out"][l].astype(BF16), na_out=p["na_out"][l].astype(BF16),
        w_o=p["w_o"][l].astype(BF16),
        router_wt=p["router_w"][l].T, router_b=p["router_b"][l][:, None],
        exp_off=l * N_EXPERTS,
        exp_w_gu=p["exp_w_gu"].astype(BF16).reshape((-1,) + p["exp_w_gu"].shape[2:]),
        exp_b_gu=p["exp_b_gu"].reshape(-1, 1, 2 * D_FF),
        exp_w_down=p["exp_w_down"].astype(BF16).reshape((-1,) + p["exp_w_down"].shape[2:]),
        exp_b_down=p["exp_b_down"].reshape(-1, 1, D_MODEL),
    )


def _kt(kc, k):
    return jnp.concatenate([kc, k], axis=1).transpose(0, 2, 1)


def kernel(x, c, ctx, c_ctx, ada_w, ada_b, norm1_g, norm2_g, w_in, conv_w, conv_b, conv_ln_g, conv_ln_b, conv_out, mla_cq_g, mla_ckv_g, mla_w_uq, mla_w_ukv, mla_qn_g, mla_kn_g, mla_out, diff_qn_g, diff_kn_g, diff_lam, diff_subln_g, diff_out, na_qn_g, na_kn_g, na_rpb, na_out, gate_w, gate_b, w_o, router_w, router_b, exp_w_gu, exp_b_gu, exp_w_down, exp_b_down):
    p = dict(norm1_g=norm1_g, norm2_g=norm2_g, w_in=w_in, conv_w=conv_w, conv_b=conv_b,
             conv_ln_g=conv_ln_g, conv_ln_b=conv_ln_b, conv_out=conv_out, mla_cq_g=mla_cq_g,
             mla_ckv_g=mla_ckv_g, mla_w_uq=mla_w_uq, mla_w_ukv=mla_w_ukv, mla_qn_g=mla_qn_g,
             mla_kn_g=mla_kn_g, mla_out=mla_out, diff_qn_g=diff_qn_g, diff_kn_g=diff_kn_g,
             diff_lam=diff_lam, diff_subln_g=diff_subln_g, diff_out=diff_out, na_qn_g=na_qn_g,
             na_kn_g=na_kn_g, na_out=na_out, gate_w=gate_w, gate_b=gate_b, w_o=w_o,
             router_w=router_w, router_b=router_b, exp_w_gu=exp_w_gu, exp_b_gu=exp_b_gu,
             exp_w_down=exp_w_down, exp_b_down=exp_b_down)
    b, s, d = x.shape
    n_ctx = ctx.shape[1]
    depth = ada_w.shape[0]
    rows = s // GRID_W
    assert d == D_MODEL and s % (2 * GRID_W) == 0 and rows >= NA_BAND_ROWS and n_ctx % LANE == 0

    mod_rows = -(-(b + 1) // 8) * 8
    cs = jnp.concatenate([c, c_ctx[None, :], jnp.zeros((mod_rows - b - 1, d), F32)], axis=0)
    mods_all = _ada_call(cs, ada_w, ada_b).reshape(depth, mod_rows, 6, d)

    tabs_x = _rope_lane_tables(s)
    tabs_c = (jnp.ones((n_ctx, 512), F32), jnp.zeros((n_ctx, 512), F32),
              jnp.ones((n_ctx, BR_W), F32), jnp.zeros((n_ctx, BR_W), F32))

    xc = ctx
    for l in range(depth):
        last = l == depth - 1
        lam_init = 0.8 - 0.6 * math.exp(-0.3 * l)
        lw = _layer_weights(l, p, lam_init)
        mods = mods_all[l]
        bias = _na_bias_tables(na_rpb[l], rows)

        u, mq, mk, mv, dq, dk, dv, nq, nk, nv = _proj_call(x, mods, None, lw, tabs_x, True)
        uc, mqc, mkc, mvc, dqc, dkc, dvc, nqc, nkc, nvc = _proj_call(xc, mods, b, lw, tabs_c, False)

        y_conv = _conv_call(u, lw)
        y_mla = _attn_call(mq, _kt(mkc, mk), jnp.concatenate([mvc, mv], axis=1), lw, MAPS_MLA)
        y_diff = _attn_call(dq, _kt(dkc, dk), jnp.concatenate([dvc, dv], axis=1), lw, MAPS_DIFF,
                            diff=True, lam_init=lam_init)
        y_na = _na_call(nq, nk, nv, nkc, nvc, bias)
        x1, h2, ids, wts = _merge_call(x, mods, None, lw, y_conv, y_mla, y_diff, y_na)
        x = _moe_sc(x1, h2, ids, wts, mods, None, lw)

        if not last:
            yc_conv = _conv_call(uc, lw)
            yc_mla = _attn_call(mqc, mkc.transpose(0, 2, 1), mvc, lw, MAPS_MLA)
            yc_diff = _attn_call(dqc, dkc.transpose(0, 2, 1), dvc, lw, MAPS_DIFF, diff=True, lam_init=lam_init)
            yc_na = _attn_call(nqc, nkc.transpose(0, 2, 1), nvc, lw, MAPS_NA)
            xc1, h2c, idsc, wtsc = _merge_call(xc, mods, b, lw, yc_conv, yc_mla, yc_diff, yc_na)
            xc = _moe_sc(xc1, h2c, idsc, wtsc, mods, b, lw)
    return x
```

```python
import functools
import math

import numpy as np
import jax
import jax.numpy as jnp
from jax import lax
from jax.experimental import pallas as pl
from jax.experimental.pallas import tpu as pltpu
from jax.experimental.pallas import tpu_sc as plsc

F32 = jnp.float32
BF16 = jnp.bfloat16

D_MODEL = 1024
GRID_W = 64
N_BRANCH = 4
N_HEADS = 4
HEAD_DIM = 64
CONV_CH = 256
CONV_WIDTH = 31
Q_LORA = 192
KV_LORA = 128
QK_NOPE = 64
QK_ROPE = 32
V_HEAD = 64
DIFF_DIM = 32
DIFF_V = 2 * DIFF_DIM
NA_KH = 8
NA_KW = 16
ROPE_DIM = 32
ROPE_BASE = 10000.0
N_EXPERTS = 32
TOP_K = 4
D_FF = 1024
SWIGLU_LIMIT = 7.0
SWIGLU_ALPHA = 1.702
EPS = 1e-6
NEG_INF = -1e30

A_IN = 2 * CONV_CH
B_IN = Q_LORA + KV_LORA + QK_ROPE
C_IN = N_HEADS * (4 * DIFF_DIM + DIFF_V)
D_IN = N_HEADS * 3 * HEAD_DIM
OFF_B = A_IN
OFF_C = OFF_B + B_IN
OFF_D = OFF_C + C_IN

LANE = 128
MLA_QK = QK_NOPE + QK_ROPE
BR_W = N_HEADS * HEAD_DIM
PROJ_W = 2560
NA_BAND_ROWS = 10
FFN_CHUNK = 256
ATTN_TQ = 512
NA_PAIRS = 2
LOG2E = math.log2(math.e)
VMEM_LIMIT = 52 * 1024 * 1024

P_A, P_G, P_CQ, P_CKV, P_KPE = 0, 256, 512, 768, 896
P_DQ, P_DK, P_DV = 1024, 1280, 1536
P_NQ, P_NK, P_NV = 1792, 2048, 2304


def _sigmoid(x):
    return 1.0 / (1.0 + jnp.exp(-x))


def _modulate(x, g, shift, scale):
    ms = jnp.mean(x * x, axis=-1, keepdims=True)
    return (x * lax.rsqrt(ms + EPS) * g) * (1.0 + scale) + shift


def _cparams(sem):
    return pltpu.CompilerParams(dimension_semantics=sem, vmem_limit_bytes=VMEM_LIMIT)


def _const_spec(shape):
    n = len(shape)
    return pl.BlockSpec(shape, lambda *_: (0,) * n)


def _group_ones(width, slot, real):
    i = np.arange(width)
    valid = (i % slot) < real
    same = (i[:, None] // slot) == (i[None, :] // slot)
    return (same & valid[:, None] & valid[None, :]).astype(np.float32)


def _rot_matrix(width, slot, start, half):
    r = np.zeros((width, width), np.float32)
    for s0 in range(0, width, slot):
        for i in range(half):
            a, b = s0 + start + i, s0 + start + half + i
            r[b, a] = -1.0
            r[a, b] = 1.0
    return r


def _rope_lane_tables(n_tokens):
    t = jnp.arange(n_tokens, dtype=jnp.int32)
    rows = (t // GRID_W).astype(F32)
    cols = (t % GRID_W).astype(F32)
    axis_dim = ROPE_DIM // 2
    inv = ROPE_BASE ** (-jnp.arange(0, axis_dim, 2, dtype=F32) / axis_dim)
    theta = jnp.concatenate([rows[:, None] * inv, cols[:, None] * inv], axis=-1)
    cos, sin = jnp.cos(theta), jnp.sin(theta)
    half = ROPE_DIM // 2
    ones = jnp.ones((n_tokens, QK_NOPE), F32)
    zeros = jnp.zeros((n_tokens, QK_NOPE), F32)
    pad1 = jnp.ones((n_tokens, LANE - MLA_QK), F32)
    pad0 = jnp.zeros((n_tokens, LANE - MLA_QK), F32)
    cm = jnp.tile(jnp.concatenate([ones, cos, cos, pad1], -1), (1, N_HEADS))
    sm = jnp.tile(jnp.concatenate([zeros, sin, sin, pad0], -1), (1, N_HEADS))
    cd = jnp.tile(jnp.concatenate([cos, cos], -1), (1, 2 * N_HEADS))
    sd = jnp.tile(jnp.concatenate([sin, sin], -1), (1, 2 * N_HEADS))
    assert half * 2 == DIFF_DIM
    return cm, sm, cd, sd


def _na_bias_tables(rpb, rows):
    kh = min(NA_KH, rows)
    nj = rows // 2
    reps = np.array([0, 1, 2, nj - 2, nj - 1])
    n_ro, n_co = 2 * NA_KH - 1, 2 * NA_KW - 1
    start = np.clip(2 * reps - 4, 0, rows - NA_BAND_ROWS)
    r = 2 * reps[:, None] + np.arange(2)[None, :]
    kr = start[:, None] + np.arange(NA_BAND_ROWS)[None, :]
    row_start = np.clip(r - kh // 2, 0, rows - kh)
    vr = (kr[:, None, :] >= row_start[:, :, None]) & (kr[:, None, :] < row_start[:, :, None] + kh)
    ro = np.clip(kr[:, None, :] - r[:, :, None] + NA_KH - 1, 0, n_ro - 1)
    qc = np.arange(GRID_W)
    win_start = np.clip(qc - NA_KW // 2, 0, GRID_W - NA_KW)
    vc = (qc[None, :] >= win_start[:, None]) & (qc[None, :] < win_start[:, None] + NA_KW)
    co = np.clip(qc[None, :] - qc[:, None] + NA_KW - 1, 0, n_co - 1)
    rsel = (ro[..., None] == np.arange(n_ro)).astype(np.float32)
    csel = (co[None] == np.arange(n_co)[:, None, None]).astype(np.float32)
    hi = lax.Precision.HIGHEST
    t1 = jnp.einsum("cqav,hvw->hcqaw", rsel, rpb.astype(F32), precision=hi)
    b = jnp.einsum("hcqaw,wxy->chqxay", t1, csel, precision=hi)
    valid = vr[:, None, :, None, :, None] & vc[None, None, None, :, None, :]
    b = jnp.where(valid, b * LOG2E, NEG_INF)
    return b.reshape(len(reps), N_HEADS, 2 * GRID_W, NA_BAND_ROWS * GRID_W)


def _ada_kernel(c_ref, w_ref, b_ref, o_ref):
    c = c_ref[...]
    s = c * _sigmoid(c)
    o_ref[0] = jnp.dot(s, w_ref[0], preferred_element_type=F32,
                       precision=lax.Precision.HIGHEST) + b_ref[0]


def _ada_call(cs, ada_w, ada_b):
    depth, d, n = ada_w.shape
    rows = cs.shape[0]
    tn = 1536
    return pl.pallas_call(
        _ada_kernel,
        grid=(depth, n // tn),
        in_specs=[
            pl.BlockSpec((rows, d), lambda l, j: (0, 0)),
            pl.BlockSpec((1, d, tn), lambda l, j: (l, 0, j)),
            pl.BlockSpec((1, 1, tn), lambda l, j: (l, 0, j)),
        ],
        out_specs=pl.BlockSpec((1, rows, tn), lambda l, j: (l, 0, j)),
        out_shape=jax.ShapeDtypeStruct((depth, rows, n), F32),
        compiler_params=_cparams(("arbitrary", "arbitrary")),
        name="ada",
    )(cs, ada_w, ada_b.reshape(depth, 1, n))


def _group_norm(x, ones_ref, inv_n):
    sq = (x * x).astype(BF16)
    ms = jnp.dot(sq, ones_ref[...], preferred_element_type=F32) * inv_n
    return x * lax.rsqrt(ms + EPS)


def _rope(x, rot_ref, cos_ref, sin_ref):
    rot = jnp.dot(x.astype(BF16), rot_ref[...], preferred_element_type=F32)
    return x * cos_ref[...] + rot * sin_ref[...]


def _proj_kernel(x_ref, mod_ref, n1g_ref, win_ref, gains_ref, wuq_ref, wk_ref, ppe_ref, wv_ref,
                 g96_ref, g32_ref, g64_ref, rm_ref, rd_ref, cm_ref, sm_ref, cd_ref, sd_ref,
                 u_ref, mq_ref, mk_ref, mv_ref, dq_ref, dk_ref, dv_ref, nq_ref, nk_ref, nv_ref,
                 *, use_rope):
    x = x_ref[0]
    mod = mod_ref[0]
    gains = gains_ref[...]
    h = _modulate(x, n1g_ref[...], mod[0:1], mod[1:2]).astype(BF16)
    proj = jnp.dot(h, win_ref[...], preferred_element_type=F32)

    u_ref[0] = proj[:, P_A:P_A + CONV_CH] * _sigmoid(proj[:, P_G:P_G + CONV_CH])

    cq = proj[:, P_CQ:P_CQ + 256]
    ms = jnp.sum(cq * cq, axis=-1, keepdims=True) * (1.0 / Q_LORA)
    cqn = (cq * lax.rsqrt(ms + EPS) * gains[0:1, :256]).astype(BF16)
    q = jnp.dot(cqn, wuq_ref[...], preferred_element_type=F32)
    q = _group_norm(q, g96_ref, 1.0 / MLA_QK) * gains[1:2, :]
    if use_rope:
        q = _rope(q, rm_ref, cm_ref, sm_ref)
    mq_ref[0] = q.astype(BF16)

    ckv = proj[:, P_CKV:P_CKV + KV_LORA]
    ms = jnp.mean(ckv * ckv, axis=-1, keepdims=True)
    ckvn = (ckv * lax.rsqrt(ms + EPS) * gains[2:3, :KV_LORA]).astype(BF16)
    kpe = proj[:, P_KPE:P_KPE + LANE].astype(BF16)
    k = (jnp.dot(ckvn, wk_ref[...], preferred_element_type=F32)
         + jnp.dot(kpe, ppe_ref[...], preferred_element_type=F32))
    k = _group_norm(k, g96_ref, 1.0 / MLA_QK) * gains[3:4, :]
    if use_rope:
        k = _rope(k, rm_ref, cm_ref, sm_ref)
    mk_ref[0] = k.astype(BF16)
    mv_ref[0] = jnp.dot(ckvn, wv_ref[...], preferred_element_type=F32).astype(BF16)

    qd = _group_norm(proj[:, P_DQ:P_DQ + BR_W], g32_ref, 1.0 / DIFF_DIM) * gains[4:5, :BR_W]
    kd = _group_norm(proj[:, P_DK:P_DK + BR_W], g32_ref, 1.0 / DIFF_DIM) * gains[5:6, :BR_W]
    if use_rope:
        qd = _rope(qd, rd_ref, cd_ref, sd_ref)
        kd = _rope(kd, rd_ref, cd_ref, sd_ref)
    dq_ref[0] = qd.astype(BF16)
    dk_ref[0] = kd.astype(BF16)
    dv_ref[0] = proj[:, P_DV:P_DV + BR_W].astype(BF16)

    qn = _group_norm(proj[:, P_NQ:P_NQ + BR_W], g64_ref, 1.0 / HEAD_DIM) * gains[6:7, :BR_W]
    kn = _group_norm(proj[:, P_NK:P_NK + BR_W], g64_ref, 1.0 / HEAD_DIM) * gains[7:8, :BR_W]
    nq_ref[0] = qn.astype(BF16)
    nk_ref[0] = kn.astype(BF16)
    nv_ref[0] = proj[:, P_NV:P_NV + BR_W].astype(BF16)


def _proj_call(x, mods, mod_row, lw, tabs, use_rope):
    b, s, d = x.shape
    t = min(s, 512)
    grid = (b, s // t)
    if mod_row is None:
        mod_map = lambda i, j: (i, 0, 0)
    else:
        mod_map = lambda i, j: (mod_row, 0, 0)
    tok = lambda w: pl.BlockSpec((1, t, w), lambda i, j: (i, j, 0))
    tab = lambda w: pl.BlockSpec((t, w), lambda i, j: (j, 0))
    in_specs = [
        tok(d),
        pl.BlockSpec((1, 6, d), mod_map),
        _const_spec((1, d)),
        _const_spec((d, PROJ_W)),
        _const_spec((8, 512)),
        _const_spec((256, 512)),
        _const_spec((KV_LORA, 512)),
        _const_spec((LANE, 512)),
        _const_spec((KV_LORA, BR_W)),
        _const_spec((512, 512)),
        _const_spec((BR_W, BR_W)),
        _const_spec((BR_W, BR_W)),
        _const_spec((512, 512)),
        _const_spec((BR_W, BR_W)),
        tab(512), tab(512), tab(BR_W), tab(BR_W),
    ]
    widths = [CONV_CH, 512, 512, BR_W, BR_W, BR_W, BR_W, BR_W, BR_W, BR_W]
    dtypes = [F32] + [BF16] * 9
    out_specs = [tok(w) for w in widths]
    out_shape = [jax.ShapeDtypeStruct((b, s, w), dt) for w, dt in zip(widths, dtypes)]
    return pl.pallas_call(
        functools.partial(_proj_kernel, use_rope=use_rope),
        grid=grid, in_specs=in_specs, out_specs=out_specs, out_shape=out_shape,
        compiler_params=_cparams(("parallel", "parallel")),
        name="proj",
    )(x, mods, lw["n1g"], lw["w_in"], lw["gains"], lw["wuq"], lw["wk"], lw["ppe"], lw["wv"],
      lw["g96"], lw["g32"], lw["g64"], lw["rm"], lw["rd"], tabs[0], tabs[1], tabs[2], tabs[3])


CONV_TILE = 128
CONV_PAD = 16


def _conv_kernel(u_ref, w_ref, cb_ref, lg_ref, lb_ref, o_ref, pad_ref, *, seq):
    zeros = jnp.zeros((CONV_PAD, CONV_CH), F32)
    pad_ref[0:CONV_PAD, :] = zeros
    pad_ref[CONV_PAD + seq:2 * CONV_PAD + seq, :] = zeros

    def fill(i, carry):
        base = pl.multiple_of(i * CONV_TILE, CONV_TILE)
        pad_ref[pl.ds(base + CONV_PAD, CONV_TILE), :] = u_ref[0, pl.ds(base, CONV_TILE), :]
        return carry

    lax.fori_loop(0, seq // CONV_TILE, fill, 0)
    w = w_ref[...]
    cb, lg, lb = cb_ref[...], lg_ref[...], lb_ref[...]

    def tile(i, carry):
        base = pl.multiple_of(i * CONV_TILE, CONV_TILE)
        win = pad_ref[pl.ds(base, CONV_TILE + 2 * CONV_PAD), :]
        acc = jnp.zeros((CONV_TILE, CONV_CH), F32)
        for j in range(CONV_WIDTH):
            acc = acc + win[j + 1:j + 1 + CONV_TILE, :] * w[j:j + 1, :]
        c = acc + cb
        mu = jnp.mean(c, axis=-1, keepdims=True)
        cc = c - mu
        var = jnp.mean(cc * cc, axis=-1, keepdims=True)
        y = cc * lax.rsqrt(var + EPS) * lg + lb
        o_ref[0, pl.ds(base, CONV_TILE), :] = (y * _sigmoid(y)).astype(BF16)
        return carry

    lax.fori_loop(0, seq // CONV_TILE, tile, 0)


def _conv_call(u, lw):
    b, s, ch = u.shape
    return pl.pallas_call(
        functools.partial(_conv_kernel, seq=s),
        grid=(b,),
        in_specs=[
            pl.BlockSpec((1, s, ch), lambda i: (i, 0, 0)),
            _const_spec((32, ch)), _const_spec((1, ch)), _const_spec((1, ch)), _const_spec((1, ch)),
        ],
        out_specs=pl.BlockSpec((1, s, ch), lambda i: (i, 0, 0)),
        out_shape=jax.ShapeDtypeStruct((b, s, ch), BF16),
        scratch_shapes=[pltpu.VMEM((s + 2 * CONV_PAD, ch), F32)],
        compiler_params=_cparams(("parallel",)),
        name="conv",
    )(u, lw["conv_w"], lw["conv_b"], lw["conv_ln_g"], lw["conv_ln_b"])


def _lane_mask(width, lo, hi):
    lane = lax.broadcasted_iota(jnp.int32, (1, width), 1)
    return (lane >= lo) & (lane < hi)


def _softmax_pv(qw, kt, v1):
    s = jnp.dot(qw, kt, preferred_element_type=F32)
    m = jnp.max(s, axis=-1, keepdims=True)
    p = jnp.exp2(s - m).astype(BF16)
    o = jnp.dot(p, v1, preferred_element_type=F32)
    return o * (1.0 / o[:, V_HEAD:V_HEAD + 1])


def _attn_kernel(q_ref, kt_ref, v_ref, lam_ref, g64_ref, sg_ref, o_ref, *, maps, diff, lam_init):
    if diff:
        lv = lam_ref[...]
        lam = (jnp.exp(jnp.sum(lv[0:1] * lv[1:2], axis=-1, keepdims=True))
               - jnp.exp(jnp.sum(lv[2:3] * lv[3:4], axis=-1, keepdims=True)) + lam_init)
    heads = []
    for h in range(N_HEADS):
        outs = []
        for (w0, lo, hi) in maps[h]:
            qw = q_ref[0, :, w0:w0 + LANE]
            if (lo, hi) != (0, LANE):
                qw = jnp.where(_lane_mask(LANE, lo, hi), qw, jnp.zeros_like(qw))
            outs.append(_softmax_pv(qw, kt_ref[0, w0:w0 + LANE, :], v_ref[0, h]))
        heads.append(outs[0] - lam * outs[1] if diff else outs[0])
    low = _lane_mask(LANE, 0, V_HEAD)
    acc = jnp.concatenate(
        [jnp.where(low, heads[h], pltpu.roll(heads[h + 1], V_HEAD, axis=1)) for h in range(0, N_HEADS, 2)],
        axis=1)
    if diff:
        acc = _group_norm(acc, g64_ref, 1.0 / DIFF_V) * sg_ref[...]
    o_ref[0] = acc.astype(BF16)


MAPS_MLA = tuple(((LANE * h, 0, LANE),) for h in range(N_HEADS))
MAPS_DIFF = tuple(tuple((LANE * (h // 2), 64 * (h % 2) + 32 * c, 64 * (h % 2) + 32 * c + 32) for c in range(2))
                  for h in range(N_HEADS))
MAPS_NA = tuple(((LANE * (h // 2), 64 * (h % 2), 64 * (h % 2) + 64),) for h in range(N_HEADS))


def _attn_call(q, kt, v, lw, maps, diff=False, lam_init=0.0):
    b, s, wq = q.shape
    sk = kt.shape[2]
    tq = min(s, ATTN_TQ)
    vh = v.reshape(b, sk, N_HEADS, V_HEAD).transpose(0, 2, 1, 3)
    v = jnp.concatenate([vh, jnp.ones_like(vh)], axis=-1)
    return pl.pallas_call(
        functools.partial(_attn_kernel, maps=maps, diff=diff, lam_init=lam_init),
        grid=(b, s // tq),
        in_specs=[
            pl.BlockSpec((1, tq, wq), lambda i, j: (i, j, 0)),
            pl.BlockSpec((1, wq, sk), lambda i, j: (i, 0, 0)),
            pl.BlockSpec((1, N_HEADS, sk, LANE), lambda i, j: (i, 0, 0, 0)),
            _const_spec((4, DIFF_DIM)),
            _const_spec((BR_W, BR_W)),
            _const_spec((1, BR_W)),
        ],
        out_specs=pl.BlockSpec((1, tq, BR_W), lambda i, j: (i, j, 0)),
        out_shape=jax.ShapeDtypeStruct((b, s, BR_W), BF16),
        compiler_params=_cparams(("parallel", "parallel")),
        name="attn_diff" if diff else "attn",
    )(q, kt, v, lw["diff_lam"], lw["g64"], lw["subln"])


_NT = (((1,), (1,)), ((), ()))


def _na_kernel(q_ref, k_ref, v_ref, kc_ref, vc_ref, bias_ref, o_ref, *, rows):
    nj = rows // 2
    band = NA_BAND_ROWS * GRID_W
    pair = 2 * GRID_W
    kc = kc_ref[0]
    vc = vc_ref[0]
    for sub in range(NA_PAIRS):
        j = pl.program_id(1) * NA_PAIRS + sub
        start = jnp.clip(2 * j - 4, 0, rows - NA_BAND_ROWS)
        base = pl.multiple_of(start * GRID_W, 2 * GRID_W)
        cls = jnp.where(j < 2, j, jnp.where(j >= nj - 2, j - (nj - 2) + 3, 2))
        kw = k_ref[0, pl.ds(base, band), :]
        vw = v_ref[0, pl.ds(base, band), :]
        q = q_ref[0, sub * pair:(sub + 1) * pair, :]
        acc = jnp.zeros((pair, BR_W), F32)
        for h in range(N_HEADS):
            qm = jnp.where(_lane_mask(BR_W, HEAD_DIM * h, HEAD_DIM * (h + 1)), q, jnp.zeros_like(q))
            s_loc = lax.dot_general(qm, kw, _NT, preferred_element_type=F32) + bias_ref[cls, h]
            s_ctx = lax.dot_general(qm, kc, _NT, preferred_element_type=F32)
            m = jnp.maximum(jnp.max(s_loc, axis=-1, keepdims=True), jnp.max(s_ctx, axis=-1, keepdims=True))
            p_loc = jnp.exp2(s_loc - m)
            p_ctx = jnp.exp2(s_ctx - m)
            l = jnp.sum(p_loc, axis=-1, keepdims=True) + jnp.sum(p_ctx, axis=-1, keepdims=True)
            o = (jnp.dot(p_ctx.astype(BF16), vc, preferred_element_type=F32)
                 + jnp.dot(p_loc.astype(BF16), vw, preferred_element_type=F32)) * (1.0 / l)
            acc = jnp.where(_lane_mask(BR_W, HEAD_DIM * h, HEAD_DIM * (h + 1)), o, acc)
        o_ref[0, sub * pair:(sub + 1) * pair, :] = acc.astype(BF16)


def _na_call(q, k, v, kc, vc, bias):
    b, s, w = q.shape
    n_ctx = kc.shape[1]
    rows = s // GRID_W
    tq = 2 * GRID_W * NA_PAIRS
    return pl.pallas_call(
        functools.partial(_na_kernel, rows=rows),
        grid=(b, rows // (2 * NA_PAIRS)),
        in_specs=[
            pl.BlockSpec((1, tq, w), lambda i, j: (i, j, 0)),
            pl.BlockSpec((1, s, w), lambda i, j: (i, 0, 0)),
            pl.BlockSpec((1, s, w), lambda i, j: (i, 0, 0)),
            pl.BlockSpec((1, n_ctx, w), lambda i, j: (i, 0, 0)),
            pl.BlockSpec((1, n_ctx, w), lambda i, j: (i, 0, 0)),
            _const_spec(bias.shape),
        ],
        out_specs=pl.BlockSpec((1, tq, w), lambda i, j: (i, j, 0)),
        out_shape=jax.ShapeDtypeStruct((b, s, w), BF16),
        compiler_params=_cparams(("parallel", "arbitrary")),
        name="na",
    )(q, k, v, kc, vc, bias)


def _merge_kernel(x_ref, mod_ref, n1g_ref, n2g_ref, uc_ref, om_ref, od_ref, on_ref,
                  gw_ref, gb_ref, wc_ref, wm_ref, wd_ref, wn_ref, wo_ref, rwt_ref, rb_ref,
                  x1_ref, h2_ref, ids_ref, wts_ref):
    x = x_ref[0]
    mod = mod_ref[0]
    h = _modulate(x, n1g_ref[...], mod[0:1], mod[1:2]).astype(BF16)
    y = jnp.zeros(x.shape, F32)
    branches = ((uc_ref, wc_ref), (om_ref, wm_ref), (od_ref, wd_ref), (on_ref, wn_ref))
    for i, (o_ref, w_ref) in enumerate(branches):
        lo = D_MODEL * i
        g = _sigmoid(jnp.dot(h, gw_ref[:, lo:lo + D_MODEL], preferred_element_type=F32)
                     + gb_ref[:, lo:lo + D_MODEL])
        y = y + g * jnp.dot(o_ref[0], w_ref[...], preferred_element_type=F32)
    out = jnp.dot(y.astype(BF16), wo_ref[...], preferred_element_type=F32)
    x1 = x + mod[2:3] * out
    x1_ref[0] = x1
    h2 = _modulate(x1, n2g_ref[...], mod[3:4], mod[4:5])
    for q in range(4):
        h2_ref[q, 0] = h2[:, ROW_Q * q:ROW_Q * (q + 1)]

    logits = lax.dot_general(rwt_ref[...], h2, _NT, preferred_element_type=F32,
                             precision=lax.Precision.HIGHEST) + rb_ref[...]
    eidx = lax.broadcasted_iota(jnp.int32, logits.shape, 0).astype(F32)
    vals, idxs = [], []
    cur = logits
    for _ in range(TOP_K):
        m = jnp.max(cur, axis=0, keepdims=True)
        idx = jnp.min(jnp.where(cur == m, eidx, float(N_EXPERTS)), axis=0, keepdims=True)
        vals.append(m)
        idxs.append(idx)
        cur = jnp.where(eidx == idx, -jnp.inf, cur)
    es = [jnp.exp(vk - vals[0]) for vk in vals]
    den = es[0] + es[1] + es[2] + es[3]
    ids_ref[0] = jnp.concatenate(idxs, axis=0).astype(jnp.int32)
    wts_ref[0] = jnp.concatenate([e / den for e in es], axis=0)


def _merge_call(x, mods, mod_row, lw, uc, om, od, on):
    b, s, d = x.shape
    t = min(s, 512)
    if mod_row is None:
        mod_map = lambda i, j: (i, 0, 0)
    else:
        mod_map = lambda i, j: (mod_row, 0, 0)
    tok = lambda w: pl.BlockSpec((1, t, w), lambda i, j: (i, j, 0))
    rt = pl.BlockSpec((1, TOP_K, t), lambda i, j: (i, 0, j))
    return pl.pallas_call(
        _merge_kernel,
        grid=(b, s // t),
        in_specs=[
            tok(d), pl.BlockSpec((1, 6, d), mod_map), _const_spec((1, d)), _const_spec((1, d)),
            tok(BR_W), tok(BR_W), tok(BR_W), tok(BR_W),
            _const_spec((d, N_BRANCH * d)), _const_spec((1, N_BRANCH * d)),
            _const_spec((BR_W, d)), _const_spec((BR_W, d)), _const_spec((BR_W, d)), _const_spec((BR_W, d)),
            _const_spec((d, d)), _const_spec((N_EXPERTS, d)), _const_spec((N_EXPERTS, 1)),
        ],
        out_specs=[tok(d), pl.BlockSpec((4, 1, t, ROW_Q), lambda i, j: (0, i, j, 0)), rt, rt],
        out_shape=[jax.ShapeDtypeStruct((b, s, d), F32), jax.ShapeDtypeStruct((4, b, s, ROW_Q), F32),
                   jax.ShapeDtypeStruct((b, TOP_K, s), jnp.int32), jax.ShapeDtypeStruct((b, TOP_K, s), F32)],
        compiler_params=_cparams(("parallel", "parallel")),
        name="merge",
    )(x, mods, lw["n1g"], lw["n2g"], uc, om, od, on, lw["gate_w"], lw["gate_b"],
      lw["conv_out"], lw["mla_out"], lw["diff_out"], lw["na_out"], lw["w_o"], lw["router_wt"], lw["router_b"])


def _route(ids, tile):
    n = ids.shape[1]
    p = TOP_K * n
    e = ids.reshape(p)
    onehot = (e[:, None] == jnp.arange(N_EXPERTS, dtype=jnp.int32)[None, :]).astype(jnp.int32)
    csum = jnp.cumsum(onehot, axis=0)
    counts = csum[-1]
    padded = ((counts + tile - 1) // tile) * tile
    gend = jnp.cumsum(padded)
    gstart = gend - padded
    slot = jnp.sum(onehot * (csum - 1 + gstart[None, :]), axis=1).astype(jnp.int32)
    n_tiles = p // tile + N_EXPERTS
    n_slots = n_tiles * tile
    pair_of_slot = jnp.full((n_slots,), -1, jnp.int32).at[slot].set(
        jnp.arange(p, dtype=jnp.int32), unique_indices=True)
    real = pair_of_slot >= 0
    src_tok = jnp.where(real, pair_of_slot, jnp.arange(n_slots, dtype=jnp.int32)) % n
    dst_row = jnp.where(real, pair_of_slot, p + jnp.arange(n_slots, dtype=jnp.int32))
    tile_start = jnp.arange(n_tiles, dtype=jnp.int32) * tile
    texp = jnp.minimum(jnp.searchsorted(gend, tile_start, side="right"), N_EXPERTS - 1).astype(jnp.int32)
    tval = (tile_start < gend[-1]).astype(jnp.int32)
    return src_tok.reshape(n_tiles, 1, tile), dst_row.reshape(n_tiles, 1, tile), texp, tval


def _ffn_kernel(texp_ref, tval_ref, src_ref, nsrc_ref, dst_ref, pdst_ref, h_hbm, wgu_ref, bgu_ref, wd_ref, bd_ref,
                y_hbm, xbuf, ybuf, sem_in, sem_out, *, tile, n_tiles):
    i = pl.program_id(0)
    slot = i % 2
    valid = tval_ref[i] > 0
    prev_valid = (i >= 1) & (tval_ref[jnp.maximum(i - 1, 0)] > 0)

    def gather_row(idx_ref, buf_slot, r):
        pltpu.make_async_copy(h_hbm.at[pl.ds(idx_ref[0, 0, r], 1)], xbuf.at[buf_slot, pl.ds(r, 1)],
                              sem_in.at[buf_slot]).start()

    def scatter_row(idx_ref, buf_slot, r):
        pltpu.make_async_copy(ybuf.at[buf_slot, pl.ds(r, 1)], y_hbm.at[pl.ds(idx_ref[0, 0, r], 1)],
                              sem_out.at[buf_slot]).start()

    def rolled(fn):
        def body(r, carry):
            fn(r)
            return carry

        lax.fori_loop(0, tile, body, 0, unroll=8)

    def wait_rows(buf, sem):
        pltpu.make_async_copy(h_hbm.at[pl.ds(0, tile)], buf, sem).wait()

    def ffn(x):
        gu = jnp.dot(x, wgu_ref[0], preferred_element_type=F32) + bgu_ref[0]
        g = jnp.minimum(gu[:, :D_FF], SWIGLU_LIMIT)
        u = jnp.clip(gu[:, D_FF:], -SWIGLU_LIMIT, SWIGLU_LIMIT)
        act = ((u + 1.0) * (g * _sigmoid(SWIGLU_ALPHA * g))).astype(BF16)
        return jnp.dot(act, wd_ref[0], preferred_element_type=F32) + bd_ref[0]

    @pl.when((i == 0) & valid)
    def _():
        rolled(lambda r: gather_row(src_ref, 0, r))
        wait_rows(xbuf.at[0], sem_in.at[0])
        x = xbuf[0].astype(BF16)
        rolled(lambda r: gather_row(nsrc_ref, 1, r))
        ybuf[0] = ffn(x)

    @pl.when((i >= 2) & valid & prev_valid)
    def _():
        wait_rows(ybuf.at[slot], sem_out.at[slot])

    @pl.when((i >= 1) & valid)
    def _():
        wait_rows(xbuf.at[slot], sem_in.at[slot])
        n_chunk = (2 * D_FF) // FFN_CHUNK
        per = tile // n_chunk
        gu = []
        for c in range(n_chunk):
            x = xbuf[slot].astype(BF16)
            for r in range(c * per, (c + 1) * per):
                gather_row(nsrc_ref, 1 - slot, r)
                scatter_row(pdst_ref, 1 - slot, r)
            lo = c * FFN_CHUNK
            gu.append(jnp.dot(x, wgu_ref[0, :, lo:lo + FFN_CHUNK], preferred_element_type=F32)
                      + bgu_ref[0, :, lo:lo + FFN_CHUNK])
        half = n_chunk // 2
        act = []
        for c in range(half):
            g = jnp.minimum(gu[c], SWIGLU_LIMIT)
            u = jnp.clip(gu[half + c], -SWIGLU_LIMIT, SWIGLU_LIMIT)
            act.append(((u + 1.0) * (g * _sigmoid(SWIGLU_ALPHA * g))).astype(BF16))
        act = jnp.concatenate(act, axis=1)
        for c in range(D_MODEL // FFN_CHUNK):
            lo = c * FFN_CHUNK
            ybuf[slot, :, lo:lo + FFN_CHUNK] = (
                jnp.dot(act, wd_ref[0, :, lo:lo + FFN_CHUNK], preferred_element_type=F32)
                + bd_ref[0, :, lo:lo + FFN_CHUNK])

    def drain(last_slot):
        wait_rows(xbuf.at[1 - last_slot], sem_in.at[1 - last_slot])
        wait_rows(ybuf.at[last_slot], sem_out.at[last_slot])

    @pl.when(jnp.logical_not(valid) & prev_valid)
    def _():
        @pl.when(i >= 2)
        def _():
            wait_rows(ybuf.at[slot], sem_out.at[slot])

        rolled(lambda r: scatter_row(pdst_ref, 1 - slot, r))
        drain(1 - slot)

    @pl.when((i == n_tiles - 1) & valid)
    def _():
        wait_rows(ybuf.at[1 - slot], sem_out.at[1 - slot])
        rolled(lambda r: scatter_row(dst_ref, slot, r))
        drain(slot)


def _ffn_call(h2, src_tok, dst_row, texp, tval, lw, tile, n_out_rows):
    n_tiles = src_tok.shape[0]
    d = h2.shape[1]
    idx_spec = lambda f: pl.BlockSpec((1, 1, tile), f, memory_space=pltpu.SMEM)
    grid_spec = pltpu.PrefetchScalarGridSpec(
        num_scalar_prefetch=2,
        grid=(n_tiles,),
        in_specs=[
            idx_spec(lambda i, te, tv: (i, 0, 0)),
            idx_spec(lambda i, te, tv: (jnp.minimum(i + 1, n_tiles - 1), 0, 0)),
            idx_spec(lambda i, te, tv: (i, 0, 0)),
            idx_spec(lambda i, te, tv: (jnp.maximum(i - 1, 0), 0, 0)),
            pl.BlockSpec(memory_space=pl.ANY),
            pl.BlockSpec((1, d, 2 * D_FF), lambda i, te, tv: (te[i], 0, 0)),
            pl.BlockSpec((1, 1, 2 * D_FF), lambda i, te, tv: (te[i], 0, 0)),
            pl.BlockSpec((1, D_FF, d), lambda i, te, tv: (te[i], 0, 0)),
            pl.BlockSpec((1, 1, d), lambda i, te, tv: (te[i], 0, 0)),
        ],
        out_specs=pl.BlockSpec(memory_space=pl.ANY),
        scratch_shapes=[pltpu.VMEM((2, tile, d), F32), pltpu.VMEM((2, tile, d), F32),
                        pltpu.SemaphoreType.DMA((2,)), pltpu.SemaphoreType.DMA((2,))],
    )
    return pl.pallas_call(
        functools.partial(_ffn_kernel, tile=tile, n_tiles=n_tiles),
        grid_spec=grid_spec,
        out_shape=jax.ShapeDtypeStruct((n_out_rows, d), F32),
        compiler_params=_cparams(("arbitrary",)),
        name="moe_ffn",
    )(texp, tval, src_tok, src_tok, dst_row, dst_row, h2, lw["exp_w_gu"], lw["exp_b_gu"], lw["exp_w_down"], lw["exp_b_down"])


def _combine_kernel(x1_ref, mod_ref, w_ref, y0_ref, y1_ref, y2_ref, y3_ref, o_ref):
    w = w_ref[0]
    acc = w[:, 0:1] * y0_ref[...]
    for k, y_ref in ((1, y1_ref), (2, y2_ref), (3, y3_ref)):
        acc = acc + w[:, k:k + 1] * y_ref[...]
    o_ref[0] = x1_ref[0] + mod_ref[0][5:6] * acc


def _combine_call(x1, mods, mod_row, wts, y):
    b, s, d = x1.shape
    t = min(s, 512)
    nt = s // t
    if mod_row is None:
        mod_map = lambda i, j: (i, 0, 0)
    else:
        mod_map = lambda i, j: (mod_row, 0, 0)
    wts = wts.transpose(0, 2, 1)
    y_spec = lambda k: pl.BlockSpec((t, d), lambda i, j: (k * b * nt + i * nt + j, 0))
    return pl.pallas_call(
        _combine_kernel,
        grid=(b, nt),
        in_specs=[
            pl.BlockSpec((1, t, d), lambda i, j: (i, j, 0)),
            pl.BlockSpec((1, 6, d), mod_map),
            pl.BlockSpec((1, t, TOP_K), lambda i, j: (i, j, 0)),
            y_spec(0), y_spec(1), y_spec(2), y_spec(3),
        ],
        out_specs=pl.BlockSpec((1, t, d), lambda i, j: (i, j, 0)),
        out_shape=jax.ShapeDtypeStruct((b, s, d), F32),
        compiler_params=_cparams(("parallel", "parallel")),
        name="moe_combine",
    )(x1, mods, wts, y, y, y, y)


def _moe(x1, h2, ids, wts, mods, mod_row, lw):
    b, s, d = x1.shape
    n = b * s
    tile = 512 if TOP_K * n >= 512 * N_EXPERTS * 4 else 256
    ids_flat = ids.transpose(1, 0, 2).reshape(TOP_K, n)
    src_tok, dst_row, texp, tval = _route(ids_flat, tile)
    n_out_rows = TOP_K * n + src_tok.shape[0] * tile
    y = _ffn_call(h2.reshape(n, d), src_tok, dst_row, texp, tval, lw, tile, n_out_rows)
    return _combine_call(x1, mods, mod_row, wts, y)


SC_WINDOW = 128
ROW_Q = D_MODEL // 4


def _route_slots(ids, tile):
    n = ids.shape[1]
    p = TOP_K * n
    e = ids.reshape(p)
    onehot = (e[:, None] == jnp.arange(N_EXPERTS, dtype=jnp.int32)[None, :])
    chunk = 512
    oh3 = onehot.astype(F32).reshape(p // chunk, chunk, N_EXPERTS)
    within = jnp.einsum("ij,cje->cie", jnp.tril(jnp.ones((chunk, chunk), F32)), oh3)
    totals = within[:, -1, :]
    before = jnp.cumsum(totals, axis=0) - totals
    csum = (within + before[:, None, :]).reshape(p, N_EXPERTS).astype(jnp.int32)
    onehot = onehot.astype(jnp.int32)
    counts = csum[-1]
    padded = ((counts + tile - 1) // tile) * tile
    gend = jnp.cumsum(padded)
    gstart = gend - padded
    slot = jnp.sum(onehot * (csum - 1 + gstart[None, :]), axis=1).astype(jnp.int32)
    n_tiles = p // tile + N_EXPERTS
    tile_start = jnp.arange(n_tiles, dtype=jnp.int32) * tile
    texp = jnp.minimum(jnp.searchsorted(gend, tile_start, side="right"), N_EXPERTS - 1).astype(jnp.int32)
    nreal = jnp.clip(gstart[texp] + counts[texp] - tile_start, 0, tile)
    nreal = jnp.where(tile_start < gend[-1], nreal, 0).astype(jnp.int32)
    return slot, texp, nreal, n_tiles


def _sc_mesh():
    return plsc.VectorSubcoreMesh(core_axis_name="c", subcore_axis_name="s")


def _sc_dispatch(hq, idx, n_slots):
    _, n, w = hq.shape
    src = hq.reshape(4 * n, w)
    m = idx.shape[0]
    blocks_per_q = n // SC_WINDOW
    per_q = TOP_K * blocks_per_q

    @pl.kernel(out_type=jax.ShapeDtypeStruct((4 * n_slots, w), hq.dtype), mesh=_sc_mesh(), scratch_types=[])
    def kern(x_hbm, i_hbm, o_hbm):
        def body(x_vmem, i_vmem):
            pltpu.sync_copy(x_vmem, o_hbm.at[i_vmem.at[0]])

        pltpu.emit_pipeline(
            body,
            grid=(m // SC_WINDOW,),
            in_specs=[
                pl.BlockSpec((SC_WINDOW, w), index_map=lambda i: ((i // per_q) * blocks_per_q + i % blocks_per_q, 0)),
                pl.BlockSpec((1, SC_WINDOW), index_map=lambda i: (0, i)),
            ],
            out_specs=[],
            core_axis_name=("c", "s"),
            dimension_semantics=(pltpu.PARALLEL,),
        )(x_hbm, i_hbm)

    return kern(src, idx.reshape(1, m)).reshape(4, n_slots, w)


def _sc_collect(ys, idx):
    _, n_slots, w = ys.shape
    src = ys.reshape(4 * n_slots, w)
    m = idx.shape[0]

    @pl.kernel(out_type=jax.ShapeDtypeStruct((m, w), ys.dtype), mesh=_sc_mesh(), scratch_types=[])
    def kern(x_hbm, i_hbm, o_hbm):
        def body(i_vmem, o_vmem):
            pltpu.sync_copy(x_hbm.at[i_vmem.at[0]], o_vmem)

        pltpu.emit_pipeline(
            body,
            grid=(m // SC_WINDOW,),
            in_specs=[pl.BlockSpec((1, SC_WINDOW), index_map=lambda i: (0, i))],
            out_specs=[pl.BlockSpec((SC_WINDOW, w), index_map=lambda i: (i, 0))],
            core_axis_name=("c", "s"),
            dimension_semantics=(pltpu.PARALLEL,),
        )(i_hbm, o_hbm)

    return kern(src, idx.reshape(1, m))


def _ffn_sorted_kernel(texp_ref, nreal_ref, x_ref, wgu_ref, bgu_ref, wd_ref, bd_ref, y_ref, wgu_bf, wd_bf, *, tile):
    i = pl.program_id(0)
    nreal = nreal_ref[i]

    @pl.when((nreal > 0) & ((i == 0) | (texp_ref[i] != texp_ref[jnp.maximum(i - 1, 0)])))
    def _():
        wgu_bf[...] = wgu_ref[0].astype(BF16)
        wd_bf[...] = wd_ref[0].astype(BF16)

    @pl.when(nreal > 0)
    def _():
        x = jnp.concatenate([x_ref[q] for q in range(4)], axis=1)
        row = lax.broadcasted_iota(jnp.int32, (tile, 1), 0)
        x = jnp.where(row < nreal, x, 0.0).astype(BF16)
        gu = jnp.dot(x, wgu_bf[...], preferred_element_type=F32) + bgu_ref[0]
        g = jnp.minimum(gu[:, :D_FF], SWIGLU_LIMIT)
        u = jnp.clip(gu[:, D_FF:], -SWIGLU_LIMIT, SWIGLU_LIMIT)
        act = ((u + 1.0) * (g * _sigmoid(SWIGLU_ALPHA * g))).astype(BF16)
        y = jnp.dot(act, wd_bf[...], preferred_element_type=F32) + bd_ref[0]
        for q in range(4):
            y_ref[q] = y[:, ROW_Q * q:ROW_Q * (q + 1)]

    @pl.when(nreal == 0)
    def _():
        y_ref[...] = jnp.zeros(y_ref.shape, F32)


def _ffn_sorted_call(xs, texp, nreal, lw, tile):
    _, n_slots, w = xs.shape
    n_tiles = n_slots // tile
    d = D_MODEL
    off = lw["exp_off"]
    grid_spec = pltpu.PrefetchScalarGridSpec(
        num_scalar_prefetch=2,
        grid=(n_tiles,),
        in_specs=[
            pl.BlockSpec((4, tile, w), lambda i, te, nr: (0, i, 0)),
            pl.BlockSpec((1, d, 2 * D_FF), lambda i, te, nr: (te[i] + off, 0, 0)),
            pl.BlockSpec((1, 1, 2 * D_FF), lambda i, te, nr: (te[i] + off, 0, 0)),
            pl.BlockSpec((1, D_FF, d), lambda i, te, nr: (te[i] + off, 0, 0)),
            pl.BlockSpec((1, 1, d), lambda i, te, nr: (te[i] + off, 0, 0)),
        ],
        out_specs=pl.BlockSpec((4, tile, w), lambda i, te, nr: (0, i, 0)),
        scratch_shapes=[pltpu.VMEM((d, 2 * D_FF), BF16), pltpu.VMEM((D_FF, d), BF16)],
    )
    return pl.pallas_call(
        functools.partial(_ffn_sorted_kernel, tile=tile),
        grid_spec=grid_spec,
        out_shape=jax.ShapeDtypeStruct((4, n_slots, w), F32),
        compiler_params=_cparams(("arbitrary",)),
        name="moe_ffn",
    )(texp, nreal, xs, lw["exp_w_gu"], lw["exp_b_gu"], lw["exp_w_down"], lw["exp_b_down"])


def _combine_q_kernel(x1_ref, mod_ref, w_ref, y_ref, o_ref):
    w = w_ref[0]
    g2 = mod_ref[0][5:6]
    for q in range(4):
        lo = ROW_Q * q
        acc = w[:, 0:1] * y_ref[q, 0, 0]
        for k in range(1, TOP_K):
            acc = acc + w[:, k:k + 1] * y_ref[q, k, 0]
        o_ref[0, :, lo:lo + ROW_Q] = x1_ref[0, :, lo:lo + ROW_Q] + g2[:, lo:lo + ROW_Q] * acc


def _combine_q_call(x1, mods, mod_row, wts, y):
    b, s, d = x1.shape
    t = min(s, 512)
    if mod_row is None:
        mod_map = lambda i, j: (i, 0, 0)
    else:
        mod_map = lambda i, j: (mod_row, 0, 0)
    wts = wts.transpose(0, 2, 1)
    y = y.reshape(4, TOP_K, b, s, ROW_Q)
    return pl.pallas_call(
        _combine_q_kernel,
        grid=(b, s // t),
        in_specs=[
            pl.BlockSpec((1, t, d), lambda i, j: (i, j, 0)),
            pl.BlockSpec((1, 6, d), mod_map),
            pl.BlockSpec((1, t, TOP_K), lambda i, j: (i, j, 0)),
            pl.BlockSpec((4, TOP_K, 1, t, ROW_Q), lambda i, j: (0, 0, i, j, 0)),
        ],
        out_specs=pl.BlockSpec((1, t, d), lambda i, j: (i, j, 0)),
        out_shape=jax.ShapeDtypeStruct((b, s, d), F32),
        compiler_params=_cparams(("parallel", "parallel")),
        name="moe_combine",
    )(x1, mods, wts, y)


def _moe_sc(x1, hq, ids, wts, mods, mod_row, lw):
    b, s, d = x1.shape
    n = b * s
    tile = 512 if TOP_K * n >= 512 * N_EXPERTS * 4 else 256
    ids_flat = ids.transpose(1, 0, 2).reshape(TOP_K, n)
    slot, texp, nreal, n_tiles = _route_slots(ids_flat, tile)
    n_slots = n_tiles * tile
    idx = (slot[None, :] + (jnp.arange(4, dtype=jnp.int32) * n_slots)[:, None]).reshape(-1)
    xs = _sc_dispatch(hq.reshape(4, n, ROW_Q), idx, n_slots)
    ys = _ffn_sorted_call(xs, texp, nreal, lw, tile)
    y = _sc_collect(ys, idx)
    return _combine_q_call(x1, mods, mod_row, wts, y)


def _layer_weights(l, p, lam_init):
    w = p["w_in"][l]
    d = w.shape[0]
    zcols = lambda n: jnp.zeros((d, n), w.dtype)
    regroup = lambda blk: blk.reshape(d, N_HEADS, 3, HEAD_DIM).transpose(0, 2, 1, 3).reshape(d, 3 * BR_W)
    w_in = jnp.concatenate([
        w[:, :A_IN],
        w[:, OFF_B:OFF_B + Q_LORA], zcols(P_CKV - P_CQ - Q_LORA),
        w[:, OFF_B + Q_LORA:OFF_B + Q_LORA + KV_LORA],
        w[:, OFF_B + Q_LORA + KV_LORA:OFF_C], zcols(P_DQ - P_KPE - QK_ROPE),
        regroup(w[:, OFF_C:OFF_D]), regroup(w[:, OFF_D:]),
    ], axis=1)
    assert w_in.shape[1] == PROJ_W

    def head_slots(w3, slot):
        w3 = jnp.pad(w3, ((0, 0), (0, 0), (0, slot - w3.shape[2])))
        return w3.reshape(w3.shape[0], N_HEADS * slot)

    wuq = head_slots(p["mla_w_uq"][l].reshape(Q_LORA, N_HEADS, MLA_QK), LANE)
    wuq = jnp.pad(wuq, ((0, 256 - Q_LORA), (0, 0)))
    wukv = p["mla_w_ukv"][l].reshape(KV_LORA, N_HEADS, QK_NOPE + V_HEAD)
    wk = head_slots(wukv[:, :, :QK_NOPE], LANE)
    wv = head_slots(wukv[:, :, QK_NOPE:], V_HEAD)
    ppe = np.zeros((LANE, 512), np.float32)
    for h in range(N_HEADS):
        for i in range(QK_ROPE):
            ppe[i, h * LANE + QK_NOPE + i] = 1.0

    def slot_gain(g, scale):
        g = jnp.concatenate([g * scale, jnp.zeros((LANE - MLA_QK,), F32)])
        return jnp.tile(g, N_HEADS)

    def row512(v):
        return jnp.concatenate([v, jnp.zeros((512 - v.shape[0],), F32)])

    gains = jnp.stack([
        row512(p["mla_cq_g"][l]),
        slot_gain(p["mla_qn_g"][l], MLA_QK ** -0.5 * LOG2E),
        row512(p["mla_ckv_g"][l]),
        slot_gain(p["mla_kn_g"][l], 1.0),
        row512(jnp.tile(p["diff_qn_g"][l], 2 * N_HEADS) * DIFF_DIM ** -0.5 * LOG2E),
        row512(jnp.tile(p["diff_kn_g"][l], 2 * N_HEADS)),
        row512(jnp.tile(p["na_qn_g"][l], N_HEADS) * HEAD_DIM ** -0.5 * LOG2E),
        row512(jnp.tile(p["na_kn_g"][l], N_HEADS)),
    ])
    conv_w = jnp.concatenate([p["conv_w"][l], jnp.zeros((1, CONV_CH), F32)], axis=0)
    return dict(
        n1g=p["norm1_g"][l][None, :], n2g=p["norm2_g"][l][None, :],
        w_in=w_in.astype(BF16), gains=gains,
        wuq=wuq.astype(BF16), wk=wk.astype(BF16), wv=wv.astype(BF16), ppe=jnp.asarray(ppe, BF16),
        g96=jnp.asarray(_group_ones(512, LANE, MLA_QK), BF16),
        g32=jnp.asarray(_group_ones(BR_W, DIFF_DIM, DIFF_DIM), BF16),
        g64=jnp.asarray(_group_ones(BR_W, HEAD_DIM, HEAD_DIM), BF16),
        rm=jnp.asarray(_rot_matrix(512, LANE, QK_NOPE, QK_ROPE // 2), BF16),
        rd=jnp.asarray(_rot_matrix(BR_W, DIFF_DIM, 0, DIFF_DIM // 2), BF16),
        conv_w=conv_w, conv_b=p["conv_b"][l][None, :],
        conv_ln_g=p["conv_ln_g"][l][None, :], conv_ln_b=p["conv_ln_b"][l][None, :],
        diff_lam=p["diff_lam"][l],
        subln=(jnp.tile(p["diff_subln_g"][l], N_HEADS) * (1.0 - lam_init))[None, :],
        gate_w=p["gate_w"][l].astype(BF16), gate_b=p["gate_b"][l][None, :],
        conv_out=p["conv_out"][l].astype(BF16), mla_out=p["mla_out"][l].astype(BF16),
        diff_out=p["diff_out"][l].astype(BF16), na_out=p["na_out"][l].astype(BF16),
        w_o=p["w_o"][l].astype(BF16),
        router_wt=p["router_w"][l].T, router_b=p["router_b"][l][:, None],
        exp_off=l * N_EXPERTS,
        exp_w_gu=p["exp_w_gu"].reshape((-1,) + p["exp_w_gu"].shape[2:]),
        exp_b_gu=p["exp_b_gu"].reshape(-1, 1, 2 * D_FF),
        exp_w_down=p["exp_w_down"].reshape((-1,) + p["exp_w_down"].shape[2:]),
        exp_b_down=p["exp_b_down"].reshape(-1, 1, D_MODEL),
    )


def _kt(kc, k):
    return jnp.concatenate([kc, k], axis=1).transpose(0, 2, 1)


def kernel(x, c, ctx, c_ctx, ada_w, ada_b, norm1_g, norm2_g, w_in, conv_w, conv_b, conv_ln_g, conv_ln_b, conv_out, mla_cq_g, mla_ckv_g, mla_w_uq, mla_w_ukv, mla_qn_g, mla_kn_g, mla_out, diff_qn_g, diff_kn_g, diff_lam, diff_subln_g, diff_out, na_qn_g, na_kn_g, na_rpb, na_out, gate_w, gate_b, w_o, router_w, router_b, exp_w_gu, exp_b_gu, exp_w_down, exp_b_down):
    p = dict(norm1_g=norm1_g, norm2_g=norm2_g, w_in=w_in, conv_w=conv_w, conv_b=conv_b,
             conv_ln_g=conv_ln_g, conv_ln_b=conv_ln_b, conv_out=conv_out, mla_cq_g=mla_cq_g,
             mla_ckv_g=mla_ckv_g, mla_w_uq=mla_w_uq, mla_w_ukv=mla_w_ukv, mla_qn_g=mla_qn_g,
             mla_kn_g=mla_kn_g, mla_out=mla_out, diff_qn_g=diff_qn_g, diff_kn_g=diff_kn_g,
             diff_lam=diff_lam, diff_subln_g=diff_subln_g, diff_out=diff_out, na_qn_g=na_qn_g,
             na_kn_g=na_kn_g, na_out=na_out, gate_w=gate_w, gate_b=gate_b, w_o=w_o,
             router_w=router_w, router_b=router_b, exp_w_gu=exp_w_gu, exp_b_gu=exp_b_gu,
             exp_w_down=exp_w_down, exp_b_down=exp_b_down)
    b, s, d = x.shape
    n_ctx = ctx.shape[1]
    depth = ada_w.shape[0]
    rows = s // GRID_W
    assert d == D_MODEL and s % (2 * GRID_W) == 0 and rows >= NA_BAND_ROWS and n_ctx % LANE == 0

    mod_rows = -(-(b + 1) // 8) * 8
    cs = jnp.concatenate([c, c_ctx[None, :], jnp.zeros((mod_rows - b - 1, d), F32)], axis=0)
    mods_all = _ada_call(cs, ada_w, ada_b).reshape(depth, mod_rows, 6, d)

    tabs_x = _rope_lane_tables(s)
    tabs_c = (jnp.ones((n_ctx, 512), F32), jnp.zeros((n_ctx, 512), F32),
              jnp.ones((n_ctx, BR_W), F32), jnp.zeros((n_ctx, BR_W), F32))

    xc = ctx
    for l in range(depth):
        last = l == depth - 1
        lam_init = 0.8 - 0.6 * math.exp(-0.3 * l)
        lw = _layer_weights(l, p, lam_init)
        mods = mods_all[l]
        bias = _na_bias_tables(na_rpb[l], rows)

        u, mq, mk, mv, dq, dk, dv, nq, nk, nv = _proj_call(x, mods, None, lw, tabs_x, True)
        uc, mqc, mkc, mvc, dqc, dkc, dvc, nqc, nkc, nvc = _proj_call(xc, mods, b, lw, tabs_c, False)

        y_conv = _conv_call(u, lw)
        y_mla = _attn_call(mq, _kt(mkc, mk), jnp.concatenate([mvc, mv], axis=1), lw, MAPS_MLA)
        y_diff = _attn_call(dq, _kt(dkc, dk), jnp.concatenate([dvc, dv], axis=1), lw, MAPS_DIFF,
                            diff=True, lam_init=lam_init)
        y_na = _na_call(nq, nk, nv, nkc, nvc, bias)
        x1, h2, ids, wts = _merge_call(x, mods, None, lw, y_conv, y_mla, y_diff, y_na)
        x = _moe_sc(x1, h2, ids, wts, mods, None, lw)

        if not last:
            yc_conv = _conv_call(uc, lw)
            yc_mla = _attn_call(mqc, mkc.transpose(0, 2, 1), mvc, lw, MAPS_MLA)
            yc_diff = _attn_call(dqc, dkc.transpose(0, 2, 1), dvc, lw, MAPS_DIFF, diff=True, lam_init=lam_init)
            yc_na = _attn_call(nqc, nkc.transpose(0, 2, 1), nvc, lw, MAPS_NA)
            xc1, h2c, idsc, wtsc = _merge_call(xc, mods, b, lw, yc_conv, yc_mla, yc_diff, yc_na)
            xc = _moe_sc(xc1, h2c, idsc, wtsc, mods, b, lw)
    return x
```

```python
import functools
import math

import numpy as np
import jax
import jax.numpy as jnp
from jax import lax
from jax.experimental import pallas as pl
from jax.experimental.pallas import tpu as pltpu
from jax.experimental.pallas import tpu_sc as plsc

F32 = jnp.float32
BF16 = jnp.bfloat16

D_MODEL = 1024
GRID_W = 64
N_BRANCH = 4
N_HEADS = 4
HEAD_DIM = 64
CONV_CH = 256
CONV_WIDTH = 31
Q_LORA = 192
KV_LORA = 128
QK_NOPE = 64
QK_ROPE = 32
V_HEAD = 64
DIFF_DIM = 32
DIFF_V = 2 * DIFF_DIM
NA_KH = 8
NA_KW = 16
ROPE_DIM = 32
ROPE_BASE = 10000.0
N_EXPERTS = 32
TOP_K = 4
D_FF = 1024
SWIGLU_LIMIT = 7.0
SWIGLU_ALPHA = 1.702
EPS = 1e-6
NEG_INF = -1e30

A_IN = 2 * CONV_CH
B_IN = Q_LORA + KV_LORA + QK_ROPE
C_IN = N_HEADS * (4 * DIFF_DIM + DIFF_V)
D_IN = N_HEADS * 3 * HEAD_DIM
OFF_B = A_IN
OFF_C = OFF_B + B_IN
OFF_D = OFF_C + C_IN

LANE = 128
MLA_QK = QK_NOPE + QK_ROPE
BR_W = N_HEADS * HEAD_DIM
PROJ_W = 2560
NA_BAND_ROWS = 10
FFN_CHUNK = 256
ATTN_TQ = 512
NA_PAIRS = 2
LOG2E = math.log2(math.e)
VMEM_LIMIT = 52 * 1024 * 1024

P_A, P_G, P_CQ, P_CKV, P_KPE = 0, 256, 512, 768, 896
P_DQ, P_DK, P_DV = 1024, 1280, 1536
P_NQ, P_NK, P_NV = 1792, 2048, 2304


def _sigmoid(x):
    return 1.0 / (1.0 + jnp.exp(-x))


def _modulate(x, g, shift, scale):
    ms = jnp.mean(x * x, axis=-1, keepdims=True)
    return (x * lax.rsqrt(ms + EPS) * g) * (1.0 + scale) + shift


def _cparams(sem):
    return pltpu.CompilerParams(dimension_semantics=sem, vmem_limit_bytes=VMEM_LIMIT)


def _const_spec(shape):
    n = len(shape)
    return pl.BlockSpec(shape, lambda *_: (0,) * n)


def _group_ones(width, slot, real):
    i = np.arange(width)
    valid = (i % slot) < real
    same = (i[:, None] // slot) == (i[None, :] // slot)
    return (same & valid[:, None] & valid[None, :]).astype(np.float32)


def _rot_matrix(width, slot, start, half):
    r = np.zeros((width, width), np.float32)
    for s0 in range(0, width, slot):
        for i in range(half):
            a, b = s0 + start + i, s0 + start + half + i
            r[b, a] = -1.0
            r[a, b] = 1.0
    return r


def _rope_lane_tables(n_tokens):
    t = jnp.arange(n_tokens, dtype=jnp.int32)
    rows = (t // GRID_W).astype(F32)
    cols = (t % GRID_W).astype(F32)
    axis_dim = ROPE_DIM // 2
    inv = ROPE_BASE ** (-jnp.arange(0, axis_dim, 2, dtype=F32) / axis_dim)
    theta = jnp.concatenate([rows[:, None] * inv, cols[:, None] * inv], axis=-1)
    cos, sin = jnp.cos(theta), jnp.sin(theta)
    half = ROPE_DIM // 2
    ones = jnp.ones((n_tokens, QK_NOPE), F32)
    zeros = jnp.zeros((n_tokens, QK_NOPE), F32)
    pad1 = jnp.ones((n_tokens, LANE - MLA_QK), F32)
    pad0 = jnp.zeros((n_tokens, LANE - MLA_QK), F32)
    cm = jnp.tile(jnp.concatenate([ones, cos, cos, pad1], -1), (1, N_HEADS))
    sm = jnp.tile(jnp.concatenate([zeros, sin, sin, pad0], -1), (1, N_HEADS))
    cd = jnp.tile(jnp.concatenate([cos, cos], -1), (1, 2 * N_HEADS))
    sd = jnp.tile(jnp.concatenate([sin, sin], -1), (1, 2 * N_HEADS))
    assert half * 2 == DIFF_DIM
    return cm, sm, cd, sd


def _na_bias_tables(rpb, rows):
    kh = min(NA_KH, rows)
    nj = rows // 2
    reps = np.array([0, 1, 2, nj - 2, nj - 1])
    n_ro, n_co = 2 * NA_KH - 1, 2 * NA_KW - 1
    start = np.clip(2 * reps - 4, 0, rows - NA_BAND_ROWS)
    r = 2 * reps[:, None] + np.arange(2)[None, :]
    kr = start[:, None] + np.arange(NA_BAND_ROWS)[None, :]
    row_start = np.clip(r - kh // 2, 0, rows - kh)
    vr = (kr[:, None, :] >= row_start[:, :, None]) & (kr[:, None, :] < row_start[:, :, None] + kh)
    ro = np.clip(kr[:, None, :] - r[:, :, None] + NA_KH - 1, 0, n_ro - 1)
    qc = np.arange(GRID_W)
    win_start = np.clip(qc - NA_KW // 2, 0, GRID_W - NA_KW)
    vc = (qc[None, :] >= win_start[:, None]) & (qc[None, :] < win_start[:, None] + NA_KW)
    co = np.clip(qc[None, :] - qc[:, None] + NA_KW - 1, 0, n_co - 1)
    rsel = (ro[..., None] == np.arange(n_ro)).astype(np.float32)
    csel = (co[None] == np.arange(n_co)[:, None, None]).astype(np.float32)
    hi = lax.Precision.HIGHEST
    t1 = jnp.einsum("cqav,hvw->hcqaw", rsel, rpb.astype(F32), precision=hi)
    b = jnp.einsum("hcqaw,wxy->chqxay", t1, csel, precision=hi)
    valid = vr[:, None, :, None, :, None] & vc[None, None, None, :, None, :]
    b = jnp.where(valid, b * LOG2E, NEG_INF)
    return b.reshape(len(reps), N_HEADS, 2 * GRID_W, NA_BAND_ROWS * GRID_W)


def _ada_kernel(c_ref, w_ref, b_ref, o_ref):
    c = c_ref[...]
    s = c * _sigmoid(c)
    o_ref[0] = jnp.dot(s, w_ref[0], preferred_element_type=F32,
                       precision=lax.Precision.HIGHEST) + b_ref[0]


def _ada_call(cs, ada_w, ada_b):
    depth, d, n = ada_w.shape
    rows = cs.shape[0]
    tn = 1536
    return pl.pallas_call(
        _ada_kernel,
        grid=(depth, n // tn),
        in_specs=[
            pl.BlockSpec((rows, d), lambda l, j: (0, 0)),
            pl.BlockSpec((1, d, tn), lambda l, j: (l, 0, j)),
            pl.BlockSpec((1, 1, tn), lambda l, j: (l, 0, j)),
        ],
        out_specs=pl.BlockSpec((1, rows, tn), lambda l, j: (l, 0, j)),
        out_shape=jax.ShapeDtypeStruct((depth, rows, n), F32),
        compiler_params=_cparams(("arbitrary", "arbitrary")),
        name="ada",
    )(cs, ada_w, ada_b.reshape(depth, 1, n))


def _group_norm(x, ones_ref, inv_n):
    sq = (x * x).astype(BF16)
    ms = jnp.dot(sq, ones_ref[...], preferred_element_type=F32) * inv_n
    return x * lax.rsqrt(ms + EPS)


def _rope(x, rot_ref, cos_ref, sin_ref):
    rot = jnp.dot(x.astype(BF16), rot_ref[...], preferred_element_type=F32)
    return x * cos_ref[...] + rot * sin_ref[...]


def _proj_kernel(x_ref, mod_ref, n1g_ref, win_ref, gains_ref, wuq_ref, wk_ref, ppe_ref, wv_ref,
                 g96_ref, g32_ref, g64_ref, rm_ref, rd_ref, cm_ref, sm_ref, cd_ref, sd_ref,
                 u_ref, mq_ref, mk_ref, mv_ref, dq_ref, dk_ref, dv_ref, nq_ref, nk_ref, nv_ref,
                 *, use_rope):
    x = x_ref[0]
    mod = mod_ref[0]
    gains = gains_ref[...]
    h = _modulate(x, n1g_ref[...], mod[0:1], mod[1:2]).astype(BF16)
    proj = jnp.dot(h, win_ref[...], preferred_element_type=F32)

    u_ref[0] = proj[:, P_A:P_A + CONV_CH] * _sigmoid(proj[:, P_G:P_G + CONV_CH])

    cq = proj[:, P_CQ:P_CQ + 256]
    ms = jnp.sum(cq * cq, axis=-1, keepdims=True) * (1.0 / Q_LORA)
    cqn = (cq * lax.rsqrt(ms + EPS) * gains[0:1, :256]).astype(BF16)
    q = jnp.dot(cqn, wuq_ref[...], preferred_element_type=F32)
    q = _group_norm(q, g96_ref, 1.0 / MLA_QK) * gains[1:2, :]
    if use_rope:
        q = _rope(q, rm_ref, cm_ref, sm_ref)
    mq_ref[0] = q.astype(BF16)

    ckv = proj[:, P_CKV:P_CKV + KV_LORA]
    ms = jnp.mean(ckv * ckv, axis=-1, keepdims=True)
    ckvn = (ckv * lax.rsqrt(ms + EPS) * gains[2:3, :KV_LORA]).astype(BF16)
    kpe = proj[:, P_KPE:P_KPE + LANE].astype(BF16)
    k = (jnp.dot(ckvn, wk_ref[...], preferred_element_type=F32)
         + jnp.dot(kpe, ppe_ref[...], preferred_element_type=F32))
    k = _group_norm(k, g96_ref, 1.0 / MLA_QK) * gains[3:4, :]
    if use_rope:
        k = _rope(k, rm_ref, cm_ref, sm_ref)
    mk_ref[0] = k.astype(BF16)
    mv_ref[0] = jnp.dot(ckvn, wv_ref[...], preferred_element_type=F32).astype(BF16)

    qd = _group_norm(proj[:, P_DQ:P_DQ + BR_W], g32_ref, 1.0 / DIFF_DIM) * gains[4:5, :BR_W]
    kd = _group_norm(proj[:, P_DK:P_DK + BR_W], g32_ref, 1.0 / DIFF_DIM) * gains[5:6, :BR_W]
    if use_rope:
        qd = _rope(qd, rd_ref, cd_ref, sd_ref)
        kd = _rope(kd, rd_ref, cd_ref, sd_ref)
    dq_ref[0] = qd.astype(BF16)
    dk_ref[0] = kd.astype(BF16)
    dv_ref[0] = proj[:, P_DV:P_DV + BR_W].astype(BF16)

    qn = _group_norm(proj[:, P_NQ:P_NQ + BR_W], g64_ref, 1.0 / HEAD_DIM) * gains[6:7, :BR_W]
    kn = _group_norm(proj[:, P_NK:P_NK + BR_W], g64_ref, 1.0 / HEAD_DIM) * gains[7:8, :BR_W]
    nq_ref[0] = qn.astype(BF16)
    nk_ref[0] = kn.astype(BF16)
    nv_ref[0] = proj[:, P_NV:P_NV + BR_W].astype(BF16)


def _proj_call(x, mods, mod_row, lw, tabs, use_rope):
    b, s, d = x.shape
    t = min(s, 512)
    grid = (b, s // t)
    if mod_row is None:
        mod_map = lambda i, j: (i, 0, 0)
    else:
        mod_map = lambda i, j: (mod_row, 0, 0)
    tok = lambda w: pl.BlockSpec((1, t, w), lambda i, j: (i, j, 0))
    tab = lambda w: pl.BlockSpec((t, w), lambda i, j: (j, 0))
    in_specs = [
        tok(d),
        pl.BlockSpec((1, 6, d), mod_map),
        _const_spec((1, d)),
        _const_spec((d, PROJ_W)),
        _const_spec((8, 512)),
        _const_spec((256, 512)),
        _const_spec((KV_LORA, 512)),
        _const_spec((LANE, 512)),
        _const_spec((KV_LORA, BR_W)),
        _const_spec((512, 512)),
        _const_spec((BR_W, BR_W)),
        _const_spec((BR_W, BR_W)),
        _const_spec((512, 512)),
        _const_spec((BR_W, BR_W)),
        tab(512), tab(512), tab(BR_W), tab(BR_W),
    ]
    widths = [CONV_CH, 512, 512, BR_W, BR_W, BR_W, BR_W, BR_W, BR_W, BR_W]
    dtypes = [F32] + [BF16] * 9
    out_specs = [tok(w) for w in widths]
    out_shape = [jax.ShapeDtypeStruct((b, s, w), dt) for w, dt in zip(widths, dtypes)]
    return pl.pallas_call(
        functools.partial(_proj_kernel, use_rope=use_rope),
        grid=grid, in_specs=in_specs, out_specs=out_specs, out_shape=out_shape,
        compiler_params=_cparams(("parallel", "parallel")),
        name="proj",
    )(x, mods, lw["n1g"], lw["w_in"], lw["gains"], lw["wuq"], lw["wk"], lw["ppe"], lw["wv"],
      lw["g96"], lw["g32"], lw["g64"], lw["rm"], lw["rd"], tabs[0], tabs[1], tabs[2], tabs[3])


CONV_TILE = 128
CONV_PAD = 16


def _conv_kernel(u_ref, w_ref, cb_ref, lg_ref, lb_ref, o_ref, pad_ref, *, seq):
    zeros = jnp.zeros((CONV_PAD, CONV_CH), F32)
    pad_ref[0:CONV_PAD, :] = zeros
    pad_ref[CONV_PAD + seq:2 * CONV_PAD + seq, :] = zeros

    def fill(i, carry):
        base = pl.multiple_of(i * CONV_TILE, CONV_TILE)
        pad_ref[pl.ds(base + CONV_PAD, CONV_TILE), :] = u_ref[0, pl.ds(base, CONV_TILE), :]
        return carry

    lax.fori_loop(0, seq // CONV_TILE, fill, 0)
    w = w_ref[...]
    cb, lg, lb = cb_ref[...], lg_ref[...], lb_ref[...]

    def tile(i, carry):
        base = pl.multiple_of(i * CONV_TILE, CONV_TILE)
        win = pad_ref[pl.ds(base, CONV_TILE + 2 * CONV_PAD), :]
        acc = jnp.zeros((CONV_TILE, CONV_CH), F32)
        for j in range(CONV_WIDTH):
            acc = acc + win[j + 1:j + 1 + CONV_TILE, :] * w[j:j + 1, :]
        c = acc + cb
        mu = jnp.mean(c, axis=-1, keepdims=True)
        cc = c - mu
        var = jnp.mean(cc * cc, axis=-1, keepdims=True)
        y = cc * lax.rsqrt(var + EPS) * lg + lb
        o_ref[0, pl.ds(base, CONV_TILE), :] = (y * _sigmoid(y)).astype(BF16)
        return carry

    lax.fori_loop(0, seq // CONV_TILE, tile, 0)


def _conv_call(u, lw):
    b, s, ch = u.shape
    return pl.pallas_call(
        functools.partial(_conv_kernel, seq=s),
        grid=(b,),
        in_specs=[
            pl.BlockSpec((1, s, ch), lambda i: (i, 0, 0)),
            _const_spec((32, ch)), _const_spec((1, ch)), _const_spec((1, ch)), _const_spec((1, ch)),
        ],
        out_specs=pl.BlockSpec((1, s, ch), lambda i: (i, 0, 0)),
        out_shape=jax.ShapeDtypeStruct((b, s, ch), BF16),
        scratch_shapes=[pltpu.VMEM((s + 2 * CONV_PAD, ch), F32)],
        compiler_params=_cparams(("parallel",)),
        name="conv",
    )(u, lw["conv_w"], lw["conv_b"], lw["conv_ln_g"], lw["conv_ln_b"])


def _lane_mask(width, lo, hi):
    lane = lax.broadcasted_iota(jnp.int32, (1, width), 1)
    return (lane >= lo) & (lane < hi)


def _softmax_pv(qw, kt, v1):
    s = jnp.dot(qw, kt, preferred_element_type=F32)
    m = jnp.max(s, axis=-1, keepdims=True)
    p = jnp.exp2(s - m).astype(BF16)
    o = jnp.dot(p, v1, preferred_element_type=F32)
    return o * (1.0 / o[:, V_HEAD:V_HEAD + 1])


def _attn_kernel(q_ref, kt_ref, v_ref, lam_ref, g64_ref, sg_ref, o_ref, *, maps, diff, lam_init):
    if diff:
        lv = lam_ref[...]
        lam = (jnp.exp(jnp.sum(lv[0:1] * lv[1:2], axis=-1, keepdims=True))
               - jnp.exp(jnp.sum(lv[2:3] * lv[3:4], axis=-1, keepdims=True)) + lam_init)
    heads = []
    for h in range(N_HEADS):
        outs = []
        for (w0, lo, hi) in maps[h]:
            qw = q_ref[0, :, w0:w0 + LANE]
            if (lo, hi) != (0, LANE):
                qw = jnp.where(_lane_mask(LANE, lo, hi), qw, jnp.zeros_like(qw))
            outs.append(_softmax_pv(qw, kt_ref[0, w0:w0 + LANE, :], v_ref[0, h]))
        heads.append(outs[0] - lam * outs[1] if diff else outs[0])
    low = _lane_mask(LANE, 0, V_HEAD)
    acc = jnp.concatenate(
        [jnp.where(low, heads[h], pltpu.roll(heads[h + 1], V_HEAD, axis=1)) for h in range(0, N_HEADS, 2)],
        axis=1)
    if diff:
        acc = _group_norm(acc, g64_ref, 1.0 / DIFF_V) * sg_ref[...]
    o_ref[0] = acc.astype(BF16)


MAPS_MLA = tuple(((LANE * h, 0, LANE),) for h in range(N_HEADS))
MAPS_DIFF = tuple(tuple((LANE * (h // 2), 64 * (h % 2) + 32 * c, 64 * (h % 2) + 32 * c + 32) for c in range(2))
                  for h in range(N_HEADS))
MAPS_NA = tuple(((LANE * (h // 2), 64 * (h % 2), 64 * (h % 2) + 64),) for h in range(N_HEADS))


def _attn_call(q, kt, v, lw, maps, diff=False, lam_init=0.0):
    b, s, wq = q.shape
    sk = kt.shape[2]
    tq = min(s, ATTN_TQ)
    vh = v.reshape(b, sk, N_HEADS, V_HEAD).transpose(0, 2, 1, 3)
    v = jnp.concatenate([vh, jnp.ones_like(vh)], axis=-1)
    return pl.pallas_call(
        functools.partial(_attn_kernel, maps=maps, diff=diff, lam_init=lam_init),
        grid=(b, s // tq),
        in_specs=[
            pl.BlockSpec((1, tq, wq), lambda i, j: (i, j, 0)),
            pl.BlockSpec((1, wq, sk), lambda i, j: (i, 0, 0)),
            pl.BlockSpec((1, N_HEADS, sk, LANE), lambda i, j: (i, 0, 0, 0)),
            _const_spec((4, DIFF_DIM)),
            _const_spec((BR_W, BR_W)),
            _const_spec((1, BR_W)),
        ],
        out_specs=pl.BlockSpec((1, tq, BR_W), lambda i, j: (i, j, 0)),
        out_shape=jax.ShapeDtypeStruct((b, s, BR_W), BF16),
        compiler_params=_cparams(("parallel", "parallel")),
        name="attn_diff" if diff else "attn",
    )(q, kt, v, lw["diff_lam"], lw["g64"], lw["subln"])


_NT = (((1,), (1,)), ((), ()))


def _na_kernel(q_ref, k_ref, v_ref, kc_ref, vc_ref, bias_ref, o_ref, *, rows):
    nj = rows // 2
    band = NA_BAND_ROWS * GRID_W
    pair = 2 * GRID_W
    kc = kc_ref[0]
    vc = vc_ref[0]
    for sub in range(NA_PAIRS):
        j = pl.program_id(1) * NA_PAIRS + sub
        start = jnp.clip(2 * j - 4, 0, rows - NA_BAND_ROWS)
        base = pl.multiple_of(start * GRID_W, 2 * GRID_W)
        cls = jnp.where(j < 2, j, jnp.where(j >= nj - 2, j - (nj - 2) + 3, 2))
        kw = k_ref[0, pl.ds(base, band), :]
        vw = v_ref[0, pl.ds(base, band), :]
        q = q_ref[0, sub * pair:(sub + 1) * pair, :]
        acc = jnp.zeros((pair, BR_W), F32)
        for h in range(N_HEADS):
            qm = jnp.where(_lane_mask(BR_W, HEAD_DIM * h, HEAD_DIM * (h + 1)), q, jnp.zeros_like(q))
            s_loc = lax.dot_general(qm, kw, _NT, preferred_element_type=F32) + bias_ref[cls, h]
            s_ctx = lax.dot_general(qm, kc, _NT, preferred_element_type=F32)
            m = jnp.maximum(jnp.max(s_loc, axis=-1, keepdims=True), jnp.max(s_ctx, axis=-1, keepdims=True))
            p_loc = jnp.exp2(s_loc - m)
            p_ctx = jnp.exp2(s_ctx - m)
            l = jnp.sum(p_loc, axis=-1, keepdims=True) + jnp.sum(p_ctx, axis=-1, keepdims=True)
            o = (jnp.dot(p_ctx.astype(BF16), vc, preferred_element_type=F32)
                 + jnp.dot(p_loc.astype(BF16), vw, preferred_element_type=F32)) * (1.0 / l)
            acc = jnp.where(_lane_mask(BR_W, HEAD_DIM * h, HEAD_DIM * (h + 1)), o, acc)
        o_ref[0, sub * pair:(sub + 1) * pair, :] = acc.astype(BF16)


def _na_call(q, k, v, kc, vc, bias):
    b, s, w = q.shape
    n_ctx = kc.shape[1]
    rows = s // GRID_W
    tq = 2 * GRID_W * NA_PAIRS
    return pl.pallas_call(
        functools.partial(_na_kernel, rows=rows),
        grid=(b, rows // (2 * NA_PAIRS)),
        in_specs=[
            pl.BlockSpec((1, tq, w), lambda i, j: (i, j, 0)),
            pl.BlockSpec((1, s, w), lambda i, j: (i, 0, 0)),
            pl.BlockSpec((1, s, w), lambda i, j: (i, 0, 0)),
            pl.BlockSpec((1, n_ctx, w), lambda i, j: (i, 0, 0)),
            pl.BlockSpec((1, n_ctx, w), lambda i, j: (i, 0, 0)),
            _const_spec(bias.shape),
        ],
        out_specs=pl.BlockSpec((1, tq, w), lambda i, j: (i, j, 0)),
        out_shape=jax.ShapeDtypeStruct((b, s, w), BF16),
        compiler_params=_cparams(("parallel", "arbitrary")),
        name="na",
    )(q, k, v, kc, vc, bias)


def _merge_kernel(x_ref, mod_ref, n1g_ref, n2g_ref, uc_ref, om_ref, od_ref, on_ref,
                  gw_ref, gb_ref, wc_ref, wm_ref, wd_ref, wn_ref, wo_ref, rwt_ref, rb_ref,
                  x1_ref, h2_ref, ids_ref, wts_ref):
    x = x_ref[0]
    mod = mod_ref[0]
    h = _modulate(x, n1g_ref[...], mod[0:1], mod[1:2]).astype(BF16)
    y = jnp.zeros(x.shape, F32)
    branches = ((uc_ref, wc_ref), (om_ref, wm_ref), (od_ref, wd_ref), (on_ref, wn_ref))
    for i, (o_ref, w_ref) in enumerate(branches):
        lo = D_MODEL * i
        g = _sigmoid(jnp.dot(h, gw_ref[:, lo:lo + D_MODEL], preferred_element_type=F32)
                     + gb_ref[:, lo:lo + D_MODEL])
        y = y + g * jnp.dot(o_ref[0], w_ref[...], preferred_element_type=F32)
    out = jnp.dot(y.astype(BF16), wo_ref[...], preferred_element_type=F32)
    x1 = x + mod[2:3] * out
    x1_ref[0] = x1
    h2 = _modulate(x1, n2g_ref[...], mod[3:4], mod[4:5])
    for q, piece in enumerate(_pack_row(h2)):
        h2_ref[q, 0] = piece

    logits = lax.dot_general(rwt_ref[...], h2, _NT, preferred_element_type=F32,
                             precision=lax.Precision.HIGHEST) + rb_ref[...]
    eidx = lax.broadcasted_iota(jnp.int32, logits.shape, 0).astype(F32)
    vals, idxs = [], []
    cur = logits
    for _ in range(TOP_K):
        m = jnp.max(cur, axis=0, keepdims=True)
        idx = jnp.min(jnp.where(cur == m, eidx, float(N_EXPERTS)), axis=0, keepdims=True)
        vals.append(m)
        idxs.append(idx)
        cur = jnp.where(eidx == idx, -jnp.inf, cur)
    es = [jnp.exp(vk - vals[0]) for vk in vals]
    den = es[0] + es[1] + es[2] + es[3]
    ids_ref[0] = jnp.concatenate(idxs, axis=0).astype(jnp.int32)
    wts_ref[0] = jnp.concatenate([e / den for e in es], axis=0)


def _merge_call(x, mods, mod_row, lw, uc, om, od, on):
    b, s, d = x.shape
    t = min(s, 512)
    if mod_row is None:
        mod_map = lambda i, j: (i, 0, 0)
    else:
        mod_map = lambda i, j: (mod_row, 0, 0)
    tok = lambda w: pl.BlockSpec((1, t, w), lambda i, j: (i, j, 0))
    rt = pl.BlockSpec((1, TOP_K, t), lambda i, j: (i, 0, j))
    return pl.pallas_call(
        _merge_kernel,
        grid=(b, s // t),
        in_specs=[
            tok(d), pl.BlockSpec((1, 6, d), mod_map), _const_spec((1, d)), _const_spec((1, d)),
            tok(BR_W), tok(BR_W), tok(BR_W), tok(BR_W),
            _const_spec((d, N_BRANCH * d)), _const_spec((1, N_BRANCH * d)),
            _const_spec((BR_W, d)), _const_spec((BR_W, d)), _const_spec((BR_W, d)), _const_spec((BR_W, d)),
            _const_spec((d, d)), _const_spec((N_EXPERTS, d)), _const_spec((N_EXPERTS, 1)),
        ],
        out_specs=[tok(d), pl.BlockSpec((ROW_PARTS, 1, t, ROW_Q), lambda i, j: (0, i, j, 0)), rt, rt],
        out_shape=[jax.ShapeDtypeStruct((b, s, d), F32), jax.ShapeDtypeStruct((ROW_PARTS, b, s, ROW_Q), jnp.int32),
                   jax.ShapeDtypeStruct((b, TOP_K, s), jnp.int32), jax.ShapeDtypeStruct((b, TOP_K, s), F32)],
        compiler_params=_cparams(("parallel", "parallel")),
        name="merge",
    )(x, mods, lw["n1g"], lw["n2g"], uc, om, od, on, lw["gate_w"], lw["gate_b"],
      lw["conv_out"], lw["mla_out"], lw["diff_out"], lw["na_out"], lw["w_o"], lw["router_wt"], lw["router_b"])


def _route(ids, tile):
    n = ids.shape[1]
    p = TOP_K * n
    e = ids.reshape(p)
    onehot = (e[:, None] == jnp.arange(N_EXPERTS, dtype=jnp.int32)[None, :]).astype(jnp.int32)
    csum = jnp.cumsum(onehot, axis=0)
    counts = csum[-1]
    padded = ((counts + tile - 1) // tile) * tile
    gend = jnp.cumsum(padded)
    gstart = gend - padded
    slot = jnp.sum(onehot * (csum - 1 + gstart[None, :]), axis=1).astype(jnp.int32)
    n_tiles = p // tile + N_EXPERTS
    n_slots = n_tiles * tile
    pair_of_slot = jnp.full((n_slots,), -1, jnp.int32).at[slot].set(
        jnp.arange(p, dtype=jnp.int32), unique_indices=True)
    real = pair_of_slot >= 0
    src_tok = jnp.where(real, pair_of_slot, jnp.arange(n_slots, dtype=jnp.int32)) % n
    dst_row = jnp.where(real, pair_of_slot, p + jnp.arange(n_slots, dtype=jnp.int32))
    tile_start = jnp.arange(n_tiles, dtype=jnp.int32) * tile
    texp = jnp.minimum(jnp.searchsorted(gend, tile_start, side="right"), N_EXPERTS - 1).astype(jnp.int32)
    tval = (tile_start < gend[-1]).astype(jnp.int32)
    return src_tok.reshape(n_tiles, 1, tile), dst_row.reshape(n_tiles, 1, tile), texp, tval


def _ffn_kernel(texp_ref, tval_ref, src_ref, nsrc_ref, dst_ref, pdst_ref, h_hbm, wgu_ref, bgu_ref, wd_ref, bd_ref,
                y_hbm, xbuf, ybuf, sem_in, sem_out, *, tile, n_tiles):
    i = pl.program_id(0)
    slot = i % 2
    valid = tval_ref[i] > 0
    prev_valid = (i >= 1) & (tval_ref[jnp.maximum(i - 1, 0)] > 0)

    def gather_row(idx_ref, buf_slot, r):
        pltpu.make_async_copy(h_hbm.at[pl.ds(idx_ref[0, 0, r], 1)], xbuf.at[buf_slot, pl.ds(r, 1)],
                              sem_in.at[buf_slot]).start()

    def scatter_row(idx_ref, buf_slot, r):
        pltpu.make_async_copy(ybuf.at[buf_slot, pl.ds(r, 1)], y_hbm.at[pl.ds(idx_ref[0, 0, r], 1)],
                              sem_out.at[buf_slot]).start()

    def rolled(fn):
        def body(r, carry):
            fn(r)
            return carry

        lax.fori_loop(0, tile, body, 0, unroll=8)

    def wait_rows(buf, sem):
        pltpu.make_async_copy(h_hbm.at[pl.ds(0, tile)], buf, sem).wait()

    def ffn(x):
        gu = jnp.dot(x, wgu_ref[0], preferred_element_type=F32) + bgu_ref[0]
        g = jnp.minimum(gu[:, :D_FF], SWIGLU_LIMIT)
        u = jnp.clip(gu[:, D_FF:], -SWIGLU_LIMIT, SWIGLU_LIMIT)
        act = ((u + 1.0) * (g * _sigmoid(SWIGLU_ALPHA * g))).astype(BF16)
        return jnp.dot(act, wd_ref[0], preferred_element_type=F32) + bd_ref[0]

    @pl.when((i == 0) & valid)
    def _():
        rolled(lambda r: gather_row(src_ref, 0, r))
        wait_rows(xbuf.at[0], sem_in.at[0])
        x = xbuf[0].astype(BF16)
        rolled(lambda r: gather_row(nsrc_ref, 1, r))
        ybuf[0] = ffn(x)

    @pl.when((i >= 2) & valid & prev_valid)
    def _():
        wait_rows(ybuf.at[slot], sem_out.at[slot])

    @pl.when((i >= 1) & valid)
    def _():
        wait_rows(xbuf.at[slot], sem_in.at[slot])
        n_chunk = (2 * D_FF) // FFN_CHUNK
        per = tile // n_chunk
        gu = []
        for c in range(n_chunk):
            x = xbuf[slot].astype(BF16)
            for r in range(c * per, (c + 1) * per):
                gather_row(nsrc_ref, 1 - slot, r)
                scatter_row(pdst_ref, 1 - slot, r)
            lo = c * FFN_CHUNK
            gu.append(jnp.dot(x, wgu_ref[0, :, lo:lo + FFN_CHUNK], preferred_element_type=F32)
                      + bgu_ref[0, :, lo:lo + FFN_CHUNK])
        half = n_chunk // 2
        act = []
        for c in range(half):
            g = jnp.minimum(gu[c], SWIGLU_LIMIT)
            u = jnp.clip(gu[half + c], -SWIGLU_LIMIT, SWIGLU_LIMIT)
            act.append(((u + 1.0) * (g * _sigmoid(SWIGLU_ALPHA * g))).astype(BF16))
        act = jnp.concatenate(act, axis=1)
        for c in range(D_MODEL // FFN_CHUNK):
            lo = c * FFN_CHUNK
            ybuf[slot, :, lo:lo + FFN_CHUNK] = (
                jnp.dot(act, wd_ref[0, :, lo:lo + FFN_CHUNK], preferred_element_type=F32)
                + bd_ref[0, :, lo:lo + FFN_CHUNK])

    def drain(last_slot):
        wait_rows(xbuf.at[1 - last_slot], sem_in.at[1 - last_slot])
        wait_rows(ybuf.at[last_slot], sem_out.at[last_slot])

    @pl.when(jnp.logical_not(valid) & prev_valid)
    def _():
        @pl.when(i >= 2)
        def _():
            wait_rows(ybuf.at[slot], sem_out.at[slot])

        rolled(lambda r: scatter_row(pdst_ref, 1 - slot, r))
        drain(1 - slot)

    @pl.when((i == n_tiles - 1) & valid)
    def _():
        wait_rows(ybuf.at[1 - slot], sem_out.at[1 - slot])
        rolled(lambda r: scatter_row(dst_ref, slot, r))
        drain(slot)


def _ffn_call(h2, src_tok, dst_row, texp, tval, lw, tile, n_out_rows):
    n_tiles = src_tok.shape[0]
    d = h2.shape[1]
    idx_spec = lambda f: pl.BlockSpec((1, 1, tile), f, memory_space=pltpu.SMEM)
    grid_spec = pltpu.PrefetchScalarGridSpec(
        num_scalar_prefetch=2,
        grid=(n_tiles,),
        in_specs=[
            idx_spec(lambda i, te, tv: (i, 0, 0)),
            idx_spec(lambda i, te, tv: (jnp.minimum(i + 1, n_tiles - 1), 0, 0)),
            idx_spec(lambda i, te, tv: (i, 0, 0)),
            idx_spec(lambda i, te, tv: (jnp.maximum(i - 1, 0), 0, 0)),
            pl.BlockSpec(memory_space=pl.ANY),
            pl.BlockSpec((1, d, 2 * D_FF), lambda i, te, tv: (te[i], 0, 0)),
            pl.BlockSpec((1, 1, 2 * D_FF), lambda i, te, tv: (te[i], 0, 0)),
            pl.BlockSpec((1, D_FF, d), lambda i, te, tv: (te[i], 0, 0)),
            pl.BlockSpec((1, 1, d), lambda i, te, tv: (te[i], 0, 0)),
        ],
        out_specs=pl.BlockSpec(memory_space=pl.ANY),
        scratch_shapes=[pltpu.VMEM((2, tile, d), F32), pltpu.VMEM((2, tile, d), F32),
                        pltpu.SemaphoreType.DMA((2,)), pltpu.SemaphoreType.DMA((2,))],
    )
    return pl.pallas_call(
        functools.partial(_ffn_kernel, tile=tile, n_tiles=n_tiles),
        grid_spec=grid_spec,
        out_shape=jax.ShapeDtypeStruct((n_out_rows, d), F32),
        compiler_params=_cparams(("arbitrary",)),
        name="moe_ffn",
    )(texp, tval, src_tok, src_tok, dst_row, dst_row, h2, lw["exp_w_gu"], lw["exp_b_gu"], lw["exp_w_down"], lw["exp_b_down"])


def _combine_kernel(x1_ref, mod_ref, w_ref, y0_ref, y1_ref, y2_ref, y3_ref, o_ref):
    w = w_ref[0]
    acc = w[:, 0:1] * y0_ref[...]
    for k, y_ref in ((1, y1_ref), (2, y2_ref), (3, y3_ref)):
        acc = acc + w[:, k:k + 1] * y_ref[...]
    o_ref[0] = x1_ref[0] + mod_ref[0][5:6] * acc


def _combine_call(x1, mods, mod_row, wts, y):
    b, s, d = x1.shape
    t = min(s, 512)
    nt = s // t
    if mod_row is None:
        mod_map = lambda i, j: (i, 0, 0)
    else:
        mod_map = lambda i, j: (mod_row, 0, 0)
    wts = wts.transpose(0, 2, 1)
    y_spec = lambda k: pl.BlockSpec((t, d), lambda i, j: (k * b * nt + i * nt + j, 0))
    return pl.pallas_call(
        _combine_kernel,
        grid=(b, nt),
        in_specs=[
            pl.BlockSpec((1, t, d), lambda i, j: (i, j, 0)),
            pl.BlockSpec((1, 6, d), mod_map),
            pl.BlockSpec((1, t, TOP_K), lambda i, j: (i, j, 0)),
            y_spec(0), y_spec(1), y_spec(2), y_spec(3),
        ],
        out_specs=pl.BlockSpec((1, t, d), lambda i, j: (i, j, 0)),
        out_shape=jax.ShapeDtypeStruct((b, s, d), F32),
        compiler_params=_cparams(("parallel", "parallel")),
        name="moe_combine",
    )(x1, mods, wts, y, y, y, y)


def _moe(x1, h2, ids, wts, mods, mod_row, lw):
    b, s, d = x1.shape
    n = b * s
    tile = 512 if TOP_K * n >= 512 * N_EXPERTS * 4 else 256
    ids_flat = ids.transpose(1, 0, 2).reshape(TOP_K, n)
    src_tok, dst_row, texp, tval = _route(ids_flat, tile)
    n_out_rows = TOP_K * n + src_tok.shape[0] * tile
    y = _ffn_call(h2.reshape(n, d), src_tok, dst_row, texp, tval, lw, tile, n_out_rows)
    return _combine_call(x1, mods, mod_row, wts, y)


SC_WINDOW = 128
ROW_Q = D_MODEL // 4
ROW_PARTS = 2
HALF_D = D_MODEL // 2


def _pack_bf16_pair(a, b):
    ua = lax.bitcast_convert_type(a.astype(BF16).astype(F32), jnp.int32)
    ub = lax.bitcast_convert_type(b.astype(BF16).astype(F32), jnp.int32)
    return ua | lax.shift_right_logical(ub, jnp.int32(16))


def _unpack_bf16_pair(w):
    a = lax.bitcast_convert_type(w & jnp.int32(-65536), F32)
    b = lax.bitcast_convert_type(lax.shift_left(w, jnp.int32(16)), F32)
    return a, b


def _pack_row(x):
    w = _pack_bf16_pair(x[:, :HALF_D], x[:, HALF_D:])
    return [w[:, ROW_Q * q:ROW_Q * (q + 1)] for q in range(ROW_PARTS)]


def _unpack_row(pieces):
    ab = [_unpack_bf16_pair(w) for w in pieces]
    return jnp.concatenate([a for a, _ in ab] + [b for _, b in ab], axis=1)


def _route_slots(ids, tile):
    n = ids.shape[1]
    p = TOP_K * n
    e = ids.reshape(p)
    onehot = (e[:, None] == jnp.arange(N_EXPERTS, dtype=jnp.int32)[None, :])
    chunk = 512
    oh3 = onehot.astype(F32).reshape(p // chunk, chunk, N_EXPERTS)
    within = jnp.einsum("ij,cje->cie", jnp.tril(jnp.ones((chunk, chunk), F32)), oh3)
    totals = within[:, -1, :]
    before = jnp.cumsum(totals, axis=0) - totals
    csum = (within + before[:, None, :]).reshape(p, N_EXPERTS).astype(jnp.int32)
    onehot = onehot.astype(jnp.int32)
    counts = csum[-1]
    padded = ((counts + tile - 1) // tile) * tile
    gend = jnp.cumsum(padded)
    gstart = gend - padded
    slot = jnp.sum(onehot * (csum - 1 + gstart[None, :]), axis=1).astype(jnp.int32)
    n_tiles = p // tile + N_EXPERTS
    tile_start = jnp.arange(n_tiles, dtype=jnp.int32) * tile
    texp = jnp.sum((tile_start[:, None] >= gend[None, :]).astype(jnp.int32), axis=1)
    texp = jnp.minimum(texp, N_EXPERTS - 1)
    nreal = jnp.clip(gstart[texp] + counts[texp] - tile_start, 0, tile)
    nreal = jnp.where(tile_start < gend[-1], nreal, 0).astype(jnp.int32)
    return slot, texp, nreal, n_tiles


def _sc_mesh():
    return plsc.VectorSubcoreMesh(core_axis_name="c", subcore_axis_name="s")


def _sc_dispatch(hq, idx, n_slots):
    parts, n, w = hq.shape
    src = hq.reshape(parts * n, w)
    m = idx.shape[0]
    blocks_per_q = n // SC_WINDOW
    per_q = TOP_K * blocks_per_q

    @pl.kernel(out_type=jax.ShapeDtypeStruct((parts * n_slots, w), hq.dtype), mesh=_sc_mesh(), scratch_types=[])
    def kern(x_hbm, i_hbm, o_hbm):
        def body(x_vmem, i_vmem):
            pltpu.sync_copy(x_vmem, o_hbm.at[i_vmem.at[0]])

        pltpu.emit_pipeline(
            body,
            grid=(m // SC_WINDOW,),
            in_specs=[
                pl.BlockSpec((SC_WINDOW, w), index_map=lambda i: ((i // per_q) * blocks_per_q + i % blocks_per_q, 0)),
                pl.BlockSpec((1, SC_WINDOW), index_map=lambda i: (0, i)),
            ],
            out_specs=[],
            core_axis_name=("c", "s"),
            dimension_semantics=(pltpu.PARALLEL,),
        )(x_hbm, i_hbm)

    return kern(src, idx.reshape(1, m)).reshape(parts, n_slots, w)


def _sc_collect(ys, idx):
    parts, n_slots, w = ys.shape
    src = ys.reshape(parts * n_slots, w)
    m = idx.shape[0]

    @pl.kernel(out_type=jax.ShapeDtypeStruct((m, w), ys.dtype), mesh=_sc_mesh(), scratch_types=[])
    def kern(x_hbm, i_hbm, o_hbm):
        def body(i_vmem, o_vmem):
            pltpu.sync_copy(x_hbm.at[i_vmem.at[0]], o_vmem)

        pltpu.emit_pipeline(
            body,
            grid=(m // SC_WINDOW,),
            in_specs=[pl.BlockSpec((1, SC_WINDOW), index_map=lambda i: (0, i))],
            out_specs=[pl.BlockSpec((SC_WINDOW, w), index_map=lambda i: (i, 0))],
            core_axis_name=("c", "s"),
            dimension_semantics=(pltpu.PARALLEL,),
        )(i_hbm, o_hbm)

    return kern(src, idx.reshape(1, m))


def _ffn_sorted_kernel(texp_ref, nreal_ref, x_ref, wgu_ref, bgu_ref, wd_ref, bd_ref, y_ref, wgu_bf, wd_bf, *, tile):
    i = pl.program_id(0)
    nreal = nreal_ref[i]

    @pl.when((nreal > 0) & ((i == 0) | (texp_ref[i] != texp_ref[jnp.maximum(i - 1, 0)])))
    def _():
        wgu_bf[...] = wgu_ref[0].astype(BF16)
        wd_bf[...] = wd_ref[0].astype(BF16)

    @pl.when(nreal > 0)
    def _():
        x = _unpack_row([x_ref[q] for q in range(ROW_PARTS)])
        row = lax.broadcasted_iota(jnp.int32, (tile, 1), 0)
        x = jnp.where(row < nreal, x, 0.0).astype(BF16)
        gu = jnp.dot(x, wgu_bf[...], preferred_element_type=F32) + bgu_ref[0]
        g = jnp.minimum(gu[:, :D_FF], SWIGLU_LIMIT)
        u = jnp.clip(gu[:, D_FF:], -SWIGLU_LIMIT, SWIGLU_LIMIT)
        act = ((u + 1.0) * (g * _sigmoid(SWIGLU_ALPHA * g))).astype(BF16)
        y = jnp.dot(act, wd_bf[...], preferred_element_type=F32) + bd_ref[0]
        for q, piece in enumerate(_pack_row(y)):
            y_ref[q] = piece

    @pl.when(nreal == 0)
    def _():
        y_ref[...] = jnp.zeros(y_ref.shape, jnp.int32)


def _ffn_sorted_call(xs, texp, nreal, lw, tile):
    _, n_slots, w = xs.shape
    n_tiles = n_slots // tile
    d = D_MODEL
    off = lw["exp_off"]
    grid_spec = pltpu.PrefetchScalarGridSpec(
        num_scalar_prefetch=2,
        grid=(n_tiles,),
        in_specs=[
            pl.BlockSpec((ROW_PARTS, tile, w), lambda i, te, nr: (0, i, 0)),
            pl.BlockSpec((1, d, 2 * D_FF), lambda i, te, nr: (te[i] + off, 0, 0)),
            pl.BlockSpec((1, 1, 2 * D_FF), lambda i, te, nr: (te[i] + off, 0, 0)),
            pl.BlockSpec((1, D_FF, d), lambda i, te, nr: (te[i] + off, 0, 0)),
            pl.BlockSpec((1, 1, d), lambda i, te, nr: (te[i] + off, 0, 0)),
        ],
        out_specs=pl.BlockSpec((ROW_PARTS, tile, w), lambda i, te, nr: (0, i, 0)),
        scratch_shapes=[pltpu.VMEM((d, 2 * D_FF), BF16), pltpu.VMEM((D_FF, d), BF16)],
    )
    return pl.pallas_call(
        functools.partial(_ffn_sorted_kernel, tile=tile),
        grid_spec=grid_spec,
        out_shape=jax.ShapeDtypeStruct((ROW_PARTS, n_slots, w), jnp.int32),
        compiler_params=_cparams(("arbitrary",)),
        name="moe_ffn",
    )(texp, nreal, xs, lw["exp_w_gu"], lw["exp_b_gu"], lw["exp_w_down"], lw["exp_b_down"])


def _combine_q_kernel(x1_ref, mod_ref, w_ref, y_ref, o_ref):
    w = w_ref[0]
    g2 = mod_ref[0][5:6]
    for q in range(ROW_PARTS):
        acc_a, acc_b = None, None
        for k in range(TOP_K):
            a, b = _unpack_bf16_pair(y_ref[q, k])
            wk = w[:, k:k + 1]
            acc_a = wk * a if acc_a is None else acc_a + wk * a
            acc_b = wk * b if acc_b is None else acc_b + wk * b
        for lo, acc in ((ROW_Q * q, acc_a), (HALF_D + ROW_Q * q, acc_b)):
            o_ref[0, :, lo:lo + ROW_Q] = x1_ref[0, :, lo:lo + ROW_Q] + g2[:, lo:lo + ROW_Q] * acc


def _combine_q_call(x1, mods, mod_row, wts, y, tok_off):
    b, s, d = x1.shape
    t = min(s, 512)
    nt = s // t
    blk_off = tok_off // t
    if mod_row is None:
        mod_map = lambda i, j: (i, 0, 0)
    else:
        mod_map = lambda i, j: (mod_row, 0, 0)
    wts = wts.transpose(0, 2, 1)
    return pl.pallas_call(
        _combine_q_kernel,
        grid=(b, nt),
        in_specs=[
            pl.BlockSpec((1, t, d), lambda i, j: (i, j, 0)),
            pl.BlockSpec((1, 6, d), mod_map),
            pl.BlockSpec((1, t, TOP_K), lambda i, j: (i, j, 0)),
            pl.BlockSpec((ROW_PARTS, TOP_K, t, ROW_Q), lambda i, j: (0, 0, blk_off + i * nt + j, 0)),
        ],
        out_specs=pl.BlockSpec((1, t, d), lambda i, j: (i, j, 0)),
        out_shape=jax.ShapeDtypeStruct((b, s, d), F32),
        compiler_params=_cparams(("parallel", "parallel")),
        name="moe_combine",
    )(x1, mods, wts, y)


def _moe_sc(streams, mods, lw):
    sizes = [st[0].shape[0] * st[0].shape[1] for st in streams]
    n = sum(sizes)
    tile = 512 if TOP_K * n >= 512 * N_EXPERTS * 4 else 256
    hq = jnp.concatenate([st[1].reshape(ROW_PARTS, m, ROW_Q) for st, m in zip(streams, sizes)], axis=1)
    ids = jnp.concatenate([st[2].transpose(1, 0, 2).reshape(TOP_K, m) for st, m in zip(streams, sizes)], axis=1)
    slot, texp, nreal, n_tiles = _route_slots(ids, tile)
    n_slots = n_tiles * tile
    idx = (slot[None, :] + (jnp.arange(ROW_PARTS, dtype=jnp.int32) * n_slots)[:, None]).reshape(-1)
    xs = _sc_dispatch(hq, idx, n_slots)
    ys = _ffn_sorted_call(xs, texp, nreal, lw, tile)
    y = _sc_collect(ys, idx).reshape(ROW_PARTS, TOP_K, n, ROW_Q)
    outs, off = [], 0
    for (x1, _, _, wts, mod_row), m in zip(streams, sizes):
        outs.append(_combine_q_call(x1, mods, mod_row, wts, y, off))
        off += m
    return outs


def _layer_weights(l, p, lam_init):
    w = p["w_in"][l]
    d = w.shape[0]
    zcols = lambda n: jnp.zeros((d, n), w.dtype)
    regroup = lambda blk: blk.reshape(d, N_HEADS, 3, HEAD_DIM).transpose(0, 2, 1, 3).reshape(d, 3 * BR_W)
    w_in = jnp.concatenate([
        w[:, :A_IN],
        w[:, OFF_B:OFF_B + Q_LORA], zcols(P_CKV - P_CQ - Q_LORA),
        w[:, OFF_B + Q_LORA:OFF_B + Q_LORA + KV_LORA],
        w[:, OFF_B + Q_LORA + KV_LORA:OFF_C], zcols(P_DQ - P_KPE - QK_ROPE),
        regroup(w[:, OFF_C:OFF_D]), regroup(w[:, OFF_D:]),
    ], axis=1)
    assert w_in.shape[1] == PROJ_W

    def head_slots(w3, slot):
        w3 = jnp.pad(w3, ((0, 0), (0, 0), (0, slot - w3.shape[2])))
        return w3.reshape(w3.shape[0], N_HEADS * slot)

    wuq = head_slots(p["mla_w_uq"][l].reshape(Q_LORA, N_HEADS, MLA_QK), LANE)
    wuq = jnp.pad(wuq, ((0, 256 - Q_LORA), (0, 0)))
    wukv = p["mla_w_ukv"][l].reshape(KV_LORA, N_HEADS, QK_NOPE + V_HEAD)
    wk = head_slots(wukv[:, :, :QK_NOPE], LANE)
    wv = head_slots(wukv[:, :, QK_NOPE:], V_HEAD)
    ppe = np.zeros((LANE, 512), np.float32)
    for h in range(N_HEADS):
        for i in range(QK_ROPE):
            ppe[i, h * LANE + QK_NOPE + i] = 1.0

    def slot_gain(g, scale):
        g = jnp.concatenate([g * scale, jnp.zeros((LANE - MLA_QK,), F32)])
        return jnp.tile(g, N_HEADS)

    def row512(v):
        return jnp.concatenate([v, jnp.zeros((512 - v.shape[0],), F32)])

    gains = jnp.stack([
        row512(p["mla_cq_g"][l]),
        slot_gain(p["mla_qn_g"][l], MLA_QK ** -0.5 * LOG2E),
        row512(p["mla_ckv_g"][l]),
        slot_gain(p["mla_kn_g"][l], 1.0),
        row512(jnp.tile(p["diff_qn_g"][l], 2 * N_HEADS) * DIFF_DIM ** -0.5 * LOG2E),
        row512(jnp.tile(p["diff_kn_g"][l], 2 * N_HEADS)),
        row512(jnp.tile(p["na_qn_g"][l], N_HEADS) * HEAD_DIM ** -0.5 * LOG2E),
        row512(jnp.tile(p["na_kn_g"][l], N_HEADS)),
    ])
    conv_w = jnp.concatenate([p["conv_w"][l], jnp.zeros((1, CONV_CH), F32)], axis=0)
    return dict(
        n1g=p["norm1_g"][l][None, :], n2g=p["norm2_g"][l][None, :],
        w_in=w_in.astype(BF16), gains=gains,
        wuq=wuq.astype(BF16), wk=wk.astype(BF16), wv=wv.astype(BF16), ppe=jnp.asarray(ppe, BF16),
        g96=jnp.asarray(_group_ones(512, LANE, MLA_QK), BF16),
        g32=jnp.asarray(_group_ones(BR_W, DIFF_DIM, DIFF_DIM), BF16),
        g64=jnp.asarray(_group_ones(BR_W, HEAD_DIM, HEAD_DIM), BF16),
        rm=jnp.asarray(_rot_matrix(512, LANE, QK_NOPE, QK_ROPE // 2), BF16),
        rd=jnp.asarray(_rot_matrix(BR_W, DIFF_DIM, 0, DIFF_DIM // 2), BF16),
        conv_w=conv_w, conv_b=p["conv_b"][l][None, :],
        conv_ln_g=p["conv_ln_g"][l][None, :], conv_ln_b=p["conv_ln_b"][l][None, :],
        diff_lam=p["diff_lam"][l],
        subln=(jnp.tile(p["diff_subln_g"][l], N_HEADS) * (1.0 - lam_init))[None, :],
        gate_w=p["gate_w"][l].astype(BF16), gate_b=p["gate_b"][l][None, :],
        conv_out=p["conv_out"][l].astype(BF16), mla_out=p["mla_out"][l].astype(BF16),
        diff_out=p["diff_out"][l].astype(BF16), na_out=p["na_out"][l].astype(BF16),
        w_o=p["w_o"][l].astype(BF16),
        router_wt=p["router_w"][l].T, router_b=p["router_b"][l][:, None],
        exp_off=l * N_EXPERTS,
        exp_w_gu=p["exp_w_gu"].reshape((-1,) + p["exp_w_gu"].shape[2:]),
        exp_b_gu=p["exp_b_gu"].reshape(-1, 1, 2 * D_FF),
        exp_w_down=p["exp_w_down"].reshape((-1,) + p["exp_w_down"].shape[2:]),
        exp_b_down=p["exp_b_down"].reshape(-1, 1, D_MODEL),
    )


def _kt(kc, k):
    return jnp.concatenate([kc, k], axis=1).transpose(0, 2, 1)


def kernel(x, c, ctx, c_ctx, ada_w, ada_b, norm1_g, norm2_g, w_in, conv_w, conv_b, conv_ln_g, conv_ln_b, conv_out, mla_cq_g, mla_ckv_g, mla_w_uq, mla_w_ukv, mla_qn_g, mla_kn_g, mla_out, diff_qn_g, diff_kn_g, diff_lam, diff_subln_g, diff_out, na_qn_g, na_kn_g, na_rpb, na_out, gate_w, gate_b, w_o, router_w, router_b, exp_w_gu, exp_b_gu, exp_w_down, exp_b_down):
    p = dict(norm1_g=norm1_g, norm2_g=norm2_g, w_in=w_in, conv_w=conv_w, conv_b=conv_b,
             conv_ln_g=conv_ln_g, conv_ln_b=conv_ln_b, conv_out=conv_out, mla_cq_g=mla_cq_g,
             mla_ckv_g=mla_ckv_g, mla_w_uq=mla_w_uq, mla_w_ukv=mla_w_ukv, mla_qn_g=mla_qn_g,
             mla_kn_g=mla_kn_g, mla_out=mla_out, diff_qn_g=diff_qn_g, diff_kn_g=diff_kn_g,
             diff_lam=diff_lam, diff_subln_g=diff_subln_g, diff_out=diff_out, na_qn_g=na_qn_g,
             na_kn_g=na_kn_g, na_out=na_out, gate_w=gate_w, gate_b=gate_b, w_o=w_o,
             router_w=router_w, router_b=router_b, exp_w_gu=exp_w_gu, exp_b_gu=exp_b_gu,
             exp_w_down=exp_w_down, exp_b_down=exp_b_down)
    b, s, d = x.shape
    n_ctx = ctx.shape[1]
    depth = ada_w.shape[0]
    rows = s // GRID_W
    assert d == D_MODEL and s % (2 * GRID_W) == 0 and rows >= NA_BAND_ROWS and n_ctx % LANE == 0

    mod_rows = -(-(b + 1) // 8) * 8
    cs = jnp.concatenate([c, c_ctx[None, :], jnp.zeros((mod_rows - b - 1, d), F32)], axis=0)
    mods_all = _ada_call(cs, ada_w, ada_b).reshape(depth, mod_rows, 6, d)

    tabs_x = _rope_lane_tables(s)
    tabs_c = (jnp.ones((n_ctx, 512), F32), jnp.zeros((n_ctx, 512), F32),
              jnp.ones((n_ctx, BR_W), F32), jnp.zeros((n_ctx, BR_W), F32))

    xc = ctx
    for l in range(depth):
        last = l == depth - 1
        lam_init = 0.8 - 0.6 * math.exp(-0.3 * l)
        lw = _layer_weights(l, p, lam_init)
        mods = mods_all[l]
        bias = _na_bias_tables(na_rpb[l], rows)

        u, mq, mk, mv, dq, dk, dv, nq, nk, nv = _proj_call(x, mods, None, lw, tabs_x, True)
        uc, mqc, mkc, mvc, dqc, dkc, dvc, nqc, nkc, nvc = _proj_call(xc, mods, b, lw, tabs_c, False)

        y_conv = _conv_call(u, lw)
        y_mla = _attn_call(mq, _kt(mkc, mk), jnp.concatenate([mvc, mv], axis=1), lw, MAPS_MLA)
        y_diff = _attn_call(dq, _kt(dkc, dk), jnp.concatenate([dvc, dv], axis=1), lw, MAPS_DIFF,
                            diff=True, lam_init=lam_init)
        y_na = _na_call(nq, nk, nv, nkc, nvc, bias)
        x1, h2, ids, wts = _merge_call(x, mods, None, lw, y_conv, y_mla, y_diff, y_na)
        streams = [(x1, h2, ids, wts, None)]

        if not last:
            yc_conv = _conv_call(uc, lw)
            yc_mla = _attn_call(mqc, mkc.transpose(0, 2, 1), mvc, lw, MAPS_MLA)
            yc_diff = _attn_call(dqc, dkc.transpose(0, 2, 1), dvc, lw, MAPS_DIFF, diff=True, lam_init=lam_init)
            yc_na = _attn_call(nqc, nkc.transpose(0, 2, 1), nvc, lw, MAPS_NA)
            xc1, h2c, idsc, wtsc = _merge_call(xc, mods, b, lw, yc_conv, yc_mla, yc_diff, yc_na)
            streams.append((xc1, h2c, idsc, wtsc, b))

        outs = _moe_sc(streams, mods, lw)
        x = outs[0]
        if not last:
            xc = outs[1]
    return x
```

```python
import functools
import math

import numpy as np
import jax
import jax.numpy as jnp
from jax import lax
from jax.experimental import pallas as pl
from jax.experimental.pallas import tpu as pltpu
from jax.experimental.pallas import tpu_sc as plsc

F32 = jnp.float32
BF16 = jnp.bfloat16

D_MODEL = 1024
GRID_W = 64
N_BRANCH = 4
N_HEADS = 4
HEAD_DIM = 64
CONV_CH = 256
CONV_WIDTH = 31
Q_LORA = 192
KV_LORA = 128
QK_NOPE = 64
QK_ROPE = 32
V_HEAD = 64
DIFF_DIM = 32
DIFF_V = 2 * DIFF_DIM
NA_KH = 8
NA_KW = 16
ROPE_DIM = 32
ROPE_BASE = 10000.0
N_EXPERTS = 32
TOP_K = 4
D_FF = 1024
SWIGLU_LIMIT = 7.0
SWIGLU_ALPHA = 1.702
EPS = 1e-6
NEG_INF = -1e30

A_IN = 2 * CONV_CH
B_IN = Q_LORA + KV_LORA + QK_ROPE
C_IN = N_HEADS * (4 * DIFF_DIM + DIFF_V)
D_IN = N_HEADS * 3 * HEAD_DIM
OFF_B = A_IN
OFF_C = OFF_B + B_IN
OFF_D = OFF_C + C_IN

LANE = 128
SUBLANE = 8
MLA_QK = QK_NOPE + QK_ROPE
BR_W = N_HEADS * HEAD_DIM
PROJ_W = 2560
NA_BAND_ROWS = 10
ATTN_TQ = 512
NA_PAIRS = 2
LOG2E = math.log2(math.e)
VMEM_LIMIT = 52 * 1024 * 1024

P_A, P_G, P_CQ, P_CKV, P_KPE = 0, 256, 512, 768, 896
P_DQ, P_DK, P_DV = 1024, 1280, 1536
P_NQ, P_NK, P_NV = 1792, 2048, 2304


def _sigmoid(x):
    return 1.0 / (1.0 + jnp.exp(-x))


def _modulate(x, g, shift, scale):
    ms = jnp.mean(x * x, axis=-1, keepdims=True)
    return (x * lax.rsqrt(ms + EPS) * g) * (1.0 + scale) + shift


def _cparams(sem):
    return pltpu.CompilerParams(dimension_semantics=sem, vmem_limit_bytes=VMEM_LIMIT)


def _const_spec(shape):
    n = len(shape)
    return pl.BlockSpec(shape, lambda *_: (0,) * n)


def _group_ones(width, slot, real):
    i = np.arange(width)
    valid = (i % slot) < real
    same = (i[:, None] // slot) == (i[None, :] // slot)
    return (same & valid[:, None] & valid[None, :]).astype(np.float32)


def _rot_matrix(width, slot, start, half):
    r = np.zeros((width, width), np.float32)
    for s0 in range(0, width, slot):
        for i in range(half):
            a, b = s0 + start + i, s0 + start + half + i
            r[b, a] = -1.0
            r[a, b] = 1.0
    return r


def _rope_lane_tables(n_tokens):
    t = jnp.arange(n_tokens, dtype=jnp.int32)
    rows = (t // GRID_W).astype(F32)
    cols = (t % GRID_W).astype(F32)
    axis_dim = ROPE_DIM // 2
    inv = ROPE_BASE ** (-jnp.arange(0, axis_dim, 2, dtype=F32) / axis_dim)
    theta = jnp.concatenate([rows[:, None] * inv, cols[:, None] * inv], axis=-1)
    cos, sin = jnp.cos(theta), jnp.sin(theta)
    half = ROPE_DIM // 2
    ones = jnp.ones((n_tokens, QK_NOPE), F32)
    zeros = jnp.zeros((n_tokens, QK_NOPE), F32)
    pad1 = jnp.ones((n_tokens, LANE - MLA_QK), F32)
    pad0 = jnp.zeros((n_tokens, LANE - MLA_QK), F32)
    cm = jnp.tile(jnp.concatenate([ones, cos, cos, pad1], -1), (1, N_HEADS))
    sm = jnp.tile(jnp.concatenate([zeros, sin, sin, pad0], -1), (1, N_HEADS))
    cd = jnp.tile(jnp.concatenate([cos, cos], -1), (1, 2 * N_HEADS))
    sd = jnp.tile(jnp.concatenate([sin, sin], -1), (1, 2 * N_HEADS))
    assert half * 2 == DIFF_DIM
    return cm, sm, cd, sd


def _na_bias_tables(rpb, rows, n_ctx):
    kh = min(NA_KH, rows)
    nj = rows // 2
    reps = np.array([0, 1, 2, nj - 2, nj - 1])
    n_ro, n_co = 2 * NA_KH - 1, 2 * NA_KW - 1
    start = np.clip(2 * reps - 4, 0, rows - NA_BAND_ROWS)
    r = 2 * reps[:, None] + np.arange(2)[None, :]
    kr = start[:, None] + np.arange(NA_BAND_ROWS)[None, :]
    row_start = np.clip(r - kh // 2, 0, rows - kh)
    vr = (kr[:, None, :] >= row_start[:, :, None]) & (kr[:, None, :] < row_start[:, :, None] + kh)
    ro = np.clip(kr[:, None, :] - r[:, :, None] + NA_KH - 1, 0, n_ro - 1)
    qc = np.arange(GRID_W)
    win_start = np.clip(qc - NA_KW // 2, 0, GRID_W - NA_KW)
    vc = (qc[None, :] >= win_start[:, None]) & (qc[None, :] < win_start[:, None] + NA_KW)
    co = np.clip(qc[None, :] - qc[:, None] + NA_KW - 1, 0, n_co - 1)
    rsel = (ro[..., None] == np.arange(n_ro)).astype(np.float32)
    csel = (co[None] == np.arange(n_co)[:, None, None]).astype(np.float32)
    hi = lax.Precision.HIGHEST
    t1 = jnp.einsum("cqav,hvw->hcqaw", rsel, rpb.astype(F32), precision=hi)
    b = jnp.einsum("hcqaw,wxy->chqxay", t1, csel, precision=hi)
    valid = vr[:, None, :, None, :, None] & vc[None, None, None, :, None, :]
    b = jnp.where(valid, b * LOG2E, NEG_INF)
    b = b.reshape(len(reps), N_HEADS, 2 * GRID_W, NA_BAND_ROWS * GRID_W)
    return jnp.concatenate([b, jnp.zeros(b.shape[:3] + (n_ctx,), F32)], axis=-1)


def _ada_kernel(c_ref, w_ref, b_ref, o_ref):
    c = c_ref[...]
    s = c * _sigmoid(c)
    o_ref[0] = jnp.dot(s, w_ref[0], preferred_element_type=F32,
                       precision=lax.Precision.HIGHEST) + b_ref[0]


def _ada_call(cs, ada_w, ada_b):
    depth, d, n = ada_w.shape
    rows = cs.shape[0]
    tn = 1536
    return pl.pallas_call(
        _ada_kernel,
        grid=(depth, n // tn),
        in_specs=[
            pl.BlockSpec((rows, d), lambda l, j: (0, 0)),
            pl.BlockSpec((1, d, tn), lambda l, j: (l, 0, j)),
            pl.BlockSpec((1, 1, tn), lambda l, j: (l, 0, j)),
        ],
        out_specs=pl.BlockSpec((1, rows, tn), lambda l, j: (l, 0, j)),
        out_shape=jax.ShapeDtypeStruct((depth, rows, n), F32),
        compiler_params=_cparams(("arbitrary", "arbitrary")),
        name="ada",
    )(cs, ada_w, ada_b.reshape(depth, 1, n))


def _group_norm(x, ones_ref, inv_n):
    sq = (x * x).astype(BF16)
    ms = jnp.dot(sq, ones_ref[...], preferred_element_type=F32) * inv_n
    return x * lax.rsqrt(ms + EPS)


def _rope(x, rot_ref, cos_ref, sin_ref):
    rot = jnp.dot(x.astype(BF16), rot_ref[...], preferred_element_type=F32)
    return x * cos_ref[...] + rot * sin_ref[...]


def _proj_kernel(x_ref, mod_ref, n1g_ref, win_ref, gains_ref, wuq_ref, wk_ref, ppe_ref, wv_ref,
                 g96_ref, g32_ref, g64_ref, rm_ref, rd_ref, cm_ref, sm_ref, cd_ref, sd_ref,
                 u_ref, mq_ref, mk_ref, mv_ref, dq_ref, dk_ref, dv_ref, nq_ref, nk_ref, nv_ref,
                 *, use_rope):
    x = x_ref[0]
    mod = mod_ref[0]
    gains = gains_ref[...]
    h = _modulate(x, n1g_ref[...], mod[0:1], mod[1:2]).astype(BF16)
    proj = jnp.dot(h, win_ref[...], preferred_element_type=F32)

    u_ref[0] = proj[:, P_A:P_A + CONV_CH] * _sigmoid(proj[:, P_G:P_G + CONV_CH])

    cq = proj[:, P_CQ:P_CQ + 256]
    ms = jnp.sum(cq * cq, axis=-1, keepdims=True) * (1.0 / Q_LORA)
    cqn = (cq * lax.rsqrt(ms + EPS) * gains[0:1, :256]).astype(BF16)
    q = jnp.dot(cqn, wuq_ref[...], preferred_element_type=F32)
    q = _group_norm(q, g96_ref, 1.0 / MLA_QK) * gains[1:2, :]
    if use_rope:
        q = _rope(q, rm_ref, cm_ref, sm_ref)
    mq_ref[0] = q.astype(BF16)

    ckv = proj[:, P_CKV:P_CKV + KV_LORA]
    ms = jnp.mean(ckv * ckv, axis=-1, keepdims=True)
    ckvn = (ckv * lax.rsqrt(ms + EPS) * gains[2:3, :KV_LORA]).astype(BF16)
    kpe = proj[:, P_KPE:P_KPE + LANE].astype(BF16)
    k = (jnp.dot(ckvn, wk_ref[...], preferred_element_type=F32)
         + jnp.dot(kpe, ppe_ref[...], preferred_element_type=F32))
    k = _group_norm(k, g96_ref, 1.0 / MLA_QK) * gains[3:4, :]
    if use_rope:
        k = _rope(k, rm_ref, cm_ref, sm_ref)
    mk_ref[0] = k.astype(BF16)
    mv_ref[0] = jnp.dot(ckvn, wv_ref[...], preferred_element_type=F32).astype(BF16)

    qd = _group_norm(proj[:, P_DQ:P_DQ + BR_W], g32_ref, 1.0 / DIFF_DIM) * gains[4:5, :BR_W]
    kd = _group_norm(proj[:, P_DK:P_DK + BR_W], g32_ref, 1.0 / DIFF_DIM) * gains[5:6, :BR_W]
    if use_rope:
        qd = _rope(qd, rd_ref, cd_ref, sd_ref)
        kd = _rope(kd, rd_ref, cd_ref, sd_ref)
    dq_ref[0] = qd.astype(BF16)
    dk_ref[0] = kd.astype(BF16)
    dv_ref[0] = proj[:, P_DV:P_DV + BR_W].astype(BF16)

    qn = _group_norm(proj[:, P_NQ:P_NQ + BR_W], g64_ref, 1.0 / HEAD_DIM) * gains[6:7, :BR_W]
    kn = _group_norm(proj[:, P_NK:P_NK + BR_W], g64_ref, 1.0 / HEAD_DIM) * gains[7:8, :BR_W]
    nq_ref[0] = qn.astype(BF16)
    nk_ref[0] = kn.astype(BF16)
    nv_ref[0] = proj[:, P_NV:P_NV + BR_W].astype(BF16)


def _proj_call(x, mods, mod_row, lw, tabs, use_rope):
    b, s, d = x.shape
    t = min(s, 512)
    grid = (b, s // t)
    if mod_row is None:
        mod_map = lambda i, j: (i, 0, 0)
    else:
        mod_map = lambda i, j: (mod_row, 0, 0)
    tok = lambda w: pl.BlockSpec((1, t, w), lambda i, j: (i, j, 0))
    tab = lambda w: pl.BlockSpec((t, w), lambda i, j: (j, 0))
    in_specs = [
        tok(d),
        pl.BlockSpec((1, 6, d), mod_map),
        _const_spec((1, d)),
        _const_spec((d, PROJ_W)),
        _const_spec((8, 512)),
        _const_spec((256, 512)),
        _const_spec((KV_LORA, 512)),
        _const_spec((LANE, 512)),
        _const_spec((KV_LORA, BR_W)),
        _const_spec((512, 512)),
        _const_spec((BR_W, BR_W)),
        _const_spec((BR_W, BR_W)),
        _const_spec((512, 512)),
        _const_spec((BR_W, BR_W)),
        tab(512), tab(512), tab(BR_W), tab(BR_W),
    ]
    widths = [CONV_CH, 512, 512, BR_W, BR_W, BR_W, BR_W, BR_W, BR_W, BR_W]
    dtypes = [F32] + [BF16] * 9
    out_specs = [tok(w) for w in widths]
    out_shape = [jax.ShapeDtypeStruct((b, s, w), dt) for w, dt in zip(widths, dtypes)]
    return pl.pallas_call(
        functools.partial(_proj_kernel, use_rope=use_rope),
        grid=grid, in_specs=in_specs, out_specs=out_specs, out_shape=out_shape,
        compiler_params=_cparams(("parallel", "parallel")),
        name="proj",
    )(x, mods, lw["n1g"], lw["w_in"], lw["gains"], lw["wuq"], lw["wk"], lw["ppe"], lw["wv"],
      lw["g96"], lw["g32"], lw["g64"], lw["rm"], lw["rd"], tabs[0], tabs[1], tabs[2], tabs[3])


CONV_TILE = 128
CONV_PAD = 16


def _conv_kernel(u_ref, w_ref, cb_ref, lg_ref, lb_ref, o_ref, pad_ref, *, seq):
    zeros = jnp.zeros((CONV_PAD, CONV_CH), F32)
    pad_ref[0:CONV_PAD, :] = zeros
    pad_ref[CONV_PAD + seq:2 * CONV_PAD + seq, :] = zeros

    def fill(i, carry):
        base = pl.multiple_of(i * CONV_TILE, CONV_TILE)
        pad_ref[pl.ds(base + CONV_PAD, CONV_TILE), :] = u_ref[0, pl.ds(base, CONV_TILE), :]
        return carry

    lax.fori_loop(0, seq // CONV_TILE, fill, 0)
    w = w_ref[...]
    cb, lg, lb = cb_ref[...], lg_ref[...], lb_ref[...]

    def tile(i, carry):
        base = pl.multiple_of(i * CONV_TILE, CONV_TILE)
        win = pad_ref[pl.ds(base, CONV_TILE + 2 * CONV_PAD), :]
        acc = jnp.zeros((CONV_TILE, CONV_CH), F32)
        for r in range(SUBLANE):
            shifted = win[r:r + CONV_TILE + 2 * CONV_PAD - SUBLANE, :]
            for j in range(CONV_WIDTH):
                if (j + 1) % SUBLANE == r:
                    a = j + 1 - r
                    acc = acc + shifted[a:a + CONV_TILE, :] * w[j:j + 1, :]
        c = acc + cb
        mu = jnp.mean(c, axis=-1, keepdims=True)
        cc = c - mu
        var = jnp.mean(cc * cc, axis=-1, keepdims=True)
        y = cc * lax.rsqrt(var + EPS) * lg + lb
        o_ref[0, pl.ds(base, CONV_TILE), :] = (y * _sigmoid(y)).astype(BF16)
        return carry

    lax.fori_loop(0, seq // CONV_TILE, tile, 0)


def _conv_call(u, lw):
    b, s, ch = u.shape
    return pl.pallas_call(
        functools.partial(_conv_kernel, seq=s),
        grid=(b,),
        in_specs=[
            pl.BlockSpec((1, s, ch), lambda i: (i, 0, 0)),
            _const_spec((32, ch)), _const_spec((1, ch)), _const_spec((1, ch)), _const_spec((1, ch)),
        ],
        out_specs=pl.BlockSpec((1, s, ch), lambda i: (i, 0, 0)),
        out_shape=jax.ShapeDtypeStruct((b, s, ch), BF16),
        scratch_shapes=[pltpu.VMEM((s + 2 * CONV_PAD, ch), F32)],
        compiler_params=_cparams(("parallel",)),
        name="conv",
    )(u, lw["conv_w"], lw["conv_b"], lw["conv_ln_g"], lw["conv_ln_b"])


def _lane_mask(width, lo, hi):
    lane = lax.broadcasted_iota(jnp.int32, (1, width), 1)
    return (lane >= lo) & (lane < hi)


def _softmax_pv(qw, kt, v1):
    s = jnp.dot(qw, kt, preferred_element_type=F32)
    m = jnp.max(s, axis=-1, keepdims=True)
    p = jnp.exp2(s - m).astype(BF16)
    o = jnp.dot(p, v1, preferred_element_type=F32)
    return o * (1.0 / o[:, V_HEAD:V_HEAD + 1])


def _attn_kernel(q_ref, kt_ref, v_ref, lam_ref, g64_ref, sg_ref, o_ref, *, maps, diff, lam_init):
    if diff:
        lv = lam_ref[...]
        lam = (jnp.exp(jnp.sum(lv[0:1] * lv[1:2], axis=-1, keepdims=True))
               - jnp.exp(jnp.sum(lv[2:3] * lv[3:4], axis=-1, keepdims=True)) + lam_init)
    heads = []
    for h in range(N_HEADS):
        outs = []
        for (w0, lo, hi) in maps[h]:
            qw = q_ref[0, :, w0:w0 + LANE]
            if (lo, hi) != (0, LANE):
                qw = jnp.where(_lane_mask(LANE, lo, hi), qw, jnp.zeros_like(qw))
            outs.append(_softmax_pv(qw, kt_ref[0, w0:w0 + LANE, :], v_ref[0, h]))
        heads.append(outs[0] - lam * outs[1] if diff else outs[0])
    low = _lane_mask(LANE, 0, V_HEAD)
    acc = jnp.concatenate(
        [jnp.where(low, heads[h], pltpu.roll(heads[h + 1], V_HEAD, axis=1)) for h in range(0, N_HEADS, 2)],
        axis=1)
    if diff:
        acc = _group_norm(acc, g64_ref, 1.0 / DIFF_V) * sg_ref[...]
    o_ref[0] = acc.astype(BF16)


MAPS_MLA = tuple(((LANE * h, 0, LANE),) for h in range(N_HEADS))
MAPS_DIFF = tuple(tuple((LANE * (h // 2), 64 * (h % 2) + 32 * c, 64 * (h % 2) + 32 * c + 32) for c in range(2))
                  for h in range(N_HEADS))
MAPS_NA = tuple(((LANE * (h // 2), 64 * (h % 2), 64 * (h % 2) + 64),) for h in range(N_HEADS))


def _attn_call(q, kt, v, lw, maps, diff=False, lam_init=0.0):
    b, s, wq = q.shape
    sk = kt.shape[2]
    tq = min(s, ATTN_TQ)
    v = _value_heads(v)
    return pl.pallas_call(
        functools.partial(_attn_kernel, maps=maps, diff=diff, lam_init=lam_init),
        grid=(b, s // tq),
        in_specs=[
            pl.BlockSpec((1, tq, wq), lambda i, j: (i, j, 0)),
            pl.BlockSpec((1, wq, sk), lambda i, j: (i, 0, 0)),
            pl.BlockSpec((1, N_HEADS, sk, LANE), lambda i, j: (i, 0, 0, 0)),
            _const_spec((4, DIFF_DIM)),
            _const_spec((BR_W, BR_W)),
            _const_spec((1, BR_W)),
        ],
        out_specs=pl.BlockSpec((1, tq, BR_W), lambda i, j: (i, j, 0)),
        out_shape=jax.ShapeDtypeStruct((b, s, BR_W), BF16),
        compiler_params=_cparams(("parallel", "parallel")),
        name="attn_diff" if diff else "attn",
    )(q, kt, v, lw["diff_lam"], lw["g64"], lw["subln"])


_NT = (((1,), (1,)), ((), ()))


def _na_kernel(q_ref, k_ref, v_ref, kc_ref, vc_ref, bias_ref, o_ref, *, rows):
    nj = rows // 2
    band = NA_BAND_ROWS * GRID_W
    pair = 2 * GRID_W
    for sub in range(NA_PAIRS):
        j = pl.program_id(1) * NA_PAIRS + sub
        start = jnp.clip(2 * j - 4, 0, rows - NA_BAND_ROWS)
        base = pl.multiple_of(start * GRID_W, 2 * GRID_W)
        cls = jnp.where(j < 2, j, jnp.where(j >= nj - 2, j - (nj - 2) + 3, 2))
        keys = jnp.concatenate([k_ref[0, pl.ds(base, band), :], kc_ref[0]], axis=0)
        q = q_ref[0, sub * pair:(sub + 1) * pair, :]
        heads = []
        for h in range(N_HEADS):
            qm = jnp.where(_lane_mask(BR_W, HEAD_DIM * h, HEAD_DIM * (h + 1)), q, jnp.zeros_like(q))
            s = lax.dot_general(qm, keys, _NT, preferred_element_type=F32) + bias_ref[cls, h]
            p = jnp.exp2(s - jnp.max(s, axis=-1, keepdims=True)).astype(BF16)
            vals = jnp.concatenate([v_ref[0, h, pl.ds(base, band), :], vc_ref[0, h]], axis=0)
            o = jnp.dot(p, vals, preferred_element_type=F32)
            heads.append(o * (1.0 / o[:, V_HEAD:V_HEAD + 1]))
        low = _lane_mask(LANE, 0, V_HEAD)
        acc = jnp.concatenate(
            [jnp.where(low, heads[h], pltpu.roll(heads[h + 1], V_HEAD, axis=1)) for h in range(0, N_HEADS, 2)],
            axis=1)
        o_ref[0, sub * pair:(sub + 1) * pair, :] = acc.astype(BF16)


def _value_heads(v):
    b, n, _ = v.shape
    vh = v.reshape(b, n, N_HEADS, V_HEAD).transpose(0, 2, 1, 3)
    return jnp.concatenate([vh, jnp.ones_like(vh)], axis=-1)


def _na_call(q, k, v, kc, vc, bias):
    b, s, w = q.shape
    n_ctx = kc.shape[1]
    rows = s // GRID_W
    tq = 2 * GRID_W * NA_PAIRS
    v, vc = _value_heads(v), _value_heads(vc)
    return pl.pallas_call(
        functools.partial(_na_kernel, rows=rows),
        grid=(b, rows // (2 * NA_PAIRS)),
        in_specs=[
            pl.BlockSpec((1, tq, w), lambda i, j: (i, j, 0)),
            pl.BlockSpec((1, s, w), lambda i, j: (i, 0, 0)),
            pl.BlockSpec((1, N_HEADS, s, LANE), lambda i, j: (i, 0, 0, 0)),
            pl.BlockSpec((1, n_ctx, w), lambda i, j: (i, 0, 0)),
            pl.BlockSpec((1, N_HEADS, n_ctx, LANE), lambda i, j: (i, 0, 0, 0)),
            _const_spec(bias.shape),
        ],
        out_specs=pl.BlockSpec((1, tq, w), lambda i, j: (i, j, 0)),
        out_shape=jax.ShapeDtypeStruct((b, s, w), BF16),
        compiler_params=_cparams(("parallel", "arbitrary")),
        name="na",
    )(q, k, v, kc, vc, bias)


def _merge_kernel(x_ref, mod_ref, n1g_ref, n2g_ref, uc_ref, om_ref, od_ref, on_ref,
                  gw_ref, gb_ref, wc_ref, wm_ref, wd_ref, wn_ref, wo_ref, rwt_ref, rb_ref,
                  x1_ref, h2_ref, ids_ref, wts_ref):
    x = x_ref[0]
    mod = mod_ref[0]
    h = _modulate(x, n1g_ref[...], mod[0:1], mod[1:2]).astype(BF16)
    y = jnp.zeros(x.shape, F32)
    branches = ((uc_ref, wc_ref), (om_ref, wm_ref), (od_ref, wd_ref), (on_ref, wn_ref))
    for i, (o_ref, w_ref) in enumerate(branches):
        lo = D_MODEL * i
        g = _sigmoid(jnp.dot(h, gw_ref[:, lo:lo + D_MODEL], preferred_element_type=F32)
                     + gb_ref[:, lo:lo + D_MODEL])
        y = y + g * jnp.dot(o_ref[0], w_ref[...], preferred_element_type=F32)
    out = jnp.dot(y.astype(BF16), wo_ref[...], preferred_element_type=F32)
    x1 = x + mod[2:3] * out
    x1_ref[0] = x1
    h2 = _modulate(x1, n2g_ref[...], mod[3:4], mod[4:5])
    for q, piece in enumerate(_pack_row(h2)):
        h2_ref[q, 0] = piece

    logits = lax.dot_general(rwt_ref[...], h2, _NT, preferred_element_type=F32,
                             precision=lax.Precision.HIGHEST) + rb_ref[...]
    eidx = lax.broadcasted_iota(jnp.int32, logits.shape, 0).astype(F32)
    vals, idxs = [], []
    cur = logits
    for _ in range(TOP_K):
        m = jnp.max(cur, axis=0, keepdims=True)
        idx = jnp.min(jnp.where(cur == m, eidx, float(N_EXPERTS)), axis=0, keepdims=True)
        vals.append(m)
        idxs.append(idx)
        cur = jnp.where(eidx == idx, -jnp.inf, cur)
    es = [jnp.exp(vk - vals[0]) for vk in vals]
    den = es[0] + es[1] + es[2] + es[3]
    ids_ref[0] = jnp.concatenate(idxs, axis=0).astype(jnp.int32)
    wts_ref[0] = jnp.concatenate([e / den for e in es], axis=0)


def _merge_call(x, mods, mod_row, lw, uc, om, od, on):
    b, s, d = x.shape
    t = min(s, 512)
    if mod_row is None:
        mod_map = lambda i, j: (i, 0, 0)
    else:
        mod_map = lambda i, j: (mod_row, 0, 0)
    tok = lambda w: pl.BlockSpec((1, t, w), lambda i, j: (i, j, 0))
    rt = pl.BlockSpec((1, TOP_K, t), lambda i, j: (i, 0, j))
    return pl.pallas_call(
        _merge_kernel,
        grid=(b, s // t),
        in_specs=[
            tok(d), pl.BlockSpec((1, 6, d), mod_map), _const_spec((1, d)), _const_spec((1, d)),
            tok(BR_W), tok(BR_W), tok(BR_W), tok(BR_W),
            _const_spec((d, N_BRANCH * d)), _const_spec((1, N_BRANCH * d)),
            _const_spec((BR_W, d)), _const_spec((BR_W, d)), _const_spec((BR_W, d)), _const_spec((BR_W, d)),
            _const_spec((d, d)), _const_spec((N_EXPERTS, d)), _const_spec((N_EXPERTS, 1)),
        ],
        out_specs=[tok(d), pl.BlockSpec((ROW_PARTS, 1, t, ROW_Q), lambda i, j: (0, i, j, 0)), rt, rt],
        out_shape=[jax.ShapeDtypeStruct((b, s, d), F32), jax.ShapeDtypeStruct((ROW_PARTS, b, s, ROW_Q), jnp.int32),
                   jax.ShapeDtypeStruct((b, TOP_K, s), jnp.int32), jax.ShapeDtypeStruct((b, TOP_K, s), F32)],
        compiler_params=_cparams(("parallel", "parallel")),
        name="merge",
    )(x, mods, lw["n1g"], lw["n2g"], uc, om, od, on, lw["gate_w"], lw["gate_b"],
      lw["conv_out"], lw["mla_out"], lw["diff_out"], lw["na_out"], lw["w_o"], lw["router_wt"], lw["router_b"])


SC_WINDOW = 128
ROW_Q = D_MODEL // 4
ROW_PARTS = 2
HALF_D = D_MODEL // 2


def _pack_bf16_pair(a, b):
    ua = lax.bitcast_convert_type(a.astype(BF16).astype(F32), jnp.int32)
    ub = lax.bitcast_convert_type(b.astype(BF16).astype(F32), jnp.int32)
    return ua | lax.shift_right_logical(ub, jnp.int32(16))


def _unpack_bf16_pair(w):
    a = lax.bitcast_convert_type(w & jnp.int32(-65536), F32)
    b = lax.bitcast_convert_type(lax.shift_left(w, jnp.int32(16)), F32)
    return a, b


def _pack_row(x):
    w = _pack_bf16_pair(x[:, :HALF_D], x[:, HALF_D:])
    return [w[:, ROW_Q * q:ROW_Q * (q + 1)] for q in range(ROW_PARTS)]


def _unpack_row(pieces):
    ab = [_unpack_bf16_pair(w) for w in pieces]
    return jnp.concatenate([a for a, _ in ab] + [b for _, b in ab], axis=1)


def _route_slots(ids, tile):
    n = ids.shape[1]
    p = TOP_K * n
    e = ids.reshape(p)
    onehot = (e[:, None] == jnp.arange(N_EXPERTS, dtype=jnp.int32)[None, :])
    chunk = 512
    oh3 = onehot.astype(F32).reshape(p // chunk, chunk, N_EXPERTS)
    within = jnp.einsum("ij,cje->cie", jnp.tril(jnp.ones((chunk, chunk), F32)), oh3)
    totals = within[:, -1, :]
    before = jnp.cumsum(totals, axis=0) - totals
    csum = (within + before[:, None, :]).reshape(p, N_EXPERTS).astype(jnp.int32)
    onehot = onehot.astype(jnp.int32)
    counts = csum[-1]
    padded = ((counts + tile - 1) // tile) * tile
    gend = jnp.cumsum(padded)
    gstart = gend - padded
    slot = jnp.sum(onehot * (csum - 1 + gstart[None, :]), axis=1).astype(jnp.int32)
    n_tiles = p // tile + N_EXPERTS
    tile_start = jnp.arange(n_tiles, dtype=jnp.int32) * tile
    texp = jnp.sum((tile_start[:, None] >= gend[None, :]).astype(jnp.int32), axis=1)
    texp = jnp.minimum(texp, N_EXPERTS - 1)
    nreal = jnp.clip(gstart[texp] + counts[texp] - tile_start, 0, tile)
    nreal = jnp.where(tile_start < gend[-1], nreal, 0).astype(jnp.int32)
    return slot, texp, nreal, n_tiles


def _sc_mesh():
    return plsc.VectorSubcoreMesh(core_axis_name="c", subcore_axis_name="s")


def _sc_dispatch(hq, idx, n_slots):
    parts, n, w = hq.shape
    src = hq.reshape(parts * n, w)
    m = idx.shape[0]
    blocks_per_q = n // SC_WINDOW
    per_q = TOP_K * blocks_per_q

    @pl.kernel(out_type=jax.ShapeDtypeStruct((parts * n_slots, w), hq.dtype), mesh=_sc_mesh(), scratch_types=[])
    def kern(x_hbm, i_hbm, o_hbm):
        def body(x_vmem, i_vmem):
            pltpu.sync_copy(x_vmem, o_hbm.at[i_vmem.at[0]])

        pltpu.emit_pipeline(
            body,
            grid=(m // SC_WINDOW,),
            in_specs=[
                pl.BlockSpec((SC_WINDOW, w), index_map=lambda i: ((i // per_q) * blocks_per_q + i % blocks_per_q, 0)),
                pl.BlockSpec((1, SC_WINDOW), index_map=lambda i: (0, i)),
            ],
            out_specs=[],
            core_axis_name=("c", "s"),
            dimension_semantics=(pltpu.PARALLEL,),
        )(x_hbm, i_hbm)

    return kern(src, idx.reshape(1, m)).reshape(parts, n_slots, w)


def _sc_collect(ys, idx):
    parts, n_slots, w = ys.shape
    src = ys.reshape(parts * n_slots, w)
    m = idx.shape[0]

    @pl.kernel(out_type=jax.ShapeDtypeStruct((m, w), ys.dtype), mesh=_sc_mesh(), scratch_types=[])
    def kern(x_hbm, i_hbm, o_hbm):
        def body(i_vmem, o_vmem):
            pltpu.sync_copy(x_hbm.at[i_vmem.at[0]], o_vmem)

        pltpu.emit_pipeline(
            body,
            grid=(m // SC_WINDOW,),
            in_specs=[pl.BlockSpec((1, SC_WINDOW), index_map=lambda i: (0, i))],
            out_specs=[pl.BlockSpec((SC_WINDOW, w), index_map=lambda i: (i, 0))],
            core_axis_name=("c", "s"),
            dimension_semantics=(pltpu.PARALLEL,),
        )(i_hbm, o_hbm)

    return kern(src, idx.reshape(1, m))


def _ffn_sorted_kernel(texp_ref, nreal_ref, x_ref, wgu_ref, bgu_ref, wd_ref, bd_ref, y_ref, wgu_bf, wd_bf, *, tile):
    i = pl.program_id(0)
    nreal = nreal_ref[i]

    @pl.when((nreal > 0) & ((i == 0) | (texp_ref[i] != texp_ref[jnp.maximum(i - 1, 0)])))
    def _():
        wgu_bf[...] = wgu_ref[0].astype(BF16)
        wd_bf[...] = wd_ref[0].astype(BF16)

    @pl.when(nreal > 0)
    def _():
        x = _unpack_row([x_ref[q] for q in range(ROW_PARTS)])
        row = lax.broadcasted_iota(jnp.int32, (tile, 1), 0)
        x = jnp.where(row < nreal, x, 0.0).astype(BF16)
        gu = jnp.dot(x, wgu_bf[...], preferred_element_type=F32) + bgu_ref[0]
        g = jnp.minimum(gu[:, :D_FF], SWIGLU_LIMIT)
        u = jnp.clip(gu[:, D_FF:], -SWIGLU_LIMIT, SWIGLU_LIMIT)
        act = ((u + 1.0) * (g * _sigmoid(SWIGLU_ALPHA * g))).astype(BF16)
        y = jnp.dot(act, wd_bf[...], preferred_element_type=F32) + bd_ref[0]
        for q, piece in enumerate(_pack_row(y)):
            y_ref[q] = piece

    @pl.when(nreal == 0)
    def _():
        y_ref[...] = jnp.zeros(y_ref.shape, jnp.int32)


def _ffn_sorted_call(xs, texp, nreal, lw, tile):
    _, n_slots, w = xs.shape
    n_tiles = n_slots // tile
    d = D_MODEL
    off = lw["exp_off"]
    grid_spec = pltpu.PrefetchScalarGridSpec(
        num_scalar_prefetch=2,
        grid=(n_tiles,),
        in_specs=[
            pl.BlockSpec((ROW_PARTS, tile, w), lambda i, te, nr: (0, i, 0)),
            pl.BlockSpec((1, d, 2 * D_FF), lambda i, te, nr: (te[i] + off, 0, 0)),
            pl.BlockSpec((1, 1, 2 * D_FF), lambda i, te, nr: (te[i] + off, 0, 0)),
            pl.BlockSpec((1, D_FF, d), lambda i, te, nr: (te[i] + off, 0, 0)),
            pl.BlockSpec((1, 1, d), lambda i, te, nr: (te[i] + off, 0, 0)),
        ],
        out_specs=pl.BlockSpec((ROW_PARTS, tile, w), lambda i, te, nr: (0, i, 0)),
        scratch_shapes=[pltpu.VMEM((d, 2 * D_FF), BF16), pltpu.VMEM((D_FF, d), BF16)],
    )
    return pl.pallas_call(
        functools.partial(_ffn_sorted_kernel, tile=tile),
        grid_spec=grid_spec,
        out_shape=jax.ShapeDtypeStruct((ROW_PARTS, n_slots, w), jnp.int32),
        compiler_params=_cparams(("arbitrary",)),
        name="moe_ffn",
    )(texp, nreal, xs, lw["exp_w_gu"], lw["exp_b_gu"], lw["exp_w_down"], lw["exp_b_down"])


def _combine_q_kernel(x1_ref, mod_ref, w_ref, y_ref, o_ref):
    w = w_ref[0]
    g2 = mod_ref[0][5:6]
    for q in range(ROW_PARTS):
        acc_a, acc_b = None, None
        for k in range(TOP_K):
            a, b = _unpack_bf16_pair(y_ref[q, k])
            wk = w[:, k:k + 1]
            acc_a = wk * a if acc_a is None else acc_a + wk * a
            acc_b = wk * b if acc_b is None else acc_b + wk * b
        for lo, acc in ((ROW_Q * q, acc_a), (HALF_D + ROW_Q * q, acc_b)):
            o_ref[0, :, lo:lo + ROW_Q] = x1_ref[0, :, lo:lo + ROW_Q] + g2[:, lo:lo + ROW_Q] * acc


def _combine_q_call(x1, mods, mod_row, wts, y, tok_off):
    b, s, d = x1.shape
    t = min(s, 512)
    nt = s // t
    blk_off = tok_off // t
    if mod_row is None:
        mod_map = lambda i, j: (i, 0, 0)
    else:
        mod_map = lambda i, j: (mod_row, 0, 0)
    wts = wts.transpose(0, 2, 1)
    return pl.pallas_call(
        _combine_q_kernel,
        grid=(b, nt),
        in_specs=[
            pl.BlockSpec((1, t, d), lambda i, j: (i, j, 0)),
            pl.BlockSpec((1, 6, d), mod_map),
            pl.BlockSpec((1, t, TOP_K), lambda i, j: (i, j, 0)),
            pl.BlockSpec((ROW_PARTS, TOP_K, t, ROW_Q), lambda i, j: (0, 0, blk_off + i * nt + j, 0)),
        ],
        out_specs=pl.BlockSpec((1, t, d), lambda i, j: (i, j, 0)),
        out_shape=jax.ShapeDtypeStruct((b, s, d), F32),
        compiler_params=_cparams(("parallel", "parallel")),
        name="moe_combine",
    )(x1, mods, wts, y)


def _moe_sc(streams, mods, lw):
    sizes = [st[0].shape[0] * st[0].shape[1] for st in streams]
    n = sum(sizes)
    tile = 512 if TOP_K * n >= 512 * N_EXPERTS * 4 else 256
    hq = jnp.concatenate([st[1].reshape(ROW_PARTS, m, ROW_Q) for st, m in zip(streams, sizes)], axis=1)
    ids = jnp.concatenate([st[2].transpose(1, 0, 2).reshape(TOP_K, m) for st, m in zip(streams, sizes)], axis=1)
    slot, texp, nreal, n_tiles = _route_slots(ids, tile)
    n_slots = n_tiles * tile
    idx = (slot[None, :] + (jnp.arange(ROW_PARTS, dtype=jnp.int32) * n_slots)[:, None]).reshape(-1)
    xs = _sc_dispatch(hq, idx, n_slots)
    ys = _ffn_sorted_call(xs, texp, nreal, lw, tile)
    y = _sc_collect(ys, idx).reshape(ROW_PARTS, TOP_K, n, ROW_Q)
    outs, off = [], 0
    for (x1, _, _, wts, mod_row), m in zip(streams, sizes):
        outs.append(_combine_q_call(x1, mods, mod_row, wts, y, off))
        off += m
    return outs


def _layer_weights(l, p, lam_init):
    w = p["w_in"][l]
    d = w.shape[0]
    zcols = lambda n: jnp.zeros((d, n), w.dtype)
    regroup = lambda blk: blk.reshape(d, N_HEADS, 3, HEAD_DIM).transpose(0, 2, 1, 3).reshape(d, 3 * BR_W)
    w_in = jnp.concatenate([
        w[:, :A_IN],
        w[:, OFF_B:OFF_B + Q_LORA], zcols(P_CKV - P_CQ - Q_LORA),
        w[:, OFF_B + Q_LORA:OFF_B + Q_LORA + KV_LORA],
        w[:, OFF_B + Q_LORA + KV_LORA:OFF_C], zcols(P_DQ - P_KPE - QK_ROPE),
        regroup(w[:, OFF_C:OFF_D]), regroup(w[:, OFF_D:]),
    ], axis=1)
    assert w_in.shape[1] == PROJ_W

    def head_slots(w3, slot):
        w3 = jnp.pad(w3, ((0, 0), (0, 0), (0, slot - w3.shape[2])))
        return w3.reshape(w3.shape[0], N_HEADS * slot)

    wuq = head_slots(p["mla_w_uq"][l].reshape(Q_LORA, N_HEADS, MLA_QK), LANE)
    wuq = jnp.pad(wuq, ((0, 256 - Q_LORA), (0, 0)))
    wukv = p["mla_w_ukv"][l].reshape(KV_LORA, N_HEADS, QK_NOPE + V_HEAD)
    wk = head_slots(wukv[:, :, :QK_NOPE], LANE)
    wv = head_slots(wukv[:, :, QK_NOPE:], V_HEAD)
    ppe = np.zeros((LANE, 512), np.float32)
    for h in range(N_HEADS):
        for i in range(QK_ROPE):
            ppe[i, h * LANE + QK_NOPE + i] = 1.0

    def slot_gain(g, scale):
        g = jnp.concatenate([g * scale, jnp.zeros((LANE - MLA_QK,), F32)])
        return jnp.tile(g, N_HEADS)

    def row512(v):
        return jnp.concatenate([v, jnp.zeros((512 - v.shape[0],), F32)])

    gains = jnp.stack([
        row512(p["mla_cq_g"][l]),
        slot_gain(p["mla_qn_g"][l], MLA_QK ** -0.5 * LOG2E),
        row512(p["mla_ckv_g"][l]),
        slot_gain(p["mla_kn_g"][l], 1.0),
        row512(jnp.tile(p["diff_qn_g"][l], 2 * N_HEADS) * DIFF_DIM ** -0.5 * LOG2E),
        row512(jnp.tile(p["diff_kn_g"][l], 2 * N_HEADS)),
        row512(jnp.tile(p["na_qn_g"][l], N_HEADS) * HEAD_DIM ** -0.5 * LOG2E),
        row512(jnp.tile(p["na_kn_g"][l], N_HEADS)),
    ])
    conv_w = jnp.concatenate([p["conv_w"][l], jnp.zeros((1, CONV_CH), F32)], axis=0)
    return dict(
        n1g=p["norm1_g"][l][None, :], n2g=p["norm2_g"][l][None, :],
        w_in=w_in.astype(BF16), gains=gains,
        wuq=wuq.astype(BF16), wk=wk.astype(BF16), wv=wv.astype(BF16), ppe=jnp.asarray(ppe, BF16),
        g96=jnp.asarray(_group_ones(512, LANE, MLA_QK), BF16),
        g32=jnp.asarray(_group_ones(BR_W, DIFF_DIM, DIFF_DIM), BF16),
        g64=jnp.asarray(_group_ones(BR_W, HEAD_DIM, HEAD_DIM), BF16),
        rm=jnp.asarray(_rot_matrix(512, LANE, QK_NOPE, QK_ROPE // 2), BF16),
        rd=jnp.asarray(_rot_matrix(BR_W, DIFF_DIM, 0, DIFF_DIM // 2), BF16),
        conv_w=conv_w, conv_b=p["conv_b"][l][None, :],
        conv_ln_g=p["conv_ln_g"][l][None, :], conv_ln_b=p["conv_ln_b"][l][None, :],
        diff_lam=p["diff_lam"][l],
        subln=(jnp.tile(p["diff_subln_g"][l], N_HEADS) * (1.0 - lam_init))[None, :],
        gate_w=p["gate_w"][l].astype(BF16), gate_b=p["gate_b"][l][None, :],
        conv_out=p["conv_out"][l].astype(BF16), mla_out=p["mla_out"][l].astype(BF16),
        diff_out=p["diff_out"][l].astype(BF16), na_out=p["na_out"][l].astype(BF16),
        w_o=p["w_o"][l].astype(BF16),
        router_wt=p["router_w"][l].T, router_b=p["router_b"][l][:, None],
        exp_off=l * N_EXPERTS,
        exp_w_gu=p["exp_w_gu"].reshape((-1,) + p["exp_w_gu"].shape[2:]),
        exp_b_gu=p["exp_b_gu"].reshape(-1, 1, 2 * D_FF),
        exp_w_down=p["exp_w_down"].reshape((-1,) + p["exp_w_down"].shape[2:]),
        exp_b_down=p["exp_b_down"].reshape(-1, 1, D_MODEL),
    )


def _kt(kc, k):
    return jnp.concatenate([kc, k], axis=1).transpose(0, 2, 1)


def kernel(x, c, ctx, c_ctx, ada_w, ada_b, norm1_g, norm2_g, w_in, conv_w, conv_b, conv_ln_g, conv_ln_b, conv_out, mla_cq_g, mla_ckv_g, mla_w_uq, mla_w_ukv, mla_qn_g, mla_kn_g, mla_out, diff_qn_g, diff_kn_g, diff_lam, diff_subln_g, diff_out, na_qn_g, na_kn_g, na_rpb, na_out, gate_w, gate_b, w_o, router_w, router_b, exp_w_gu, exp_b_gu, exp_w_down, exp_b_down):
    p = dict(norm1_g=norm1_g, norm2_g=norm2_g, w_in=w_in, conv_w=conv_w, conv_b=conv_b,
             conv_ln_g=conv_ln_g, conv_ln_b=conv_ln_b, conv_out=conv_out, mla_cq_g=mla_cq_g,
             mla_ckv_g=mla_ckv_g, mla_w_uq=mla_w_uq, mla_w_ukv=mla_w_ukv, mla_qn_g=mla_qn_g,
             mla_kn_g=mla_kn_g, mla_out=mla_out, diff_qn_g=diff_qn_g, diff_kn_g=diff_kn_g,
             diff_lam=diff_lam, diff_subln_g=diff_subln_g, diff_out=diff_out, na_qn_g=na_qn_g,
             na_kn_g=na_kn_g, na_out=na_out, gate_w=gate_w, gate_b=gate_b, w_o=w_o,
             router_w=router_w, router_b=router_b, exp_w_gu=exp_w_gu, exp_b_gu=exp_b_gu,
             exp_w_down=exp_w_down, exp_b_down=exp_b_down)
    b, s, d = x.shape
    n_ctx = ctx.shape[1]
    depth = ada_w.shape[0]
    rows = s // GRID_W
    assert d == D_MODEL and s % (2 * GRID_W) == 0 and rows >= NA_BAND_ROWS and n_ctx % LANE == 0

    mod_rows = -(-(b + 1) // 8) * 8
    cs = jnp.concatenate([c, c_ctx[None, :], jnp.zeros((mod_rows - b - 1, d), F32)], axis=0)
    mods_all = _ada_call(cs, ada_w, ada_b).reshape(depth, mod_rows, 6, d)

    tabs_x = _rope_lane_tables(s)
    tabs_c = (jnp.ones((n_ctx, 512), F32), jnp.zeros((n_ctx, 512), F32),
              jnp.ones((n_ctx, BR_W), F32), jnp.zeros((n_ctx, BR_W), F32))

    xc = ctx
    for l in range(depth):
        last = l == depth - 1
        lam_init = 0.8 - 0.6 * math.exp(-0.3 * l)
        lw = _layer_weights(l, p, lam_init)
        mods = mods_all[l]
        bias = _na_bias_tables(na_rpb[l], rows, n_ctx)

        u, mq, mk, mv, dq, dk, dv, nq, nk, nv = _proj_call(x, mods, None, lw, tabs_x, True)
        uc, mqc, mkc, mvc, dqc, dkc, dvc, nqc, nkc, nvc = _proj_call(xc, mods, b, lw, tabs_c, False)

        y_conv = _conv_call(u, lw)
        y_mla = _attn_call(mq, _kt(mkc, mk), jnp.concatenate([mvc, mv], axis=1), lw, MAPS_MLA)
        y_diff = _attn_call(dq, _kt(dkc, dk), jnp.concatenate([dvc, dv], axis=1), lw, MAPS_DIFF,
                            diff=True, lam_init=lam_init)
        y_na = _na_call(nq, nk, nv, nkc, nvc, bias)
        x1, h2, ids, wts = _merge_call(x, mods, None, lw, y_conv, y_mla, y_diff, y_na)
        streams = [(x1, h2, ids, wts, None)]

        if not last:
            yc_conv = _conv_call(uc, lw)
            yc_mla = _attn_call(mqc, mkc.transpose(0, 2, 1), mvc, lw, MAPS_MLA)
            yc_diff = _attn_call(dqc, dkc.transpose(0, 2, 1), dvc, lw, MAPS_DIFF, diff=True, lam_init=lam_init)
            yc_na = _attn_call(nqc, nkc.transpose(0, 2, 1), nvc, lw, MAPS_NA)
            xc1, h2c, idsc, wtsc = _merge_call(xc, mods, b, lw, yc_conv, yc_mla, yc_diff, yc_na)
            streams.append((xc1, h2c, idsc, wtsc, b))

        outs = _moe_sc(streams, mods, lw)
        x = outs[0]
        if not last:
            xc = outs[1]
    return x
```

```python
import functools
import math

import numpy as np
import jax
import jax.numpy as jnp
from jax import lax
from jax.experimental import pallas as pl
from jax.experimental.pallas import tpu as pltpu
from jax.experimental.pallas import tpu_sc as plsc

F32 = jnp.float32
BF16 = jnp.bfloat16

D_MODEL = 1024
GRID_W = 64
N_BRANCH = 4
N_HEADS = 4
HEAD_DIM = 64
CONV_CH = 256
CONV_WIDTH = 31
Q_LORA = 192
KV_LORA = 128
QK_NOPE = 64
QK_ROPE = 32
V_HEAD = 64
DIFF_DIM = 32
DIFF_V = 2 * DIFF_DIM
NA_KH = 8
NA_KW = 16
ROPE_DIM = 32
ROPE_BASE = 10000.0
N_EXPERTS = 32
TOP_K = 4
D_FF = 1024
SWIGLU_LIMIT = 7.0
SWIGLU_ALPHA = 1.702
EPS = 1e-6
NEG_INF = -1e30

A_IN = 2 * CONV_CH
B_IN = Q_LORA + KV_LORA + QK_ROPE
C_IN = N_HEADS * (4 * DIFF_DIM + DIFF_V)
D_IN = N_HEADS * 3 * HEAD_DIM
OFF_B = A_IN
OFF_C = OFF_B + B_IN
OFF_D = OFF_C + C_IN

LANE = 128
SUBLANE = 8
MLA_QK = QK_NOPE + QK_ROPE
BR_W = N_HEADS * HEAD_DIM
PROJ_W = 2560
NA_BAND_ROWS = 10
ATTN_TQ = 512
NA_PAIRS = 4
LOG2E = math.log2(math.e)
VMEM_LIMIT = 52 * 1024 * 1024

P_A, P_G, P_CQ, P_CKV, P_KPE = 0, 256, 512, 768, 896
P_DQ, P_DK, P_DV = 1024, 1280, 1536
P_NQ, P_NK, P_NV = 1792, 2048, 2304


def _sigmoid(x):
    return 1.0 / (1.0 + jnp.exp(-x))


def _modulate(x, g, shift, scale):
    ms = jnp.mean(x * x, axis=-1, keepdims=True)
    return (x * lax.rsqrt(ms + EPS) * g) * (1.0 + scale) + shift


def _cparams(sem):
    return pltpu.CompilerParams(dimension_semantics=sem, vmem_limit_bytes=VMEM_LIMIT)


def _const_spec(shape):
    n = len(shape)
    return pl.BlockSpec(shape, lambda *_: (0,) * n)


def _group_ones(width, slot, real):
    i = np.arange(width)
    valid = (i % slot) < real
    same = (i[:, None] // slot) == (i[None, :] // slot)
    return (same & valid[:, None] & valid[None, :]).astype(np.float32)


def _rot_matrix(width, slot, start, half):
    r = np.zeros((width, width), np.float32)
    for s0 in range(0, width, slot):
        for i in range(half):
            a, b = s0 + start + i, s0 + start + half + i
            r[b, a] = -1.0
            r[a, b] = 1.0
    return r


def _rope_lane_tables(n_tokens):
    t = jnp.arange(n_tokens, dtype=jnp.int32)
    rows = (t // GRID_W).astype(F32)
    cols = (t % GRID_W).astype(F32)
    axis_dim = ROPE_DIM // 2
    inv = ROPE_BASE ** (-jnp.arange(0, axis_dim, 2, dtype=F32) / axis_dim)
    theta = jnp.concatenate([rows[:, None] * inv, cols[:, None] * inv], axis=-1)
    cos, sin = jnp.cos(theta), jnp.sin(theta)
    half = ROPE_DIM // 2
    ones = jnp.ones((n_tokens, QK_NOPE), F32)
    zeros = jnp.zeros((n_tokens, QK_NOPE), F32)
    pad1 = jnp.ones((n_tokens, LANE - MLA_QK), F32)
    pad0 = jnp.zeros((n_tokens, LANE - MLA_QK), F32)
    cm = jnp.tile(jnp.concatenate([ones, cos, cos, pad1], -1), (1, N_HEADS))
    sm = jnp.tile(jnp.concatenate([zeros, sin, sin, pad0], -1), (1, N_HEADS))
    cd = jnp.tile(jnp.concatenate([cos, cos], -1), (1, 2 * N_HEADS))
    sd = jnp.tile(jnp.concatenate([sin, sin], -1), (1, 2 * N_HEADS))
    assert half * 2 == DIFF_DIM
    return cm, sm, cd, sd


def _na_bias_tables(rpb, rows, n_ctx):
    kh = min(NA_KH, rows)
    nj = rows // 2
    reps = np.array([0, 1, 2, nj - 2, nj - 1])
    n_ro, n_co = 2 * NA_KH - 1, 2 * NA_KW - 1
    start = np.clip(2 * reps - 4, 0, rows - NA_BAND_ROWS)
    r = 2 * reps[:, None] + np.arange(2)[None, :]
    kr = start[:, None] + np.arange(NA_BAND_ROWS)[None, :]
    row_start = np.clip(r - kh // 2, 0, rows - kh)
    vr = (kr[:, None, :] >= row_start[:, :, None]) & (kr[:, None, :] < row_start[:, :, None] + kh)
    ro = np.clip(kr[:, None, :] - r[:, :, None] + NA_KH - 1, 0, n_ro - 1)
    qc = np.arange(GRID_W)
    win_start = np.clip(qc - NA_KW // 2, 0, GRID_W - NA_KW)
    vc = (qc[None, :] >= win_start[:, None]) & (qc[None, :] < win_start[:, None] + NA_KW)
    co = np.clip(qc[None, :] - qc[:, None] + NA_KW - 1, 0, n_co - 1)
    rsel = (ro[..., None] == np.arange(n_ro)).astype(np.float32)
    csel = (co[None] == np.arange(n_co)[:, None, None]).astype(np.float32)
    hi = lax.Precision.HIGHEST
    t1 = jnp.einsum("cqav,hvw->hcqaw", rsel, rpb.astype(F32), precision=hi)
    b = jnp.einsum("hcqaw,wxy->chqxay", t1, csel, precision=hi)
    valid = vr[:, None, :, None, :, None] & vc[None, None, None, :, None, :]
    b = jnp.where(valid, b * LOG2E, NEG_INF)
    b = b.reshape(len(reps), N_HEADS, 2 * GRID_W, NA_BAND_ROWS * GRID_W)
    return jnp.concatenate([b, jnp.zeros(b.shape[:3] + (n_ctx,), F32)], axis=-1)


def _ada_kernel(c_ref, w_ref, b_ref, o_ref):
    c = c_ref[...]
    s = c * _sigmoid(c)
    o_ref[0] = jnp.dot(s, w_ref[0], preferred_element_type=F32,
                       precision=lax.Precision.HIGHEST) + b_ref[0]


def _ada_call(cs, ada_w, ada_b):
    depth, d, n = ada_w.shape
    rows = cs.shape[0]
    tn = 1536
    return pl.pallas_call(
        _ada_kernel,
        grid=(depth, n // tn),
        in_specs=[
            pl.BlockSpec((rows, d), lambda l, j: (0, 0)),
            pl.BlockSpec((1, d, tn), lambda l, j: (l, 0, j)),
            pl.BlockSpec((1, 1, tn), lambda l, j: (l, 0, j)),
        ],
        out_specs=pl.BlockSpec((1, rows, tn), lambda l, j: (l, 0, j)),
        out_shape=jax.ShapeDtypeStruct((depth, rows, n), F32),
        compiler_params=_cparams(("arbitrary", "arbitrary")),
        name="ada",
    )(cs, ada_w, ada_b.reshape(depth, 1, n))


def _group_norm(x, ones_ref, inv_n):
    sq = (x * x).astype(BF16)
    ms = jnp.dot(sq, ones_ref[...], preferred_element_type=F32) * inv_n
    return x * lax.rsqrt(ms + EPS)


def _rope(x, rot_ref, cos_ref, sin_ref):
    rot = jnp.dot(x.astype(BF16), rot_ref[...], preferred_element_type=F32)
    return x * cos_ref[...] + rot * sin_ref[...]


def _proj_kernel(x_ref, mod_ref, n1g_ref, win_ref, gains_ref, wuq_ref, wk_ref, ppe_ref, wv_ref,
                 g96_ref, g32_ref, g64_ref, rm_ref, rd_ref, cm_ref, sm_ref, cd_ref, sd_ref,
                 u_ref, mq_ref, mk_ref, mv_ref, dq_ref, dk_ref, dv_ref, nq_ref, nk_ref, nv_ref,
                 *, use_rope):
    x = x_ref[0]
    mod = mod_ref[0]
    gains = gains_ref[...]
    h = _modulate(x, n1g_ref[...], mod[0:1], mod[1:2]).astype(BF16)
    proj = jnp.dot(h, win_ref[...], preferred_element_type=F32)

    u_ref[0] = proj[:, P_A:P_A + CONV_CH] * _sigmoid(proj[:, P_G:P_G + CONV_CH])

    cq = proj[:, P_CQ:P_CQ + 256]
    ms = jnp.sum(cq * cq, axis=-1, keepdims=True) * (1.0 / Q_LORA)
    cqn = (cq * lax.rsqrt(ms + EPS) * gains[0:1, :256]).astype(BF16)
    q = jnp.dot(cqn, wuq_ref[...], preferred_element_type=F32)
    q = _group_norm(q, g96_ref, 1.0 / MLA_QK) * gains[1:2, :]
    if use_rope:
        q = _rope(q, rm_ref, cm_ref, sm_ref)
    mq_ref[0] = q.astype(BF16)

    ckv = proj[:, P_CKV:P_CKV + KV_LORA]
    ms = jnp.mean(ckv * ckv, axis=-1, keepdims=True)
    ckvn = (ckv * lax.rsqrt(ms + EPS) * gains[2:3, :KV_LORA]).astype(BF16)
    kpe = proj[:, P_KPE:P_KPE + LANE].astype(BF16)
    k = (jnp.dot(ckvn, wk_ref[...], preferred_element_type=F32)
         + jnp.dot(kpe, ppe_ref[...], preferred_element_type=F32))
    k = _group_norm(k, g96_ref, 1.0 / MLA_QK) * gains[3:4, :]
    if use_rope:
        k = _rope(k, rm_ref, cm_ref, sm_ref)
    mk_ref[0] = k.astype(BF16)
    mv_ref[0] = jnp.dot(ckvn, wv_ref[...], preferred_element_type=F32).astype(BF16)

    qd = _group_norm(proj[:, P_DQ:P_DQ + BR_W], g32_ref, 1.0 / DIFF_DIM) * gains[4:5, :BR_W]
    kd = _group_norm(proj[:, P_DK:P_DK + BR_W], g32_ref, 1.0 / DIFF_DIM) * gains[5:6, :BR_W]
    if use_rope:
        qd = _rope(qd, rd_ref, cd_ref, sd_ref)
        kd = _rope(kd, rd_ref, cd_ref, sd_ref)
    dq_ref[0] = qd.astype(BF16)
    dk_ref[0] = kd.astype(BF16)
    dv_ref[0] = proj[:, P_DV:P_DV + BR_W].astype(BF16)

    qn = _group_norm(proj[:, P_NQ:P_NQ + BR_W], g64_ref, 1.0 / HEAD_DIM) * gains[6:7, :BR_W]
    kn = _group_norm(proj[:, P_NK:P_NK + BR_W], g64_ref, 1.0 / HEAD_DIM) * gains[7:8, :BR_W]
    nq_ref[0] = qn.astype(BF16)
    nk_ref[0] = kn.astype(BF16)
    nv_ref[0] = proj[:, P_NV:P_NV + BR_W].astype(BF16)


def _proj_call(x, mods, mod_row, lw, tabs, use_rope):
    b, s, d = x.shape
    t = min(s, 512)
    grid = (b, s // t)
    if mod_row is None:
        mod_map = lambda i, j: (i, 0, 0)
    else:
        mod_map = lambda i, j: (mod_row, 0, 0)
    tok = lambda w: pl.BlockSpec((1, t, w), lambda i, j: (i, j, 0))
    tab = lambda w: pl.BlockSpec((t, w), lambda i, j: (j, 0))
    in_specs = [
        tok(d),
        pl.BlockSpec((1, 6, d), mod_map),
        _const_spec((1, d)),
        _const_spec((d, PROJ_W)),
        _const_spec((8, 512)),
        _const_spec((256, 512)),
        _const_spec((KV_LORA, 512)),
        _const_spec((LANE, 512)),
        _const_spec((KV_LORA, BR_W)),
        _const_spec((512, 512)),
        _const_spec((BR_W, BR_W)),
        _const_spec((BR_W, BR_W)),
        _const_spec((512, 512)),
        _const_spec((BR_W, BR_W)),
        tab(512), tab(512), tab(BR_W), tab(BR_W),
    ]
    widths = [CONV_CH, 512, 512, BR_W, BR_W, BR_W, BR_W, BR_W, BR_W, BR_W]
    dtypes = [F32] + [BF16] * 9
    out_specs = [tok(w) for w in widths]
    out_shape = [jax.ShapeDtypeStruct((b, s, w), dt) for w, dt in zip(widths, dtypes)]
    return pl.pallas_call(
        functools.partial(_proj_kernel, use_rope=use_rope),
        grid=grid, in_specs=in_specs, out_specs=out_specs, out_shape=out_shape,
        compiler_params=_cparams(("parallel", "parallel")),
        name="proj",
    )(x, mods, lw["n1g"], lw["w_in"], lw["gains"], lw["wuq"], lw["wk"], lw["ppe"], lw["wv"],
      lw["g96"], lw["g32"], lw["g64"], lw["rm"], lw["rd"], tabs[0], tabs[1], tabs[2], tabs[3])


CONV_TILE = 128
CONV_PAD = 16


def _conv_kernel(u_ref, w_ref, cb_ref, lg_ref, lb_ref, o_ref, pad_ref, *, seq):
    zeros = jnp.zeros((CONV_PAD, CONV_CH), F32)
    pad_ref[0:CONV_PAD, :] = zeros
    pad_ref[CONV_PAD + seq:2 * CONV_PAD + seq, :] = zeros

    def fill(i, carry):
        base = pl.multiple_of(i * CONV_TILE, CONV_TILE)
        pad_ref[pl.ds(base + CONV_PAD, CONV_TILE), :] = u_ref[0, pl.ds(base, CONV_TILE), :]
        return carry

    lax.fori_loop(0, seq // CONV_TILE, fill, 0)
    w = w_ref[...]
    cb, lg, lb = cb_ref[...], lg_ref[...], lb_ref[...]

    def tile(i, carry):
        base = pl.multiple_of(i * CONV_TILE, CONV_TILE)
        win = pad_ref[pl.ds(base, CONV_TILE + 2 * CONV_PAD), :]
        acc = jnp.zeros((CONV_TILE, CONV_CH), F32)
        for r in range(SUBLANE):
            shifted = win[r:r + CONV_TILE + 2 * CONV_PAD - SUBLANE, :]
            for j in range(CONV_WIDTH):
                if (j + 1) % SUBLANE == r:
                    a = j + 1 - r
                    acc = acc + shifted[a:a + CONV_TILE, :] * w[j:j + 1, :]
        c = acc + cb
        mu = jnp.mean(c, axis=-1, keepdims=True)
        cc = c - mu
        var = jnp.mean(cc * cc, axis=-1, keepdims=True)
        y = cc * lax.rsqrt(var + EPS) * lg + lb
        o_ref[0, pl.ds(base, CONV_TILE), :] = (y * _sigmoid(y)).astype(BF16)
        return carry

    lax.fori_loop(0, seq // CONV_TILE, tile, 0)


def _conv_call(u, lw):
    b, s, ch = u.shape
    return pl.pallas_call(
        functools.partial(_conv_kernel, seq=s),
        grid=(b,),
        in_specs=[
            pl.BlockSpec((1, s, ch), lambda i: (i, 0, 0)),
            _const_spec((32, ch)), _const_spec((1, ch)), _const_spec((1, ch)), _const_spec((1, ch)),
        ],
        out_specs=pl.BlockSpec((1, s, ch), lambda i: (i, 0, 0)),
        out_shape=jax.ShapeDtypeStruct((b, s, ch), BF16),
        scratch_shapes=[pltpu.VMEM((s + 2 * CONV_PAD, ch), F32)],
        compiler_params=_cparams(("parallel",)),
        name="conv",
    )(u, lw["conv_w"], lw["conv_b"], lw["conv_ln_g"], lw["conv_ln_b"])


def _lane_mask(width, lo, hi):
    lane = lax.broadcasted_iota(jnp.int32, (1, width), 1)
    return (lane >= lo) & (lane < hi)


def _softmax_pv(qw, kt, v1):
    s = jnp.dot(qw, kt, preferred_element_type=F32)
    m = jnp.max(s, axis=-1, keepdims=True)
    p = jnp.exp2(s - m).astype(BF16)
    o = jnp.dot(p, v1, preferred_element_type=F32)
    return o * (1.0 / o[:, V_HEAD:V_HEAD + 1])


def _attn_kernel(q_ref, kt_ref, v_ref, lam_ref, g64_ref, sg_ref, o_ref, *, maps, diff, lam_init):
    if diff:
        lv = lam_ref[...]
        lam = (jnp.exp(jnp.sum(lv[0:1] * lv[1:2], axis=-1, keepdims=True))
               - jnp.exp(jnp.sum(lv[2:3] * lv[3:4], axis=-1, keepdims=True)) + lam_init)
    heads = []
    for h in range(N_HEADS):
        outs = []
        for (w0, lo, hi) in maps[h]:
            qw = q_ref[0, :, w0:w0 + LANE]
            if (lo, hi) != (0, LANE):
                qw = jnp.where(_lane_mask(LANE, lo, hi), qw, jnp.zeros_like(qw))
            outs.append(_softmax_pv(qw, kt_ref[0, w0:w0 + LANE, :], v_ref[0, h]))
        heads.append(outs[0] - lam * outs[1] if diff else outs[0])
    low = _lane_mask(LANE, 0, V_HEAD)
    acc = jnp.concatenate(
        [jnp.where(low, heads[h], pltpu.roll(heads[h + 1], V_HEAD, axis=1)) for h in range(0, N_HEADS, 2)],
        axis=1)
    if diff:
        acc = _group_norm(acc, g64_ref, 1.0 / DIFF_V) * sg_ref[...]
    o_ref[0] = acc.astype(BF16)


MAPS_MLA = tuple(((LANE * h, 0, LANE),) for h in range(N_HEADS))
MAPS_DIFF = tuple(tuple((LANE * (h // 2), 64 * (h % 2) + 32 * c, 64 * (h % 2) + 32 * c + 32) for c in range(2))
                  for h in range(N_HEADS))
MAPS_NA = tuple(((LANE * (h // 2), 64 * (h % 2), 64 * (h % 2) + 64),) for h in range(N_HEADS))


def _attn_call(q, kt, v, lw, maps, diff=False, lam_init=0.0):
    b, s, wq = q.shape
    sk = kt.shape[2]
    tq = min(s, ATTN_TQ)
    v = _value_heads(v)
    return pl.pallas_call(
        functools.partial(_attn_kernel, maps=maps, diff=diff, lam_init=lam_init),
        grid=(b, s // tq),
        in_specs=[
            pl.BlockSpec((1, tq, wq), lambda i, j: (i, j, 0)),
            pl.BlockSpec((1, wq, sk), lambda i, j: (i, 0, 0)),
            pl.BlockSpec((1, N_HEADS, sk, LANE), lambda i, j: (i, 0, 0, 0)),
            _const_spec((4, DIFF_DIM)),
            _const_spec((BR_W, BR_W)),
            _const_spec((1, BR_W)),
        ],
        out_specs=pl.BlockSpec((1, tq, BR_W), lambda i, j: (i, j, 0)),
        out_shape=jax.ShapeDtypeStruct((b, s, BR_W), BF16),
        compiler_params=_cparams(("parallel", "parallel")),
        name="attn_diff" if diff else "attn",
    )(q, kt, v, lw["diff_lam"], lw["g64"], lw["subln"])


_NT = (((1,), (1,)), ((), ()))


def _na_kernel(q_ref, k_ref, v_ref, kc_ref, vc_ref, bias_ref, o_ref, *, rows):
    nj = rows // 2
    band = NA_BAND_ROWS * GRID_W
    pair = 2 * GRID_W
    for sub in range(NA_PAIRS):
        j = pl.program_id(1) * NA_PAIRS + sub
        start = jnp.clip(2 * j - 4, 0, rows - NA_BAND_ROWS)
        base = pl.multiple_of(start * GRID_W, 2 * GRID_W)
        cls = jnp.where(j < 2, j, jnp.where(j >= nj - 2, j - (nj - 2) + 3, 2))
        keys = jnp.concatenate([k_ref[0, pl.ds(base, band), :], kc_ref[0]], axis=0)
        q = q_ref[0, sub * pair:(sub + 1) * pair, :]
        heads = []
        for h in range(N_HEADS):
            (w0, lo, hi), = MAPS_NA[h]
            qw = q[:, w0:w0 + LANE]
            qm = jnp.where(_lane_mask(LANE, lo, hi), qw, jnp.zeros_like(qw))
            s = lax.dot_general(qm, keys[:, w0:w0 + LANE], _NT, preferred_element_type=F32) + bias_ref[cls, h]
            p = jnp.exp2(s - jnp.max(s, axis=-1, keepdims=True)).astype(BF16)
            vals = jnp.concatenate([v_ref[0, h, pl.ds(base, band), :], vc_ref[0, h]], axis=0)
            o = jnp.dot(p, vals, preferred_element_type=F32)
            heads.append(o * (1.0 / o[:, V_HEAD:V_HEAD + 1]))
        low = _lane_mask(LANE, 0, V_HEAD)
        acc = jnp.concatenate(
            [jnp.where(low, heads[h], pltpu.roll(heads[h + 1], V_HEAD, axis=1)) for h in range(0, N_HEADS, 2)],
            axis=1)
        o_ref[0, sub * pair:(sub + 1) * pair, :] = acc.astype(BF16)


def _value_heads(v):
    b, n, _ = v.shape
    vh = v.reshape(b, n, N_HEADS, V_HEAD).transpose(0, 2, 1, 3)
    return jnp.concatenate([vh, jnp.ones_like(vh)], axis=-1)


def _na_call(q, k, v, kc, vc, bias):
    b, s, w = q.shape
    n_ctx = kc.shape[1]
    rows = s // GRID_W
    tq = 2 * GRID_W * NA_PAIRS
    v, vc = _value_heads(v), _value_heads(vc)
    return pl.pallas_call(
        functools.partial(_na_kernel, rows=rows),
        grid=(b, rows // (2 * NA_PAIRS)),
        in_specs=[
            pl.BlockSpec((1, tq, w), lambda i, j: (i, j, 0)),
            pl.BlockSpec((1, s, w), lambda i, j: (i, 0, 0)),
            pl.BlockSpec((1, N_HEADS, s, LANE), lambda i, j: (i, 0, 0, 0)),
            pl.BlockSpec((1, n_ctx, w), lambda i, j: (i, 0, 0)),
            pl.BlockSpec((1, N_HEADS, n_ctx, LANE), lambda i, j: (i, 0, 0, 0)),
            _const_spec(bias.shape),
        ],
        out_specs=pl.BlockSpec((1, tq, w), lambda i, j: (i, j, 0)),
        out_shape=jax.ShapeDtypeStruct((b, s, w), BF16),
        compiler_params=_cparams(("parallel", "arbitrary")),
        name="na",
    )(q, k, v, kc, vc, bias)


def _merge_kernel(x_ref, mod_ref, n1g_ref, n2g_ref, uc_ref, om_ref, od_ref, on_ref,
                  gw_ref, gb_ref, wc_ref, wm_ref, wd_ref, wn_ref, wo_ref, rwt_ref, rb_ref,
                  x1_ref, h2_ref, ids_ref, wts_ref):
    x = x_ref[0]
    mod = mod_ref[0]
    h = _modulate(x, n1g_ref[...], mod[0:1], mod[1:2]).astype(BF16)
    y = jnp.zeros(x.shape, F32)
    branches = ((uc_ref, wc_ref), (om_ref, wm_ref), (od_ref, wd_ref), (on_ref, wn_ref))
    for i, (o_ref, w_ref) in enumerate(branches):
        lo = D_MODEL * i
        g = _sigmoid(jnp.dot(h, gw_ref[:, lo:lo + D_MODEL], preferred_element_type=F32)
                     + gb_ref[:, lo:lo + D_MODEL])
        y = y + g * jnp.dot(o_ref[0], w_ref[...], preferred_element_type=F32)
    out = jnp.dot(y.astype(BF16), wo_ref[...], preferred_element_type=F32)
    x1 = x + mod[2:3] * out
    x1_ref[0] = x1
    h2 = _modulate(x1, n2g_ref[...], mod[3:4], mod[4:5])
    for q, piece in enumerate(_pack_row(h2)):
        h2_ref[q, 0] = piece

    logits = lax.dot_general(rwt_ref[...], h2, _NT, preferred_element_type=F32,
                             precision=lax.Precision.HIGHEST) + rb_ref[...]
    eidx = lax.broadcasted_iota(jnp.int32, logits.shape, 0).astype(F32)
    vals, idxs = [], []
    cur = logits
    for _ in range(TOP_K):
        m = jnp.max(cur, axis=0, keepdims=True)
        idx = jnp.min(jnp.where(cur == m, eidx, float(N_EXPERTS)), axis=0, keepdims=True)
        vals.append(m)
        idxs.append(idx)
        cur = jnp.where(eidx == idx, -jnp.inf, cur)
    es = [jnp.exp(vk - vals[0]) for vk in vals]
    den = es[0] + es[1] + es[2] + es[3]
    ids_ref[0] = jnp.concatenate(idxs, axis=0).astype(jnp.int32)
    wts_ref[0] = jnp.concatenate([e / den for e in es], axis=0)


def _merge_call(x, mods, mod_row, lw, uc, om, od, on):
    b, s, d = x.shape
    t = min(s, 512)
    if mod_row is None:
        mod_map = lambda i, j: (i, 0, 0)
    else:
        mod_map = lambda i, j: (mod_row, 0, 0)
    tok = lambda w: pl.BlockSpec((1, t, w), lambda i, j: (i, j, 0))
    rt = pl.BlockSpec((1, TOP_K, t), lambda i, j: (i, 0, j))
    return pl.pallas_call(
        _merge_kernel,
        grid=(b, s // t),
        in_specs=[
            tok(d), pl.BlockSpec((1, 6, d), mod_map), _const_spec((1, d)), _const_spec((1, d)),
            tok(BR_W), tok(BR_W), tok(BR_W), tok(BR_W),
            _const_spec((d, N_BRANCH * d)), _const_spec((1, N_BRANCH * d)),
            _const_spec((BR_W, d)), _const_spec((BR_W, d)), _const_spec((BR_W, d)), _const_spec((BR_W, d)),
            _const_spec((d, d)), _const_spec((N_EXPERTS, d)), _const_spec((N_EXPERTS, 1)),
        ],
        out_specs=[tok(d), pl.BlockSpec((ROW_PARTS, 1, t, ROW_Q), lambda i, j: (0, i, j, 0)), rt, rt],
        out_shape=[jax.ShapeDtypeStruct((b, s, d), F32), jax.ShapeDtypeStruct((ROW_PARTS, b, s, ROW_Q), jnp.int32),
                   jax.ShapeDtypeStruct((b, TOP_K, s), jnp.int32), jax.ShapeDtypeStruct((b, TOP_K, s), F32)],
        compiler_params=_cparams(("parallel", "parallel")),
        name="merge",
    )(x, mods, lw["n1g"], lw["n2g"], uc, om, od, on, lw["gate_w"], lw["gate_b"],
      lw["conv_out"], lw["mla_out"], lw["diff_out"], lw["na_out"], lw["w_o"], lw["router_wt"], lw["router_b"])


SC_WINDOW = 128
ROW_Q = D_MODEL // 4
ROW_PARTS = 2
HALF_D = D_MODEL // 2


def _pack_bf16_pair(a, b):
    ua = lax.bitcast_convert_type(a.astype(BF16).astype(F32), jnp.int32)
    ub = lax.bitcast_convert_type(b.astype(BF16).astype(F32), jnp.int32)
    return ua | lax.shift_right_logical(ub, jnp.int32(16))


def _unpack_bf16_pair(w):
    a = lax.bitcast_convert_type(w & jnp.int32(-65536), F32)
    b = lax.bitcast_convert_type(lax.shift_left(w, jnp.int32(16)), F32)
    return a, b


def _pack_row(x):
    w = _pack_bf16_pair(x[:, :HALF_D], x[:, HALF_D:])
    return [w[:, ROW_Q * q:ROW_Q * (q + 1)] for q in range(ROW_PARTS)]


def _unpack_row(pieces):
    ab = [_unpack_bf16_pair(w) for w in pieces]
    return jnp.concatenate([a for a, _ in ab] + [b for _, b in ab], axis=1)


def _route_slots(ids, tile):
    n = ids.shape[1]
    p = TOP_K * n
    e = ids.reshape(p)
    onehot = (e[:, None] == jnp.arange(N_EXPERTS, dtype=jnp.int32)[None, :])
    chunk = 512
    oh3 = onehot.astype(F32).reshape(p // chunk, chunk, N_EXPERTS)
    within = jnp.einsum("ij,cje->cie", jnp.tril(jnp.ones((chunk, chunk), F32)), oh3)
    totals = within[:, -1, :]
    before = jnp.cumsum(totals, axis=0) - totals
    csum = (within + before[:, None, :]).reshape(p, N_EXPERTS).astype(jnp.int32)
    onehot = onehot.astype(jnp.int32)
    counts = csum[-1]
    padded = ((counts + tile - 1) // tile) * tile
    gend = jnp.cumsum(padded)
    gstart = gend - padded
    slot = jnp.sum(onehot * (csum - 1 + gstart[None, :]), axis=1).astype(jnp.int32)
    n_tiles = p // tile + N_EXPERTS
    tile_start = jnp.arange(n_tiles, dtype=jnp.int32) * tile
    texp = jnp.sum((tile_start[:, None] >= gend[None, :]).astype(jnp.int32), axis=1)
    texp = jnp.minimum(texp, N_EXPERTS - 1)
    nreal = jnp.clip(gstart[texp] + counts[texp] - tile_start, 0, tile)
    nreal = jnp.where(tile_start < gend[-1], nreal, 0).astype(jnp.int32)
    return slot, texp, nreal, n_tiles


def _sc_mesh():
    return plsc.VectorSubcoreMesh(core_axis_name="c", subcore_axis_name="s")


def _sc_dispatch(hq, idx, n_slots):
    parts, n, w = hq.shape
    src = hq.reshape(parts * n, w)
    m = idx.shape[0]
    blocks_per_q = n // SC_WINDOW
    per_q = TOP_K * blocks_per_q

    @pl.kernel(out_type=jax.ShapeDtypeStruct((parts * n_slots, w), hq.dtype), mesh=_sc_mesh(), scratch_types=[])
    def kern(x_hbm, i_hbm, o_hbm):
        def body(x_vmem, i_vmem):
            pltpu.sync_copy(x_vmem, o_hbm.at[i_vmem.at[0]])

        pltpu.emit_pipeline(
            body,
            grid=(m // SC_WINDOW,),
            in_specs=[
                pl.BlockSpec((SC_WINDOW, w), index_map=lambda i: ((i // per_q) * blocks_per_q + i % blocks_per_q, 0)),
                pl.BlockSpec((1, SC_WINDOW), index_map=lambda i: (0, i)),
            ],
            out_specs=[],
            core_axis_name=("c", "s"),
            dimension_semantics=(pltpu.PARALLEL,),
        )(x_hbm, i_hbm)

    return kern(src, idx.reshape(1, m)).reshape(parts, n_slots, w)


def _sc_collect(ys, idx):
    parts, n_slots, w = ys.shape
    src = ys.reshape(parts * n_slots, w)
    m = idx.shape[0]

    @pl.kernel(out_type=jax.ShapeDtypeStruct((m, w), ys.dtype), mesh=_sc_mesh(), scratch_types=[])
    def kern(x_hbm, i_hbm, o_hbm):
        def body(i_vmem, o_vmem):
            pltpu.sync_copy(x_hbm.at[i_vmem.at[0]], o_vmem)

        pltpu.emit_pipeline(
            body,
            grid=(m // SC_WINDOW,),
            in_specs=[pl.BlockSpec((1, SC_WINDOW), index_map=lambda i: (0, i))],
            out_specs=[pl.BlockSpec((SC_WINDOW, w), index_map=lambda i: (i, 0))],
            core_axis_name=("c", "s"),
            dimension_semantics=(pltpu.PARALLEL,),
        )(i_hbm, o_hbm)

    return kern(src, idx.reshape(1, m))


def _ffn_sorted_kernel(texp_ref, nreal_ref, x_ref, wgu_ref, bgu_ref, wd_ref, bd_ref, y_ref, wgu_bf, wd_bf, *, tile):
    i = pl.program_id(0)
    nreal = nreal_ref[i]

    @pl.when((nreal > 0) & ((i == 0) | (texp_ref[i] != texp_ref[jnp.maximum(i - 1, 0)])))
    def _():
        wgu_bf[...] = wgu_ref[0].astype(BF16)
        wd_bf[...] = wd_ref[0].astype(BF16)

    @pl.when(nreal > 0)
    def _():
        x = _unpack_row([x_ref[q] for q in range(ROW_PARTS)])
        row = lax.broadcasted_iota(jnp.int32, (tile, 1), 0)
        x = jnp.where(row < nreal, x, 0.0).astype(BF16)
        gu = jnp.dot(x, wgu_bf[...], preferred_element_type=F32) + bgu_ref[0]
        g = jnp.minimum(gu[:, :D_FF], SWIGLU_LIMIT)
        u = jnp.clip(gu[:, D_FF:], -SWIGLU_LIMIT, SWIGLU_LIMIT)
        act = ((u + 1.0) * (g * _sigmoid(SWIGLU_ALPHA * g))).astype(BF16)
        y = jnp.dot(act, wd_bf[...], preferred_element_type=F32) + bd_ref[0]
        for q, piece in enumerate(_pack_row(y)):
            y_ref[q] = piece

    @pl.when(nreal == 0)
    def _():
        y_ref[...] = jnp.zeros(y_ref.shape, jnp.int32)


def _ffn_sorted_call(xs, texp, nreal, lw, tile):
    _, n_slots, w = xs.shape
    n_tiles = n_slots // tile
    d = D_MODEL
    off = lw["exp_off"]
    grid_spec = pltpu.PrefetchScalarGridSpec(
        num_scalar_prefetch=2,
        grid=(n_tiles,),
        in_specs=[
            pl.BlockSpec((ROW_PARTS, tile, w), lambda i, te, nr: (0, i, 0)),
            pl.BlockSpec((1, d, 2 * D_FF), lambda i, te, nr: (te[i] + off, 0, 0)),
            pl.BlockSpec((1, 1, 2 * D_FF), lambda i, te, nr: (te[i] + off, 0, 0)),
            pl.BlockSpec((1, D_FF, d), lambda i, te, nr: (te[i] + off, 0, 0)),
            pl.BlockSpec((1, 1, d), lambda i, te, nr: (te[i] + off, 0, 0)),
        ],
        out_specs=pl.BlockSpec((ROW_PARTS, tile, w), lambda i, te, nr: (0, i, 0)),
        scratch_shapes=[pltpu.VMEM((d, 2 * D_FF), BF16), pltpu.VMEM((D_FF, d), BF16)],
    )
    return pl.pallas_call(
        functools.partial(_ffn_sorted_kernel, tile=tile),
        grid_spec=grid_spec,
        out_shape=jax.ShapeDtypeStruct((ROW_PARTS, n_slots, w), jnp.int32),
        compiler_params=_cparams(("arbitrary",)),
        name="moe_ffn",
    )(texp, nreal, xs, lw["exp_w_gu"], lw["exp_b_gu"], lw["exp_w_down"], lw["exp_b_down"])


def _combine_q_kernel(x1_ref, mod_ref, w_ref, y_ref, o_ref):
    w = w_ref[0]
    g2 = mod_ref[0][5:6]
    for q in range(ROW_PARTS):
        acc_a, acc_b = None, None
        for k in range(TOP_K):
            a, b = _unpack_bf16_pair(y_ref[q, k])
            wk = w[:, k:k + 1]
            acc_a = wk * a if acc_a is None else acc_a + wk * a
            acc_b = wk * b if acc_b is None else acc_b + wk * b
        for lo, acc in ((ROW_Q * q, acc_a), (HALF_D + ROW_Q * q, acc_b)):
            o_ref[0, :, lo:lo + ROW_Q] = x1_ref[0, :, lo:lo + ROW_Q] + g2[:, lo:lo + ROW_Q] * acc


def _combine_q_call(x1, mods, mod_row, wts, y, tok_off):
    b, s, d = x1.shape
    t = min(s, 512)
    nt = s // t
    blk_off = tok_off // t
    if mod_row is None:
        mod_map = lambda i, j: (i, 0, 0)
    else:
        mod_map = lambda i, j: (mod_row, 0, 0)
    wts = wts.transpose(0, 2, 1)
    return pl.pallas_call(
        _combine_q_kernel,
        grid=(b, nt),
        in_specs=[
            pl.BlockSpec((1, t, d), lambda i, j: (i, j, 0)),
            pl.BlockSpec((1, 6, d), mod_map),
            pl.BlockSpec((1, t, TOP_K), lambda i, j: (i, j, 0)),
            pl.BlockSpec((ROW_PARTS, TOP_K, t, ROW_Q), lambda i, j: (0, 0, blk_off + i * nt + j, 0)),
        ],
        out_specs=pl.BlockSpec((1, t, d), lambda i, j: (i, j, 0)),
        out_shape=jax.ShapeDtypeStruct((b, s, d), F32),
        compiler_params=_cparams(("parallel", "parallel")),
        name="moe_combine",
    )(x1, mods, wts, y)


def _moe_sc(streams, mods, lw):
    sizes = [st[0].shape[0] * st[0].shape[1] for st in streams]
    n = sum(sizes)
    tile = 512 if TOP_K * n >= 512 * N_EXPERTS * 4 else 256
    hq = jnp.concatenate([st[1].reshape(ROW_PARTS, m, ROW_Q) for st, m in zip(streams, sizes)], axis=1)
    ids = jnp.concatenate([st[2].transpose(1, 0, 2).reshape(TOP_K, m) for st, m in zip(streams, sizes)], axis=1)
    slot, texp, nreal, n_tiles = _route_slots(ids, tile)
    n_slots = n_tiles * tile
    idx = (slot[None, :] + (jnp.arange(ROW_PARTS, dtype=jnp.int32) * n_slots)[:, None]).reshape(-1)
    xs = _sc_dispatch(hq, idx, n_slots)
    ys = _ffn_sorted_call(xs, texp, nreal, lw, tile)
    y = _sc_collect(ys, idx).reshape(ROW_PARTS, TOP_K, n, ROW_Q)
    outs, off = [], 0
    for (x1, _, _, wts, mod_row), m in zip(streams, sizes):
        outs.append(_combine_q_call(x1, mods, mod_row, wts, y, off))
        off += m
    return outs


def _layer_weights(l, p, lam_init):
    w = p["w_in"][l].astype(BF16)
    d = w.shape[0]
    zcols = lambda n: jnp.zeros((d, n), w.dtype)
    regroup = lambda blk: blk.reshape(d, N_HEADS, 3, HEAD_DIM).transpose(0, 2, 1, 3).reshape(d, 3 * BR_W)
    w_in = jnp.concatenate([
        w[:, :A_IN],
        w[:, OFF_B:OFF_B + Q_LORA], zcols(P_CKV - P_CQ - Q_LORA),
        w[:, OFF_B + Q_LORA:OFF_B + Q_LORA + KV_LORA],
        w[:, OFF_B + Q_LORA + KV_LORA:OFF_C], zcols(P_DQ - P_KPE - QK_ROPE),
        regroup(w[:, OFF_C:OFF_D]), regroup(w[:, OFF_D:]),
    ], axis=1)
    assert w_in.shape[1] == PROJ_W

    def head_slots(w3, slot):
        w3 = jnp.pad(w3, ((0, 0), (0, 0), (0, slot - w3.shape[2])))
        return w3.reshape(w3.shape[0], N_HEADS * slot)

    wuq = head_slots(p["mla_w_uq"][l].reshape(Q_LORA, N_HEADS, MLA_QK), LANE)
    wuq = jnp.pad(wuq, ((0, 256 - Q_LORA), (0, 0)))
    wukv = p["mla_w_ukv"][l].reshape(KV_LORA, N_HEADS, QK_NOPE + V_HEAD)
    wk = head_slots(wukv[:, :, :QK_NOPE], LANE)
    wv = head_slots(wukv[:, :, QK_NOPE:], V_HEAD)
    ppe = np.zeros((LANE, 512), np.float32)
    for h in range(N_HEADS):
        for i in range(QK_ROPE):
            ppe[i, h * LANE + QK_NOPE + i] = 1.0

    def slot_gain(g, scale):
        g = jnp.concatenate([g * scale, jnp.zeros((LANE - MLA_QK,), F32)])
        return jnp.tile(g, N_HEADS)

    def row512(v):
        return jnp.concatenate([v, jnp.zeros((512 - v.shape[0],), F32)])

    gains = jnp.stack([
        row512(p["mla_cq_g"][l]),
        slot_gain(p["mla_qn_g"][l], MLA_QK ** -0.5 * LOG2E),
        row512(p["mla_ckv_g"][l]),
        slot_gain(p["mla_kn_g"][l], 1.0),
        row512(jnp.tile(p["diff_qn_g"][l], 2 * N_HEADS) * DIFF_DIM ** -0.5 * LOG2E),
        row512(jnp.tile(p["diff_kn_g"][l], 2 * N_HEADS)),
        row512(jnp.tile(p["na_qn_g"][l], N_HEADS) * HEAD_DIM ** -0.5 * LOG2E),
        row512(jnp.tile(p["na_kn_g"][l], N_HEADS)),
    ])
    conv_w = jnp.concatenate([p["conv_w"][l], jnp.zeros((1, CONV_CH), F32)], axis=0)
    return dict(
        n1g=p["norm1_g"][l][None, :], n2g=p["norm2_g"][l][None, :],
        w_in=w_in, gains=gains,
        wuq=wuq.astype(BF16), wk=wk.astype(BF16), wv=wv.astype(BF16), ppe=jnp.asarray(ppe, BF16),
        g96=jnp.asarray(_group_ones(512, LANE, MLA_QK), BF16),
        g32=jnp.asarray(_group_ones(BR_W, DIFF_DIM, DIFF_DIM), BF16),
        g64=jnp.asarray(_group_ones(BR_W, HEAD_DIM, HEAD_DIM), BF16),
        rm=jnp.asarray(_rot_matrix(512, LANE, QK_NOPE, QK_ROPE // 2), BF16),
        rd=jnp.asarray(_rot_matrix(BR_W, DIFF_DIM, 0, DIFF_DIM // 2), BF16),
        conv_w=conv_w, conv_b=p["conv_b"][l][None, :],
        conv_ln_g=p["conv_ln_g"][l][None, :], conv_ln_b=p["conv_ln_b"][l][None, :],
        diff_lam=p["diff_lam"][l],
        subln=(jnp.tile(p["diff_subln_g"][l], N_HEADS) * (1.0 - lam_init))[None, :],
        gate_w=p["gate_w"][l].astype(BF16), gate_b=p["gate_b"][l][None, :],
        conv_out=p["conv_out"][l].astype(BF16), mla_out=p["mla_out"][l].astype(BF16),
        diff_out=p["diff_out"][l].astype(BF16), na_out=p["na_out"][l].astype(BF16),
        w_o=p["w_o"][l].astype(BF16),
        router_wt=p["router_w"][l].T, router_b=p["router_b"][l][:, None],
        exp_off=l * N_EXPERTS,
        exp_w_gu=p["exp_w_gu"].reshape((-1,) + p["exp_w_gu"].shape[2:]),
        exp_b_gu=p["exp_b_gu"].reshape(-1, 1, 2 * D_FF),
        exp_w_down=p["exp_w_down"].reshape((-1,) + p["exp_w_down"].shape[2:]),
        exp_b_down=p["exp_b_down"].reshape(-1, 1, D_MODEL),
    )


def _kt(kc, k):
    return jnp.concatenate([kc, k], axis=1).transpose(0, 2, 1)


def kernel(x, c, ctx, c_ctx, ada_w, ada_b, norm1_g, norm2_g, w_in, conv_w, conv_b, conv_ln_g, conv_ln_b, conv_out, mla_cq_g, mla_ckv_g, mla_w_uq, mla_w_ukv, mla_qn_g, mla_kn_g, mla_out, diff_qn_g, diff_kn_g, diff_lam, diff_subln_g, diff_out, na_qn_g, na_kn_g, na_rpb, na_out, gate_w, gate_b, w_o, router_w, router_b, exp_w_gu, exp_b_gu, exp_w_down, exp_b_down):
    p = dict(norm1_g=norm1_g, norm2_g=norm2_g, w_in=w_in, conv_w=conv_w, conv_b=conv_b,
             conv_ln_g=conv_ln_g, conv_ln_b=conv_ln_b, conv_out=conv_out, mla_cq_g=mla_cq_g,
             mla_ckv_g=mla_ckv_g, mla_w_uq=mla_w_uq, mla_w_ukv=mla_w_ukv, mla_qn_g=mla_qn_g,
             mla_kn_g=mla_kn_g, mla_out=mla_out, diff_qn_g=diff_qn_g, diff_kn_g=diff_kn_g,
             diff_lam=diff_lam, diff_subln_g=diff_subln_g, diff_out=diff_out, na_qn_g=na_qn_g,
             na_kn_g=na_kn_g, na_out=na_out, gate_w=gate_w, gate_b=gate_b, w_o=w_o,
             router_w=router_w, router_b=router_b, exp_w_gu=exp_w_gu, exp_b_gu=exp_b_gu,
             exp_w_down=exp_w_down, exp_b_down=exp_b_down)
    b, s, d = x.shape
    n_ctx = ctx.shape[1]
    depth = ada_w.shape[0]
    rows = s // GRID_W
    assert d == D_MODEL and s % (2 * GRID_W) == 0 and rows >= NA_BAND_ROWS and n_ctx % LANE == 0

    mod_rows = -(-(b + 1) // 8) * 8
    cs = jnp.concatenate([c, c_ctx[None, :], jnp.zeros((mod_rows - b - 1, d), F32)], axis=0)
    mods_all = _ada_call(cs, ada_w, ada_b).reshape(depth, mod_rows, 6, d)

    tabs_x = _rope_lane_tables(s)
    tabs_c = (jnp.ones((n_ctx, 512), F32), jnp.zeros((n_ctx, 512), F32),
              jnp.ones((n_ctx, BR_W), F32), jnp.zeros((n_ctx, BR_W), F32))

    xc = ctx
    for l in range(depth):
        last = l == depth - 1
        lam_init = 0.8 - 0.6 * math.exp(-0.3 * l)
        lw = _layer_weights(l, p, lam_init)
        mods = mods_all[l]
        bias = _na_bias_tables(na_rpb[l], rows, n_ctx)

        u, mq, mk, mv, dq, dk, dv, nq, nk, nv = _proj_call(x, mods, None, lw, tabs_x, True)
        uc, mqc, mkc, mvc, dqc, dkc, dvc, nqc, nkc, nvc = _proj_call(xc, mods, b, lw, tabs_c, False)

        y_conv = _conv_call(u, lw)
        y_mla = _attn_call(mq, _kt(mkc, mk), jnp.concatenate([mvc, mv], axis=1), lw, MAPS_MLA)
        y_diff = _attn_call(dq, _kt(dkc, dk), jnp.concatenate([dvc, dv], axis=1), lw, MAPS_DIFF,
                            diff=True, lam_init=lam_init)
        y_na = _na_call(nq, nk, nv, nkc, nvc, bias)
        x1, h2, ids, wts = _merge_call(x, mods, None, lw, y_conv, y_mla, y_diff, y_na)
        streams = [(x1, h2, ids, wts, None)]

        if not last:
            yc_conv = _conv_call(uc, lw)
            yc_mla = _attn_call(mqc, mkc.transpose(0, 2, 1), mvc, lw, MAPS_MLA)
            yc_diff = _attn_call(dqc, dkc.transpose(0, 2, 1), dvc, lw, MAPS_DIFF, diff=True, lam_init=lam_init)
            yc_na = _attn_call(nqc, nkc.transpose(0, 2, 1), nvc, lw, MAPS_NA)
            xc1, h2c, idsc, wtsc = _merge_call(xc, mods, b, lw, yc_conv, yc_mla, yc_diff, yc_na)
            streams.append((xc1, h2c, idsc, wtsc, b))

        outs = _moe_sc(streams, mods, lw)
        x = outs[0]
        if not last:
            xc = outs[1]
    return x
```

```python
import functools
import math

import numpy as np
import jax
import jax.numpy as jnp
from jax import lax
from jax.experimental import pallas as pl
from jax.experimental.pallas import tpu as pltpu
from jax.experimental.pallas import tpu_sc as plsc

F32 = jnp.float32
BF16 = jnp.bfloat16

D_MODEL = 1024
GRID_W = 64
N_BRANCH = 4
N_HEADS = 4
HEAD_DIM = 64
CONV_CH = 256
CONV_WIDTH = 31
Q_LORA = 192
KV_LORA = 128
QK_NOPE = 64
QK_ROPE = 32
V_HEAD = 64
DIFF_DIM = 32
DIFF_V = 2 * DIFF_DIM
NA_KH = 8
NA_KW = 16
ROPE_DIM = 32
ROPE_BASE = 10000.0
N_EXPERTS = 32
TOP_K = 4
D_FF = 1024
SWIGLU_LIMIT = 7.0
SWIGLU_ALPHA = 1.702
EPS = 1e-6
NEG_INF = -1e30

A_IN = 2 * CONV_CH
B_IN = Q_LORA + KV_LORA + QK_ROPE
C_IN = N_HEADS * (4 * DIFF_DIM + DIFF_V)
D_IN = N_HEADS * 3 * HEAD_DIM
OFF_B = A_IN
OFF_C = OFF_B + B_IN
OFF_D = OFF_C + C_IN

LANE = 128
SUBLANE = 8
MLA_QK = QK_NOPE + QK_ROPE
BR_W = N_HEADS * HEAD_DIM
PROJ_W = 2560
NA_BAND_ROWS = 10
ATTN_TQ = 512
NA_PAIRS = 4
LOG2E = math.log2(math.e)
VMEM_LIMIT = 52 * 1024 * 1024

P_A, P_G, P_CQ, P_CKV, P_KPE = 0, 256, 512, 768, 896
P_DQ, P_DK, P_DV = 1024, 1280, 1536
P_NQ, P_NK, P_NV = 1792, 2048, 2304


def _sigmoid(x):
    return 1.0 / (1.0 + jnp.exp(-x))


def _modulate(x, g, shift, scale):
    ms = jnp.mean(x * x, axis=-1, keepdims=True)
    return (x * lax.rsqrt(ms + EPS) * g) * (1.0 + scale) + shift


def _cparams(sem):
    return pltpu.CompilerParams(dimension_semantics=sem, vmem_limit_bytes=VMEM_LIMIT)


def _const_spec(shape):
    n = len(shape)
    return pl.BlockSpec(shape, lambda *_: (0,) * n)


def _group_ones(width, slot, real):
    i = np.arange(width)
    valid = (i % slot) < real
    same = (i[:, None] // slot) == (i[None, :] // slot)
    return (same & valid[:, None] & valid[None, :]).astype(np.float32)


def _rot_matrix(width, slot, start, half):
    r = np.zeros((width, width), np.float32)
    for s0 in range(0, width, slot):
        for i in range(half):
            a, b = s0 + start + i, s0 + start + half + i
            r[b, a] = -1.0
            r[a, b] = 1.0
    return r


def _rope_lane_tables(n_tokens):
    t = jnp.arange(n_tokens, dtype=jnp.int32)
    rows = (t // GRID_W).astype(F32)
    cols = (t % GRID_W).astype(F32)
    axis_dim = ROPE_DIM // 2
    inv = ROPE_BASE ** (-jnp.arange(0, axis_dim, 2, dtype=F32) / axis_dim)
    theta = jnp.concatenate([rows[:, None] * inv, cols[:, None] * inv], axis=-1)
    cos, sin = jnp.cos(theta), jnp.sin(theta)
    half = ROPE_DIM // 2
    ones = jnp.ones((n_tokens, QK_NOPE), F32)
    zeros = jnp.zeros((n_tokens, QK_NOPE), F32)
    pad1 = jnp.ones((n_tokens, LANE - MLA_QK), F32)
    pad0 = jnp.zeros((n_tokens, LANE - MLA_QK), F32)
    cm = jnp.tile(jnp.concatenate([ones, cos, cos, pad1], -1), (1, N_HEADS))
    sm = jnp.tile(jnp.concatenate([zeros, sin, sin, pad0], -1), (1, N_HEADS))
    cd = jnp.tile(jnp.concatenate([cos, cos], -1), (1, 2 * N_HEADS))
    sd = jnp.tile(jnp.concatenate([sin, sin], -1), (1, 2 * N_HEADS))
    assert half * 2 == DIFF_DIM
    return cm, sm, cd, sd


def _na_bias_tables(rpb, rows, n_ctx):
    kh = min(NA_KH, rows)
    nj = rows // 2
    reps = np.array([0, 1, 2, nj - 2, nj - 1])
    n_ro, n_co = 2 * NA_KH - 1, 2 * NA_KW - 1
    start = np.clip(2 * reps - 4, 0, rows - NA_BAND_ROWS)
    r = 2 * reps[:, None] + np.arange(2)[None, :]
    kr = start[:, None] + np.arange(NA_BAND_ROWS)[None, :]
    row_start = np.clip(r - kh // 2, 0, rows - kh)
    vr = (kr[:, None, :] >= row_start[:, :, None]) & (kr[:, None, :] < row_start[:, :, None] + kh)
    ro = np.clip(kr[:, None, :] - r[:, :, None] + NA_KH - 1, 0, n_ro - 1)
    qc = np.arange(GRID_W)
    win_start = np.clip(qc - NA_KW // 2, 0, GRID_W - NA_KW)
    vc = (qc[None, :] >= win_start[:, None]) & (qc[None, :] < win_start[:, None] + NA_KW)
    co = np.clip(qc[None, :] - qc[:, None] + NA_KW - 1, 0, n_co - 1)
    rsel = (ro[..., None] == np.arange(n_ro)).astype(np.float32)
    csel = (co[None] == np.arange(n_co)[:, None, None]).astype(np.float32)
    hi = lax.Precision.HIGHEST
    t1 = jnp.einsum("cqav,hvw->hcqaw", rsel, rpb.astype(F32), precision=hi)
    b = jnp.einsum("hcqaw,wxy->chqxay", t1, csel, precision=hi)
    valid = vr[:, None, :, None, :, None] & vc[None, None, None, :, None, :]
    b = jnp.where(valid, b * LOG2E, NEG_INF)
    b = b.reshape(len(reps), N_HEADS, 2 * GRID_W, NA_BAND_ROWS * GRID_W)
    return jnp.concatenate([b, jnp.zeros(b.shape[:3] + (n_ctx,), F32)], axis=-1)


def _ada_kernel(c_ref, w_ref, b_ref, o_ref):
    c = c_ref[...]
    s = c * _sigmoid(c)
    o_ref[0] = jnp.dot(s, w_ref[0], preferred_element_type=F32,
                       precision=lax.Precision.HIGHEST) + b_ref[0]


def _ada_call(cs, ada_w, ada_b):
    depth, d, n = ada_w.shape
    rows = cs.shape[0]
    tn = 1536
    return pl.pallas_call(
        _ada_kernel,
        grid=(depth, n // tn),
        in_specs=[
            pl.BlockSpec((rows, d), lambda l, j: (0, 0)),
            pl.BlockSpec((1, d, tn), lambda l, j: (l, 0, j)),
            pl.BlockSpec((1, 1, tn), lambda l, j: (l, 0, j)),
        ],
        out_specs=pl.BlockSpec((1, rows, tn), lambda l, j: (l, 0, j)),
        out_shape=jax.ShapeDtypeStruct((depth, rows, n), F32),
        compiler_params=_cparams(("arbitrary", "arbitrary")),
        name="ada",
    )(cs, ada_w, ada_b.reshape(depth, 1, n))


def _group_norm(x, ones_ref, inv_n):
    sq = (x * x).astype(BF16)
    ms = jnp.dot(sq, ones_ref[...], preferred_element_type=F32) * inv_n
    return x * lax.rsqrt(ms + EPS)


def _store_value_heads(ref, v):
    low = _lane_mask(LANE, 0, V_HEAD)
    for h in range(N_HEADS):
        win = v[:, LANE * (h // 2):LANE * (h // 2 + 1)]
        if h % 2:
            win = pltpu.roll(win, V_HEAD, axis=1)
        ref[0, h] = jnp.where(low, win, 1.0).astype(BF16)


def _rope(x, rot_ref, cos_ref, sin_ref):
    rot = jnp.dot(x.astype(BF16), rot_ref[...], preferred_element_type=F32)
    return x * cos_ref[...] + rot * sin_ref[...]


def _proj_kernel(x_ref, mod_ref, n1g_ref, win_ref, gains_ref, wuq_ref, wk_ref, ppe_ref, wv_ref,
                 g96_ref, g32_ref, g64_ref, rm_ref, rd_ref, cm_ref, sm_ref, cd_ref, sd_ref,
                 u_ref, mq_ref, mk_ref, mv_ref, dq_ref, dk_ref, dv_ref, nq_ref, nk_ref, nv_ref,
                 *, use_rope):
    x = x_ref[0]
    mod = mod_ref[0]
    gains = gains_ref[...]
    h = _modulate(x, n1g_ref[...], mod[0:1], mod[1:2]).astype(BF16)
    proj = jnp.dot(h, win_ref[...], preferred_element_type=F32)

    u_ref[0] = proj[:, P_A:P_A + CONV_CH] * _sigmoid(proj[:, P_G:P_G + CONV_CH])

    cq = proj[:, P_CQ:P_CQ + 256]
    ms = jnp.sum(cq * cq, axis=-1, keepdims=True) * (1.0 / Q_LORA)
    cqn = (cq * lax.rsqrt(ms + EPS) * gains[0:1, :256]).astype(BF16)
    q = jnp.dot(cqn, wuq_ref[...], preferred_element_type=F32)
    q = _group_norm(q, g96_ref, 1.0 / MLA_QK) * gains[1:2, :]
    if use_rope:
        q = _rope(q, rm_ref, cm_ref, sm_ref)
    mq_ref[0] = q.astype(BF16)

    ckv = proj[:, P_CKV:P_CKV + KV_LORA]
    ms = jnp.mean(ckv * ckv, axis=-1, keepdims=True)
    ckvn = (ckv * lax.rsqrt(ms + EPS) * gains[2:3, :KV_LORA]).astype(BF16)
    kpe = proj[:, P_KPE:P_KPE + LANE].astype(BF16)
    k = (jnp.dot(ckvn, wk_ref[...], preferred_element_type=F32)
         + jnp.dot(kpe, ppe_ref[...], preferred_element_type=F32))
    k = _group_norm(k, g96_ref, 1.0 / MLA_QK) * gains[3:4, :]
    if use_rope:
        k = _rope(k, rm_ref, cm_ref, sm_ref)
    mk_ref[0] = k.astype(BF16)
    _store_value_heads(mv_ref, jnp.dot(ckvn, wv_ref[...], preferred_element_type=F32))

    qd = _group_norm(proj[:, P_DQ:P_DQ + BR_W], g32_ref, 1.0 / DIFF_DIM) * gains[4:5, :BR_W]
    kd = _group_norm(proj[:, P_DK:P_DK + BR_W], g32_ref, 1.0 / DIFF_DIM) * gains[5:6, :BR_W]
    if use_rope:
        qd = _rope(qd, rd_ref, cd_ref, sd_ref)
        kd = _rope(kd, rd_ref, cd_ref, sd_ref)
    dq_ref[0] = qd.astype(BF16)
    dk_ref[0] = kd.astype(BF16)
    _store_value_heads(dv_ref, proj[:, P_DV:P_DV + BR_W])

    qn = _group_norm(proj[:, P_NQ:P_NQ + BR_W], g64_ref, 1.0 / HEAD_DIM) * gains[6:7, :BR_W]
    kn = _group_norm(proj[:, P_NK:P_NK + BR_W], g64_ref, 1.0 / HEAD_DIM) * gains[7:8, :BR_W]
    nq_ref[0] = qn.astype(BF16)
    nk_ref[0] = kn.astype(BF16)
    _store_value_heads(nv_ref, proj[:, P_NV:P_NV + BR_W])


def _proj_call(x, mods, mod_row, lw, tabs, use_rope):
    b, s, d = x.shape
    t = min(s, 512)
    grid = (b, s // t)
    if mod_row is None:
        mod_map = lambda i, j: (i, 0, 0)
    else:
        mod_map = lambda i, j: (mod_row, 0, 0)
    tok = lambda w: pl.BlockSpec((1, t, w), lambda i, j: (i, j, 0))
    tab = lambda w: pl.BlockSpec((t, w), lambda i, j: (j, 0))
    in_specs = [
        tok(d),
        pl.BlockSpec((1, 6, d), mod_map),
        _const_spec((1, d)),
        _const_spec((d, PROJ_W)),
        _const_spec((8, 512)),
        _const_spec((256, 512)),
        _const_spec((KV_LORA, 512)),
        _const_spec((LANE, 512)),
        _const_spec((KV_LORA, BR_W)),
        _const_spec((512, 512)),
        _const_spec((BR_W, BR_W)),
        _const_spec((BR_W, BR_W)),
        _const_spec((512, 512)),
        _const_spec((BR_W, BR_W)),
        tab(512), tab(512), tab(BR_W), tab(BR_W),
    ]
    widths = [CONV_CH, 512, 512, BR_W, BR_W, BR_W, BR_W, BR_W, BR_W, BR_W]
    dtypes = [F32] + [BF16] * 9
    out_specs = [tok(w) for w in widths]
    out_shape = [jax.ShapeDtypeStruct((b, s, w), dt) for w, dt in zip(widths, dtypes)]
    for i in (3, 6, 9):
        out_specs[i] = pl.BlockSpec((1, N_HEADS, t, LANE), lambda i_, j: (i_, 0, j, 0))
        out_shape[i] = jax.ShapeDtypeStruct((b, N_HEADS, s, LANE), BF16)
    return pl.pallas_call(
        functools.partial(_proj_kernel, use_rope=use_rope),
        grid=grid, in_specs=in_specs, out_specs=out_specs, out_shape=out_shape,
        compiler_params=_cparams(("parallel", "parallel")),
        name="proj",
    )(x, mods, lw["n1g"], lw["w_in"], lw["gains"], lw["wuq"], lw["wk"], lw["ppe"], lw["wv"],
      lw["g96"], lw["g32"], lw["g64"], lw["rm"], lw["rd"], tabs[0], tabs[1], tabs[2], tabs[3])


CONV_TILE = 128
CONV_PAD = 16


def _conv_kernel(u_ref, w_ref, cb_ref, lg_ref, lb_ref, o_ref, pad_ref, *, seq):
    zeros = jnp.zeros((CONV_PAD, CONV_CH), F32)
    pad_ref[0:CONV_PAD, :] = zeros
    pad_ref[CONV_PAD + seq:2 * CONV_PAD + seq, :] = zeros

    def fill(i, carry):
        base = pl.multiple_of(i * CONV_TILE, CONV_TILE)
        pad_ref[pl.ds(base + CONV_PAD, CONV_TILE), :] = u_ref[0, pl.ds(base, CONV_TILE), :]
        return carry

    lax.fori_loop(0, seq // CONV_TILE, fill, 0)
    w = w_ref[...]
    cb, lg, lb = cb_ref[...], lg_ref[...], lb_ref[...]

    def tile(i, carry):
        base = pl.multiple_of(i * CONV_TILE, CONV_TILE)
        win = pad_ref[pl.ds(base, CONV_TILE + 2 * CONV_PAD), :]
        acc = jnp.zeros((CONV_TILE, CONV_CH), F32)
        for r in range(SUBLANE):
            shifted = win[r:r + CONV_TILE + 2 * CONV_PAD - SUBLANE, :]
            for j in range(CONV_WIDTH):
                if (j + 1) % SUBLANE == r:
                    a = j + 1 - r
                    acc = acc + shifted[a:a + CONV_TILE, :] * w[j:j + 1, :]
        c = acc + cb
        mu = jnp.mean(c, axis=-1, keepdims=True)
        cc = c - mu
        var = jnp.mean(cc * cc, axis=-1, keepdims=True)
        y = cc * lax.rsqrt(var + EPS) * lg + lb
        o_ref[0, pl.ds(base, CONV_TILE), :] = (y * _sigmoid(y)).astype(BF16)
        return carry

    lax.fori_loop(0, seq // CONV_TILE, tile, 0)


def _conv_call(u, lw):
    b, s, ch = u.shape
    return pl.pallas_call(
        functools.partial(_conv_kernel, seq=s),
        grid=(b,),
        in_specs=[
            pl.BlockSpec((1, s, ch), lambda i: (i, 0, 0)),
            _const_spec((32, ch)), _const_spec((1, ch)), _const_spec((1, ch)), _const_spec((1, ch)),
        ],
        out_specs=pl.BlockSpec((1, s, ch), lambda i: (i, 0, 0)),
        out_shape=jax.ShapeDtypeStruct((b, s, ch), BF16),
        scratch_shapes=[pltpu.VMEM((s + 2 * CONV_PAD, ch), F32)],
        compiler_params=_cparams(("parallel",)),
        name="conv",
    )(u, lw["conv_w"], lw["conv_b"], lw["conv_ln_g"], lw["conv_ln_b"])


def _lane_mask(width, lo, hi):
    lane = lax.broadcasted_iota(jnp.int32, (1, width), 1)
    return (lane >= lo) & (lane < hi)


def _softmax_pv(qw, kt, v1):
    s = jnp.dot(qw, kt, preferred_element_type=F32)
    m = jnp.max(s, axis=-1, keepdims=True)
    p = jnp.exp2(s - m).astype(BF16)
    o = jnp.dot(p, v1, preferred_element_type=F32)
    return o * (1.0 / o[:, V_HEAD:V_HEAD + 1])


def _attn_kernel(q_ref, kt_ref, v_ref, lam_ref, g64_ref, sg_ref, o_ref, *, maps, diff, lam_init):
    if diff:
        lv = lam_ref[...]
        lam = (jnp.exp(jnp.sum(lv[0:1] * lv[1:2], axis=-1, keepdims=True))
               - jnp.exp(jnp.sum(lv[2:3] * lv[3:4], axis=-1, keepdims=True)) + lam_init)
    heads = []
    for h in range(N_HEADS):
        outs = []
        for (w0, lo, hi) in maps[h]:
            qw = q_ref[0, :, w0:w0 + LANE]
            if (lo, hi) != (0, LANE):
                qw = jnp.where(_lane_mask(LANE, lo, hi), qw, jnp.zeros_like(qw))
            outs.append(_softmax_pv(qw, kt_ref[0, w0:w0 + LANE, :], v_ref[0, h]))
        heads.append(outs[0] - lam * outs[1] if diff else outs[0])
    low = _lane_mask(LANE, 0, V_HEAD)
    acc = jnp.concatenate(
        [jnp.where(low, heads[h], pltpu.roll(heads[h + 1], V_HEAD, axis=1)) for h in range(0, N_HEADS, 2)],
        axis=1)
    if diff:
        acc = _group_norm(acc, g64_ref, 1.0 / DIFF_V) * sg_ref[...]
    o_ref[0] = acc.astype(BF16)


MAPS_MLA = tuple(((LANE * h, 0, LANE),) for h in range(N_HEADS))
MAPS_DIFF = tuple(tuple((LANE * (h // 2), 64 * (h % 2) + 32 * c, 64 * (h % 2) + 32 * c + 32) for c in range(2))
                  for h in range(N_HEADS))
MAPS_NA = tuple(((LANE * (h // 2), 64 * (h % 2), 64 * (h % 2) + 64),) for h in range(N_HEADS))


def _attn_call(q, kt, v, lw, maps, diff=False, lam_init=0.0):
    b, s, wq = q.shape
    sk = kt.shape[2]
    tq = min(s, ATTN_TQ)
    return pl.pallas_call(
        functools.partial(_attn_kernel, maps=maps, diff=diff, lam_init=lam_init),
        grid=(b, s // tq),
        in_specs=[
            pl.BlockSpec((1, tq, wq), lambda i, j: (i, j, 0)),
            pl.BlockSpec((1, wq, sk), lambda i, j: (i, 0, 0)),
            pl.BlockSpec((1, N_HEADS, sk, LANE), lambda i, j: (i, 0, 0, 0)),
            _const_spec((4, DIFF_DIM)),
            _const_spec((BR_W, BR_W)),
            _const_spec((1, BR_W)),
        ],
        out_specs=pl.BlockSpec((1, tq, BR_W), lambda i, j: (i, j, 0)),
        out_shape=jax.ShapeDtypeStruct((b, s, BR_W), BF16),
        compiler_params=_cparams(("parallel", "parallel")),
        name="attn_diff" if diff else "attn",
    )(q, kt, v, lw["diff_lam"], lw["g64"], lw["subln"])


_NT = (((1,), (1,)), ((), ()))


def _na_kernel(q_ref, k_ref, v_ref, kc_ref, vc_ref, bias_ref, o_ref, *, rows):
    nj = rows // 2
    band = NA_BAND_ROWS * GRID_W
    pair = 2 * GRID_W
    for sub in range(NA_PAIRS):
        j = pl.program_id(1) * NA_PAIRS + sub
        start = jnp.clip(2 * j - 4, 0, rows - NA_BAND_ROWS)
        base = pl.multiple_of(start * GRID_W, 2 * GRID_W)
        cls = jnp.where(j < 2, j, jnp.where(j >= nj - 2, j - (nj - 2) + 3, 2))
        keys = jnp.concatenate([k_ref[0, pl.ds(base, band), :], kc_ref[0]], axis=0)
        q = q_ref[0, sub * pair:(sub + 1) * pair, :]
        heads = []
        for h in range(N_HEADS):
            (w0, lo, hi), = MAPS_NA[h]
            qw = q[:, w0:w0 + LANE]
            qm = jnp.where(_lane_mask(LANE, lo, hi), qw, jnp.zeros_like(qw))
            s = lax.dot_general(qm, keys[:, w0:w0 + LANE], _NT, preferred_element_type=F32) + bias_ref[cls, h]
            p = jnp.exp2(s - jnp.max(s, axis=-1, keepdims=True)).astype(BF16)
            vals = jnp.concatenate([v_ref[0, h, pl.ds(base, band), :], vc_ref[0, h]], axis=0)
            o = jnp.dot(p, vals, preferred_element_type=F32)
            heads.append(o * (1.0 / o[:, V_HEAD:V_HEAD + 1]))
        low = _lane_mask(LANE, 0, V_HEAD)
        acc = jnp.concatenate(
            [jnp.where(low, heads[h], pltpu.roll(heads[h + 1], V_HEAD, axis=1)) for h in range(0, N_HEADS, 2)],
            axis=1)
        o_ref[0, sub * pair:(sub + 1) * pair, :] = acc.astype(BF16)


def _na_call(q, k, v, kc, vc, bias):
    b, s, w = q.shape
    n_ctx = kc.shape[1]
    rows = s // GRID_W
    tq = 2 * GRID_W * NA_PAIRS
    return pl.pallas_call(
        functools.partial(_na_kernel, rows=rows),
        grid=(b, rows // (2 * NA_PAIRS)),
        in_specs=[
            pl.BlockSpec((1, tq, w), lambda i, j: (i, j, 0)),
            pl.BlockSpec((1, s, w), lambda i, j: (i, 0, 0)),
            pl.BlockSpec((1, N_HEADS, s, LANE), lambda i, j: (i, 0, 0, 0)),
            pl.BlockSpec((1, n_ctx, w), lambda i, j: (i, 0, 0)),
            pl.BlockSpec((1, N_HEADS, n_ctx, LANE), lambda i, j: (i, 0, 0, 0)),
            _const_spec(bias.shape),
        ],
        out_specs=pl.BlockSpec((1, tq, w), lambda i, j: (i, j, 0)),
        out_shape=jax.ShapeDtypeStruct((b, s, w), BF16),
        compiler_params=_cparams(("parallel", "arbitrary")),
        name="na",
    )(q, k, v, kc, vc, bias)


def _merge_kernel(x_ref, mod_ref, n1g_ref, n2g_ref, uc_ref, om_ref, od_ref, on_ref,
                  gw_ref, gb_ref, wc_ref, wm_ref, wd_ref, wn_ref, wo_ref, rwt_ref, rb_ref,
                  x1_ref, h2_ref, ids_ref, wts_ref):
    x = x_ref[0]
    mod = mod_ref[0]
    h = _modulate(x, n1g_ref[...], mod[0:1], mod[1:2]).astype(BF16)
    y = jnp.zeros(x.shape, F32)
    branches = ((uc_ref, wc_ref), (om_ref, wm_ref), (od_ref, wd_ref), (on_ref, wn_ref))
    for i, (o_ref, w_ref) in enumerate(branches):
        lo = D_MODEL * i
        g = _sigmoid(jnp.dot(h, gw_ref[:, lo:lo + D_MODEL], preferred_element_type=F32)
                     + gb_ref[:, lo:lo + D_MODEL])
        y = y + g * jnp.dot(o_ref[0], w_ref[...], preferred_element_type=F32)
    out = jnp.dot(y.astype(BF16), wo_ref[...], preferred_element_type=F32)
    x1 = x + mod[2:3] * out
    x1_ref[0] = x1
    h2 = _modulate(x1, n2g_ref[...], mod[3:4], mod[4:5])
    for q, piece in enumerate(_pack_row(h2)):
        h2_ref[q, 0] = piece

    logits = lax.dot_general(rwt_ref[...], h2, _NT, preferred_element_type=F32,
                             precision=lax.Precision.HIGHEST) + rb_ref[...]
    eidx = lax.broadcasted_iota(jnp.int32, logits.shape, 0).astype(F32)
    vals, idxs = [], []
    cur = logits
    for _ in range(TOP_K):
        m = jnp.max(cur, axis=0, keepdims=True)
        idx = jnp.min(jnp.where(cur == m, eidx, float(N_EXPERTS)), axis=0, keepdims=True)
        vals.append(m)
        idxs.append(idx)
        cur = jnp.where(eidx == idx, -jnp.inf, cur)
    es = [jnp.exp(vk - vals[0]) for vk in vals]
    den = es[0] + es[1] + es[2] + es[3]
    ids_ref[0] = jnp.concatenate(idxs, axis=0).astype(jnp.int32)
    wts_ref[0] = jnp.concatenate([e / den for e in es], axis=0)


def _merge_call(x, mods, mod_row, lw, uc, om, od, on):
    b, s, d = x.shape
    t = min(s, 512)
    if mod_row is None:
        mod_map = lambda i, j: (i, 0, 0)
    else:
        mod_map = lambda i, j: (mod_row, 0, 0)
    tok = lambda w: pl.BlockSpec((1, t, w), lambda i, j: (i, j, 0))
    rt = pl.BlockSpec((1, TOP_K, t), lambda i, j: (i, 0, j))
    return pl.pallas_call(
        _merge_kernel,
        grid=(b, s // t),
        in_specs=[
            tok(d), pl.BlockSpec((1, 6, d), mod_map), _const_spec((1, d)), _const_spec((1, d)),
            tok(BR_W), tok(BR_W), tok(BR_W), tok(BR_W),
            _const_spec((d, N_BRANCH * d)), _const_spec((1, N_BRANCH * d)),
            _const_spec((BR_W, d)), _const_spec((BR_W, d)), _const_spec((BR_W, d)), _const_spec((BR_W, d)),
            _const_spec((d, d)), _const_spec((N_EXPERTS, d)), _const_spec((N_EXPERTS, 1)),
        ],
        out_specs=[tok(d), pl.BlockSpec((ROW_PARTS, 1, t, ROW_Q), lambda i, j: (0, i, j, 0)), rt, rt],
        out_shape=[jax.ShapeDtypeStruct((b, s, d), F32), jax.ShapeDtypeStruct((ROW_PARTS, b, s, ROW_Q), jnp.int32),
                   jax.ShapeDtypeStruct((b, TOP_K, s), jnp.int32), jax.ShapeDtypeStruct((b, TOP_K, s), F32)],
        compiler_params=_cparams(("parallel", "parallel")),
        name="merge",
    )(x, mods, lw["n1g"], lw["n2g"], uc, om, od, on, lw["gate_w"], lw["gate_b"],
      lw["conv_out"], lw["mla_out"], lw["diff_out"], lw["na_out"], lw["w_o"], lw["router_wt"], lw["router_b"])


SC_WINDOW = 128
ROW_Q = D_MODEL // 4
ROW_PARTS = 2
HALF_D = D_MODEL // 2


def _pack_bf16_pair(a, b):
    ua = lax.bitcast_convert_type(a.astype(BF16).astype(F32), jnp.int32)
    ub = lax.bitcast_convert_type(b.astype(BF16).astype(F32), jnp.int32)
    return ua | lax.shift_right_logical(ub, jnp.int32(16))


def _unpack_bf16_pair(w):
    a = lax.bitcast_convert_type(w & jnp.int32(-65536), F32)
    b = lax.bitcast_convert_type(lax.shift_left(w, jnp.int32(16)), F32)
    return a, b


def _pack_row(x):
    w = _pack_bf16_pair(x[:, :HALF_D], x[:, HALF_D:])
    return [w[:, ROW_Q * q:ROW_Q * (q + 1)] for q in range(ROW_PARTS)]


def _unpack_row(pieces):
    ab = [_unpack_bf16_pair(w) for w in pieces]
    return jnp.concatenate([a for a, _ in ab] + [b for _, b in ab], axis=1)


def _route_slots(ids, tile):
    n = ids.shape[1]
    p = TOP_K * n
    e = ids.reshape(p)
    onehot = (e[:, None] == jnp.arange(N_EXPERTS, dtype=jnp.int32)[None, :])
    chunk = 512
    oh3 = onehot.astype(F32).reshape(p // chunk, chunk, N_EXPERTS)
    within = jnp.einsum("ij,cje->cie", jnp.tril(jnp.ones((chunk, chunk), F32)), oh3)
    totals = within[:, -1, :]
    before = jnp.cumsum(totals, axis=0) - totals
    csum = (within + before[:, None, :]).reshape(p, N_EXPERTS).astype(jnp.int32)
    onehot = onehot.astype(jnp.int32)
    counts = csum[-1]
    padded = ((counts + tile - 1) // tile) * tile
    gend = jnp.cumsum(padded)
    gstart = gend - padded
    slot = jnp.sum(onehot * (csum - 1 + gstart[None, :]), axis=1).astype(jnp.int32)
    n_tiles = p // tile + N_EXPERTS
    tile_start = jnp.arange(n_tiles, dtype=jnp.int32) * tile
    texp = jnp.sum((tile_start[:, None] >= gend[None, :]).astype(jnp.int32), axis=1)
    texp = jnp.minimum(texp, N_EXPERTS - 1)
    nreal = jnp.clip(gstart[texp] + counts[texp] - tile_start, 0, tile)
    nreal = jnp.where(tile_start < gend[-1], nreal, 0).astype(jnp.int32)
    return slot, texp, nreal, n_tiles


def _sc_mesh():
    return plsc.VectorSubcoreMesh(core_axis_name="c", subcore_axis_name="s")


def _sc_dispatch(hq, idx, n_slots):
    parts, n, w = hq.shape
    src = hq.reshape(parts * n, w)
    m = idx.shape[0]
    blocks_per_q = n // SC_WINDOW
    per_q = TOP_K * blocks_per_q

    @pl.kernel(out_type=jax.ShapeDtypeStruct((parts * n_slots, w), hq.dtype), mesh=_sc_mesh(), scratch_types=[])
    def kern(x_hbm, i_hbm, o_hbm):
        def body(x_vmem, i_vmem):
            pltpu.sync_copy(x_vmem, o_hbm.at[i_vmem.at[0]])

        pltpu.emit_pipeline(
            body,
            grid=(m // SC_WINDOW,),
            in_specs=[
                pl.BlockSpec((SC_WINDOW, w), index_map=lambda i: ((i // per_q) * blocks_per_q + i % blocks_per_q, 0)),
                pl.BlockSpec((1, SC_WINDOW), index_map=lambda i: (0, i)),
            ],
            out_specs=[],
            core_axis_name=("c", "s"),
            dimension_semantics=(pltpu.PARALLEL,),
        )(x_hbm, i_hbm)

    return kern(src, idx.reshape(1, m)).reshape(parts, n_slots, w)


def _sc_collect(ys, idx):
    parts, n_slots, w = ys.shape
    src = ys.reshape(parts * n_slots, w)
    m = idx.shape[0]

    @pl.kernel(out_type=jax.ShapeDtypeStruct((m, w), ys.dtype), mesh=_sc_mesh(), scratch_types=[])
    def kern(x_hbm, i_hbm, o_hbm):
        def body(i_vmem, o_vmem):
            pltpu.sync_copy(x_hbm.at[i_vmem.at[0]], o_vmem)

        pltpu.emit_pipeline(
            body,
            grid=(m // SC_WINDOW,),
            in_specs=[pl.BlockSpec((1, SC_WINDOW), index_map=lambda i: (0, i))],
            out_specs=[pl.BlockSpec((SC_WINDOW, w), index_map=lambda i: (i, 0))],
            core_axis_name=("c", "s"),
            dimension_semantics=(pltpu.PARALLEL,),
        )(i_hbm, o_hbm)

    return kern(src, idx.reshape(1, m))


def _ffn_sorted_kernel(texp_ref, nreal_ref, x_ref, wgu_ref, bgu_ref, wd_ref, bd_ref, y_ref, wgu_bf, wd_bf, *, tile):
    i = pl.program_id(0)
    nreal = nreal_ref[i]

    @pl.when((nreal > 0) & ((i == 0) | (texp_ref[i] != texp_ref[jnp.maximum(i - 1, 0)])))
    def _():
        wgu_bf[...] = wgu_ref[0].astype(BF16)
        wd_bf[...] = wd_ref[0].astype(BF16)

    @pl.when(nreal > 0)
    def _():
        x = _unpack_row([x_ref[q] for q in range(ROW_PARTS)])
        row = lax.broadcasted_iota(jnp.int32, (tile, 1), 0)
        x = jnp.where(row < nreal, x, 0.0).astype(BF16)
        gu = jnp.dot(x, wgu_bf[...], preferred_element_type=F32) + bgu_ref[0]
        g = jnp.minimum(gu[:, :D_FF], SWIGLU_LIMIT)
        u = jnp.clip(gu[:, D_FF:], -SWIGLU_LIMIT, SWIGLU_LIMIT)
        act = ((u + 1.0) * (g * _sigmoid(SWIGLU_ALPHA * g))).astype(BF16)
        y = jnp.dot(act, wd_bf[...], preferred_element_type=F32) + bd_ref[0]
        for q, piece in enumerate(_pack_row(y)):
            y_ref[q] = piece

    @pl.when(nreal == 0)
    def _():
        y_ref[...] = jnp.zeros(y_ref.shape, jnp.int32)


def _ffn_sorted_call(xs, texp, nreal, lw, tile):
    _, n_slots, w = xs.shape
    n_tiles = n_slots // tile
    d = D_MODEL
    off = lw["exp_off"]
    grid_spec = pltpu.PrefetchScalarGridSpec(
        num_scalar_prefetch=2,
        grid=(n_tiles,),
        in_specs=[
            pl.BlockSpec((ROW_PARTS, tile, w), lambda i, te, nr: (0, i, 0)),
            pl.BlockSpec((1, d, 2 * D_FF), lambda i, te, nr: (te[i] + off, 0, 0)),
            pl.BlockSpec((1, 1, 2 * D_FF), lambda i, te, nr: (te[i] + off, 0, 0)),
            pl.BlockSpec((1, D_FF, d), lambda i, te, nr: (te[i] + off, 0, 0)),
            pl.BlockSpec((1, 1, d), lambda i, te, nr: (te[i] + off, 0, 0)),
        ],
        out_specs=pl.BlockSpec((ROW_PARTS, tile, w), lambda i, te, nr: (0, i, 0)),
        scratch_shapes=[pltpu.VMEM((d, 2 * D_FF), BF16), pltpu.VMEM((D_FF, d), BF16)],
    )
    return pl.pallas_call(
        functools.partial(_ffn_sorted_kernel, tile=tile),
        grid_spec=grid_spec,
        out_shape=jax.ShapeDtypeStruct((ROW_PARTS, n_slots, w), jnp.int32),
        compiler_params=_cparams(("arbitrary",)),
        name="moe_ffn",
    )(texp, nreal, xs, lw["exp_w_gu"], lw["exp_b_gu"], lw["exp_w_down"], lw["exp_b_down"])


def _combine_q_kernel(x1_ref, mod_ref, w_ref, y_ref, o_ref):
    w = w_ref[0]
    g2 = mod_ref[0][5:6]
    for q in range(ROW_PARTS):
        acc_a, acc_b = None, None
        for k in range(TOP_K):
            a, b = _unpack_bf16_pair(y_ref[q, k])
            wk = w[:, k:k + 1]
            acc_a = wk * a if acc_a is None else acc_a + wk * a
            acc_b = wk * b if acc_b is None else acc_b + wk * b
        for lo, acc in ((ROW_Q * q, acc_a), (HALF_D + ROW_Q * q, acc_b)):
            o_ref[0, :, lo:lo + ROW_Q] = x1_ref[0, :, lo:lo + ROW_Q] + g2[:, lo:lo + ROW_Q] * acc


def _combine_q_call(x1, mods, mod_row, wts, y, tok_off):
    b, s, d = x1.shape
    t = min(s, 512)
    nt = s // t
    blk_off = tok_off // t
    if mod_row is None:
        mod_map = lambda i, j: (i, 0, 0)
    else:
        mod_map = lambda i, j: (mod_row, 0, 0)
    wts = wts.transpose(0, 2, 1)
    return pl.pallas_call(
        _combine_q_kernel,
        grid=(b, nt),
        in_specs=[
            pl.BlockSpec((1, t, d), lambda i, j: (i, j, 0)),
            pl.BlockSpec((1, 6, d), mod_map),
            pl.BlockSpec((1, t, TOP_K), lambda i, j: (i, j, 0)),
            pl.BlockSpec((ROW_PARTS, TOP_K, t, ROW_Q), lambda i, j: (0, 0, blk_off + i * nt + j, 0)),
        ],
        out_specs=pl.BlockSpec((1, t, d), lambda i, j: (i, j, 0)),
        out_shape=jax.ShapeDtypeStruct((b, s, d), F32),
        compiler_params=_cparams(("parallel", "parallel")),
        name="moe_combine",
    )(x1, mods, wts, y)


def _moe_sc(streams, mods, lw):
    sizes = [st[0].shape[0] * st[0].shape[1] for st in streams]
    n = sum(sizes)
    tile = 512 if TOP_K * n >= 512 * N_EXPERTS * 4 else 256
    hq = jnp.concatenate([st[1].reshape(ROW_PARTS, m, ROW_Q) for st, m in zip(streams, sizes)], axis=1)
    ids = jnp.concatenate([st[2].transpose(1, 0, 2).reshape(TOP_K, m) for st, m in zip(streams, sizes)], axis=1)
    slot, texp, nreal, n_tiles = _route_slots(ids, tile)
    n_slots = n_tiles * tile
    idx = (slot[None, :] + (jnp.arange(ROW_PARTS, dtype=jnp.int32) * n_slots)[:, None]).reshape(-1)
    xs = _sc_dispatch(hq, idx, n_slots)
    ys = _ffn_sorted_call(xs, texp, nreal, lw, tile)
    y = _sc_collect(ys, idx).reshape(ROW_PARTS, TOP_K, n, ROW_Q)
    outs, off = [], 0
    for (x1, _, _, wts, mod_row), m in zip(streams, sizes):
        outs.append(_combine_q_call(x1, mods, mod_row, wts, y, off))
        off += m
    return outs


def _layer_weights(l, p, lam_init):
    w = p["w_in"][l].astype(BF16)
    d = w.shape[0]
    zcols = lambda n: jnp.zeros((d, n), w.dtype)
    regroup = lambda blk: blk.reshape(d, N_HEADS, 3, HEAD_DIM).transpose(0, 2, 1, 3).reshape(d, 3 * BR_W)
    w_in = jnp.concatenate([
        w[:, :A_IN],
        w[:, OFF_B:OFF_B + Q_LORA], zcols(P_CKV - P_CQ - Q_LORA),
        w[:, OFF_B + Q_LORA:OFF_B + Q_LORA + KV_LORA],
        w[:, OFF_B + Q_LORA + KV_LORA:OFF_C], zcols(P_DQ - P_KPE - QK_ROPE),
        regroup(w[:, OFF_C:OFF_D]), regroup(w[:, OFF_D:]),
    ], axis=1)
    assert w_in.shape[1] == PROJ_W

    def head_slots(w3, slot):
        w3 = jnp.pad(w3, ((0, 0), (0, 0), (0, slot - w3.shape[2])))
        return w3.reshape(w3.shape[0], N_HEADS * slot)

    wuq = head_slots(p["mla_w_uq"][l].reshape(Q_LORA, N_HEADS, MLA_QK), LANE)
    wuq = jnp.pad(wuq, ((0, 256 - Q_LORA), (0, 0)))
    wukv = p["mla_w_ukv"][l].reshape(KV_LORA, N_HEADS, QK_NOPE + V_HEAD)
    wk = head_slots(wukv[:, :, :QK_NOPE], LANE)
    wv = head_slots(wukv[:, :, QK_NOPE:], V_HEAD)
    ppe = np.zeros((LANE, 512), np.float32)
    for h in range(N_HEADS):
        for i in range(QK_ROPE):
            ppe[i, h * LANE + QK_NOPE + i] = 1.0

    def slot_gain(g, scale):
        g = jnp.concatenate([g * scale, jnp.zeros((LANE - MLA_QK,), F32)])
        return jnp.tile(g, N_HEADS)

    def row512(v):
        return jnp.concatenate([v, jnp.zeros((512 - v.shape[0],), F32)])

    gains = jnp.stack([
        row512(p["mla_cq_g"][l]),
        slot_gain(p["mla_qn_g"][l], MLA_QK ** -0.5 * LOG2E),
        row512(p["mla_ckv_g"][l]),
        slot_gain(p["mla_kn_g"][l], 1.0),
        row512(jnp.tile(p["diff_qn_g"][l], 2 * N_HEADS) * DIFF_DIM ** -0.5 * LOG2E),
        row512(jnp.tile(p["diff_kn_g"][l], 2 * N_HEADS)),
        row512(jnp.tile(p["na_qn_g"][l], N_HEADS) * HEAD_DIM ** -0.5 * LOG2E),
        row512(jnp.tile(p["na_kn_g"][l], N_HEADS)),
    ])
    conv_w = jnp.concatenate([p["conv_w"][l], jnp.zeros((1, CONV_CH), F32)], axis=0)
    return dict(
        n1g=p["norm1_g"][l][None, :], n2g=p["norm2_g"][l][None, :],
        w_in=w_in, gains=gains,
        wuq=wuq.astype(BF16), wk=wk.astype(BF16), wv=wv.astype(BF16), ppe=jnp.asarray(ppe, BF16),
        g96=jnp.asarray(_group_ones(512, LANE, MLA_QK), BF16),
        g32=jnp.asarray(_group_ones(BR_W, DIFF_DIM, DIFF_DIM), BF16),
        g64=jnp.asarray(_group_ones(BR_W, HEAD_DIM, HEAD_DIM), BF16),
        rm=jnp.asarray(_rot_matrix(512, LANE, QK_NOPE, QK_ROPE // 2), BF16),
        rd=jnp.asarray(_rot_matrix(BR_W, DIFF_DIM, 0, DIFF_DIM // 2), BF16),
        conv_w=conv_w, conv_b=p["conv_b"][l][None, :],
        conv_ln_g=p["conv_ln_g"][l][None, :], conv_ln_b=p["conv_ln_b"][l][None, :],
        diff_lam=p["diff_lam"][l],
        subln=(jnp.tile(p["diff_subln_g"][l], N_HEADS) * (1.0 - lam_init))[None, :],
        gate_w=p["gate_w"][l].astype(BF16), gate_b=p["gate_b"][l][None, :],
        conv_out=p["conv_out"][l].astype(BF16), mla_out=p["mla_out"][l].astype(BF16),
        diff_out=p["diff_out"][l].astype(BF16), na_out=p["na_out"][l].astype(BF16),
        w_o=p["w_o"][l].astype(BF16),
        router_wt=p["router_w"][l].T, router_b=p["router_b"][l][:, None],
        exp_off=l * N_EXPERTS,
        exp_w_gu=p["exp_w_gu"].reshape((-1,) + p["exp_w_gu"].shape[2:]),
        exp_b_gu=p["exp_b_gu"].reshape(-1, 1, 2 * D_FF),
        exp_w_down=p["exp_w_down"].reshape((-1,) + p["exp_w_down"].shape[2:]),
        exp_b_down=p["exp_b_down"].reshape(-1, 1, D_MODEL),
    )


def _kt(kc, k):
    return jnp.concatenate([kc, k], axis=1).transpose(0, 2, 1)


def kernel(x, c, ctx, c_ctx, ada_w, ada_b, norm1_g, norm2_g, w_in, conv_w, conv_b, conv_ln_g, conv_ln_b, conv_out, mla_cq_g, mla_ckv_g, mla_w_uq, mla_w_ukv, mla_qn_g, mla_kn_g, mla_out, diff_qn_g, diff_kn_g, diff_lam, diff_subln_g, diff_out, na_qn_g, na_kn_g, na_rpb, na_out, gate_w, gate_b, w_o, router_w, router_b, exp_w_gu, exp_b_gu, exp_w_down, exp_b_down):
    p = dict(norm1_g=norm1_g, norm2_g=norm2_g, w_in=w_in, conv_w=conv_w, conv_b=conv_b,
             conv_ln_g=conv_ln_g, conv_ln_b=conv_ln_b, conv_out=conv_out, mla_cq_g=mla_cq_g,
             mla_ckv_g=mla_ckv_g, mla_w_uq=mla_w_uq, mla_w_ukv=mla_w_ukv, mla_qn_g=mla_qn_g,
             mla_kn_g=mla_kn_g, mla_out=mla_out, diff_qn_g=diff_qn_g, diff_kn_g=diff_kn_g,
             diff_lam=diff_lam, diff_subln_g=diff_subln_g, diff_out=diff_out, na_qn_g=na_qn_g,
             na_kn_g=na_kn_g, na_out=na_out, gate_w=gate_w, gate_b=gate_b, w_o=w_o,
             router_w=router_w, router_b=router_b, exp_w_gu=exp_w_gu, exp_b_gu=exp_b_gu,
             exp_w_down=exp_w_down, exp_b_down=exp_b_down)
    b, s, d = x.shape
    n_ctx = ctx.shape[1]
    depth = ada_w.shape[0]
    rows = s // GRID_W
    assert d == D_MODEL and s % (2 * GRID_W) == 0 and rows >= NA_BAND_ROWS and n_ctx % LANE == 0

    mod_rows = -(-(b + 1) // 8) * 8
    cs = jnp.concatenate([c, c_ctx[None, :], jnp.zeros((mod_rows - b - 1, d), F32)], axis=0)
    mods_all = _ada_call(cs, ada_w, ada_b).reshape(depth, mod_rows, 6, d)

    tabs_x = _rope_lane_tables(s)
    tabs_c = (jnp.ones((n_ctx, 512), F32), jnp.zeros((n_ctx, 512), F32),
              jnp.ones((n_ctx, BR_W), F32), jnp.zeros((n_ctx, BR_W), F32))

    xc = ctx
    for l in range(depth):
        last = l == depth - 1
        lam_init = 0.8 - 0.6 * math.exp(-0.3 * l)
        lw = _layer_weights(l, p, lam_init)
        mods = mods_all[l]
        bias = _na_bias_tables(na_rpb[l], rows, n_ctx)

        u, mq, mk, mv, dq, dk, dv, nq, nk, nv = _proj_call(x, mods, None, lw, tabs_x, True)
        uc, mqc, mkc, mvc, dqc, dkc, dvc, nqc, nkc, nvc = _proj_call(xc, mods, b, lw, tabs_c, False)

        y_conv = _conv_call(u, lw)
        y_mla = _attn_call(mq, _kt(mkc, mk), jnp.concatenate([mvc, mv], axis=2), lw, MAPS_MLA)
        y_diff = _attn_call(dq, _kt(dkc, dk), jnp.concatenate([dvc, dv], axis=2), lw, MAPS_DIFF,
                            diff=True, lam_init=lam_init)
        y_na = _na_call(nq, nk, nv, nkc, nvc, bias)
        x1, h2, ids, wts = _merge_call(x, mods, None, lw, y_conv, y_mla, y_diff, y_na)
        streams = [(x1, h2, ids, wts, None)]

        if not last:
            yc_conv = _conv_call(uc, lw)
            yc_mla = _attn_call(mqc, mkc.transpose(0, 2, 1), mvc, lw, MAPS_MLA)
            yc_diff = _attn_call(dqc, dkc.transpose(0, 2, 1), dvc, lw, MAPS_DIFF, diff=True, lam_init=lam_init)
            yc_na = _attn_call(nqc, nkc.transpose(0, 2, 1), nvc, lw, MAPS_NA)
            xc1, h2c, idsc, wtsc = _merge_call(xc, mods, b, lw, yc_conv, yc_mla, yc_diff, yc_na)
            streams.append((xc1, h2c, idsc, wtsc, b))

        outs = _moe_sc(streams, mods, lw)
        x = outs[0]
        if not last:
            xc = outs[1]
    return x
```

```python
import functools
import math

import numpy as np
import jax
import jax.numpy as jnp
from jax import lax
from jax.experimental import pallas as pl
from jax.experimental.pallas import tpu as pltpu
from jax.experimental.pallas import tpu_sc as plsc

F32 = jnp.float32
BF16 = jnp.bfloat16

D_MODEL = 1024
GRID_W = 64
N_BRANCH = 4
N_HEADS = 4
HEAD_DIM = 64
CONV_CH = 256
CONV_WIDTH = 31
Q_LORA = 192
KV_LORA = 128
QK_NOPE = 64
QK_ROPE = 32
V_HEAD = 64
DIFF_DIM = 32
DIFF_V = 2 * DIFF_DIM
NA_KH = 8
NA_KW = 16
ROPE_DIM = 32
ROPE_BASE = 10000.0
N_EXPERTS = 32
TOP_K = 4
D_FF = 1024
SWIGLU_LIMIT = 7.0
SWIGLU_ALPHA = 1.702
EPS = 1e-6
NEG_INF = -1e30

A_IN = 2 * CONV_CH
B_IN = Q_LORA + KV_LORA + QK_ROPE
C_IN = N_HEADS * (4 * DIFF_DIM + DIFF_V)
D_IN = N_HEADS * 3 * HEAD_DIM
OFF_B = A_IN
OFF_C = OFF_B + B_IN
OFF_D = OFF_C + C_IN

LANE = 128
SUBLANE = 8
MLA_QK = QK_NOPE + QK_ROPE
BR_W = N_HEADS * HEAD_DIM
PROJ_W = 2560
NA_BAND_ROWS = 10
ATTN_TQ = 512
NA_PAIRS = 4
LOG2E = math.log2(math.e)
VMEM_LIMIT = 52 * 1024 * 1024

P_A, P_G, P_CQ, P_CKV, P_KPE = 0, 256, 512, 768, 896
P_DQ, P_DK, P_DV = 1024, 1280, 1536
P_NQ, P_NK, P_NV = 1792, 2048, 2304


def _sigmoid(x):
    return 1.0 / (1.0 + jnp.exp(-x))


def _modulate(x, g, shift, scale):
    ms = jnp.mean(x * x, axis=-1, keepdims=True)
    return (x * lax.rsqrt(ms + EPS) * g) * (1.0 + scale) + shift


def _cparams(sem):
    return pltpu.CompilerParams(dimension_semantics=sem, vmem_limit_bytes=VMEM_LIMIT)


def _const_spec(shape):
    n = len(shape)
    return pl.BlockSpec(shape, lambda *_: (0,) * n)


def _group_ones(width, slot, real):
    i = np.arange(width)
    valid = (i % slot) < real
    same = (i[:, None] // slot) == (i[None, :] // slot)
    return (same & valid[:, None] & valid[None, :]).astype(np.float32)


def _rot_matrix(width, slot, start, half):
    r = np.zeros((width, width), np.float32)
    for s0 in range(0, width, slot):
        for i in range(half):
            a, b = s0 + start + i, s0 + start + half + i
            r[b, a] = -1.0
            r[a, b] = 1.0
    return r


def _rope_lane_tables(n_tokens):
    t = jnp.arange(n_tokens, dtype=jnp.int32)
    rows = (t // GRID_W).astype(F32)
    cols = (t % GRID_W).astype(F32)
    axis_dim = ROPE_DIM // 2
    inv = ROPE_BASE ** (-jnp.arange(0, axis_dim, 2, dtype=F32) / axis_dim)
    theta = jnp.concatenate([rows[:, None] * inv, cols[:, None] * inv], axis=-1)
    cos, sin = jnp.cos(theta), jnp.sin(theta)
    half = ROPE_DIM // 2
    ones = jnp.ones((n_tokens, QK_NOPE), F32)
    zeros = jnp.zeros((n_tokens, QK_NOPE), F32)
    pad1 = jnp.ones((n_tokens, LANE - MLA_QK), F32)
    pad0 = jnp.zeros((n_tokens, LANE - MLA_QK), F32)
    cm = jnp.tile(jnp.concatenate([ones, cos, cos, pad1], -1), (1, N_HEADS))
    sm = jnp.tile(jnp.concatenate([zeros, sin, sin, pad0], -1), (1, N_HEADS))
    cd = jnp.tile(jnp.concatenate([cos, cos], -1), (1, 2 * N_HEADS))
    sd = jnp.tile(jnp.concatenate([sin, sin], -1), (1, 2 * N_HEADS))
    assert half * 2 == DIFF_DIM
    return cm, sm, cd, sd


def _na_bias_tables(rpb, rows, n_ctx):
    kh = min(NA_KH, rows)
    nj = rows // 2
    reps = np.array([0, 1, 2, nj - 2, nj - 1])
    n_ro, n_co = 2 * NA_KH - 1, 2 * NA_KW - 1
    start = np.clip(2 * reps - 4, 0, rows - NA_BAND_ROWS)
    r = 2 * reps[:, None] + np.arange(2)[None, :]
    kr = start[:, None] + np.arange(NA_BAND_ROWS)[None, :]
    row_start = np.clip(r - kh // 2, 0, rows - kh)
    vr = (kr[:, None, :] >= row_start[:, :, None]) & (kr[:, None, :] < row_start[:, :, None] + kh)
    ro = np.clip(kr[:, None, :] - r[:, :, None] + NA_KH - 1, 0, n_ro - 1)
    qc = np.arange(GRID_W)
    win_start = np.clip(qc - NA_KW // 2, 0, GRID_W - NA_KW)
    vc = (qc[None, :] >= win_start[:, None]) & (qc[None, :] < win_start[:, None] + NA_KW)
    co = np.clip(qc[None, :] - qc[:, None] + NA_KW - 1, 0, n_co - 1)
    rsel = (ro[..., None] == np.arange(n_ro)).astype(np.float32)
    csel = (co[None] == np.arange(n_co)[:, None, None]).astype(np.float32)
    hi = lax.Precision.HIGHEST
    t1 = jnp.einsum("cqav,hvw->hcqaw", rsel, rpb.astype(F32), precision=hi)
    b = jnp.einsum("hcqaw,wxy->chqxay", t1, csel, precision=hi)
    valid = vr[:, None, :, None, :, None] & vc[None, None, None, :, None, :]
    b = jnp.where(valid, b * LOG2E, NEG_INF)
    b = b.reshape(len(reps), N_HEADS, 2 * GRID_W, NA_BAND_ROWS * GRID_W)
    return jnp.concatenate([b, jnp.zeros(b.shape[:3] + (n_ctx,), F32)], axis=-1)


def _ada_kernel(c_ref, w_ref, b_ref, o_ref):
    c = c_ref[...]
    s = c * _sigmoid(c)
    o_ref[0] = jnp.dot(s, w_ref[0], preferred_element_type=F32,
                       precision=lax.Precision.HIGHEST) + b_ref[0]


def _ada_call(cs, ada_w, ada_b):
    depth, d, n = ada_w.shape
    rows = cs.shape[0]
    tn = 1536
    return pl.pallas_call(
        _ada_kernel,
        grid=(depth, n // tn),
        in_specs=[
            pl.BlockSpec((rows, d), lambda l, j: (0, 0)),
            pl.BlockSpec((1, d, tn), lambda l, j: (l, 0, j)),
            pl.BlockSpec((1, 1, tn), lambda l, j: (l, 0, j)),
        ],
        out_specs=pl.BlockSpec((1, rows, tn), lambda l, j: (l, 0, j)),
        out_shape=jax.ShapeDtypeStruct((depth, rows, n), F32),
        compiler_params=_cparams(("arbitrary", "arbitrary")),
        name="ada",
    )(cs, ada_w, ada_b.reshape(depth, 1, n))


def _group_norm(x, ones_ref, inv_n):
    sq = (x * x).astype(BF16)
    ms = jnp.dot(sq, ones_ref[...], preferred_element_type=F32) * inv_n
    return x * lax.rsqrt(ms + EPS)


def _store_value_heads(ref, v):
    low = _lane_mask(LANE, 0, V_HEAD)
    for h in range(N_HEADS):
        win = v[:, LANE * (h // 2):LANE * (h // 2 + 1)]
        if h % 2:
            win = pltpu.roll(win, V_HEAD, axis=1)
        ref[0, h] = jnp.where(low, win, 1.0).astype(BF16)


def _rope(x, rot_ref, cos_ref, sin_ref):
    rot = jnp.dot(x.astype(BF16), rot_ref[...], preferred_element_type=F32)
    return x * cos_ref[...] + rot * sin_ref[...]


def _proj_kernel(x_ref, mod_ref, n1g_ref, win_ref, gains_ref, wuq_ref, wk_ref, ppe_ref, wv_ref,
                 g96_ref, g32_ref, g64_ref, rm_ref, rd_ref, cm_ref, sm_ref, cd_ref, sd_ref,
                 u_ref, mq_ref, mk_ref, mv_ref, dq_ref, dk_ref, dv_ref, nq_ref, nk_ref, nv_ref,
                 *, use_rope):
    x = x_ref[0]
    mod = mod_ref[0]
    gains = gains_ref[...]
    h = _modulate(x, n1g_ref[...], mod[0:1], mod[1:2]).astype(BF16)
    proj = jnp.dot(h, win_ref[...], preferred_element_type=F32)

    u_ref[0] = proj[:, P_A:P_A + CONV_CH] * _sigmoid(proj[:, P_G:P_G + CONV_CH])

    cq = proj[:, P_CQ:P_CQ + 256]
    ms = jnp.sum(cq * cq, axis=-1, keepdims=True) * (1.0 / Q_LORA)
    cqn = (cq * lax.rsqrt(ms + EPS) * gains[0:1, :256]).astype(BF16)
    q = jnp.dot(cqn, wuq_ref[...], preferred_element_type=F32)
    q = _group_norm(q, g96_ref, 1.0 / MLA_QK) * gains[1:2, :]
    if use_rope:
        q = _rope(q, rm_ref, cm_ref, sm_ref)
    mq_ref[0] = q.astype(BF16)

    ckv = proj[:, P_CKV:P_CKV + KV_LORA]
    ms = jnp.mean(ckv * ckv, axis=-1, keepdims=True)
    ckvn = (ckv * lax.rsqrt(ms + EPS) * gains[2:3, :KV_LORA]).astype(BF16)
    kpe = proj[:, P_KPE:P_KPE + LANE].astype(BF16)
    k = (jnp.dot(ckvn, wk_ref[...], preferred_element_type=F32)
         + jnp.dot(kpe, ppe_ref[...], preferred_element_type=F32))
    k = _group_norm(k, g96_ref, 1.0 / MLA_QK) * gains[3:4, :]
    if use_rope:
        k = _rope(k, rm_ref, cm_ref, sm_ref)
    mk_ref[0] = k.astype(BF16)
    _store_value_heads(mv_ref, jnp.dot(ckvn, wv_ref[...], preferred_element_type=F32))

    qd = _group_norm(proj[:, P_DQ:P_DQ + BR_W], g32_ref, 1.0 / DIFF_DIM) * gains[4:5, :BR_W]
    kd = _group_norm(proj[:, P_DK:P_DK + BR_W], g32_ref, 1.0 / DIFF_DIM) * gains[5:6, :BR_W]
    if use_rope:
        qd = _rope(qd, rd_ref, cd_ref, sd_ref)
        kd = _rope(kd, rd_ref, cd_ref, sd_ref)
    dq_ref[0] = qd.astype(BF16)
    dk_ref[0] = kd.astype(BF16)
    _store_value_heads(dv_ref, proj[:, P_DV:P_DV + BR_W])

    qn = _group_norm(proj[:, P_NQ:P_NQ + BR_W], g64_ref, 1.0 / HEAD_DIM) * gains[6:7, :BR_W]
    kn = _group_norm(proj[:, P_NK:P_NK + BR_W], g64_ref, 1.0 / HEAD_DIM) * gains[7:8, :BR_W]
    nq_ref[0] = qn.astype(BF16)
    nk_ref[0] = kn.astype(BF16)
    _store_value_heads(nv_ref, proj[:, P_NV:P_NV + BR_W])


def _proj_call(x, mods, mod_row, lw, tabs, use_rope):
    b, s, d = x.shape
    t = min(s, 512)
    grid = (b, s // t)
    if mod_row is None:
        mod_map = lambda i, j: (i, 0, 0)
    else:
        mod_map = lambda i, j: (mod_row, 0, 0)
    tok = lambda w: pl.BlockSpec((1, t, w), lambda i, j: (i, j, 0))
    tab = lambda w: pl.BlockSpec((t, w), lambda i, j: (j, 0))
    in_specs = [
        tok(d),
        pl.BlockSpec((1, 6, d), mod_map),
        _const_spec((1, d)),
        _const_spec((d, PROJ_W)),
        _const_spec((8, 512)),
        _const_spec((256, 512)),
        _const_spec((KV_LORA, 512)),
        _const_spec((LANE, 512)),
        _const_spec((KV_LORA, BR_W)),
        _const_spec((512, 512)),
        _const_spec((BR_W, BR_W)),
        _const_spec((BR_W, BR_W)),
        _const_spec((512, 512)),
        _const_spec((BR_W, BR_W)),
        tab(512), tab(512), tab(BR_W), tab(BR_W),
    ]
    widths = [CONV_CH, 512, 512, BR_W, BR_W, BR_W, BR_W, BR_W, BR_W, BR_W]
    dtypes = [F32] + [BF16] * 9
    out_specs = [tok(w) for w in widths]
    out_shape = [jax.ShapeDtypeStruct((b, s, w), dt) for w, dt in zip(widths, dtypes)]
    for i in (3, 6, 9):
        out_specs[i] = pl.BlockSpec((1, N_HEADS, t, LANE), lambda i_, j: (i_, 0, j, 0))
        out_shape[i] = jax.ShapeDtypeStruct((b, N_HEADS, s, LANE), BF16)
    return pl.pallas_call(
        functools.partial(_proj_kernel, use_rope=use_rope),
        grid=grid, in_specs=in_specs, out_specs=out_specs, out_shape=out_shape,
        compiler_params=_cparams(("parallel", "parallel")),
        name="proj",
    )(x, mods, lw["n1g"], lw["w_in"], lw["gains"], lw["wuq"], lw["wk"], lw["ppe"], lw["wv"],
      lw["g96"], lw["g32"], lw["g64"], lw["rm"], lw["rd"], tabs[0], tabs[1], tabs[2], tabs[3])


CONV_TILE = 128
CONV_PAD = 16


def _conv_kernel(u_ref, w_ref, cb_ref, lg_ref, lb_ref, o_ref, pad_ref, *, seq):
    zeros = jnp.zeros((CONV_PAD, CONV_CH), F32)
    pad_ref[0:CONV_PAD, :] = zeros
    pad_ref[CONV_PAD + seq:2 * CONV_PAD + seq, :] = zeros

    def fill(i, carry):
        base = pl.multiple_of(i * CONV_TILE, CONV_TILE)
        pad_ref[pl.ds(base + CONV_PAD, CONV_TILE), :] = u_ref[0, pl.ds(base, CONV_TILE), :]
        return carry

    lax.fori_loop(0, seq // CONV_TILE, fill, 0)
    w = w_ref[...]
    cb, lg, lb = cb_ref[...], lg_ref[...], lb_ref[...]

    def tile(i, carry):
        base = pl.multiple_of(i * CONV_TILE, CONV_TILE)
        win = pad_ref[pl.ds(base, CONV_TILE + 2 * CONV_PAD), :]
        acc = jnp.zeros((CONV_TILE, CONV_CH), F32)
        for r in range(SUBLANE):
            shifted = win[r:r + CONV_TILE + 2 * CONV_PAD - SUBLANE, :]
            for j in range(CONV_WIDTH):
                if (j + 1) % SUBLANE == r:
                    a = j + 1 - r
                    acc = acc + shifted[a:a + CONV_TILE, :] * w[j:j + 1, :]
        c = acc + cb
        mu = jnp.mean(c, axis=-1, keepdims=True)
        cc = c - mu
        var = jnp.mean(cc * cc, axis=-1, keepdims=True)
        y = cc * lax.rsqrt(var + EPS) * lg + lb
        o_ref[0, pl.ds(base, CONV_TILE), :] = (y * _sigmoid(y)).astype(BF16)
        return carry

    lax.fori_loop(0, seq // CONV_TILE, tile, 0)


def _conv_call(u, lw):
    b, s, ch = u.shape
    return pl.pallas_call(
        functools.partial(_conv_kernel, seq=s),
        grid=(b,),
        in_specs=[
            pl.BlockSpec((1, s, ch), lambda i: (i, 0, 0)),
            _const_spec((32, ch)), _const_spec((1, ch)), _const_spec((1, ch)), _const_spec((1, ch)),
        ],
        out_specs=pl.BlockSpec((1, s, ch), lambda i: (i, 0, 0)),
        out_shape=jax.ShapeDtypeStruct((b, s, ch), BF16),
        scratch_shapes=[pltpu.VMEM((s + 2 * CONV_PAD, ch), F32)],
        compiler_params=_cparams(("parallel",)),
        name="conv",
    )(u, lw["conv_w"], lw["conv_b"], lw["conv_ln_g"], lw["conv_ln_b"])


def _lane_mask(width, lo, hi):
    lane = lax.broadcasted_iota(jnp.int32, (1, width), 1)
    return (lane >= lo) & (lane < hi)


def _softmax_pv(qw, kt, v1):
    s = jnp.dot(qw, kt, preferred_element_type=F32)
    m = jnp.max(s, axis=-1, keepdims=True)
    p = jnp.exp2(s - m).astype(BF16)
    o = jnp.dot(p, v1, preferred_element_type=F32)
    return o * (1.0 / o[:, V_HEAD:V_HEAD + 1])


def _attn_kernel(q_ref, kt_ref, v_ref, lam_ref, g64_ref, sg_ref, o_ref, *, maps, diff, lam_init):
    if diff:
        lv = lam_ref[...]
        lam = (jnp.exp(jnp.sum(lv[0:1] * lv[1:2], axis=-1, keepdims=True))
               - jnp.exp(jnp.sum(lv[2:3] * lv[3:4], axis=-1, keepdims=True)) + lam_init)
    heads = []
    for h in range(N_HEADS):
        outs = []
        for (w0, lo, hi) in maps[h]:
            qw = q_ref[0, :, w0:w0 + LANE]
            if (lo, hi) != (0, LANE):
                qw = jnp.where(_lane_mask(LANE, lo, hi), qw, jnp.zeros_like(qw))
            outs.append(_softmax_pv(qw, kt_ref[0, w0:w0 + LANE, :], v_ref[0, h]))
        heads.append(outs[0] - lam * outs[1] if diff else outs[0])
    low = _lane_mask(LANE, 0, V_HEAD)
    acc = jnp.concatenate(
        [jnp.where(low, heads[h], pltpu.roll(heads[h + 1], V_HEAD, axis=1)) for h in range(0, N_HEADS, 2)],
        axis=1)
    if diff:
        acc = _group_norm(acc, g64_ref, 1.0 / DIFF_V) * sg_ref[...]
    o_ref[0] = acc.astype(BF16)


def _attn_pair_kernel(mq_ref, mkt_ref, mv_ref, dq_ref, dkt_ref, dv_ref, lam_ref, g64_ref, sg_ref, om_ref, od_ref,
                      *, lam_init):
    _attn_kernel(mq_ref, mkt_ref, mv_ref, lam_ref, g64_ref, sg_ref, om_ref, maps=MAPS_MLA, diff=False, lam_init=0.0)
    _attn_kernel(dq_ref, dkt_ref, dv_ref, lam_ref, g64_ref, sg_ref, od_ref, maps=MAPS_DIFF, diff=True,
                 lam_init=lam_init)


def _attn_pair_call(mq, mkt, mv, dq, dkt, dv, lw, lam_init):
    b, s, _ = mq.shape
    sk = mkt.shape[2]
    tq = min(s, ATTN_TQ)
    tile = lambda w: pl.BlockSpec((1, tq, w), lambda i, j: (i, j, 0))
    keys = lambda w: pl.BlockSpec((1, w, sk), lambda i, j: (i, 0, 0))
    vals = pl.BlockSpec((1, N_HEADS, sk, LANE), lambda i, j: (i, 0, 0, 0))
    return pl.pallas_call(
        functools.partial(_attn_pair_kernel, lam_init=lam_init),
        grid=(b, s // tq),
        in_specs=[tile(mq.shape[2]), keys(mq.shape[2]), vals, tile(dq.shape[2]), keys(dq.shape[2]), vals,
                  _const_spec((4, DIFF_DIM)), _const_spec((BR_W, BR_W)), _const_spec((1, BR_W))],
        out_specs=[tile(BR_W), tile(BR_W)],
        out_shape=[jax.ShapeDtypeStruct((b, s, BR_W), BF16), jax.ShapeDtypeStruct((b, s, BR_W), BF16)],
        compiler_params=_cparams(("parallel", "parallel")),
        name="attn_pair",
    )(mq, mkt, mv, dq, dkt, dv, lw["diff_lam"], lw["g64"], lw["subln"])


MAPS_MLA = tuple(((LANE * h, 0, LANE),) for h in range(N_HEADS))
MAPS_DIFF = tuple(tuple((LANE * (h // 2), 64 * (h % 2) + 32 * c, 64 * (h % 2) + 32 * c + 32) for c in range(2))
                  for h in range(N_HEADS))
MAPS_NA = tuple(((LANE * (h // 2), 64 * (h % 2), 64 * (h % 2) + 64),) for h in range(N_HEADS))


def _attn_call(q, kt, v, lw, maps, diff=False, lam_init=0.0):
    b, s, wq = q.shape
    sk = kt.shape[2]
    tq = min(s, ATTN_TQ)
    return pl.pallas_call(
        functools.partial(_attn_kernel, maps=maps, diff=diff, lam_init=lam_init),
        grid=(b, s // tq),
        in_specs=[
            pl.BlockSpec((1, tq, wq), lambda i, j: (i, j, 0)),
            pl.BlockSpec((1, wq, sk), lambda i, j: (i, 0, 0)),
            pl.BlockSpec((1, N_HEADS, sk, LANE), lambda i, j: (i, 0, 0, 0)),
            _const_spec((4, DIFF_DIM)),
            _const_spec((BR_W, BR_W)),
            _const_spec((1, BR_W)),
        ],
        out_specs=pl.BlockSpec((1, tq, BR_W), lambda i, j: (i, j, 0)),
        out_shape=jax.ShapeDtypeStruct((b, s, BR_W), BF16),
        compiler_params=_cparams(("parallel", "parallel")),
        name="attn_diff" if diff else "attn",
    )(q, kt, v, lw["diff_lam"], lw["g64"], lw["subln"])


_NT = (((1,), (1,)), ((), ()))


def _na_kernel(q_ref, k_ref, v_ref, kc_ref, vc_ref, bias_ref, o_ref, *, rows):
    nj = rows // 2
    band = NA_BAND_ROWS * GRID_W
    pair = 2 * GRID_W
    for sub in range(NA_PAIRS):
        j = pl.program_id(1) * NA_PAIRS + sub
        start = jnp.clip(2 * j - 4, 0, rows - NA_BAND_ROWS)
        base = pl.multiple_of(start * GRID_W, 2 * GRID_W)
        cls = jnp.where(j < 2, j, jnp.where(j >= nj - 2, j - (nj - 2) + 3, 2))
        keys = jnp.concatenate([k_ref[0, pl.ds(base, band), :], kc_ref[0]], axis=0)
        q = q_ref[0, sub * pair:(sub + 1) * pair, :]
        heads = []
        for h in range(N_HEADS):
            (w0, lo, hi), = MAPS_NA[h]
            qw = q[:, w0:w0 + LANE]
            qm = jnp.where(_lane_mask(LANE, lo, hi), qw, jnp.zeros_like(qw))
            s = lax.dot_general(qm, keys[:, w0:w0 + LANE], _NT, preferred_element_type=F32) + bias_ref[cls, h]
            p = jnp.exp2(s - jnp.max(s, axis=-1, keepdims=True)).astype(BF16)
            vals = jnp.concatenate([v_ref[0, h, pl.ds(base, band), :], vc_ref[0, h]], axis=0)
            o = jnp.dot(p, vals, preferred_element_type=F32)
            heads.append(o * (1.0 / o[:, V_HEAD:V_HEAD + 1]))
        low = _lane_mask(LANE, 0, V_HEAD)
        acc = jnp.concatenate(
            [jnp.where(low, heads[h], pltpu.roll(heads[h + 1], V_HEAD, axis=1)) for h in range(0, N_HEADS, 2)],
            axis=1)
        o_ref[0, sub * pair:(sub + 1) * pair, :] = acc.astype(BF16)


def _na_call(q, k, v, kc, vc, bias):
    b, s, w = q.shape
    n_ctx = kc.shape[1]
    rows = s // GRID_W
    tq = 2 * GRID_W * NA_PAIRS
    return pl.pallas_call(
        functools.partial(_na_kernel, rows=rows),
        grid=(b, rows // (2 * NA_PAIRS)),
        in_specs=[
            pl.BlockSpec((1, tq, w), lambda i, j: (i, j, 0)),
            pl.BlockSpec((1, s, w), lambda i, j: (i, 0, 0)),
            pl.BlockSpec((1, N_HEADS, s, LANE), lambda i, j: (i, 0, 0, 0)),
            pl.BlockSpec((1, n_ctx, w), lambda i, j: (i, 0, 0)),
            pl.BlockSpec((1, N_HEADS, n_ctx, LANE), lambda i, j: (i, 0, 0, 0)),
            _const_spec(bias.shape),
        ],
        out_specs=pl.BlockSpec((1, tq, w), lambda i, j: (i, j, 0)),
        out_shape=jax.ShapeDtypeStruct((b, s, w), BF16),
        compiler_params=_cparams(("parallel", "arbitrary")),
        name="na",
    )(q, k, v, kc, vc, bias)


def _merge_kernel(x_ref, mod_ref, n1g_ref, n2g_ref, uc_ref, om_ref, od_ref, on_ref,
                  gw_ref, gb_ref, wc_ref, wm_ref, wd_ref, wn_ref, wo_ref, rwt_ref, rb_ref,
                  x1_ref, h2_ref, ids_ref, wts_ref):
    x = x_ref[0]
    mod = mod_ref[0]
    h = _modulate(x, n1g_ref[...], mod[0:1], mod[1:2]).astype(BF16)
    y = jnp.zeros(x.shape, F32)
    branches = ((uc_ref, wc_ref), (om_ref, wm_ref), (od_ref, wd_ref), (on_ref, wn_ref))
    for i, (o_ref, w_ref) in enumerate(branches):
        lo = D_MODEL * i
        g = _sigmoid(jnp.dot(h, gw_ref[:, lo:lo + D_MODEL], preferred_element_type=F32)
                     + gb_ref[:, lo:lo + D_MODEL])
        y = y + g * jnp.dot(o_ref[0], w_ref[...], preferred_element_type=F32)
    out = jnp.dot(y.astype(BF16), wo_ref[...], preferred_element_type=F32)
    x1 = x + mod[2:3] * out
    x1_ref[0] = x1
    h2 = _modulate(x1, n2g_ref[...], mod[3:4], mod[4:5])
    for q, piece in enumerate(_pack_row(h2)):
        h2_ref[q, 0] = piece

    logits = lax.dot_general(rwt_ref[...], h2, _NT, preferred_element_type=F32,
                             precision=lax.Precision.HIGHEST) + rb_ref[...]
    eidx = lax.broadcasted_iota(jnp.int32, logits.shape, 0).astype(F32)
    vals, idxs = [], []
    cur = logits
    for _ in range(TOP_K):
        m = jnp.max(cur, axis=0, keepdims=True)
        idx = jnp.min(jnp.where(cur == m, eidx, float(N_EXPERTS)), axis=0, keepdims=True)
        vals.append(m)
        idxs.append(idx)
        cur = jnp.where(eidx == idx, -jnp.inf, cur)
    es = [jnp.exp(vk - vals[0]) for vk in vals]
    den = es[0] + es[1] + es[2] + es[3]
    ids_ref[0] = jnp.concatenate(idxs, axis=0).astype(jnp.int32)
    wts_ref[0] = jnp.concatenate([e / den for e in es], axis=0)


def _merge_call(x, mods, mod_row, lw, uc, om, od, on):
    b, s, d = x.shape
    t = min(s, 512)
    if mod_row is None:
        mod_map = lambda i, j: (i, 0, 0)
    else:
        mod_map = lambda i, j: (mod_row, 0, 0)
    tok = lambda w: pl.BlockSpec((1, t, w), lambda i, j: (i, j, 0))
    rt = pl.BlockSpec((1, TOP_K, t), lambda i, j: (i, 0, j))
    return pl.pallas_call(
        _merge_kernel,
        grid=(b, s // t),
        in_specs=[
            tok(d), pl.BlockSpec((1, 6, d), mod_map), _const_spec((1, d)), _const_spec((1, d)),
            tok(BR_W), tok(BR_W), tok(BR_W), tok(BR_W),
            _const_spec((d, N_BRANCH * d)), _const_spec((1, N_BRANCH * d)),
            _const_spec((BR_W, d)), _const_spec((BR_W, d)), _const_spec((BR_W, d)), _const_spec((BR_W, d)),
            _const_spec((d, d)), _const_spec((N_EXPERTS, d)), _const_spec((N_EXPERTS, 1)),
        ],
        out_specs=[tok(d), pl.BlockSpec((ROW_PARTS, 1, t, ROW_Q), lambda i, j: (0, i, j, 0)), rt, rt],
        out_shape=[jax.ShapeDtypeStruct((b, s, d), F32), jax.ShapeDtypeStruct((ROW_PARTS, b, s, ROW_Q), jnp.int32),
                   jax.ShapeDtypeStruct((b, TOP_K, s), jnp.int32), jax.ShapeDtypeStruct((b, TOP_K, s), F32)],
        compiler_params=_cparams(("parallel", "parallel")),
        name="merge",
    )(x, mods, lw["n1g"], lw["n2g"], uc, om, od, on, lw["gate_w"], lw["gate_b"],
      lw["conv_out"], lw["mla_out"], lw["diff_out"], lw["na_out"], lw["w_o"], lw["router_wt"], lw["router_b"])


SC_WINDOW = 128
ROW_Q = D_MODEL // 4
ROW_PARTS = 2
HALF_D = D_MODEL // 2


def _pack_bf16_pair(a, b):
    ua = lax.bitcast_convert_type(a.astype(BF16).astype(F32), jnp.int32)
    ub = lax.bitcast_convert_type(b.astype(BF16).astype(F32), jnp.int32)
    return ua | lax.shift_right_logical(ub, jnp.int32(16))


def _unpack_bf16_pair(w):
    a = lax.bitcast_convert_type(w & jnp.int32(-65536), F32)
    b = lax.bitcast_convert_type(lax.shift_left(w, jnp.int32(16)), F32)
    return a, b


def _pack_row(x):
    w = _pack_bf16_pair(x[:, :HALF_D], x[:, HALF_D:])
    return [w[:, ROW_Q * q:ROW_Q * (q + 1)] for q in range(ROW_PARTS)]


def _unpack_row(pieces):
    ab = [_unpack_bf16_pair(w) for w in pieces]
    return jnp.concatenate([a for a, _ in ab] + [b for _, b in ab], axis=1)


def _route_slots(ids, tile):
    n = ids.shape[1]
    p = TOP_K * n
    e = ids.reshape(p)
    onehot = (e[:, None] == jnp.arange(N_EXPERTS, dtype=jnp.int32)[None, :])
    chunk = 512
    oh3 = onehot.astype(F32).reshape(p // chunk, chunk, N_EXPERTS)
    within = jnp.einsum("ij,cje->cie", jnp.tril(jnp.ones((chunk, chunk), F32)), oh3)
    totals = within[:, -1, :]
    before = jnp.cumsum(totals, axis=0) - totals
    csum = (within + before[:, None, :]).reshape(p, N_EXPERTS).astype(jnp.int32)
    onehot = onehot.astype(jnp.int32)
    counts = csum[-1]
    padded = ((counts + tile - 1) // tile) * tile
    gend = jnp.cumsum(padded)
    gstart = gend - padded
    slot = jnp.sum(onehot * (csum - 1 + gstart[None, :]), axis=1).astype(jnp.int32)
    n_tiles = p // tile + N_EXPERTS
    tile_start = jnp.arange(n_tiles, dtype=jnp.int32) * tile
    texp = jnp.sum((tile_start[:, None] >= gend[None, :]).astype(jnp.int32), axis=1)
    texp = jnp.minimum(texp, N_EXPERTS - 1)
    nreal = jnp.clip(gstart[texp] + counts[texp] - tile_start, 0, tile)
    nreal = jnp.where(tile_start < gend[-1], nreal, 0).astype(jnp.int32)
    return slot, texp, nreal, n_tiles


def _sc_mesh():
    return plsc.VectorSubcoreMesh(core_axis_name="c", subcore_axis_name="s")


def _sc_dispatch(hq, idx, n_slots):
    parts, n, w = hq.shape
    src = hq.reshape(parts * n, w)
    m = idx.shape[0]
    blocks_per_q = n // SC_WINDOW
    per_q = TOP_K * blocks_per_q

    @pl.kernel(out_type=jax.ShapeDtypeStruct((parts * n_slots, w), hq.dtype), mesh=_sc_mesh(), scratch_types=[])
    def kern(x_hbm, i_hbm, o_hbm):
        def body(x_vmem, i_vmem):
            pltpu.sync_copy(x_vmem, o_hbm.at[i_vmem.at[0]])

        pltpu.emit_pipeline(
            body,
            grid=(m // SC_WINDOW,),
            in_specs=[
                pl.BlockSpec((SC_WINDOW, w), index_map=lambda i: ((i // per_q) * blocks_per_q + i % blocks_per_q, 0)),
                pl.BlockSpec((1, SC_WINDOW), index_map=lambda i: (0, i)),
            ],
            out_specs=[],
            core_axis_name=("c", "s"),
            dimension_semantics=(pltpu.PARALLEL,),
        )(x_hbm, i_hbm)

    return kern(src, idx.reshape(1, m)).reshape(parts, n_slots, w)


def _sc_collect(ys, idx):
    parts, n_slots, w = ys.shape
    src = ys.reshape(parts * n_slots, w)
    m = idx.shape[0]

    @pl.kernel(out_type=jax.ShapeDtypeStruct((m, w), ys.dtype), mesh=_sc_mesh(), scratch_types=[])
    def kern(x_hbm, i_hbm, o_hbm):
        def body(i_vmem, o_vmem):
            pltpu.sync_copy(x_hbm.at[i_vmem.at[0]], o_vmem)

        pltpu.emit_pipeline(
            body,
            grid=(m // SC_WINDOW,),
            in_specs=[pl.BlockSpec((1, SC_WINDOW), index_map=lambda i: (0, i))],
            out_specs=[pl.BlockSpec((SC_WINDOW, w), index_map=lambda i: (i, 0))],
            core_axis_name=("c", "s"),
            dimension_semantics=(pltpu.PARALLEL,),
        )(i_hbm, o_hbm)

    return kern(src, idx.reshape(1, m))


def _ffn_sorted_kernel(texp_ref, nreal_ref, x_ref, wgu_ref, bgu_ref, wd_ref, bd_ref, y_ref, wgu_bf, wd_bf, *, tile):
    i = pl.program_id(0)
    nreal = nreal_ref[i]

    @pl.when((nreal > 0) & ((i == 0) | (texp_ref[i] != texp_ref[jnp.maximum(i - 1, 0)])))
    def _():
        wgu_bf[...] = wgu_ref[0].astype(BF16)
        wd_bf[...] = wd_ref[0].astype(BF16)

    @pl.when(nreal > 0)
    def _():
        x = _unpack_row([x_ref[q] for q in range(ROW_PARTS)])
        row = lax.broadcasted_iota(jnp.int32, (tile, 1), 0)
        x = jnp.where(row < nreal, x, 0.0).astype(BF16)
        gu = jnp.dot(x, wgu_bf[...], preferred_element_type=F32) + bgu_ref[0]
        g = jnp.minimum(gu[:, :D_FF], SWIGLU_LIMIT)
        u = jnp.clip(gu[:, D_FF:], -SWIGLU_LIMIT, SWIGLU_LIMIT)
        act = ((u + 1.0) * (g * _sigmoid(SWIGLU_ALPHA * g))).astype(BF16)
        y = jnp.dot(act, wd_bf[...], preferred_element_type=F32) + bd_ref[0]
        for q, piece in enumerate(_pack_row(y)):
            y_ref[q] = piece

    @pl.when(nreal == 0)
    def _():
        y_ref[...] = jnp.zeros(y_ref.shape, jnp.int32)


def _ffn_sorted_call(xs, texp, nreal, lw, tile):
    _, n_slots, w = xs.shape
    n_tiles = n_slots // tile
    d = D_MODEL
    off = lw["exp_off"]
    grid_spec = pltpu.PrefetchScalarGridSpec(
        num_scalar_prefetch=2,
        grid=(n_tiles,),
        in_specs=[
            pl.BlockSpec((ROW_PARTS, tile, w), lambda i, te, nr: (0, i, 0)),
            pl.BlockSpec((1, d, 2 * D_FF), lambda i, te, nr: (te[i] + off, 0, 0)),
            pl.BlockSpec((1, 1, 2 * D_FF), lambda i, te, nr: (te[i] + off, 0, 0)),
            pl.BlockSpec((1, D_FF, d), lambda i, te, nr: (te[i] + off, 0, 0)),
            pl.BlockSpec((1, 1, d), lambda i, te, nr: (te[i] + off, 0, 0)),
        ],
        out_specs=pl.BlockSpec((ROW_PARTS, tile, w), lambda i, te, nr: (0, i, 0)),
        scratch_shapes=[pltpu.VMEM((d, 2 * D_FF), BF16), pltpu.VMEM((D_FF, d), BF16)],
    )
    return pl.pallas_call(
        functools.partial(_ffn_sorted_kernel, tile=tile),
        grid_spec=grid_spec,
        out_shape=jax.ShapeDtypeStruct((ROW_PARTS, n_slots, w), jnp.int32),
        compiler_params=_cparams(("arbitrary",)),
        name="moe_ffn",
    )(texp, nreal, xs, lw["exp_w_gu"], lw["exp_b_gu"], lw["exp_w_down"], lw["exp_b_down"])


def _combine_q_kernel(x1_ref, mod_ref, w_ref, y_ref, o_ref):
    w = w_ref[0]
    g2 = mod_ref[0][5:6]
    for q in range(ROW_PARTS):
        acc_a, acc_b = None, None
        for k in range(TOP_K):
            a, b = _unpack_bf16_pair(y_ref[q, k])
            wk = w[:, k:k + 1]
            acc_a = wk * a if acc_a is None else acc_a + wk * a
            acc_b = wk * b if acc_b is None else acc_b + wk * b
        for lo, acc in ((ROW_Q * q, acc_a), (HALF_D + ROW_Q * q, acc_b)):
            o_ref[0, :, lo:lo + ROW_Q] = x1_ref[0, :, lo:lo + ROW_Q] + g2[:, lo:lo + ROW_Q] * acc


def _combine_q_call(x1, mods, mod_row, wts, y, tok_off):
    b, s, d = x1.shape
    t = min(s, 512)
    nt = s // t
    blk_off = tok_off // t
    if mod_row is None:
        mod_map = lambda i, j: (i, 0, 0)
    else:
        mod_map = lambda i, j: (mod_row, 0, 0)
    wts = wts.transpose(0, 2, 1)
    return pl.pallas_call(
        _combine_q_kernel,
        grid=(b, nt),
        in_specs=[
            pl.BlockSpec((1, t, d), lambda i, j: (i, j, 0)),
            pl.BlockSpec((1, 6, d), mod_map),
            pl.BlockSpec((1, t, TOP_K), lambda i, j: (i, j, 0)),
            pl.BlockSpec((ROW_PARTS, TOP_K, t, ROW_Q), lambda i, j: (0, 0, blk_off + i * nt + j, 0)),
        ],
        out_specs=pl.BlockSpec((1, t, d), lambda i, j: (i, j, 0)),
        out_shape=jax.ShapeDtypeStruct((b, s, d), F32),
        compiler_params=_cparams(("parallel", "parallel")),
        name="moe_combine",
    )(x1, mods, wts, y)


def _moe_sc(streams, mods, lw):
    sizes = [st[0].shape[0] * st[0].shape[1] for st in streams]
    n = sum(sizes)
    tile = 512 if TOP_K * n >= 512 * N_EXPERTS * 4 else 256
    hq = jnp.concatenate([st[1].reshape(ROW_PARTS, m, ROW_Q) for st, m in zip(streams, sizes)], axis=1)
    ids = jnp.concatenate([st[2].transpose(1, 0, 2).reshape(TOP_K, m) for st, m in zip(streams, sizes)], axis=1)
    slot, texp, nreal, n_tiles = _route_slots(ids, tile)
    n_slots = n_tiles * tile
    idx = (slot[None, :] + (jnp.arange(ROW_PARTS, dtype=jnp.int32) * n_slots)[:, None]).reshape(-1)
    xs = _sc_dispatch(hq, idx, n_slots)
    ys = _ffn_sorted_call(xs, texp, nreal, lw, tile)
    y = _sc_collect(ys, idx).reshape(ROW_PARTS, TOP_K, n, ROW_Q)
    outs, off = [], 0
    for (x1, _, _, wts, mod_row), m in zip(streams, sizes):
        outs.append(_combine_q_call(x1, mods, mod_row, wts, y, off))
        off += m
    return outs


def _layer_weights(l, p, lam_init):
    w = p["w_in"][l].astype(BF16)
    d = w.shape[0]
    zcols = lambda n: jnp.zeros((d, n), w.dtype)
    regroup = lambda blk: blk.reshape(d, N_HEADS, 3, HEAD_DIM).transpose(0, 2, 1, 3).reshape(d, 3 * BR_W)
    w_in = jnp.concatenate([
        w[:, :A_IN],
        w[:, OFF_B:OFF_B + Q_LORA], zcols(P_CKV - P_CQ - Q_LORA),
        w[:, OFF_B + Q_LORA:OFF_B + Q_LORA + KV_LORA],
        w[:, OFF_B + Q_LORA + KV_LORA:OFF_C], zcols(P_DQ - P_KPE - QK_ROPE),
        regroup(w[:, OFF_C:OFF_D]), regroup(w[:, OFF_D:]),
    ], axis=1)
    assert w_in.shape[1] == PROJ_W

    def head_slots(w3, slot):
        w3 = jnp.pad(w3, ((0, 0), (0, 0), (0, slot - w3.shape[2])))
        return w3.reshape(w3.shape[0], N_HEADS * slot)

    wuq = head_slots(p["mla_w_uq"][l].reshape(Q_LORA, N_HEADS, MLA_QK), LANE)
    wuq = jnp.pad(wuq, ((0, 256 - Q_LORA), (0, 0)))
    wukv = p["mla_w_ukv"][l].reshape(KV_LORA, N_HEADS, QK_NOPE + V_HEAD)
    wk = head_slots(wukv[:, :, :QK_NOPE], LANE)
    wv = head_slots(wukv[:, :, QK_NOPE:], V_HEAD)
    ppe = np.zeros((LANE, 512), np.float32)
    for h in range(N_HEADS):
        for i in range(QK_ROPE):
            ppe[i, h * LANE + QK_NOPE + i] = 1.0

    def slot_gain(g, scale):
        g = jnp.concatenate([g * scale, jnp.zeros((LANE - MLA_QK,), F32)])
        return jnp.tile(g, N_HEADS)

    def row512(v):
        return jnp.concatenate([v, jnp.zeros((512 - v.shape[0],), F32)])

    gains = jnp.stack([
        row512(p["mla_cq_g"][l]),
        slot_gain(p["mla_qn_g"][l], MLA_QK ** -0.5 * LOG2E),
        row512(p["mla_ckv_g"][l]),
        slot_gain(p["mla_kn_g"][l], 1.0),
        row512(jnp.tile(p["diff_qn_g"][l], 2 * N_HEADS) * DIFF_DIM ** -0.5 * LOG2E),
        row512(jnp.tile(p["diff_kn_g"][l], 2 * N_HEADS)),
        row512(jnp.tile(p["na_qn_g"][l], N_HEADS) * HEAD_DIM ** -0.5 * LOG2E),
        row512(jnp.tile(p["na_kn_g"][l], N_HEADS)),
    ])
    conv_w = jnp.concatenate([p["conv_w"][l], jnp.zeros((1, CONV_CH), F32)], axis=0)
    return dict(
        n1g=p["norm1_g"][l][None, :], n2g=p["norm2_g"][l][None, :],
        w_in=w_in, gains=gains,
        wuq=wuq.astype(BF16), wk=wk.astype(BF16), wv=wv.astype(BF16), ppe=jnp.asarray(ppe, BF16),
        g96=jnp.asarray(_group_ones(512, LANE, MLA_QK), BF16),
        g32=jnp.asarray(_group_ones(BR_W, DIFF_DIM, DIFF_DIM), BF16),
        g64=jnp.asarray(_group_ones(BR_W, HEAD_DIM, HEAD_DIM), BF16),
        rm=jnp.asarray(_rot_matrix(512, LANE, QK_NOPE, QK_ROPE // 2), BF16),
        rd=jnp.asarray(_rot_matrix(BR_W, DIFF_DIM, 0, DIFF_DIM // 2), BF16),
        conv_w=conv_w, conv_b=p["conv_b"][l][None, :],
        conv_ln_g=p["conv_ln_g"][l][None, :], conv_ln_b=p["conv_ln_b"][l][None, :],
        diff_lam=p["diff_lam"][l],
        subln=(jnp.tile(p["diff_subln_g"][l], N_HEADS) * (1.0 - lam_init))[None, :],
        gate_w=p["gate_w"][l].astype(BF16), gate_b=p["gate_b"][l][None, :],
        conv_out=p["conv_out"][l].astype(BF16), mla_out=p["mla_out"][l].astype(BF16),
        diff_out=p["diff_out"][l].astype(BF16), na_out=p["na_out"][l].astype(BF16),
        w_o=p["w_o"][l].astype(BF16),
        router_wt=p["router_w"][l].T, router_b=p["router_b"][l][:, None],
        exp_off=l * N_EXPERTS,
        exp_w_gu=p["exp_w_gu"].reshape((-1,) + p["exp_w_gu"].shape[2:]),
        exp_b_gu=p["exp_b_gu"].reshape(-1, 1, 2 * D_FF),
        exp_w_down=p["exp_w_down"].reshape((-1,) + p["exp_w_down"].shape[2:]),
        exp_b_down=p["exp_b_down"].reshape(-1, 1, D_MODEL),
    )


def _kt(kc, k):
    return jnp.concatenate([kc, k], axis=1).transpose(0, 2, 1)


def kernel(x, c, ctx, c_ctx, ada_w, ada_b, norm1_g, norm2_g, w_in, conv_w, conv_b, conv_ln_g, conv_ln_b, conv_out, mla_cq_g, mla_ckv_g, mla_w_uq, mla_w_ukv, mla_qn_g, mla_kn_g, mla_out, diff_qn_g, diff_kn_g, diff_lam, diff_subln_g, diff_out, na_qn_g, na_kn_g, na_rpb, na_out, gate_w, gate_b, w_o, router_w, router_b, exp_w_gu, exp_b_gu, exp_w_down, exp_b_down):
    p = dict(norm1_g=norm1_g, norm2_g=norm2_g, w_in=w_in, conv_w=conv_w, conv_b=conv_b,
             conv_ln_g=conv_ln_g, conv_ln_b=conv_ln_b, conv_out=conv_out, mla_cq_g=mla_cq_g,
             mla_ckv_g=mla_ckv_g, mla_w_uq=mla_w_uq, mla_w_ukv=mla_w_ukv, mla_qn_g=mla_qn_g,
             mla_kn_g=mla_kn_g, mla_out=mla_out, diff_qn_g=diff_qn_g, diff_kn_g=diff_kn_g,
             diff_lam=diff_lam, diff_subln_g=diff_subln_g, diff_out=diff_out, na_qn_g=na_qn_g,
             na_kn_g=na_kn_g, na_out=na_out, gate_w=gate_w, gate_b=gate_b, w_o=w_o,
             router_w=router_w, router_b=router_b, exp_w_gu=exp_w_gu, exp_b_gu=exp_b_gu,
             exp_w_down=exp_w_down, exp_b_down=exp_b_down)
    b, s, d = x.shape
    n_ctx = ctx.shape[1]
    depth = ada_w.shape[0]
    rows = s // GRID_W
    assert d == D_MODEL and s % (2 * GRID_W) == 0 and rows >= NA_BAND_ROWS and n_ctx % LANE == 0

    mod_rows = -(-(b + 1) // 8) * 8
    cs = jnp.concatenate([c, c_ctx[None, :], jnp.zeros((mod_rows - b - 1, d), F32)], axis=0)
    mods_all = _ada_call(cs, ada_w, ada_b).reshape(depth, mod_rows, 6, d)

    tabs_x = _rope_lane_tables(s)
    tabs_c = (jnp.ones((n_ctx, 512), F32), jnp.zeros((n_ctx, 512), F32),
              jnp.ones((n_ctx, BR_W), F32), jnp.zeros((n_ctx, BR_W), F32))

    xc = ctx
    for l in range(depth):
        last = l == depth - 1
        lam_init = 0.8 - 0.6 * math.exp(-0.3 * l)
        lw = _layer_weights(l, p, lam_init)
        mods = mods_all[l]
        bias = _na_bias_tables(na_rpb[l], rows, n_ctx)

        u, mq, mk, mv, dq, dk, dv, nq, nk, nv = _proj_call(x, mods, None, lw, tabs_x, True)
        uc, mqc, mkc, mvc, dqc, dkc, dvc, nqc, nkc, nvc = _proj_call(xc, mods, b, lw, tabs_c, False)

        y_conv = _conv_call(u, lw)
        y_mla, y_diff = _attn_pair_call(mq, _kt(mkc, mk), jnp.concatenate([mvc, mv], axis=2),
                                        dq, _kt(dkc, dk), jnp.concatenate([dvc, dv], axis=2), lw, lam_init)
        y_na = _na_call(nq, nk, nv, nkc, nvc, bias)
        x1, h2, ids, wts = _merge_call(x, mods, None, lw, y_conv, y_mla, y_diff, y_na)
        streams = [(x1, h2, ids, wts, None)]

        if not last:
            yc_conv = _conv_call(uc, lw)
            yc_mla = _attn_call(mqc, mkc.transpose(0, 2, 1), mvc, lw, MAPS_MLA)
            yc_diff = _attn_call(dqc, dkc.transpose(0, 2, 1), dvc, lw, MAPS_DIFF, diff=True, lam_init=lam_init)
            yc_na = _attn_call(nqc, nkc.transpose(0, 2, 1), nvc, lw, MAPS_NA)
            xc1, h2c, idsc, wtsc = _merge_call(xc, mods, b, lw, yc_conv, yc_mla, yc_diff, yc_na)
            streams.append((xc1, h2c, idsc, wtsc, b))

        outs = _moe_sc(streams, mods, lw)
        x = outs[0]
        if not last:
            xc = outs[1]
    return x
```

```python
import functools
import math

import numpy as np
import jax
import jax.numpy as jnp
from jax import lax
from jax.experimental import pallas as pl
from jax.experimental.pallas import tpu as pltpu
from jax.experimental.pallas import tpu_sc as plsc

F32 = jnp.float32
BF16 = jnp.bfloat16

D_MODEL = 1024
GRID_W = 64
N_BRANCH = 4
N_HEADS = 4
HEAD_DIM = 64
CONV_CH = 256
CONV_WIDTH = 31
Q_LORA = 192
KV_LORA = 128
QK_NOPE = 64
QK_ROPE = 32
V_HEAD = 64
DIFF_DIM = 32
DIFF_V = 2 * DIFF_DIM
NA_KH = 8
NA_KW = 16
ROPE_DIM = 32
ROPE_BASE = 10000.0
N_EXPERTS = 32
TOP_K = 4
D_FF = 1024
SWIGLU_LIMIT = 7.0
SWIGLU_ALPHA = 1.702
EPS = 1e-6
NEG_INF = -1e30

A_IN = 2 * CONV_CH
B_IN = Q_LORA + KV_LORA + QK_ROPE
C_IN = N_HEADS * (4 * DIFF_DIM + DIFF_V)
D_IN = N_HEADS * 3 * HEAD_DIM
OFF_B = A_IN
OFF_C = OFF_B + B_IN
OFF_D = OFF_C + C_IN

LANE = 128
SUBLANE = 8
MLA_QK = QK_NOPE + QK_ROPE
BR_W = N_HEADS * HEAD_DIM
PROJ_W = 2560
NA_BAND_ROWS = 10
ATTN_TQ = 512
NA_PAIRS = 4
LOG2E = math.log2(math.e)
VMEM_LIMIT = 52 * 1024 * 1024

P_A, P_G, P_CQ, P_CKV, P_KPE = 0, 256, 512, 768, 896
P_DQ, P_DK, P_DV = 1024, 1280, 1536
P_NQ, P_NK, P_NV = 1792, 2048, 2304


def _sigmoid(x):
    return 0.5 * jnp.tanh(0.5 * x) + 0.5


def _modulate(x, g, shift, scale):
    ms = jnp.mean(x * x, axis=-1, keepdims=True)
    return (x * lax.rsqrt(ms + EPS) * g) * (1.0 + scale) + shift


def _cparams(sem):
    return pltpu.CompilerParams(dimension_semantics=sem, vmem_limit_bytes=VMEM_LIMIT)


def _const_spec(shape):
    n = len(shape)
    return pl.BlockSpec(shape, lambda *_: (0,) * n)


def _group_ones(width, slot, real):
    i = np.arange(width)
    valid = (i % slot) < real
    same = (i[:, None] // slot) == (i[None, :] // slot)
    return (same & valid[:, None] & valid[None, :]).astype(np.float32)


def _rot_matrix(width, slot, start, half):
    r = np.zeros((width, width), np.float32)
    for s0 in range(0, width, slot):
        for i in range(half):
            a, b = s0 + start + i, s0 + start + half + i
            r[b, a] = -1.0
            r[a, b] = 1.0
    return r


def _rope_lane_tables(n_tokens):
    t = jnp.arange(n_tokens, dtype=jnp.int32)
    rows = (t // GRID_W).astype(F32)
    cols = (t % GRID_W).astype(F32)
    axis_dim = ROPE_DIM // 2
    inv = ROPE_BASE ** (-jnp.arange(0, axis_dim, 2, dtype=F32) / axis_dim)
    theta = jnp.concatenate([rows[:, None] * inv, cols[:, None] * inv], axis=-1)
    cos, sin = jnp.cos(theta), jnp.sin(theta)
    half = ROPE_DIM // 2
    ones = jnp.ones((n_tokens, QK_NOPE), F32)
    zeros = jnp.zeros((n_tokens, QK_NOPE), F32)
    pad1 = jnp.ones((n_tokens, LANE - MLA_QK), F32)
    pad0 = jnp.zeros((n_tokens, LANE - MLA_QK), F32)
    cm = jnp.tile(jnp.concatenate([ones, cos, cos, pad1], -1), (1, N_HEADS))
    sm = jnp.tile(jnp.concatenate([zeros, sin, sin, pad0], -1), (1, N_HEADS))
    cd = jnp.tile(jnp.concatenate([cos, cos], -1), (1, 2 * N_HEADS))
    sd = jnp.tile(jnp.concatenate([sin, sin], -1), (1, 2 * N_HEADS))
    assert half * 2 == DIFF_DIM
    return cm, sm, cd, sd


def _na_bias_tables(rpb, rows, n_ctx):
    kh = min(NA_KH, rows)
    nj = rows // 2
    reps = np.array([0, 1, 2, nj - 2, nj - 1])
    n_ro, n_co = 2 * NA_KH - 1, 2 * NA_KW - 1
    start = np.clip(2 * reps - 4, 0, rows - NA_BAND_ROWS)
    r = 2 * reps[:, None] + np.arange(2)[None, :]
    kr = start[:, None] + np.arange(NA_BAND_ROWS)[None, :]
    row_start = np.clip(r - kh // 2, 0, rows - kh)
    vr = (kr[:, None, :] >= row_start[:, :, None]) & (kr[:, None, :] < row_start[:, :, None] + kh)
    ro = np.clip(kr[:, None, :] - r[:, :, None] + NA_KH - 1, 0, n_ro - 1)
    qc = np.arange(GRID_W)
    win_start = np.clip(qc - NA_KW // 2, 0, GRID_W - NA_KW)
    vc = (qc[None, :] >= win_start[:, None]) & (qc[None, :] < win_start[:, None] + NA_KW)
    co = np.clip(qc[None, :] - qc[:, None] + NA_KW - 1, 0, n_co - 1)
    rsel = (ro[..., None] == np.arange(n_ro)).astype(np.float32)
    csel = (co[None] == np.arange(n_co)[:, None, None]).astype(np.float32)
    hi = lax.Precision.HIGHEST
    t1 = jnp.einsum("cqav,hvw->hcqaw", rsel, rpb.astype(F32), precision=hi)
    b = jnp.einsum("hcqaw,wxy->chqxay", t1, csel, precision=hi)
    valid = vr[:, None, :, None, :, None] & vc[None, None, None, :, None, :]
    b = jnp.where(valid, b * LOG2E, NEG_INF)
    b = b.reshape(len(reps), N_HEADS, 2 * GRID_W, NA_BAND_ROWS * GRID_W)
    return jnp.concatenate([b, jnp.zeros(b.shape[:3] + (n_ctx,), F32)], axis=-1)


def _ada_kernel(c_ref, w_ref, b_ref, o_ref):
    c = c_ref[...]
    s = c * _sigmoid(c)
    o_ref[0] = jnp.dot(s, w_ref[0], preferred_element_type=F32,
                       precision=lax.Precision.HIGHEST) + b_ref[0]


def _ada_call(cs, ada_w, ada_b):
    depth, d, n = ada_w.shape
    rows = cs.shape[0]
    tn = 1536
    return pl.pallas_call(
        _ada_kernel,
        grid=(depth, n // tn),
        in_specs=[
            pl.BlockSpec((rows, d), lambda l, j: (0, 0)),
            pl.BlockSpec((1, d, tn), lambda l, j: (l, 0, j)),
            pl.BlockSpec((1, 1, tn), lambda l, j: (l, 0, j)),
        ],
        out_specs=pl.BlockSpec((1, rows, tn), lambda l, j: (l, 0, j)),
        out_shape=jax.ShapeDtypeStruct((depth, rows, n), F32),
        compiler_params=_cparams(("arbitrary", "arbitrary")),
        name="ada",
    )(cs, ada_w, ada_b.reshape(depth, 1, n))


def _group_norm(x, ones_ref, inv_n):
    sq = (x * x).astype(BF16)
    ms = jnp.dot(sq, ones_ref[...], preferred_element_type=F32) * inv_n
    return x * lax.rsqrt(ms + EPS)


def _store_value_heads(ref, v):
    low = _lane_mask(LANE, 0, V_HEAD)
    for h in range(N_HEADS):
        win = v[:, LANE * (h // 2):LANE * (h // 2 + 1)]
        if h % 2:
            win = pltpu.roll(win, V_HEAD, axis=1)
        ref[0, h] = jnp.where(low, win, 1.0).astype(BF16)


def _rope(x, rot_ref, cos_ref, sin_ref):
    rot = jnp.dot(x.astype(BF16), rot_ref[...], preferred_element_type=F32)
    return x * cos_ref[...] + rot * sin_ref[...]


def _proj_kernel(x_ref, mod_ref, n1g_ref, win_ref, gains_ref, wuq_ref, wk_ref, ppe_ref, wv_ref,
                 g96_ref, g32_ref, g64_ref, rm_ref, rd_ref, cm_ref, sm_ref, cd_ref, sd_ref,
                 u_ref, mq_ref, mk_ref, mv_ref, dq_ref, dk_ref, dv_ref, nq_ref, nk_ref, nv_ref,
                 *, use_rope):
    x = x_ref[0]
    mod = mod_ref[0]
    gains = gains_ref[...]
    h = _modulate(x, n1g_ref[...], mod[0:1], mod[1:2]).astype(BF16)
    proj = jnp.dot(h, win_ref[...], preferred_element_type=F32)

    u_ref[0] = proj[:, P_A:P_A + CONV_CH] * _sigmoid(proj[:, P_G:P_G + CONV_CH])

    cq = proj[:, P_CQ:P_CQ + 256]
    ms = jnp.sum(cq * cq, axis=-1, keepdims=True) * (1.0 / Q_LORA)
    cqn = (cq * lax.rsqrt(ms + EPS) * gains[0:1, :256]).astype(BF16)
    q = jnp.dot(cqn, wuq_ref[...], preferred_element_type=F32)
    q = _group_norm(q, g96_ref, 1.0 / MLA_QK) * gains[1:2, :]
    if use_rope:
        q = _rope(q, rm_ref, cm_ref, sm_ref)
    mq_ref[0] = q.astype(BF16)

    ckv = proj[:, P_CKV:P_CKV + KV_LORA]
    ms = jnp.mean(ckv * ckv, axis=-1, keepdims=True)
    ckvn = (ckv * lax.rsqrt(ms + EPS) * gains[2:3, :KV_LORA]).astype(BF16)
    kpe = proj[:, P_KPE:P_KPE + LANE].astype(BF16)
    k = (jnp.dot(ckvn, wk_ref[...], preferred_element_type=F32)
         + jnp.dot(kpe, ppe_ref[...], preferred_element_type=F32))
    k = _group_norm(k, g96_ref, 1.0 / MLA_QK) * gains[3:4, :]
    if use_rope:
        k = _rope(k, rm_ref, cm_ref, sm_ref)
    mk_ref[0] = k.astype(BF16)
    _store_value_heads(mv_ref, jnp.dot(ckvn, wv_ref[...], preferred_element_type=F32))

    qd = _group_norm(proj[:, P_DQ:P_DQ + BR_W], g32_ref, 1.0 / DIFF_DIM) * gains[4:5, :BR_W]
    kd = _group_norm(proj[:, P_DK:P_DK + BR_W], g32_ref, 1.0 / DIFF_DIM) * gains[5:6, :BR_W]
    if use_rope:
        qd = _rope(qd, rd_ref, cd_ref, sd_ref)
        kd = _rope(kd, rd_ref, cd_ref, sd_ref)
    dq_ref[0] = qd.astype(BF16)
    dk_ref[0] = kd.astype(BF16)
    _store_value_heads(dv_ref, proj[:, P_DV:P_DV + BR_W])

    qn = _group_norm(proj[:, P_NQ:P_NQ + BR_W], g64_ref, 1.0 / HEAD_DIM) * gains[6:7, :BR_W]
    kn = _group_norm(proj[:, P_NK:P_NK + BR_W], g64_ref, 1.0 / HEAD_DIM) * gains[7:8, :BR_W]
    nq_ref[0] = qn.astype(BF16)
    nk_ref[0] = kn.astype(BF16)
    _store_value_heads(nv_ref, proj[:, P_NV:P_NV + BR_W])


def _proj_call(x, mods, mod_row, lw, tabs, use_rope):
    b, s, d = x.shape
    t = min(s, 512)
    grid = (b, s // t)
    if mod_row is None:
        mod_map = lambda i, j: (i, 0, 0)
    else:
        mod_map = lambda i, j: (mod_row, 0, 0)
    tok = lambda w: pl.BlockSpec((1, t, w), lambda i, j: (i, j, 0))
    tab = lambda w: pl.BlockSpec((t, w), lambda i, j: (j, 0))
    in_specs = [
        tok(d),
        pl.BlockSpec((1, 6, d), mod_map),
        _const_spec((1, d)),
        _const_spec((d, PROJ_W)),
        _const_spec((8, 512)),
        _const_spec((256, 512)),
        _const_spec((KV_LORA, 512)),
        _const_spec((LANE, 512)),
        _const_spec((KV_LORA, BR_W)),
        _const_spec((512, 512)),
        _const_spec((BR_W, BR_W)),
        _const_spec((BR_W, BR_W)),
        _const_spec((512, 512)),
        _const_spec((BR_W, BR_W)),
        tab(512), tab(512), tab(BR_W), tab(BR_W),
    ]
    widths = [CONV_CH, 512, 512, BR_W, BR_W, BR_W, BR_W, BR_W, BR_W, BR_W]
    dtypes = [F32] + [BF16] * 9
    out_specs = [tok(w) for w in widths]
    out_shape = [jax.ShapeDtypeStruct((b, s, w), dt) for w, dt in zip(widths, dtypes)]
    for i in (3, 6, 9):
        out_specs[i] = pl.BlockSpec((1, N_HEADS, t, LANE), lambda i_, j: (i_, 0, j, 0))
        out_shape[i] = jax.ShapeDtypeStruct((b, N_HEADS, s, LANE), BF16)
    return pl.pallas_call(
        functools.partial(_proj_kernel, use_rope=use_rope),
        grid=grid, in_specs=in_specs, out_specs=out_specs, out_shape=out_shape,
        compiler_params=_cparams(("parallel", "parallel")),
        name="proj",
    )(x, mods, lw["n1g"], lw["w_in"], lw["gains"], lw["wuq"], lw["wk"], lw["ppe"], lw["wv"],
      lw["g96"], lw["g32"], lw["g64"], lw["rm"], lw["rd"], tabs[0], tabs[1], tabs[2], tabs[3])


CONV_TILE = 128
CONV_PAD = 16


def _conv_kernel(u_ref, w_ref, cb_ref, lg_ref, lb_ref, o_ref, pad_ref, *, seq):
    zeros = jnp.zeros((CONV_PAD, CONV_CH), F32)
    pad_ref[0:CONV_PAD, :] = zeros
    pad_ref[CONV_PAD + seq:2 * CONV_PAD + seq, :] = zeros

    def fill(i, carry):
        base = pl.multiple_of(i * CONV_TILE, CONV_TILE)
        pad_ref[pl.ds(base + CONV_PAD, CONV_TILE), :] = u_ref[0, pl.ds(base, CONV_TILE), :]
        return carry

    lax.fori_loop(0, seq // CONV_TILE, fill, 0)
    w = w_ref[...]
    cb, lg, lb = cb_ref[...], lg_ref[...], lb_ref[...]

    def tile(i, carry):
        base = pl.multiple_of(i * CONV_TILE, CONV_TILE)
        win = pad_ref[pl.ds(base, CONV_TILE + 2 * CONV_PAD), :]
        acc = jnp.zeros((CONV_TILE, CONV_CH), F32)
        for r in range(SUBLANE):
            shifted = win[r:r + CONV_TILE + 2 * CONV_PAD - SUBLANE, :]
            for j in range(CONV_WIDTH):
                if (j + 1) % SUBLANE == r:
                    a = j + 1 - r
                    acc = acc + shifted[a:a + CONV_TILE, :] * w[j:j + 1, :]
        c = acc + cb
        mu = jnp.mean(c, axis=-1, keepdims=True)
        cc = c - mu
        var = jnp.mean(cc * cc, axis=-1, keepdims=True)
        y = cc * lax.rsqrt(var + EPS) * lg + lb
        o_ref[0, pl.ds(base, CONV_TILE), :] = (y * _sigmoid(y)).astype(BF16)
        return carry

    lax.fori_loop(0, seq // CONV_TILE, tile, 0)


def _conv_call(u, lw):
    b, s, ch = u.shape
    return pl.pallas_call(
        functools.partial(_conv_kernel, seq=s),
        grid=(b,),
        in_specs=[
            pl.BlockSpec((1, s, ch), lambda i: (i, 0, 0)),
            _const_spec((32, ch)), _const_spec((1, ch)), _const_spec((1, ch)), _const_spec((1, ch)),
        ],
        out_specs=pl.BlockSpec((1, s, ch), lambda i: (i, 0, 0)),
        out_shape=jax.ShapeDtypeStruct((b, s, ch), BF16),
        scratch_shapes=[pltpu.VMEM((s + 2 * CONV_PAD, ch), F32)],
        compiler_params=_cparams(("parallel",)),
        name="conv",
    )(u, lw["conv_w"], lw["conv_b"], lw["conv_ln_g"], lw["conv_ln_b"])


def _lane_mask(width, lo, hi):
    lane = lax.broadcasted_iota(jnp.int32, (1, width), 1)
    return (lane >= lo) & (lane < hi)


def _softmax_pv(qw, kt, v1):
    s = jnp.dot(qw, kt, preferred_element_type=F32)
    m = jnp.max(s, axis=-1, keepdims=True)
    p = jnp.exp2(s - m).astype(BF16)
    o = jnp.dot(p, v1, preferred_element_type=F32)
    return o * (1.0 / o[:, V_HEAD:V_HEAD + 1])


def _attn_kernel(q_ref, kt_ref, v_ref, lam_ref, g64_ref, sg_ref, o_ref, *, maps, diff, lam_init):
    if diff:
        lv = lam_ref[...]
        lam = (jnp.exp(jnp.sum(lv[0:1] * lv[1:2], axis=-1, keepdims=True))
               - jnp.exp(jnp.sum(lv[2:3] * lv[3:4], axis=-1, keepdims=True)) + lam_init)
    heads = []
    for h in range(N_HEADS):
        outs = []
        for (w0, lo, hi) in maps[h]:
            qw = q_ref[0, :, w0:w0 + LANE]
            if (lo, hi) != (0, LANE):
                qw = jnp.where(_lane_mask(LANE, lo, hi), qw, jnp.zeros_like(qw))
            outs.append(_softmax_pv(qw, kt_ref[0, w0:w0 + LANE, :], v_ref[0, h]))
        heads.append(outs[0] - lam * outs[1] if diff else outs[0])
    low = _lane_mask(LANE, 0, V_HEAD)
    acc = jnp.concatenate(
        [jnp.where(low, heads[h], pltpu.roll(heads[h + 1], V_HEAD, axis=1)) for h in range(0, N_HEADS, 2)],
        axis=1)
    if diff:
        acc = _group_norm(acc, g64_ref, 1.0 / DIFF_V) * sg_ref[...]
    o_ref[0] = acc.astype(BF16)


def _attn_pair_kernel(mq_ref, mkt_ref, mv_ref, dq_ref, dkt_ref, dv_ref, lam_ref, g64_ref, sg_ref, om_ref, od_ref,
                      *, lam_init):
    _attn_kernel(mq_ref, mkt_ref, mv_ref, lam_ref, g64_ref, sg_ref, om_ref, maps=MAPS_MLA, diff=False, lam_init=0.0)
    _attn_kernel(dq_ref, dkt_ref, dv_ref, lam_ref, g64_ref, sg_ref, od_ref, maps=MAPS_DIFF, diff=True,
                 lam_init=lam_init)


def _attn_pair_call(mq, mkt, mv, dq, dkt, dv, lw, lam_init):
    b, s, _ = mq.shape
    sk = mkt.shape[2]
    tq = min(s, ATTN_TQ)
    tile = lambda w: pl.BlockSpec((1, tq, w), lambda i, j: (i, j, 0))
    keys = lambda w: pl.BlockSpec((1, w, sk), lambda i, j: (i, 0, 0))
    vals = pl.BlockSpec((1, N_HEADS, sk, LANE), lambda i, j: (i, 0, 0, 0))
    return pl.pallas_call(
        functools.partial(_attn_pair_kernel, lam_init=lam_init),
        grid=(b, s // tq),
        in_specs=[tile(mq.shape[2]), keys(mq.shape[2]), vals, tile(dq.shape[2]), keys(dq.shape[2]), vals,
                  _const_spec((4, DIFF_DIM)), _const_spec((BR_W, BR_W)), _const_spec((1, BR_W))],
        out_specs=[tile(BR_W), tile(BR_W)],
        out_shape=[jax.ShapeDtypeStruct((b, s, BR_W), BF16), jax.ShapeDtypeStruct((b, s, BR_W), BF16)],
        compiler_params=_cparams(("parallel", "parallel")),
        name="attn_pair",
    )(mq, mkt, mv, dq, dkt, dv, lw["diff_lam"], lw["g64"], lw["subln"])


MAPS_MLA = tuple(((LANE * h, 0, LANE),) for h in range(N_HEADS))
MAPS_DIFF = tuple(tuple((LANE * (h // 2), 64 * (h % 2) + 32 * c, 64 * (h % 2) + 32 * c + 32) for c in range(2))
                  for h in range(N_HEADS))
MAPS_NA = tuple(((LANE * (h // 2), 64 * (h % 2), 64 * (h % 2) + 64),) for h in range(N_HEADS))


def _attn_call(q, kt, v, lw, maps, diff=False, lam_init=0.0):
    b, s, wq = q.shape
    sk = kt.shape[2]
    tq = min(s, ATTN_TQ)
    return pl.pallas_call(
        functools.partial(_attn_kernel, maps=maps, diff=diff, lam_init=lam_init),
        grid=(b, s // tq),
        in_specs=[
            pl.BlockSpec((1, tq, wq), lambda i, j: (i, j, 0)),
            pl.BlockSpec((1, wq, sk), lambda i, j: (i, 0, 0)),
            pl.BlockSpec((1, N_HEADS, sk, LANE), lambda i, j: (i, 0, 0, 0)),
            _const_spec((4, DIFF_DIM)),
            _const_spec((BR_W, BR_W)),
            _const_spec((1, BR_W)),
        ],
        out_specs=pl.BlockSpec((1, tq, BR_W), lambda i, j: (i, j, 0)),
        out_shape=jax.ShapeDtypeStruct((b, s, BR_W), BF16),
        compiler_params=_cparams(("parallel", "parallel")),
        name="attn_diff" if diff else "attn",
    )(q, kt, v, lw["diff_lam"], lw["g64"], lw["subln"])


_NT = (((1,), (1,)), ((), ()))


def _na_kernel(q_ref, k_ref, v_ref, kc_ref, vc_ref, bias_ref, o_ref, *, rows):
    nj = rows // 2
    band = NA_BAND_ROWS * GRID_W
    pair = 2 * GRID_W
    for sub in range(NA_PAIRS):
        j = pl.program_id(1) * NA_PAIRS + sub
        start = jnp.clip(2 * j - 4, 0, rows - NA_BAND_ROWS)
        base = pl.multiple_of(start * GRID_W, 2 * GRID_W)
        cls = jnp.where(j < 2, j, jnp.where(j >= nj - 2, j - (nj - 2) + 3, 2))
        keys = jnp.concatenate([k_ref[0, pl.ds(base, band), :], kc_ref[0]], axis=0)
        q = q_ref[0, sub * pair:(sub + 1) * pair, :]
        heads = []
        for h in range(N_HEADS):
            (w0, lo, hi), = MAPS_NA[h]
            qw = q[:, w0:w0 + LANE]
            qm = jnp.where(_lane_mask(LANE, lo, hi), qw, jnp.zeros_like(qw))
            s = lax.dot_general(qm, keys[:, w0:w0 + LANE], _NT, preferred_element_type=F32) + bias_ref[cls, h]
            p = jnp.exp2(s - jnp.max(s, axis=-1, keepdims=True)).astype(BF16)
            vals = jnp.concatenate([v_ref[0, h, pl.ds(base, band), :], vc_ref[0, h]], axis=0)
            o = jnp.dot(p, vals, preferred_element_type=F32)
            heads.append(o * (1.0 / o[:, V_HEAD:V_HEAD + 1]))
        low = _lane_mask(LANE, 0, V_HEAD)
        acc = jnp.concatenate(
            [jnp.where(low, heads[h], pltpu.roll(heads[h + 1], V_HEAD, axis=1)) for h in range(0, N_HEADS, 2)],
            axis=1)
        o_ref[0, sub * pair:(sub + 1) * pair, :] = acc.astype(BF16)


def _na_call(q, k, v, kc, vc, bias):
    b, s, w = q.shape
    n_ctx = kc.shape[1]
    rows = s // GRID_W
    tq = 2 * GRID_W * NA_PAIRS
    return pl.pallas_call(
        functools.partial(_na_kernel, rows=rows),
        grid=(b, rows // (2 * NA_PAIRS)),
        in_specs=[
            pl.BlockSpec((1, tq, w), lambda i, j: (i, j, 0)),
            pl.BlockSpec((1, s, w), lambda i, j: (i, 0, 0)),
            pl.BlockSpec((1, N_HEADS, s, LANE), lambda i, j: (i, 0, 0, 0)),
            pl.BlockSpec((1, n_ctx, w), lambda i, j: (i, 0, 0)),
            pl.BlockSpec((1, N_HEADS, n_ctx, LANE), lambda i, j: (i, 0, 0, 0)),
            _const_spec(bias.shape),
        ],
        out_specs=pl.BlockSpec((1, tq, w), lambda i, j: (i, j, 0)),
        out_shape=jax.ShapeDtypeStruct((b, s, w), BF16),
        compiler_params=_cparams(("parallel", "arbitrary")),
        name="na",
    )(q, k, v, kc, vc, bias)


def _merge_kernel(x_ref, mod_ref, n1g_ref, n2g_ref, uc_ref, om_ref, od_ref, on_ref,
                  gw_ref, gb_ref, wc_ref, wm_ref, wd_ref, wn_ref, wo_ref, rwt_ref, rb_ref,
                  x1_ref, h2_ref, ids_ref, wts_ref):
    x = x_ref[0]
    mod = mod_ref[0]
    h = _modulate(x, n1g_ref[...], mod[0:1], mod[1:2]).astype(BF16)
    y = jnp.zeros(x.shape, F32)
    branches = ((uc_ref, wc_ref), (om_ref, wm_ref), (od_ref, wd_ref), (on_ref, wn_ref))
    for i, (o_ref, w_ref) in enumerate(branches):
        lo = D_MODEL * i
        g = _sigmoid(jnp.dot(h, gw_ref[:, lo:lo + D_MODEL], preferred_element_type=F32)
                     + gb_ref[:, lo:lo + D_MODEL])
        y = y + g * jnp.dot(o_ref[0], w_ref[...], preferred_element_type=F32)
    out = jnp.dot(y.astype(BF16), wo_ref[...], preferred_element_type=F32)
    x1 = x + mod[2:3] * out
    x1_ref[0] = x1
    h2 = _modulate(x1, n2g_ref[...], mod[3:4], mod[4:5])
    for q, piece in enumerate(_pack_row(h2)):
        h2_ref[q, 0] = piece

    logits = lax.dot_general(rwt_ref[...], h2, _NT, preferred_element_type=F32,
                             precision=lax.Precision.HIGHEST) + rb_ref[...]
    eidx = lax.broadcasted_iota(jnp.int32, logits.shape, 0).astype(F32)
    vals, idxs = [], []
    cur = logits
    for _ in range(TOP_K):
        m = jnp.max(cur, axis=0, keepdims=True)
        idx = jnp.min(jnp.where(cur == m, eidx, float(N_EXPERTS)), axis=0, keepdims=True)
        vals.append(m)
        idxs.append(idx)
        cur = jnp.where(eidx == idx, -jnp.inf, cur)
    es = [jnp.exp(vk - vals[0]) for vk in vals]
    den = es[0] + es[1] + es[2] + es[3]
    ids_ref[0] = jnp.concatenate(idxs, axis=0).astype(jnp.int32)
    wts_ref[0] = jnp.concatenate([e / den for e in es], axis=0)


def _merge_call(x, mods, mod_row, lw, uc, om, od, on):
    b, s, d = x.shape
    t = min(s, 512)
    if mod_row is None:
        mod_map = lambda i, j: (i, 0, 0)
    else:
        mod_map = lambda i, j: (mod_row, 0, 0)
    tok = lambda w: pl.BlockSpec((1, t, w), lambda i, j: (i, j, 0))
    rt = pl.BlockSpec((1, TOP_K, t), lambda i, j: (i, 0, j))
    return pl.pallas_call(
        _merge_kernel,
        grid=(b, s // t),
        in_specs=[
            tok(d), pl.BlockSpec((1, 6, d), mod_map), _const_spec((1, d)), _const_spec((1, d)),
            tok(BR_W), tok(BR_W), tok(BR_W), tok(BR_W),
            _const_spec((d, N_BRANCH * d)), _const_spec((1, N_BRANCH * d)),
            _const_spec((BR_W, d)), _const_spec((BR_W, d)), _const_spec((BR_W, d)), _const_spec((BR_W, d)),
            _const_spec((d, d)), _const_spec((N_EXPERTS, d)), _const_spec((N_EXPERTS, 1)),
        ],
        out_specs=[tok(d), pl.BlockSpec((ROW_PARTS, 1, t, ROW_Q), lambda i, j: (0, i, j, 0)), rt, rt],
        out_shape=[jax.ShapeDtypeStruct((b, s, d), F32), jax.ShapeDtypeStruct((ROW_PARTS, b, s, ROW_Q), jnp.int32),
                   jax.ShapeDtypeStruct((b, TOP_K, s), jnp.int32), jax.ShapeDtypeStruct((b, TOP_K, s), F32)],
        compiler_params=_cparams(("parallel", "parallel")),
        name="merge",
    )(x, mods, lw["n1g"], lw["n2g"], uc, om, od, on, lw["gate_w"], lw["gate_b"],
      lw["conv_out"], lw["mla_out"], lw["diff_out"], lw["na_out"], lw["w_o"], lw["router_wt"], lw["router_b"])


SC_WINDOW = 128
ROW_Q = D_MODEL // 4
ROW_PARTS = 2
HALF_D = D_MODEL // 2


def _pack_bf16_pair(a, b):
    ua = lax.bitcast_convert_type(a.astype(BF16).astype(F32), jnp.int32)
    ub = lax.bitcast_convert_type(b.astype(BF16).astype(F32), jnp.int32)
    return ua | lax.shift_right_logical(ub, jnp.int32(16))


def _unpack_bf16_pair(w):
    a = lax.bitcast_convert_type(w & jnp.int32(-65536), F32)
    b = lax.bitcast_convert_type(lax.shift_left(w, jnp.int32(16)), F32)
    return a, b


def _pack_row(x):
    w = _pack_bf16_pair(x[:, :HALF_D], x[:, HALF_D:])
    return [w[:, ROW_Q * q:ROW_Q * (q + 1)] for q in range(ROW_PARTS)]


def _unpack_row(pieces):
    ab = [_unpack_bf16_pair(w) for w in pieces]
    return jnp.concatenate([a for a, _ in ab] + [b for _, b in ab], axis=1)


def _route_slots(ids, tile):
    n = ids.shape[1]
    p = TOP_K * n
    e = ids.reshape(p)
    onehot = (e[:, None] == jnp.arange(N_EXPERTS, dtype=jnp.int32)[None, :])
    chunk = 512
    oh3 = onehot.astype(F32).reshape(p // chunk, chunk, N_EXPERTS)
    within = jnp.einsum("ij,cje->cie", jnp.tril(jnp.ones((chunk, chunk), F32)), oh3)
    totals = within[:, -1, :]
    before = jnp.cumsum(totals, axis=0) - totals
    csum = (within + before[:, None, :]).reshape(p, N_EXPERTS).astype(jnp.int32)
    onehot = onehot.astype(jnp.int32)
    counts = csum[-1]
    padded = ((counts + tile - 1) // tile) * tile
    gend = jnp.cumsum(padded)
    gstart = gend - padded
    slot = jnp.sum(onehot * (csum - 1 + gstart[None, :]), axis=1).astype(jnp.int32)
    n_tiles = p // tile + N_EXPERTS
    tile_start = jnp.arange(n_tiles, dtype=jnp.int32) * tile
    texp = jnp.sum((tile_start[:, None] >= gend[None, :]).astype(jnp.int32), axis=1)
    texp = jnp.minimum(texp, N_EXPERTS - 1)
    nreal = jnp.clip(gstart[texp] + counts[texp] - tile_start, 0, tile)
    nreal = jnp.where(tile_start < gend[-1], nreal, 0).astype(jnp.int32)
    return slot, texp, nreal, n_tiles


def _sc_mesh():
    return plsc.VectorSubcoreMesh(core_axis_name="c", subcore_axis_name="s")


def _sc_dispatch(hq, idx, n_slots):
    parts, n, w = hq.shape
    src = hq.reshape(parts * n, w)
    m = idx.shape[0]
    blocks_per_q = n // SC_WINDOW
    per_q = TOP_K * blocks_per_q

    @pl.kernel(out_type=jax.ShapeDtypeStruct((parts * n_slots, w), hq.dtype), mesh=_sc_mesh(), scratch_types=[])
    def kern(x_hbm, i_hbm, o_hbm):
        def body(x_vmem, i_vmem):
            pltpu.sync_copy(x_vmem, o_hbm.at[i_vmem.at[0]])

        pltpu.emit_pipeline(
            body,
            grid=(m // SC_WINDOW,),
            in_specs=[
                pl.BlockSpec((SC_WINDOW, w), index_map=lambda i: ((i // per_q) * blocks_per_q + i % blocks_per_q, 0)),
                pl.BlockSpec((1, SC_WINDOW), index_map=lambda i: (0, i)),
            ],
            out_specs=[],
            core_axis_name=("c", "s"),
            dimension_semantics=(pltpu.PARALLEL,),
        )(x_hbm, i_hbm)

    return kern(src, idx.reshape(1, m)).reshape(parts, n_slots, w)


def _sc_collect(ys, idx):
    parts, n_slots, w = ys.shape
    src = ys.reshape(parts * n_slots, w)
    m = idx.shape[0]

    @pl.kernel(out_type=jax.ShapeDtypeStruct((m, w), ys.dtype), mesh=_sc_mesh(), scratch_types=[])
    def kern(x_hbm, i_hbm, o_hbm):
        def body(i_vmem, o_vmem):
            pltpu.sync_copy(x_hbm.at[i_vmem.at[0]], o_vmem)

        pltpu.emit_pipeline(
            body,
            grid=(m // SC_WINDOW,),
            in_specs=[pl.BlockSpec((1, SC_WINDOW), index_map=lambda i: (0, i))],
            out_specs=[pl.BlockSpec((SC_WINDOW, w), index_map=lambda i: (i, 0))],
            core_axis_name=("c", "s"),
            dimension_semantics=(pltpu.PARALLEL,),
        )(i_hbm, o_hbm)

    return kern(src, idx.reshape(1, m))


def _ffn_sorted_kernel(texp_ref, nreal_ref, x_ref, wgu_ref, bgu_ref, wd_ref, bd_ref, y_ref, wgu_bf, wd_bf, *, tile):
    i = pl.program_id(0)
    nreal = nreal_ref[i]

    @pl.when((nreal > 0) & ((i == 0) | (texp_ref[i] != texp_ref[jnp.maximum(i - 1, 0)])))
    def _():
        wgu_bf[...] = wgu_ref[0].astype(BF16)
        wd_bf[...] = wd_ref[0].astype(BF16)

    @pl.when(nreal > 0)
    def _():
        x = _unpack_row([x_ref[q] for q in range(ROW_PARTS)])
        row = lax.broadcasted_iota(jnp.int32, (tile, 1), 0)
        x = jnp.where(row < nreal, x, 0.0).astype(BF16)
        gu = jnp.dot(x, wgu_bf[...], preferred_element_type=F32) + bgu_ref[0]
        g = jnp.minimum(gu[:, :D_FF], SWIGLU_LIMIT)
        u = jnp.clip(gu[:, D_FF:], -SWIGLU_LIMIT, SWIGLU_LIMIT)
        act = ((u + 1.0) * (g * _sigmoid(SWIGLU_ALPHA * g))).astype(BF16)
        y = jnp.dot(act, wd_bf[...], preferred_element_type=F32) + bd_ref[0]
        for q, piece in enumerate(_pack_row(y)):
            y_ref[q] = piece

    @pl.when(nreal == 0)
    def _():
        y_ref[...] = jnp.zeros(y_ref.shape, jnp.int32)


def _ffn_sorted_call(xs, texp, nreal, lw, tile):
    _, n_slots, w = xs.shape
    n_tiles = n_slots // tile
    d = D_MODEL
    off = lw["exp_off"]
    grid_spec = pltpu.PrefetchScalarGridSpec(
        num_scalar_prefetch=2,
        grid=(n_tiles,),
        in_specs=[
            pl.BlockSpec((ROW_PARTS, tile, w), lambda i, te, nr: (0, i, 0)),
            pl.BlockSpec((1, d, 2 * D_FF), lambda i, te, nr: (te[i] + off, 0, 0)),
            pl.BlockSpec((1, 1, 2 * D_FF), lambda i, te, nr: (te[i] + off, 0, 0)),
            pl.BlockSpec((1, D_FF, d), lambda i, te, nr: (te[i] + off, 0, 0)),
            pl.BlockSpec((1, 1, d), lambda i, te, nr: (te[i] + off, 0, 0)),
        ],
        out_specs=pl.BlockSpec((ROW_PARTS, tile, w), lambda i, te, nr: (0, i, 0)),
        scratch_shapes=[pltpu.VMEM((d, 2 * D_FF), BF16), pltpu.VMEM((D_FF, d), BF16)],
    )
    return pl.pallas_call(
        functools.partial(_ffn_sorted_kernel, tile=tile),
        grid_spec=grid_spec,
        out_shape=jax.ShapeDtypeStruct((ROW_PARTS, n_slots, w), jnp.int32),
        compiler_params=_cparams(("arbitrary",)),
        name="moe_ffn",
    )(texp, nreal, xs, lw["exp_w_gu"], lw["exp_b_gu"], lw["exp_w_down"], lw["exp_b_down"])


def _combine_q_kernel(x1_ref, mod_ref, w_ref, y_ref, o_ref):
    w = w_ref[0]
    g2 = mod_ref[0][5:6]
    for q in range(ROW_PARTS):
        acc_a, acc_b = None, None
        for k in range(TOP_K):
            a, b = _unpack_bf16_pair(y_ref[q, k])
            wk = w[:, k:k + 1]
            acc_a = wk * a if acc_a is None else acc_a + wk * a
            acc_b = wk * b if acc_b is None else acc_b + wk * b
        for lo, acc in ((ROW_Q * q, acc_a), (HALF_D + ROW_Q * q, acc_b)):
            o_ref[0, :, lo:lo + ROW_Q] = x1_ref[0, :, lo:lo + ROW_Q] + g2[:, lo:lo + ROW_Q] * acc


def _combine_q_call(x1, mods, mod_row, wts, y, tok_off):
    b, s, d = x1.shape
    t = min(s, 512)
    nt = s // t
    blk_off = tok_off // t
    if mod_row is None:
        mod_map = lambda i, j: (i, 0, 0)
    else:
        mod_map = lambda i, j: (mod_row, 0, 0)
    wts = wts.transpose(0, 2, 1)
    return pl.pallas_call(
        _combine_q_kernel,
        grid=(b, nt),
        in_specs=[
            pl.BlockSpec((1, t, d), lambda i, j: (i, j, 0)),
            pl.BlockSpec((1, 6, d), mod_map),
            pl.BlockSpec((1, t, TOP_K), lambda i, j: (i, j, 0)),
            pl.BlockSpec((ROW_PARTS, TOP_K, t, ROW_Q), lambda i, j: (0, 0, blk_off + i * nt + j, 0)),
        ],
        out_specs=pl.BlockSpec((1, t, d), lambda i, j: (i, j, 0)),
        out_shape=jax.ShapeDtypeStruct((b, s, d), F32),
        compiler_params=_cparams(("parallel", "parallel")),
        name="moe_combine",
    )(x1, mods, wts, y)


def _moe_sc(streams, mods, lw):
    sizes = [st[0].shape[0] * st[0].shape[1] for st in streams]
    n = sum(sizes)
    tile = 512 if TOP_K * n >= 512 * N_EXPERTS * 4 else 256
    hq = jnp.concatenate([st[1].reshape(ROW_PARTS, m, ROW_Q) for st, m in zip(streams, sizes)], axis=1)
    ids = jnp.concatenate([st[2].transpose(1, 0, 2).reshape(TOP_K, m) for st, m in zip(streams, sizes)], axis=1)
    slot, texp, nreal, n_tiles = _route_slots(ids, tile)
    n_slots = n_tiles * tile
    idx = (slot[None, :] + (jnp.arange(ROW_PARTS, dtype=jnp.int32) * n_slots)[:, None]).reshape(-1)
    xs = _sc_dispatch(hq, idx, n_slots)
    ys = _ffn_sorted_call(xs, texp, nreal, lw, tile)
    y = _sc_collect(ys, idx).reshape(ROW_PARTS, TOP_K, n, ROW_Q)
    outs, off = [], 0
    for (x1, _, _, wts, mod_row), m in zip(streams, sizes):
        outs.append(_combine_q_call(x1, mods, mod_row, wts, y, off))
        off += m
    return outs


def _layer_weights(l, p, lam_init):
    w = p["w_in"][l].astype(BF16)
    d = w.shape[0]
    zcols = lambda n: jnp.zeros((d, n), w.dtype)
    regroup = lambda blk: blk.reshape(d, N_HEADS, 3, HEAD_DIM).transpose(0, 2, 1, 3).reshape(d, 3 * BR_W)
    w_in = jnp.concatenate([
        w[:, :A_IN],
        w[:, OFF_B:OFF_B + Q_LORA], zcols(P_CKV - P_CQ - Q_LORA),
        w[:, OFF_B + Q_LORA:OFF_B + Q_LORA + KV_LORA],
        w[:, OFF_B + Q_LORA + KV_LORA:OFF_C], zcols(P_DQ - P_KPE - QK_ROPE),
        regroup(w[:, OFF_C:OFF_D]), regroup(w[:, OFF_D:]),
    ], axis=1)
    assert w_in.shape[1] == PROJ_W

    def head_slots(w3, slot):
        w3 = jnp.pad(w3, ((0, 0), (0, 0), (0, slot - w3.shape[2])))
        return w3.reshape(w3.shape[0], N_HEADS * slot)

    wuq = head_slots(p["mla_w_uq"][l].reshape(Q_LORA, N_HEADS, MLA_QK), LANE)
    wuq = jnp.pad(wuq, ((0, 256 - Q_LORA), (0, 0)))
    wukv = p["mla_w_ukv"][l].reshape(KV_LORA, N_HEADS, QK_NOPE + V_HEAD)
    wk = head_slots(wukv[:, :, :QK_NOPE], LANE)
    wv = head_slots(wukv[:, :, QK_NOPE:], V_HEAD)
    ppe = np.zeros((LANE, 512), np.float32)
    for h in range(N_HEADS):
        for i in range(QK_ROPE):
            ppe[i, h * LANE + QK_NOPE + i] = 1.0

    def slot_gain(g, scale):
        g = jnp.concatenate([g * scale, jnp.zeros((LANE - MLA_QK,), F32)])
        return jnp.tile(g, N_HEADS)

    def row512(v):
        return jnp.concatenate([v, jnp.zeros((512 - v.shape[0],), F32)])

    gains = jnp.stack([
        row512(p["mla_cq_g"][l]),
        slot_gain(p["mla_qn_g"][l], MLA_QK ** -0.5 * LOG2E),
        row512(p["mla_ckv_g"][l]),
        slot_gain(p["mla_kn_g"][l], 1.0),
        row512(jnp.tile(p["diff_qn_g"][l], 2 * N_HEADS) * DIFF_DIM ** -0.5 * LOG2E),
        row512(jnp.tile(p["diff_kn_g"][l], 2 * N_HEADS)),
        row512(jnp.tile(p["na_qn_g"][l], N_HEADS) * HEAD_DIM ** -0.5 * LOG2E),
        row512(jnp.tile(p["na_kn_g"][l], N_HEADS)),
    ])
    conv_w = jnp.concatenate([p["conv_w"][l], jnp.zeros((1, CONV_CH), F32)], axis=0)
    return dict(
        n1g=p["norm1_g"][l][None, :], n2g=p["norm2_g"][l][None, :],
        w_in=w_in, gains=gains,
        wuq=wuq.astype(BF16), wk=wk.astype(BF16), wv=wv.astype(BF16), ppe=jnp.asarray(ppe, BF16),
        g96=jnp.asarray(_group_ones(512, LANE, MLA_QK), BF16),
        g32=jnp.asarray(_group_ones(BR_W, DIFF_DIM, DIFF_DIM), BF16),
        g64=jnp.asarray(_group_ones(BR_W, HEAD_DIM, HEAD_DIM), BF16),
        rm=jnp.asarray(_rot_matrix(512, LANE, QK_NOPE, QK_ROPE // 2), BF16),
        rd=jnp.asarray(_rot_matrix(BR_W, DIFF_DIM, 0, DIFF_DIM // 2), BF16),
        conv_w=conv_w, conv_b=p["conv_b"][l][None, :],
        conv_ln_g=p["conv_ln_g"][l][None, :], conv_ln_b=p["conv_ln_b"][l][None, :],
        diff_lam=p["diff_lam"][l],
        subln=(jnp.tile(p["diff_subln_g"][l], N_HEADS) * (1.0 - lam_init))[None, :],
        gate_w=p["gate_w"][l].astype(BF16), gate_b=p["gate_b"][l][None, :],
        conv_out=p["conv_out"][l].astype(BF16), mla_out=p["mla_out"][l].astype(BF16),
        diff_out=p["diff_out"][l].astype(BF16), na_out=p["na_out"][l].astype(BF16),
        w_o=p["w_o"][l].astype(BF16),
        router_wt=p["router_w"][l].T, router_b=p["router_b"][l][:, None],
        exp_off=l * N_EXPERTS,
        exp_w_gu=p["exp_w_gu"].reshape((-1,) + p["exp_w_gu"].shape[2:]),
        exp_b_gu=p["exp_b_gu"].reshape(-1, 1, 2 * D_FF),
        exp_w_down=p["exp_w_down"].reshape((-1,) + p["exp_w_down"].shape[2:]),
        exp_b_down=p["exp_b_down"].reshape(-1, 1, D_MODEL),
    )


def _kt(kc, k):
    return jnp.concatenate([kc, k], axis=1).transpose(0, 2, 1)


def kernel(x, c, ctx, c_ctx, ada_w, ada_b, norm1_g, norm2_g, w_in, conv_w, conv_b, conv_ln_g, conv_ln_b, conv_out, mla_cq_g, mla_ckv_g, mla_w_uq, mla_w_ukv, mla_qn_g, mla_kn_g, mla_out, diff_qn_g, diff_kn_g, diff_lam, diff_subln_g, diff_out, na_qn_g, na_kn_g, na_rpb, na_out, gate_w, gate_b, w_o, router_w, router_b, exp_w_gu, exp_b_gu, exp_w_down, exp_b_down):
    p = dict(norm1_g=norm1_g, norm2_g=norm2_g, w_in=w_in, conv_w=conv_w, conv_b=conv_b,
             conv_ln_g=conv_ln_g, conv_ln_b=conv_ln_b, conv_out=conv_out, mla_cq_g=mla_cq_g,
             mla_ckv_g=mla_ckv_g, mla_w_uq=mla_w_uq, mla_w_ukv=mla_w_ukv, mla_qn_g=mla_qn_g,
             mla_kn_g=mla_kn_g, mla_out=mla_out, diff_qn_g=diff_qn_g, diff_kn_g=diff_kn_g,
             diff_lam=diff_lam, diff_subln_g=diff_subln_g, diff_out=diff_out, na_qn_g=na_qn_g,
             na_kn_g=na_kn_g, na_out=na_out, gate_w=gate_w, gate_b=gate_b, w_o=w_o,
             router_w=router_w, router_b=router_b, exp_w_gu=exp_w_gu, exp_b_gu=exp_b_gu,
             exp_w_down=exp_w_down, exp_b_down=exp_b_down)
    b, s, d = x.shape
    n_ctx = ctx.shape[1]
    depth = ada_w.shape[0]
    rows = s // GRID_W
    assert d == D_MODEL and s % (2 * GRID_W) == 0 and rows >= NA_BAND_ROWS and n_ctx % LANE == 0

    mod_rows = -(-(b + 1) // 8) * 8
    cs = jnp.concatenate([c, c_ctx[None, :], jnp.zeros((mod_rows - b - 1, d), F32)], axis=0)
    mods_all = _ada_call(cs, ada_w, ada_b).reshape(depth, mod_rows, 6, d)

    tabs_x = _rope_lane_tables(s)
    tabs_c = (jnp.ones((n_ctx, 512), F32), jnp.zeros((n_ctx, 512), F32),
              jnp.ones((n_ctx, BR_W), F32), jnp.zeros((n_ctx, BR_W), F32))

    xc = ctx
    for l in range(depth):
        last = l == depth - 1
        lam_init = 0.8 - 0.6 * math.exp(-0.3 * l)
        lw = _layer_weights(l, p, lam_init)
        mods = mods_all[l]
        bias = _na_bias_tables(na_rpb[l], rows, n_ctx)

        u, mq, mk, mv, dq, dk, dv, nq, nk, nv = _proj_call(x, mods, None, lw, tabs_x, True)
        uc, mqc, mkc, mvc, dqc, dkc, dvc, nqc, nkc, nvc = _proj_call(xc, mods, b, lw, tabs_c, False)

        y_conv = _conv_call(u, lw)
        y_mla, y_diff = _attn_pair_call(mq, _kt(mkc, mk), jnp.concatenate([mvc, mv], axis=2),
                                        dq, _kt(dkc, dk), jnp.concatenate([dvc, dv], axis=2), lw, lam_init)
        y_na = _na_call(nq, nk, nv, nkc, nvc, bias)
        x1, h2, ids, wts = _merge_call(x, mods, None, lw, y_conv, y_mla, y_diff, y_na)
        streams = [(x1, h2, ids, wts, None)]

        if not last:
            yc_conv = _conv_call(uc, lw)
            yc_mla = _attn_call(mqc, mkc.transpose(0, 2, 1), mvc, lw, MAPS_MLA)
            yc_diff = _attn_call(dqc, dkc.transpose(0, 2, 1), dvc, lw, MAPS_DIFF, diff=True, lam_init=lam_init)
            yc_na = _attn_call(nqc, nkc.transpose(0, 2, 1), nvc, lw, MAPS_NA)
            xc1, h2c, idsc, wtsc = _merge_call(xc, mods, b, lw, yc_conv, yc_mla, yc_diff, yc_na)
            streams.append((xc1, h2c, idsc, wtsc, b))

        outs = _moe_sc(streams, mods, lw)
        x = outs[0]
        if not last:
            xc = outs[1]
    return x
```

```python
import functools
import math

import numpy as np
import jax
import jax.numpy as jnp
from jax import lax
from jax.experimental import pallas as pl
from jax.experimental.pallas import tpu as pltpu
from jax.experimental.pallas import tpu_sc as plsc

F32 = jnp.float32
BF16 = jnp.bfloat16

D_MODEL = 1024
GRID_W = 64
N_BRANCH = 4
N_HEADS = 4
HEAD_DIM = 64
CONV_CH = 256
CONV_WIDTH = 31
Q_LORA = 192
KV_LORA = 128
QK_NOPE = 64
QK_ROPE = 32
V_HEAD = 64
DIFF_DIM = 32
DIFF_V = 2 * DIFF_DIM
NA_KH = 8
NA_KW = 16
ROPE_DIM = 32
ROPE_BASE = 10000.0
N_EXPERTS = 32
TOP_K = 4
D_FF = 1024
SWIGLU_LIMIT = 7.0
SWIGLU_ALPHA = 1.702
EPS = 1e-6
NEG_INF = -1e30

A_IN = 2 * CONV_CH
B_IN = Q_LORA + KV_LORA + QK_ROPE
C_IN = N_HEADS * (4 * DIFF_DIM + DIFF_V)
D_IN = N_HEADS * 3 * HEAD_DIM
OFF_B = A_IN
OFF_C = OFF_B + B_IN
OFF_D = OFF_C + C_IN

LANE = 128
SUBLANE = 8
MLA_QK = QK_NOPE + QK_ROPE
BR_W = N_HEADS * HEAD_DIM
PROJ_W = 2560
NA_BAND_ROWS = 10
ATTN_TQ = 512
NA_PAIRS = 4
LOG2E = math.log2(math.e)
VMEM_LIMIT = 52 * 1024 * 1024

P_A, P_G, P_CQ, P_CKV, P_KPE = 0, 256, 512, 768, 896
P_DQ, P_DK, P_DV = 1024, 1280, 1536
P_NQ, P_NK, P_NV = 1792, 2048, 2304


def _sigmoid(x):
    return 0.5 * jnp.tanh(0.5 * x) + 0.5


def _modulate(x, g, shift, scale):
    ms = jnp.mean(x * x, axis=-1, keepdims=True)
    return (x * lax.rsqrt(ms + EPS) * g) * (1.0 + scale) + shift


def _cparams(sem):
    return pltpu.CompilerParams(dimension_semantics=sem, vmem_limit_bytes=VMEM_LIMIT)


def _const_spec(shape):
    n = len(shape)
    return pl.BlockSpec(shape, lambda *_: (0,) * n)


def _group_ones(width, slot, real):
    i = np.arange(width)
    valid = (i % slot) < real
    same = (i[:, None] // slot) == (i[None, :] // slot)
    return (same & valid[:, None] & valid[None, :]).astype(np.float32)


def _rot_matrix(width, slot, start, half):
    r = np.zeros((width, width), np.float32)
    for s0 in range(0, width, slot):
        for i in range(half):
            a, b = s0 + start + i, s0 + start + half + i
            r[b, a] = -1.0
            r[a, b] = 1.0
    return r


def _rope_lane_tables(n_tokens):
    t = jnp.arange(n_tokens, dtype=jnp.int32)
    rows = (t // GRID_W).astype(F32)
    cols = (t % GRID_W).astype(F32)
    axis_dim = ROPE_DIM // 2
    inv = ROPE_BASE ** (-jnp.arange(0, axis_dim, 2, dtype=F32) / axis_dim)
    theta = jnp.concatenate([rows[:, None] * inv, cols[:, None] * inv], axis=-1)
    cos, sin = jnp.cos(theta), jnp.sin(theta)
    half = ROPE_DIM // 2
    ones = jnp.ones((n_tokens, QK_NOPE), F32)
    zeros = jnp.zeros((n_tokens, QK_NOPE), F32)
    pad1 = jnp.ones((n_tokens, LANE - MLA_QK), F32)
    pad0 = jnp.zeros((n_tokens, LANE - MLA_QK), F32)
    cm = jnp.tile(jnp.concatenate([ones, cos, cos, pad1], -1), (1, N_HEADS))
    sm = jnp.tile(jnp.concatenate([zeros, sin, sin, pad0], -1), (1, N_HEADS))
    cd = jnp.tile(jnp.concatenate([cos, cos], -1), (1, 2 * N_HEADS))
    sd = jnp.tile(jnp.concatenate([sin, sin], -1), (1, 2 * N_HEADS))
    assert half * 2 == DIFF_DIM
    return cm, sm, cd, sd


def _na_bias_tables(rpb, rows, n_ctx):
    kh = min(NA_KH, rows)
    nj = rows // 2
    reps = np.array([0, 1, 2, nj - 2, nj - 1])
    n_ro, n_co = 2 * NA_KH - 1, 2 * NA_KW - 1
    start = np.clip(2 * reps - 4, 0, rows - NA_BAND_ROWS)
    r = 2 * reps[:, None] + np.arange(2)[None, :]
    kr = start[:, None] + np.arange(NA_BAND_ROWS)[None, :]
    row_start = np.clip(r - kh // 2, 0, rows - kh)
    vr = (kr[:, None, :] >= row_start[:, :, None]) & (kr[:, None, :] < row_start[:, :, None] + kh)
    ro = np.clip(kr[:, None, :] - r[:, :, None] + NA_KH - 1, 0, n_ro - 1)
    qc = np.arange(GRID_W)
    win_start = np.clip(qc - NA_KW // 2, 0, GRID_W - NA_KW)
    vc = (qc[None, :] >= win_start[:, None]) & (qc[None, :] < win_start[:, None] + NA_KW)
    co = np.clip(qc[None, :] - qc[:, None] + NA_KW - 1, 0, n_co - 1)
    rsel = (ro[..., None] == np.arange(n_ro)).astype(np.float32)
    csel = (co[None] == np.arange(n_co)[:, None, None]).astype(np.float32)
    hi = lax.Precision.HIGHEST
    t1 = jnp.einsum("cqav,hvw->hcqaw", rsel, rpb.astype(F32), precision=hi)
    b = jnp.einsum("hcqaw,wxy->chqxay", t1, csel, precision=hi)
    valid = vr[:, None, :, None, :, None] & vc[None, None, None, :, None, :]
    b = jnp.where(valid, b * LOG2E, NEG_INF)
    b = b.reshape(len(reps), N_HEADS, 2 * GRID_W, NA_BAND_ROWS * GRID_W)
    return jnp.concatenate([b, jnp.zeros(b.shape[:3] + (n_ctx,), F32)], axis=-1)


def _ada_kernel(c_ref, w_ref, b_ref, o_ref):
    c = c_ref[...]
    s = c * _sigmoid(c)
    o_ref[0] = jnp.dot(s, w_ref[0], preferred_element_type=F32,
                       precision=lax.Precision.HIGHEST) + b_ref[0]


def _ada_call(cs, ada_w, ada_b):
    depth, d, n = ada_w.shape
    rows = cs.shape[0]
    tn = 1536
    return pl.pallas_call(
        _ada_kernel,
        grid=(depth, n // tn),
        in_specs=[
            pl.BlockSpec((rows, d), lambda l, j: (0, 0)),
            pl.BlockSpec((1, d, tn), lambda l, j: (l, 0, j)),
            pl.BlockSpec((1, 1, tn), lambda l, j: (l, 0, j)),
        ],
        out_specs=pl.BlockSpec((1, rows, tn), lambda l, j: (l, 0, j)),
        out_shape=jax.ShapeDtypeStruct((depth, rows, n), F32),
        compiler_params=_cparams(("arbitrary", "arbitrary")),
        name="ada",
    )(cs, ada_w, ada_b.reshape(depth, 1, n))


def _group_norm(x, ones_ref, inv_n):
    sq = (x * x).astype(BF16)
    ms = jnp.dot(sq, ones_ref[...], preferred_element_type=F32) * inv_n
    return x * lax.rsqrt(ms + EPS)


def _store_value_heads(ref, v):
    low = _lane_mask(LANE, 0, V_HEAD)
    for h in range(N_HEADS):
        win = v[:, LANE * (h // 2):LANE * (h // 2 + 1)]
        if h % 2:
            win = pltpu.roll(win, V_HEAD, axis=1)
        ref[0, h] = jnp.where(low, win, 1.0).astype(BF16)


def _rope(x, rot_ref, cos_ref, sin_ref):
    rot = jnp.dot(x.astype(BF16), rot_ref[...], preferred_element_type=F32)
    return x * cos_ref[...] + rot * sin_ref[...]


def _proj_kernel(x_ref, mod_ref, n1g_ref, win_ref, gains_ref, wuq_ref, wk_ref, ppe_ref, wv_ref,
                 g96_ref, g32_ref, g64_ref, rm_ref, rd_ref, cm_ref, sm_ref, cd_ref, sd_ref,
                 u_ref, mq_ref, mk_ref, mv_ref, dq_ref, dk_ref, dv_ref, nq_ref, nk_ref, nv_ref,
                 *, use_rope):
    x = x_ref[0]
    mod = mod_ref[0]
    gains = gains_ref[...]
    h = _modulate(x, n1g_ref[...], mod[0:1], mod[1:2]).astype(BF16)
    proj = jnp.dot(h, win_ref[...], preferred_element_type=F32)

    u_ref[0] = proj[:, P_A:P_A + CONV_CH] * _sigmoid(proj[:, P_G:P_G + CONV_CH])

    cq = proj[:, P_CQ:P_CQ + 256]
    ms = jnp.sum(cq * cq, axis=-1, keepdims=True) * (1.0 / Q_LORA)
    cqn = (cq * lax.rsqrt(ms + EPS) * gains[0:1, :256]).astype(BF16)
    q = jnp.dot(cqn, wuq_ref[...], preferred_element_type=F32)
    q = _group_norm(q, g96_ref, 1.0 / MLA_QK) * gains[1:2, :]
    if use_rope:
        q = _rope(q, rm_ref, cm_ref, sm_ref)
    mq_ref[0] = q.astype(BF16)

    ckv = proj[:, P_CKV:P_CKV + KV_LORA]
    ms = jnp.mean(ckv * ckv, axis=-1, keepdims=True)
    ckvn = (ckv * lax.rsqrt(ms + EPS) * gains[2:3, :KV_LORA]).astype(BF16)
    kpe = proj[:, P_KPE:P_KPE + LANE].astype(BF16)
    k = (jnp.dot(ckvn, wk_ref[...], preferred_element_type=F32)
         + jnp.dot(kpe, ppe_ref[...], preferred_element_type=F32))
    k = _group_norm(k, g96_ref, 1.0 / MLA_QK) * gains[3:4, :]
    if use_rope:
        k = _rope(k, rm_ref, cm_ref, sm_ref)
    mk_ref[0] = k.astype(BF16)
    _store_value_heads(mv_ref, jnp.dot(ckvn, wv_ref[...], preferred_element_type=F32))

    qd = _group_norm(proj[:, P_DQ:P_DQ + BR_W], g32_ref, 1.0 / DIFF_DIM) * gains[4:5, :BR_W]
    kd = _group_norm(proj[:, P_DK:P_DK + BR_W], g32_ref, 1.0 / DIFF_DIM) * gains[5:6, :BR_W]
    if use_rope:
        qd = _rope(qd, rd_ref, cd_ref, sd_ref)
        kd = _rope(kd, rd_ref, cd_ref, sd_ref)
    dq_ref[0] = qd.astype(BF16)
    dk_ref[0] = kd.astype(BF16)
    _store_value_heads(dv_ref, proj[:, P_DV:P_DV + BR_W])

    qn = _group_norm(proj[:, P_NQ:P_NQ + BR_W], g64_ref, 1.0 / HEAD_DIM) * gains[6:7, :BR_W]
    kn = _group_norm(proj[:, P_NK:P_NK + BR_W], g64_ref, 1.0 / HEAD_DIM) * gains[7:8, :BR_W]
    nq_ref[0] = qn.astype(BF16)
    nk_ref[0] = kn.astype(BF16)
    _store_value_heads(nv_ref, proj[:, P_NV:P_NV + BR_W])


def _proj_call(x, mods, mod_row, lw, tabs, use_rope):
    b, s, d = x.shape
    t = min(s, 512)
    grid = (b, s // t)
    if mod_row is None:
        mod_map = lambda i, j: (i, 0, 0)
    else:
        mod_map = lambda i, j: (mod_row, 0, 0)
    tok = lambda w: pl.BlockSpec((1, t, w), lambda i, j: (i, j, 0))
    tab = lambda w: pl.BlockSpec((t, w), lambda i, j: (j, 0))
    in_specs = [
        tok(d),
        pl.BlockSpec((1, 6, d), mod_map),
        _const_spec((1, d)),
        _const_spec((d, PROJ_W)),
        _const_spec((8, 512)),
        _const_spec((256, 512)),
        _const_spec((KV_LORA, 512)),
        _const_spec((LANE, 512)),
        _const_spec((KV_LORA, BR_W)),
        _const_spec((512, 512)),
        _const_spec((BR_W, BR_W)),
        _const_spec((BR_W, BR_W)),
        _const_spec((512, 512)),
        _const_spec((BR_W, BR_W)),
        tab(512), tab(512), tab(BR_W), tab(BR_W),
    ]
    widths = [CONV_CH, 512, 512, BR_W, BR_W, BR_W, BR_W, BR_W, BR_W, BR_W]
    dtypes = [F32] + [BF16] * 9
    out_specs = [tok(w) for w in widths]
    out_shape = [jax.ShapeDtypeStruct((b, s, w), dt) for w, dt in zip(widths, dtypes)]
    for i in (3, 6, 9):
        out_specs[i] = pl.BlockSpec((1, N_HEADS, t, LANE), lambda i_, j: (i_, 0, j, 0))
        out_shape[i] = jax.ShapeDtypeStruct((b, N_HEADS, s, LANE), BF16)
    return pl.pallas_call(
        functools.partial(_proj_kernel, use_rope=use_rope),
        grid=grid, in_specs=in_specs, out_specs=out_specs, out_shape=out_shape,
        compiler_params=_cparams(("parallel", "parallel")),
        name="proj",
    )(x, mods, lw["n1g"], lw["w_in"], lw["gains"], lw["wuq"], lw["wk"], lw["ppe"], lw["wv"],
      lw["g96"], lw["g32"], lw["g64"], lw["rm"], lw["rd"], tabs[0], tabs[1], tabs[2], tabs[3])


CONV_TILE = 128
CONV_PAD = 16
CONV_SPAN = CONV_TILE + 2 * CONV_PAD - 8


def _conv_kernel(u_ref, w_ref, cb_ref, lg_ref, lb_ref, o_ref, pad_ref, sh_ref, *, seq):
    zeros = jnp.zeros((CONV_PAD, CONV_CH), F32)
    pad_ref[0:CONV_PAD, :] = zeros
    pad_ref[CONV_PAD + seq:2 * CONV_PAD + seq, :] = zeros

    def fill(i, carry):
        base = pl.multiple_of(i * CONV_TILE, CONV_TILE)
        pad_ref[pl.ds(base + CONV_PAD, CONV_TILE), :] = u_ref[0, pl.ds(base, CONV_TILE), :]
        return carry

    lax.fori_loop(0, seq // CONV_TILE, fill, 0)
    w = w_ref[...]
    cb, lg, lb = cb_ref[...], lg_ref[...], lb_ref[...]

    def tile(i, carry):
        base = pl.multiple_of(i * CONV_TILE, CONV_TILE)
        win = pad_ref[pl.ds(base, CONV_TILE + 2 * CONV_PAD), :]
        acc = jnp.zeros((CONV_TILE, CONV_CH), F32)
        for r in range(SUBLANE):
            sh_ref[r] = win[r:r + CONV_SPAN, :]
        for r in range(SUBLANE):
            for j in range(CONV_WIDTH):
                if (j + 1) % SUBLANE == r:
                    a = j + 1 - r
                    acc = acc + sh_ref[r, a:a + CONV_TILE, :] * w[j:j + 1, :]
        c = acc + cb
        mu = jnp.mean(c, axis=-1, keepdims=True)
        cc = c - mu
        var = jnp.mean(cc * cc, axis=-1, keepdims=True)
        y = cc * lax.rsqrt(var + EPS) * lg + lb
        o_ref[0, pl.ds(base, CONV_TILE), :] = (y * _sigmoid(y)).astype(BF16)
        return carry

    lax.fori_loop(0, seq // CONV_TILE, tile, 0)


def _conv_call(u, lw):
    b, s, ch = u.shape
    return pl.pallas_call(
        functools.partial(_conv_kernel, seq=s),
        grid=(b,),
        in_specs=[
            pl.BlockSpec((1, s, ch), lambda i: (i, 0, 0)),
            _const_spec((32, ch)), _const_spec((1, ch)), _const_spec((1, ch)), _const_spec((1, ch)),
        ],
        out_specs=pl.BlockSpec((1, s, ch), lambda i: (i, 0, 0)),
        out_shape=jax.ShapeDtypeStruct((b, s, ch), BF16),
        scratch_shapes=[pltpu.VMEM((s + 2 * CONV_PAD, ch), F32), pltpu.VMEM((SUBLANE, CONV_SPAN, ch), F32)],
        compiler_params=_cparams(("parallel",)),
        name="conv",
    )(u, lw["conv_w"], lw["conv_b"], lw["conv_ln_g"], lw["conv_ln_b"])


def _lane_mask(width, lo, hi):
    lane = lax.broadcasted_iota(jnp.int32, (1, width), 1)
    return (lane >= lo) & (lane < hi)


def _softmax_pv(qw, kt, v1):
    s = jnp.dot(qw, kt, preferred_element_type=F32)
    m = jnp.max(s, axis=-1, keepdims=True)
    p = jnp.exp2(s - m).astype(BF16)
    o = jnp.dot(p, v1, preferred_element_type=F32)
    return o * (1.0 / o[:, V_HEAD:V_HEAD + 1])


def _attn_kernel(q_ref, kt_ref, v_ref, lam_ref, g64_ref, sg_ref, o_ref, *, maps, diff, lam_init):
    if diff:
        lv = lam_ref[...]
        lam = (jnp.exp(jnp.sum(lv[0:1] * lv[1:2], axis=-1, keepdims=True))
               - jnp.exp(jnp.sum(lv[2:3] * lv[3:4], axis=-1, keepdims=True)) + lam_init)
    heads = []
    for h in range(N_HEADS):
        outs = []
        for (w0, lo, hi) in maps[h]:
            qw = q_ref[0, :, w0:w0 + LANE]
            if (lo, hi) != (0, LANE):
                qw = jnp.where(_lane_mask(LANE, lo, hi), qw, jnp.zeros_like(qw))
            outs.append(_softmax_pv(qw, kt_ref[0, w0:w0 + LANE, :], v_ref[0, h]))
        heads.append(outs[0] - lam * outs[1] if diff else outs[0])
    low = _lane_mask(LANE, 0, V_HEAD)
    acc = jnp.concatenate(
        [jnp.where(low, heads[h], pltpu.roll(heads[h + 1], V_HEAD, axis=1)) for h in range(0, N_HEADS, 2)],
        axis=1)
    if diff:
        acc = _group_norm(acc, g64_ref, 1.0 / DIFF_V) * sg_ref[...]
    o_ref[0] = acc.astype(BF16)


def _attn_pair_kernel(mq_ref, mkt_ref, mv_ref, dq_ref, dkt_ref, dv_ref, lam_ref, g64_ref, sg_ref, om_ref, od_ref,
                      *, lam_init):
    _attn_kernel(mq_ref, mkt_ref, mv_ref, lam_ref, g64_ref, sg_ref, om_ref, maps=MAPS_MLA, diff=False, lam_init=0.0)
    _attn_kernel(dq_ref, dkt_ref, dv_ref, lam_ref, g64_ref, sg_ref, od_ref, maps=MAPS_DIFF, diff=True,
                 lam_init=lam_init)


def _attn_pair_call(mq, mkt, mv, dq, dkt, dv, lw, lam_init):
    b, s, _ = mq.shape
    sk = mkt.shape[2]
    tq = min(s, ATTN_TQ)
    tile = lambda w: pl.BlockSpec((1, tq, w), lambda i, j: (i, j, 0))
    keys = lambda w: pl.BlockSpec((1, w, sk), lambda i, j: (i, 0, 0))
    vals = pl.BlockSpec((1, N_HEADS, sk, LANE), lambda i, j: (i, 0, 0, 0))
    return pl.pallas_call(
        functools.partial(_attn_pair_kernel, lam_init=lam_init),
        grid=(b, s // tq),
        in_specs=[tile(mq.shape[2]), keys(mq.shape[2]), vals, tile(dq.shape[2]), keys(dq.shape[2]), vals,
                  _const_spec((4, DIFF_DIM)), _const_spec((BR_W, BR_W)), _const_spec((1, BR_W))],
        out_specs=[tile(BR_W), tile(BR_W)],
        out_shape=[jax.ShapeDtypeStruct((b, s, BR_W), BF16), jax.ShapeDtypeStruct((b, s, BR_W), BF16)],
        compiler_params=_cparams(("parallel", "parallel")),
        name="attn_pair",
    )(mq, mkt, mv, dq, dkt, dv, lw["diff_lam"], lw["g64"], lw["subln"])


MAPS_MLA = tuple(((LANE * h, 0, LANE),) for h in range(N_HEADS))
MAPS_DIFF = tuple(tuple((LANE * (h // 2), 64 * (h % 2) + 32 * c, 64 * (h % 2) + 32 * c + 32) for c in range(2))
                  for h in range(N_HEADS))
MAPS_NA = tuple(((LANE * (h // 2), 64 * (h % 2), 64 * (h % 2) + 64),) for h in range(N_HEADS))


def _attn_call(q, kt, v, lw, maps, diff=False, lam_init=0.0):
    b, s, wq = q.shape
    sk = kt.shape[2]
    tq = min(s, ATTN_TQ)
    return pl.pallas_call(
        functools.partial(_attn_kernel, maps=maps, diff=diff, lam_init=lam_init),
        grid=(b, s // tq),
        in_specs=[
            pl.BlockSpec((1, tq, wq), lambda i, j: (i, j, 0)),
            pl.BlockSpec((1, wq, sk), lambda i, j: (i, 0, 0)),
            pl.BlockSpec((1, N_HEADS, sk, LANE), lambda i, j: (i, 0, 0, 0)),
            _const_spec((4, DIFF_DIM)),
            _const_spec((BR_W, BR_W)),
            _const_spec((1, BR_W)),
        ],
        out_specs=pl.BlockSpec((1, tq, BR_W), lambda i, j: (i, j, 0)),
        out_shape=jax.ShapeDtypeStruct((b, s, BR_W), BF16),
        compiler_params=_cparams(("parallel", "parallel")),
        name="attn_diff" if diff else "attn",
    )(q, kt, v, lw["diff_lam"], lw["g64"], lw["subln"])


_NT = (((1,), (1,)), ((), ()))


def _na_kernel(q_ref, k_ref, v_ref, kc_ref, vc_ref, bias_ref, o_ref, *, rows):
    nj = rows // 2
    band = NA_BAND_ROWS * GRID_W
    pair = 2 * GRID_W
    for sub in range(NA_PAIRS):
        j = pl.program_id(1) * NA_PAIRS + sub
        start = jnp.clip(2 * j - 4, 0, rows - NA_BAND_ROWS)
        base = pl.multiple_of(start * GRID_W, 2 * GRID_W)
        cls = jnp.where(j < 2, j, jnp.where(j >= nj - 2, j - (nj - 2) + 3, 2))
        keys = jnp.concatenate([k_ref[0, pl.ds(base, band), :], kc_ref[0]], axis=0)
        q = q_ref[0, sub * pair:(sub + 1) * pair, :]
        heads = []
        for h in range(N_HEADS):
            (w0, lo, hi), = MAPS_NA[h]
            qw = q[:, w0:w0 + LANE]
            qm = jnp.where(_lane_mask(LANE, lo, hi), qw, jnp.zeros_like(qw))
            s = lax.dot_general(qm, keys[:, w0:w0 + LANE], _NT, preferred_element_type=F32) + bias_ref[cls, h]
            p = jnp.exp2(s - jnp.max(s, axis=-1, keepdims=True)).astype(BF16)
            vals = jnp.concatenate([v_ref[0, h, pl.ds(base, band), :], vc_ref[0, h]], axis=0)
            o = jnp.dot(p, vals, preferred_element_type=F32)
            heads.append(o * (1.0 / o[:, V_HEAD:V_HEAD + 1]))
        low = _lane_mask(LANE, 0, V_HEAD)
        acc = jnp.concatenate(
            [jnp.where(low, heads[h], pltpu.roll(heads[h + 1], V_HEAD, axis=1)) for h in range(0, N_HEADS, 2)],
            axis=1)
        o_ref[0, sub * pair:(sub + 1) * pair, :] = acc.astype(BF16)


def _na_call(q, k, v, kc, vc, bias):
    b, s, w = q.shape
    n_ctx = kc.shape[1]
    rows = s // GRID_W
    tq = 2 * GRID_W * NA_PAIRS
    return pl.pallas_call(
        functools.partial(_na_kernel, rows=rows),
        grid=(b, rows // (2 * NA_PAIRS)),
        in_specs=[
            pl.BlockSpec((1, tq, w), lambda i, j: (i, j, 0)),
            pl.BlockSpec((1, s, w), lambda i, j: (i, 0, 0)),
            pl.BlockSpec((1, N_HEADS, s, LANE), lambda i, j: (i, 0, 0, 0)),
            pl.BlockSpec((1, n_ctx, w), lambda i, j: (i, 0, 0)),
            pl.BlockSpec((1, N_HEADS, n_ctx, LANE), lambda i, j: (i, 0, 0, 0)),
            _const_spec(bias.shape),
        ],
        out_specs=pl.BlockSpec((1, tq, w), lambda i, j: (i, j, 0)),
        out_shape=jax.ShapeDtypeStruct((b, s, w), BF16),
        compiler_params=_cparams(("parallel", "arbitrary")),
        name="na",
    )(q, k, v, kc, vc, bias)


def _merge_kernel(x_ref, mod_ref, n1g_ref, n2g_ref, uc_ref, om_ref, od_ref, on_ref,
                  gw_ref, gb_ref, wc_ref, wm_ref, wd_ref, wn_ref, wo_ref, rwt_ref, rb_ref,
                  x1_ref, h2_ref, ids_ref, wts_ref):
    x = x_ref[0]
    mod = mod_ref[0]
    h = _modulate(x, n1g_ref[...], mod[0:1], mod[1:2]).astype(BF16)
    y = jnp.zeros(x.shape, F32)
    branches = ((uc_ref, wc_ref), (om_ref, wm_ref), (od_ref, wd_ref), (on_ref, wn_ref))
    for i, (o_ref, w_ref) in enumerate(branches):
        lo = D_MODEL * i
        g = _sigmoid(jnp.dot(h, gw_ref[:, lo:lo + D_MODEL], preferred_element_type=F32)
                     + gb_ref[:, lo:lo + D_MODEL])
        y = y + g * jnp.dot(o_ref[0], w_ref[...], preferred_element_type=F32)
    out = jnp.dot(y.astype(BF16), wo_ref[...], preferred_element_type=F32)
    x1 = x + mod[2:3] * out
    x1_ref[0] = x1
    h2 = _modulate(x1, n2g_ref[...], mod[3:4], mod[4:5])
    for q, piece in enumerate(_pack_row(h2)):
        h2_ref[q, 0] = piece

    logits = lax.dot_general(rwt_ref[...], h2, _NT, preferred_element_type=F32,
                             precision=lax.Precision.HIGHEST) + rb_ref[...]
    eidx = lax.broadcasted_iota(jnp.int32, logits.shape, 0).astype(F32)
    vals, idxs = [], []
    cur = logits
    for _ in range(TOP_K):
        m = jnp.max(cur, axis=0, keepdims=True)
        idx = jnp.min(jnp.where(cur == m, eidx, float(N_EXPERTS)), axis=0, keepdims=True)
        vals.append(m)
        idxs.append(idx)
        cur = jnp.where(eidx == idx, -jnp.inf, cur)
    es = [jnp.exp(vk - vals[0]) for vk in vals]
    den = es[0] + es[1] + es[2] + es[3]
    ids_ref[0] = jnp.concatenate(idxs, axis=0).astype(jnp.int32)
    wts_ref[0] = jnp.concatenate([e / den for e in es], axis=0)


def _merge_call(x, mods, mod_row, lw, uc, om, od, on):
    b, s, d = x.shape
    t = min(s, 512)
    if mod_row is None:
        mod_map = lambda i, j: (i, 0, 0)
    else:
        mod_map = lambda i, j: (mod_row, 0, 0)
    tok = lambda w: pl.BlockSpec((1, t, w), lambda i, j: (i, j, 0))
    rt = pl.BlockSpec((1, TOP_K, t), lambda i, j: (i, 0, j))
    return pl.pallas_call(
        _merge_kernel,
        grid=(b, s // t),
        in_specs=[
            tok(d), pl.BlockSpec((1, 6, d), mod_map), _const_spec((1, d)), _const_spec((1, d)),
            tok(BR_W), tok(BR_W), tok(BR_W), tok(BR_W),
            _const_spec((d, N_BRANCH * d)), _const_spec((1, N_BRANCH * d)),
            _const_spec((BR_W, d)), _const_spec((BR_W, d)), _const_spec((BR_W, d)), _const_spec((BR_W, d)),
            _const_spec((d, d)), _const_spec((N_EXPERTS, d)), _const_spec((N_EXPERTS, 1)),
        ],
        out_specs=[tok(d), pl.BlockSpec((ROW_PARTS, 1, t, ROW_Q), lambda i, j: (0, i, j, 0)), rt, rt],
        out_shape=[jax.ShapeDtypeStruct((b, s, d), F32), jax.ShapeDtypeStruct((ROW_PARTS, b, s, ROW_Q), jnp.int32),
                   jax.ShapeDtypeStruct((b, TOP_K, s), jnp.int32), jax.ShapeDtypeStruct((b, TOP_K, s), F32)],
        compiler_params=_cparams(("parallel", "parallel")),
        name="merge",
    )(x, mods, lw["n1g"], lw["n2g"], uc, om, od, on, lw["gate_w"], lw["gate_b"],
      lw["conv_out"], lw["mla_out"], lw["diff_out"], lw["na_out"], lw["w_o"], lw["router_wt"], lw["router_b"])


SC_WINDOW = 128
ROW_Q = D_MODEL // 4
ROW_PARTS = 2
HALF_D = D_MODEL // 2


def _pack_bf16_pair(a, b):
    ua = lax.bitcast_convert_type(a.astype(BF16).astype(F32), jnp.int32)
    ub = lax.bitcast_convert_type(b.astype(BF16).astype(F32), jnp.int32)
    return ua | lax.shift_right_logical(ub, jnp.int32(16))


def _unpack_bf16_pair(w):
    a = lax.bitcast_convert_type(w & jnp.int32(-65536), F32)
    b = lax.bitcast_convert_type(lax.shift_left(w, jnp.int32(16)), F32)
    return a, b


def _pack_row(x):
    w = _pack_bf16_pair(x[:, :HALF_D], x[:, HALF_D:])
    return [w[:, ROW_Q * q:ROW_Q * (q + 1)] for q in range(ROW_PARTS)]


def _unpack_row(pieces):
    ab = [_unpack_bf16_pair(w) for w in pieces]
    return jnp.concatenate([a for a, _ in ab] + [b for _, b in ab], axis=1)


def _route_slots(ids, tile):
    n = ids.shape[1]
    p = TOP_K * n
    e = ids.reshape(p)
    onehot = (e[:, None] == jnp.arange(N_EXPERTS, dtype=jnp.int32)[None, :])
    chunk = 512
    oh3 = onehot.astype(F32).reshape(p // chunk, chunk, N_EXPERTS)
    within = jnp.einsum("ij,cje->cie", jnp.tril(jnp.ones((chunk, chunk), F32)), oh3)
    totals = within[:, -1, :]
    before = jnp.cumsum(totals, axis=0) - totals
    csum = (within + before[:, None, :]).reshape(p, N_EXPERTS).astype(jnp.int32)
    onehot = onehot.astype(jnp.int32)
    counts = csum[-1]
    padded = ((counts + tile - 1) // tile) * tile
    gend = jnp.cumsum(padded)
    gstart = gend - padded
    slot = jnp.sum(onehot * (csum - 1 + gstart[None, :]), axis=1).astype(jnp.int32)
    n_tiles = p // tile + N_EXPERTS
    tile_start = jnp.arange(n_tiles, dtype=jnp.int32) * tile
    texp = jnp.sum((tile_start[:, None] >= gend[None, :]).astype(jnp.int32), axis=1)
    texp = jnp.minimum(texp, N_EXPERTS - 1)
    nreal = jnp.clip(gstart[texp] + counts[texp] - tile_start, 0, tile)
    nreal = jnp.where(tile_start < gend[-1], nreal, 0).astype(jnp.int32)
    return slot, texp, nreal, n_tiles


def _sc_mesh():
    return plsc.VectorSubcoreMesh(core_axis_name="c", subcore_axis_name="s")


def _sc_dispatch(hq, idx, n_slots):
    parts, n, w = hq.shape
    src = hq.reshape(parts * n, w)
    m = idx.shape[0]
    blocks_per_q = n // SC_WINDOW
    per_q = TOP_K * blocks_per_q

    @pl.kernel(out_type=jax.ShapeDtypeStruct((parts * n_slots, w), hq.dtype), mesh=_sc_mesh(), scratch_types=[])
    def kern(x_hbm, i_hbm, o_hbm):
        def body(x_vmem, i_vmem):
            pltpu.sync_copy(x_vmem, o_hbm.at[i_vmem.at[0]])

        pltpu.emit_pipeline(
            body,
            grid=(m // SC_WINDOW,),
            in_specs=[
                pl.BlockSpec((SC_WINDOW, w), index_map=lambda i: ((i // per_q) * blocks_per_q + i % blocks_per_q, 0)),
                pl.BlockSpec((1, SC_WINDOW), index_map=lambda i: (0, i)),
            ],
            out_specs=[],
            core_axis_name=("c", "s"),
            dimension_semantics=(pltpu.PARALLEL,),
        )(x_hbm, i_hbm)

    return kern(src, idx.reshape(1, m)).reshape(parts, n_slots, w)


def _sc_collect(ys, idx):
    parts, n_slots, w = ys.shape
    src = ys.reshape(parts * n_slots, w)
    m = idx.shape[0]

    @pl.kernel(out_type=jax.ShapeDtypeStruct((m, w), ys.dtype), mesh=_sc_mesh(), scratch_types=[])
    def kern(x_hbm, i_hbm, o_hbm):
        def body(i_vmem, o_vmem):
            pltpu.sync_copy(x_hbm.at[i_vmem.at[0]], o_vmem)

        pltpu.emit_pipeline(
            body,
            grid=(m // SC_WINDOW,),
            in_specs=[pl.BlockSpec((1, SC_WINDOW), index_map=lambda i: (0, i))],
            out_specs=[pl.BlockSpec((SC_WINDOW, w), index_map=lambda i: (i, 0))],
            core_axis_name=("c", "s"),
            dimension_semantics=(pltpu.PARALLEL,),
        )(i_hbm, o_hbm)

    return kern(src, idx.reshape(1, m))


def _ffn_sorted_kernel(texp_ref, nreal_ref, x_ref, wgu_ref, bgu_ref, wd_ref, bd_ref, y_ref, wgu_bf, wd_bf, *, tile):
    i = pl.program_id(0)
    nreal = nreal_ref[i]

    @pl.when((nreal > 0) & ((i == 0) | (texp_ref[i] != texp_ref[jnp.maximum(i - 1, 0)])))
    def _():
        wgu_bf[...] = wgu_ref[0].astype(BF16)
        wd_bf[...] = wd_ref[0].astype(BF16)

    @pl.when(nreal > 0)
    def _():
        x = _unpack_row([x_ref[q] for q in range(ROW_PARTS)])
        row = lax.broadcasted_iota(jnp.int32, (tile, 1), 0)
        x = jnp.where(row < nreal, x, 0.0).astype(BF16)
        gu = jnp.dot(x, wgu_bf[...], preferred_element_type=F32) + bgu_ref[0]
        g = jnp.minimum(gu[:, :D_FF], SWIGLU_LIMIT)
        u = jnp.clip(gu[:, D_FF:], -SWIGLU_LIMIT, SWIGLU_LIMIT)
        act = ((u + 1.0) * (g * _sigmoid(SWIGLU_ALPHA * g))).astype(BF16)
        y = jnp.dot(act, wd_bf[...], preferred_element_type=F32) + bd_ref[0]
        for q, piece in enumerate(_pack_row(y)):
            y_ref[q] = piece

    @pl.when(nreal == 0)
    def _():
        y_ref[...] = jnp.zeros(y_ref.shape, jnp.int32)


def _ffn_sorted_call(xs, texp, nreal, lw, tile):
    _, n_slots, w = xs.shape
    n_tiles = n_slots // tile
    d = D_MODEL
    off = lw["exp_off"]
    grid_spec = pltpu.PrefetchScalarGridSpec(
        num_scalar_prefetch=2,
        grid=(n_tiles,),
        in_specs=[
            pl.BlockSpec((ROW_PARTS, tile, w), lambda i, te, nr: (0, i, 0)),
            pl.BlockSpec((1, d, 2 * D_FF), lambda i, te, nr: (te[i] + off, 0, 0)),
            pl.BlockSpec((1, 1, 2 * D_FF), lambda i, te, nr: (te[i] + off, 0, 0)),
            pl.BlockSpec((1, D_FF, d), lambda i, te, nr: (te[i] + off, 0, 0)),
            pl.BlockSpec((1, 1, d), lambda i, te, nr: (te[i] + off, 0, 0)),
        ],
        out_specs=pl.BlockSpec((ROW_PARTS, tile, w), lambda i, te, nr: (0, i, 0)),
        scratch_shapes=[pltpu.VMEM((d, 2 * D_FF), BF16), pltpu.VMEM((D_FF, d), BF16)],
    )
    return pl.pallas_call(
        functools.partial(_ffn_sorted_kernel, tile=tile),
        grid_spec=grid_spec,
        out_shape=jax.ShapeDtypeStruct((ROW_PARTS, n_slots, w), jnp.int32),
        compiler_params=_cparams(("arbitrary",)),
        name="moe_ffn",
    )(texp, nreal, xs, lw["exp_w_gu"], lw["exp_b_gu"], lw["exp_w_down"], lw["exp_b_down"])


def _combine_q_kernel(x1_ref, mod_ref, w_ref, y_ref, o_ref):
    w = w_ref[0]
    g2 = mod_ref[0][5:6]
    for q in range(ROW_PARTS):
        acc_a, acc_b = None, None
        for k in range(TOP_K):
            a, b = _unpack_bf16_pair(y_ref[q, k])
            wk = w[:, k:k + 1]
            acc_a = wk * a if acc_a is None else acc_a + wk * a
            acc_b = wk * b if acc_b is None else acc_b + wk * b
        for lo, acc in ((ROW_Q * q, acc_a), (HALF_D + ROW_Q * q, acc_b)):
            o_ref[0, :, lo:lo + ROW_Q] = x1_ref[0, :, lo:lo + ROW_Q] + g2[:, lo:lo + ROW_Q] * acc


def _combine_q_call(x1, mods, mod_row, wts, y, tok_off):
    b, s, d = x1.shape
    t = min(s, 512)
    nt = s // t
    blk_off = tok_off // t
    if mod_row is None:
        mod_map = lambda i, j: (i, 0, 0)
    else:
        mod_map = lambda i, j: (mod_row, 0, 0)
    wts = wts.transpose(0, 2, 1)
    return pl.pallas_call(
        _combine_q_kernel,
        grid=(b, nt),
        in_specs=[
            pl.BlockSpec((1, t, d), lambda i, j: (i, j, 0)),
            pl.BlockSpec((1, 6, d), mod_map),
            pl.BlockSpec((1, t, TOP_K), lambda i, j: (i, j, 0)),
            pl.BlockSpec((ROW_PARTS, TOP_K, t, ROW_Q), lambda i, j: (0, 0, blk_off + i * nt + j, 0)),
        ],
        out_specs=pl.BlockSpec((1, t, d), lambda i, j: (i, j, 0)),
        out_shape=jax.ShapeDtypeStruct((b, s, d), F32),
        compiler_params=_cparams(("parallel", "parallel")),
        name="moe_combine",
    )(x1, mods, wts, y)


def _moe_sc(streams, mods, lw):
    sizes = [st[0].shape[0] * st[0].shape[1] for st in streams]
    n = sum(sizes)
    tile = 512 if TOP_K * n >= 512 * N_EXPERTS * 4 else 256
    hq = jnp.concatenate([st[1].reshape(ROW_PARTS, m, ROW_Q) for st, m in zip(streams, sizes)], axis=1)
    ids = jnp.concatenate([st[2].transpose(1, 0, 2).reshape(TOP_K, m) for st, m in zip(streams, sizes)], axis=1)
    slot, texp, nreal, n_tiles = _route_slots(ids, tile)
    n_slots = n_tiles * tile
    idx = (slot[None, :] + (jnp.arange(ROW_PARTS, dtype=jnp.int32) * n_slots)[:, None]).reshape(-1)
    xs = _sc_dispatch(hq, idx, n_slots)
    ys = _ffn_sorted_call(xs, texp, nreal, lw, tile)
    y = _sc_collect(ys, idx).reshape(ROW_PARTS, TOP_K, n, ROW_Q)
    outs, off = [], 0
    for (x1, _, _, wts, mod_row), m in zip(streams, sizes):
        outs.append(_combine_q_call(x1, mods, mod_row, wts, y, off))
        off += m
    return outs


def _layer_weights(l, p, lam_init):
    w = p["w_in"][l].astype(BF16)
    d = w.shape[0]
    zcols = lambda n: jnp.zeros((d, n), w.dtype)
    regroup = lambda blk: blk.reshape(d, N_HEADS, 3, HEAD_DIM).transpose(0, 2, 1, 3).reshape(d, 3 * BR_W)
    w_in = jnp.concatenate([
        w[:, :A_IN],
        w[:, OFF_B:OFF_B + Q_LORA], zcols(P_CKV - P_CQ - Q_LORA),
        w[:, OFF_B + Q_LORA:OFF_B + Q_LORA + KV_LORA],
        w[:, OFF_B + Q_LORA + KV_LORA:OFF_C], zcols(P_DQ - P_KPE - QK_ROPE),
        regroup(w[:, OFF_C:OFF_D]), regroup(w[:, OFF_D:]),
    ], axis=1)
    assert w_in.shape[1] == PROJ_W

    def head_slots(w3, slot):
        w3 = jnp.pad(w3, ((0, 0), (0, 0), (0, slot - w3.shape[2])))
        return w3.reshape(w3.shape[0], N_HEADS * slot)

    wuq = head_slots(p["mla_w_uq"][l].reshape(Q_LORA, N_HEADS, MLA_QK), LANE)
    wuq = jnp.pad(wuq, ((0, 256 - Q_LORA), (0, 0)))
    wukv = p["mla_w_ukv"][l].reshape(KV_LORA, N_HEADS, QK_NOPE + V_HEAD)
    wk = head_slots(wukv[:, :, :QK_NOPE], LANE)
    wv = head_slots(wukv[:, :, QK_NOPE:], V_HEAD)
    ppe = np.zeros((LANE, 512), np.float32)
    for h in range(N_HEADS):
        for i in range(QK_ROPE):
            ppe[i, h * LANE + QK_NOPE + i] = 1.0

    def slot_gain(g, scale):
        g = jnp.concatenate([g * scale, jnp.zeros((LANE - MLA_QK,), F32)])
        return jnp.tile(g, N_HEADS)

    def row512(v):
        return jnp.concatenate([v, jnp.zeros((512 - v.shape[0],), F32)])

    gains = jnp.stack([
        row512(p["mla_cq_g"][l]),
        slot_gain(p["mla_qn_g"][l], MLA_QK ** -0.5 * LOG2E),
        row512(p["mla_ckv_g"][l]),
        slot_gain(p["mla_kn_g"][l], 1.0),
        row512(jnp.tile(p["diff_qn_g"][l], 2 * N_HEADS) * DIFF_DIM ** -0.5 * LOG2E),
        row512(jnp.tile(p["diff_kn_g"][l], 2 * N_HEADS)),
        row512(jnp.tile(p["na_qn_g"][l], N_HEADS) * HEAD_DIM ** -0.5 * LOG2E),
        row512(jnp.tile(p["na_kn_g"][l], N_HEADS)),
    ])
    conv_w = jnp.concatenate([p["conv_w"][l], jnp.zeros((1, CONV_CH), F32)], axis=0)
    return dict(
        n1g=p["norm1_g"][l][None, :], n2g=p["norm2_g"][l][None, :],
        w_in=w_in, gains=gains,
        wuq=wuq.astype(BF16), wk=wk.astype(BF16), wv=wv.astype(BF16), ppe=jnp.asarray(ppe, BF16),
        g96=jnp.asarray(_group_ones(512, LANE, MLA_QK), BF16),
        g32=jnp.asarray(_group_ones(BR_W, DIFF_DIM, DIFF_DIM), BF16),
        g64=jnp.asarray(_group_ones(BR_W, HEAD_DIM, HEAD_DIM), BF16),
        rm=jnp.asarray(_rot_matrix(512, LANE, QK_NOPE, QK_ROPE // 2), BF16),
        rd=jnp.asarray(_rot_matrix(BR_W, DIFF_DIM, 0, DIFF_DIM // 2), BF16),
        conv_w=conv_w, conv_b=p["conv_b"][l][None, :],
        conv_ln_g=p["conv_ln_g"][l][None, :], conv_ln_b=p["conv_ln_b"][l][None, :],
        diff_lam=p["diff_lam"][l],
        subln=(jnp.tile(p["diff_subln_g"][l], N_HEADS) * (1.0 - lam_init))[None, :],
        gate_w=p["gate_w"][l].astype(BF16), gate_b=p["gate_b"][l][None, :],
        conv_out=p["conv_out"][l].astype(BF16), mla_out=p["mla_out"][l].astype(BF16),
        diff_out=p["diff_out"][l].astype(BF16), na_out=p["na_out"][l].astype(BF16),
        w_o=p["w_o"][l].astype(BF16),
        router_wt=p["router_w"][l].T, router_b=p["router_b"][l][:, None],
        exp_off=l * N_EXPERTS,
        exp_w_gu=p["exp_w_gu"].reshape((-1,) + p["exp_w_gu"].shape[2:]),
        exp_b_gu=p["exp_b_gu"].reshape(-1, 1, 2 * D_FF),
        exp_w_down=p["exp_w_down"].reshape((-1,) + p["exp_w_down"].shape[2:]),
        exp_b_down=p["exp_b_down"].reshape(-1, 1, D_MODEL),
    )


def _kt(kc, k):
    return jnp.concatenate([kc, k], axis=1).transpose(0, 2, 1)


def kernel(x, c, ctx, c_ctx, ada_w, ada_b, norm1_g, norm2_g, w_in, conv_w, conv_b, conv_ln_g, conv_ln_b, conv_out, mla_cq_g, mla_ckv_g, mla_w_uq, mla_w_ukv, mla_qn_g, mla_kn_g, mla_out, diff_qn_g, diff_kn_g, diff_lam, diff_subln_g, diff_out, na_qn_g, na_kn_g, na_rpb, na_out, gate_w, gate_b, w_o, router_w, router_b, exp_w_gu, exp_b_gu, exp_w_down, exp_b_down):
    p = dict(norm1_g=norm1_g, norm2_g=norm2_g, w_in=w_in, conv_w=conv_w, conv_b=conv_b,
             conv_ln_g=conv_ln_g, conv_ln_b=conv_ln_b, conv_out=conv_out, mla_cq_g=mla_cq_g,
             mla_ckv_g=mla_ckv_g, mla_w_uq=mla_w_uq, mla_w_ukv=mla_w_ukv, mla_qn_g=mla_qn_g,
             mla_kn_g=mla_kn_g, mla_out=mla_out, diff_qn_g=diff_qn_g, diff_kn_g=diff_kn_g,
             diff_lam=diff_lam, diff_subln_g=diff_subln_g, diff_out=diff_out, na_qn_g=na_qn_g,
             na_kn_g=na_kn_g, na_out=na_out, gate_w=gate_w, gate_b=gate_b, w_o=w_o,
             router_w=router_w, router_b=router_b, exp_w_gu=exp_w_gu, exp_b_gu=exp_b_gu,
             exp_w_down=exp_w_down, exp_b_down=exp_b_down)
    b, s, d = x.shape
    n_ctx = ctx.shape[1]
    depth = ada_w.shape[0]
    rows = s // GRID_W
    assert d == D_MODEL and s % (2 * GRID_W) == 0 and rows >= NA_BAND_ROWS and n_ctx % LANE == 0

    mod_rows = -(-(b + 1) // 8) * 8
    cs = jnp.concatenate([c, c_ctx[None, :], jnp.zeros((mod_rows - b - 1, d), F32)], axis=0)
    mods_all = _ada_call(cs, ada_w, ada_b).reshape(depth, mod_rows, 6, d)

    tabs_x = _rope_lane_tables(s)
    tabs_c = (jnp.ones((n_ctx, 512), F32), jnp.zeros((n_ctx, 512), F32),
              jnp.ones((n_ctx, BR_W), F32), jnp.zeros((n_ctx, BR_W), F32))

    xc = ctx
    for l in range(depth):
        last = l == depth - 1
        lam_init = 0.8 - 0.6 * math.exp(-0.3 * l)
        lw = _layer_weights(l, p, lam_init)
        mods = mods_all[l]
        bias = _na_bias_tables(na_rpb[l], rows, n_ctx)

        u, mq, mk, mv, dq, dk, dv, nq, nk, nv = _proj_call(x, mods, None, lw, tabs_x, True)
        uc, mqc, mkc, mvc, dqc, dkc, dvc, nqc, nkc, nvc = _proj_call(xc, mods, b, lw, tabs_c, False)

        y_conv = _conv_call(u, lw)
        y_mla, y_diff = _attn_pair_call(mq, _kt(mkc, mk), jnp.concatenate([mvc, mv], axis=2),
                                        dq, _kt(dkc, dk), jnp.concatenate([dvc, dv], axis=2), lw, lam_init)
        y_na = _na_call(nq, nk, nv, nkc, nvc, bias)
        x1, h2, ids, wts = _merge_call(x, mods, None, lw, y_conv, y_mla, y_diff, y_na)
        streams = [(x1, h2, ids, wts, None)]

        if not last:
            yc_conv = _conv_call(uc, lw)
            yc_mla = _attn_call(mqc, mkc.transpose(0, 2, 1), mvc, lw, MAPS_MLA)
            yc_diff = _attn_call(dqc, dkc.transpose(0, 2, 1), dvc, lw, MAPS_DIFF, diff=True, lam_init=lam_init)
            yc_na = _attn_call(nqc, nkc.transpose(0, 2, 1), nvc, lw, MAPS_NA)
            xc1, h2c, idsc, wtsc = _merge_call(xc, mods, b, lw, yc_conv, yc_mla, yc_diff, yc_na)
            streams.append((xc1, h2c, idsc, wtsc, b))

        outs = _moe_sc(streams, mods, lw)
        x = outs[0]
        if not last:
            xc = outs[1]
    return x
```
